```python
import math
import jax
import jax.numpy as jnp
from jax import lax
import numpy as np

D_MODEL = 4096
BATCH = 2
SEQ = 8192
DEPTH = 2

CHUNK = 64
Q_BLOCK = 128
EPS = 1e-6
NEG = -1e30
F_FLOOR = 1e-20
W_MIX = D_MODEL
GROUP = W_MIX // 4
HEAD_DIM = 128
A_HEADS = GROUP // HEAD_DIM
A_DK = HEAD_DIM
A_DV = GROUP // A_HEADS
B_HEADS = GROUP // HEAD_DIM
B_DQK = HEAD_DIM // 2
B_DV = GROUP // B_HEADS
CONV_W = 4
C_HEADS = GROUP // (2 * HEAD_DIM)
C_DQK = HEAD_DIM
C_DV = 2 * HEAD_DIM
D_HEADS = GROUP // HEAD_DIM
D_DH = HEAD_DIM
IDX_HEADS = 16
IDX_DIM = 64
TOPK_MAX = 256
D_FF = 4 * D_MODEL

SPLIT_SIZES = (
    GROUP, GROUP, GROUP, GROUP,
    2 * B_HEADS * B_DQK, GROUP, GROUP, 2 * B_HEADS,
    2 * C_HEADS * C_DQK, 2 * C_HEADS * C_DQK, C_HEADS * C_DV,
    D_HEADS * D_DH, D_HEADS * D_DH, D_HEADS * D_DH,
    IDX_HEADS * IDX_DIM, IDX_DIM, IDX_HEADS,
)
N_IN = sum(SPLIT_SIZES)

kernel_name = 'hybrid_hgrn2_mlstm_diffattn_dsa_block'


def _split_points():
    pts, acc = [], 0
    for size in SPLIT_SIZES[:-1]:
        acc += size
        pts.append(acc)
    return pts


def rmsnorm(x, g):
    xf = x.astype(jnp.float32)
    y = xf * lax.rsqrt(jnp.mean(jnp.square(xf), axis=-1, keepdims=True) + EPS)
    return (y * g.astype(jnp.float32)).astype(x.dtype)


def _heads(t, n_heads):
    b, s, _ = t.shape
    return t.astype(jnp.float32).reshape(b, s, n_heads, -1).transpose(0, 2, 1, 3)


def _to_chunks(t):
    b, h, s = t.shape[:3]
    t = t.reshape((b, h, s // CHUNK, CHUNK) + t.shape[3:])
    return jnp.moveaxis(t, 2, 0)


def _from_chunks(y):
    nc, b, h, c, d = y.shape
    return jnp.moveaxis(y, 0, 2).reshape(b, h, nc * c, d).transpose(0, 2, 1, 3)


def causal_conv(x, w):
    c = x.shape[-1]
    return lax.conv_general_dilated(
        x, w[:, None, :].astype(x.dtype), window_strides=(1,), padding=[(CONV_W - 1, 0)],
        dimension_numbers=('NWC', 'WIO', 'NWC'), feature_group_count=c)


def hgrn2_mixer(q_raw, f_raw, i_raw, g_raw, lb, g_norm):
    bsz, s, _ = q_raw.shape
    z = f_raw.astype(jnp.float32)
    lb = lb.astype(jnp.float32)
    f = lb + (1.0 - lb) * jax.nn.sigmoid(z)
    log_f = jnp.log(jnp.maximum(f, F_FLOOR))
    k_in = 1.0 - f
    q = _heads(jax.nn.silu(q_raw.astype(jnp.float32)), A_HEADS) * A_DK ** -0.5
    k = _heads(k_in, A_HEADS)
    v = _heads(i_raw, A_HEADS)
    g = _heads(log_f, A_HEADS)
    causal = jnp.tril(jnp.ones((CHUNK, CHUNK), dtype=bool))

    def step(state, inp):
        qc, kc, vc, gc = inp
        b = jnp.cumsum(gc, axis=2)
        rel = b[:, :, :, None, :] - b[:, :, None, :, :]
        decay = jnp.where(causal[:, :, None], jnp.exp(jnp.minimum(rel, 0.0)), 0.0)
        scores = jnp.einsum('bhtk,bhtsk,bhsk->bhts', qc, decay, kc)
        o = (jnp.einsum('bhts,bhsv->bhtv', scores, vc)
             + jnp.einsum('bhtk,bhkv->bhtv', qc * jnp.exp(b), state))
        b_last = b[:, :, -1:, :]
        state = (jnp.exp(b_last[:, :, 0, :, None]) * state
                 + jnp.einsum('bhsk,bhsv->bhkv', kc * jnp.exp(b_last - b), vc))
        return state, o

    s0 = jnp.zeros((bsz, A_HEADS, A_DK, A_DV), jnp.float32)
    _, o = lax.scan(step, s0, (_to_chunks(q), _to_chunks(k), _to_chunks(v), _to_chunks(g)))
    o = _from_chunks(o)
    gate = jax.nn.silu(g_raw.astype(jnp.float32)).reshape(bsz, s, A_HEADS, A_DV)
    o = rmsnorm(o, g_norm) * gate
    return o.reshape(bsz, s, GROUP).astype(q_raw.dtype)


def mlstm_mixer(qk_raw, v_raw, o_raw, if_raw, conv_w, gate_bias, h_norm):
    bsz, s, _ = qk_raw.shape
    qk = jax.nn.silu(causal_conv(qk_raw, conv_w))
    q_half, k_half = jnp.split(qk, 2, axis=-1)
    q = _heads(q_half, B_HEADS)
    k = _heads(k_half, B_HEADS) * B_DQK ** -0.5
    v = _heads(v_raw, B_HEADS)
    gates = (if_raw.astype(jnp.float32) + gate_bias.astype(jnp.float32)).transpose(0, 2, 1)
    ig, fg = jnp.split(gates, 2, axis=1)
    log_f = jax.nn.log_sigmoid(fg)
    causal = jnp.tril(jnp.ones((CHUNK, CHUNK), dtype=bool))

    def step(carry, inp):
        c_mat, n_vec, m_prev = carry
        qc, kc, vc, igc, lfc = inp
        b = jnp.cumsum(lfc, axis=-1)
        log_d = jnp.where(causal, b[..., :, None] - b[..., None, :] + igc[..., None, :], NEG)
        log_inter = b + m_prev[..., None]
        m_t = jnp.maximum(jnp.max(log_d, axis=-1), log_inter)
        qk_w = jnp.einsum('bhtd,bhsd->bhts', qc, kc) * jnp.exp(log_d - m_t[..., None])
        w_inter = jnp.exp(log_inter - m_t)
        num = (jnp.einsum('bhts,bhsv->bhtv', qk_w, vc)
               + w_inter[..., None] * jnp.einsum('bhtd,bhdv->bhtv', qc, c_mat))
        den = jnp.sum(qk_w, axis=-1) + w_inter * jnp.einsum('bhtd,bhd->bht', qc, n_vec)
        h = num / jnp.maximum(jnp.abs(den), jnp.exp(-m_t))[..., None]
        m_new = m_t[..., -1]
        w_state = jnp.exp(b[..., -1:] - b + igc - m_new[..., None])
        decay = jnp.exp(b[..., -1] + m_prev - m_new)
        c_mat = decay[..., None, None] * c_mat + jnp.einsum('bhs,bhsd,bhsv->bhdv', w_state, kc, vc)
        n_vec = decay[..., None] * n_vec + jnp.einsum('bhs,bhsd->bhd', w_state, kc)
        return (c_mat, n_vec, m_new), h

    carry0 = (jnp.zeros((bsz, B_HEADS, B_DQK, B_DV), jnp.float32),
              jnp.zeros((bsz, B_HEADS, B_DQK), jnp.float32),
              jnp.zeros((bsz, B_HEADS), jnp.float32))
    _, h = lax.scan(step, carry0, (_to_chunks(q), _to_chunks(k), _to_chunks(v),
                                   _to_chunks(ig), _to_chunks(log_f)))
    h = _from_chunks(h)
    o_gate = jax.nn.sigmoid(o_raw.astype(jnp.float32)).reshape(bsz, s, B_HEADS, B_DV)
    return (rmsnorm(h, h_norm) * o_gate).reshape(bsz, s, GROUP).astype(qk_raw.dtype)


def diff_attention(q_raw, k_raw, v_raw, qk_norm, lam, sub_norm, lambda_init):
    bsz, s, _ = q_raw.shape
    q = rmsnorm(q_raw.reshape(bsz, s, C_HEADS, 2, C_DQK), qk_norm[0]) * C_DQK ** -0.5
    k = rmsnorm(k_raw.reshape(bsz, s, C_HEADS, 2, C_DQK), qk_norm[1])
    v = v_raw.reshape(bsz, s, C_HEADS, C_DV)
    lam_f = lam.astype(jnp.float32)
    lam_full = (jnp.exp(jnp.sum(lam_f[0] * lam_f[1])) - jnp.exp(jnp.sum(lam_f[2] * lam_f[3]))
                + lambda_init)
    k_chunk = jnp.arange(s) // CHUNK

    def block(i):
        start = i * Q_BLOCK
        qb = lax.dynamic_slice_in_dim(q, start, Q_BLOCK, axis=1)
        scores = jnp.einsum('bthmd,bshmd->bhmts', qb, k).astype(jnp.float32)
        q_chunk = (start + jnp.arange(Q_BLOCK)) // CHUNK
        mask = k_chunk[None, :] <= q_chunk[:, None]
        p = jax.nn.softmax(jnp.where(mask, scores, NEG), axis=-1)
        a = p[:, :, 0] - lam_full * p[:, :, 1]
        return jnp.einsum('bhts,bshv->bthv', a.astype(v.dtype), v)

    o = lax.map(block, jnp.arange(s // Q_BLOCK))
    o = jnp.moveaxis(o, 0, 1).reshape(bsz, s, C_HEADS, C_DV)
    o = rmsnorm(o, sub_norm) * (1.0 - lambda_init)
    return o.reshape(bsz, s, GROUP)


def dsa_attention(q_raw, k_raw, v_raw, qi_raw, ki_raw, wi_raw, qk_norm, ki_norm):
    bsz, s, _ = q_raw.shape
    q = rmsnorm(q_raw.reshape(bsz, s, D_HEADS, D_DH), qk_norm[0]) * D_DH ** -0.5
    k = rmsnorm(k_raw.reshape(bsz, s, D_HEADS, D_DH), qk_norm[1])
    v = v_raw.reshape(bsz, s, D_HEADS, D_DH)
    qi = qi_raw.reshape(bsz, s, IDX_HEADS, IDX_DIM)
    ki = rmsnorm(ki_raw, ki_norm)
    wi = wi_raw * IDX_HEADS ** -0.5
    topk = min(TOPK_MAX, s // 4)
    k_chunk = jnp.arange(s) // CHUNK
    gather = jax.vmap(lambda table, idx: table[idx])

    def block(i):
        start = i * Q_BLOCK
        qb = lax.dynamic_slice_in_dim(q, start, Q_BLOCK, axis=1)
        qib = lax.dynamic_slice_in_dim(qi, start, Q_BLOCK, axis=1)
        wib = lax.dynamic_slice_in_dim(wi, start, Q_BLOCK, axis=1)
        q_chunk = (start + jnp.arange(Q_BLOCK)) // CHUNK
        logits = jnp.einsum('bthd,bsd->bths', qib, ki).astype(jnp.float32) * IDX_DIM ** -0.5
        index_score = jnp.einsum('bths,bth->bts', jax.nn.relu(logits), wib.astype(jnp.float32))
        admissible = k_chunk[None, :] <= q_chunk[:, None]
        index_score = jnp.where(admissible[None], index_score, NEG)
        _, sel = lax.top_k(index_score, topk)
        valid = (sel // CHUNK) <= q_chunk[None, :, None]
        k_sel = gather(k, sel)
        v_sel = gather(v, sel)
        scores = jnp.einsum('bthd,btkhd->bhtk', qb, k_sel).astype(jnp.float32)
        p = jax.nn.softmax(jnp.where(valid[:, None], scores, NEG), axis=-1)
        return jnp.einsum('bhtk,btkhd->bthd', p.astype(v.dtype), v_sel)

    o = lax.map(block, jnp.arange(s // Q_BLOCK))
    return jnp.moveaxis(o, 0, 1).reshape(bsz, s, GROUP)


def setup_inputs(seed: int = 0) -> dict:
    key = jax.random.key(seed)
    ks = jax.random.split(key, 20)
    f32 = jnp.float32

    def nrm(k, shape, scale):
        return scale * jax.random.normal(k, shape, f32)

    fg_bias = jnp.linspace(3.0, 6.0, B_HEADS, dtype=f32)
    return {
        'x': nrm(ks[0], (BATCH, SEQ, D_MODEL), 1.0),
        'norm_mix': 1.0 + nrm(ks[1], (DEPTH, D_MODEL), 0.02),
        'w_in': nrm(ks[2], (DEPTH, D_MODEL, N_IN), D_MODEL ** -0.5),
        'hgrn_lb_logits': nrm(ks[3], (DEPTH, GROUP), 1.0),
        'hgrn_gnorm': 1.0 + nrm(ks[4], (DEPTH, A_DV), 0.02),
        'mlstm_conv': nrm(ks[5], (DEPTH, CONV_W, 2 * B_HEADS * B_DQK), CONV_W ** -0.5),
        'mlstm_gate_bias': jnp.concatenate(
            [nrm(ks[6], (DEPTH, B_HEADS), 0.1), fg_bias[None] + nrm(ks[7], (DEPTH, B_HEADS), 0.1)], axis=-1),
        'mlstm_hnorm': 1.0 + nrm(ks[8], (DEPTH, B_DV), 0.02),
        'diff_qk_norm': 1.0 + nrm(ks[9], (DEPTH, 2, C_DQK), 0.02),
        'diff_lambda': nrm(ks[10], (DEPTH, 4, C_DQK), 0.1),
        'diff_subln': 1.0 + nrm(ks[11], (DEPTH, C_DV), 0.02),
        'dsa_qk_norm': 1.0 + nrm(ks[12], (DEPTH, 2, D_DH), 0.02),
        'dsa_idx_knorm': 1.0 + nrm(ks[13], (DEPTH, IDX_DIM), 0.02),
        'w_out': nrm(ks[14], (DEPTH, W_MIX, D_MODEL), W_MIX ** -0.5),
        'norm_mlp': 1.0 + nrm(ks[15], (DEPTH, D_MODEL), 0.02),
        'w1': nrm(ks[16], (DEPTH, D_MODEL, D_FF), D_MODEL ** -0.5),
        'w2': nrm(ks[17], (DEPTH, D_FF, D_MODEL), D_FF ** -0.5),
    }


def reference(x, norm_mix, w_in, hgrn_lb_logits, hgrn_gnorm, mlstm_conv, mlstm_gate_bias,
              mlstm_hnorm, diff_qk_norm, diff_lambda, diff_subln, dsa_qk_norm, dsa_idx_knorm,
              w_out, norm_mlp, w1, w2):
    p_lb = jax.nn.softmax(hgrn_lb_logits.astype(jnp.float32), axis=0)
    lower_bounds = jnp.cumsum(p_lb, axis=0) - p_lb[0:1]
    points = _split_points()
    for l in range(DEPTH):
        lambda_init = 0.8 - 0.6 * math.exp(-0.3 * l)
        h = rmsnorm(x, norm_mix[l])
        proj = jnp.einsum('bsd,dn->bsn', h, w_in[l])
        (a_q, a_f, a_i, a_g, b_qk, b_v, b_o, b_if, c_q, c_k, c_v,
         d_q, d_k, d_v, d_qi, d_ki, d_w) = jnp.split(proj, points, axis=-1)
        y_a = hgrn2_mixer(a_q, a_f, a_i, a_g, lower_bounds[l], hgrn_gnorm[l])
        y_b = mlstm_mixer(b_qk, b_v, b_o, b_if, mlstm_conv[l], mlstm_gate_bias[l], mlstm_hnorm[l])
        y_c = diff_attention(c_q, c_k, c_v, diff_qk_norm[l], diff_lambda[l], diff_subln[l], lambda_init)
        y_d = dsa_attention(d_q, d_k, d_v, d_qi, d_ki, d_w, dsa_qk_norm[l], dsa_idx_knorm[l])
        y = jnp.concatenate([y_a, y_b, y_c, y_d], axis=-1)
        x = x + jnp.einsum('bsw,wd->bsd', y, w_out[l])
        h = rmsnorm(x, norm_mlp[l])
        hidden = jnp.square(jax.nn.relu(jnp.einsum('bsd,df->bsf', h, w1[l])))
        x = x + jnp.einsum('bsf,fd->bsd', hidden, w2[l])
    return x
```

```python
import functools
import math

import numpy as np
import jax
import jax.numpy as jnp
from jax import lax
from jax.experimental import pallas as pl
from jax.experimental.pallas import tpu as pltpu

CHUNK = 64
EPS = 1e-6
NEG = -1e30
F_FLOOR = 1e-20
HEAD_DIM = 128
CONV_W = 4
IDX_HEADS = 16
IDX_DIM = 64
TOPK_MAX = 256
LANE = 128
VMEM_LIMIT = 56 * 1024 * 1024

F32 = jnp.float32
BF16 = jnp.bfloat16
NT_DIMS = (((1,), (1,)), ((), ()))
TN_DIMS = (((0,), (0,)), ((), ()))


def _params(*sem):
    return pltpu.CompilerParams(dimension_semantics=sem, vmem_limit_bytes=VMEM_LIMIT)


def _sigmoid(x):
    return 1.0 / (1.0 + jnp.exp(-x))


def _log_sigmoid(x):
    return jnp.minimum(x, 0.0) - jnp.log1p(jnp.exp(-jnp.abs(x)))


def _split3(x):
    hi = x.astype(BF16)
    r1 = x - hi.astype(F32)
    mid = r1.astype(BF16)
    lo = (r1 - mid.astype(F32)).astype(BF16)
    return hi, mid, lo


def _pick_tile(n, candidates):
    for c in candidates:
        if n % c == 0:
            return c
    raise ValueError(f"no tile in {candidates} divides {n}")


def _rmsnorm_kernel(x_ref, g_ref, o_ref):
    x = x_ref[...]
    ms = jnp.mean(x * x, axis=-1, keepdims=True)
    o_ref[...] = (x * lax.rsqrt(ms + EPS) * g_ref[...]).astype(o_ref.dtype)


def _rmsnorm_rows(x2, g):
    m, d = x2.shape
    tm = _pick_tile(m, (256, 128, 64, 8))
    return pl.pallas_call(
        _rmsnorm_kernel,
        grid=(m // tm,),
        in_specs=[pl.BlockSpec((tm, d), lambda i: (i, 0)), pl.BlockSpec((1, d), lambda i: (0, 0))],
        out_specs=pl.BlockSpec((tm, d), lambda i: (i, 0)),
        out_shape=jax.ShapeDtypeStruct((m, d), BF16),
        compiler_params=_params("parallel"),
        name="rmsnorm",
    )(x2, g.reshape(1, d).astype(F32))


def _mm_kernel(*refs, nk, act, has_res):
    if has_res:
        a_ref, b_ref, r_ref, o_ref, acc_ref = refs
    else:
        a_ref, b_ref, o_ref, acc_ref = refs
    k = pl.program_id(2)

    @pl.when(k == 0)
    def _():
        acc_ref[...] = jnp.zeros_like(acc_ref)

    acc_ref[...] += jnp.dot(a_ref[...], b_ref[...], preferred_element_type=F32)

    @pl.when(k == nk - 1)
    def _():
        r = acc_ref[...]
        if act == "relu2":
            r = jnp.square(jnp.maximum(r, 0.0))
        if has_res:
            r = r + r_ref[...]
        o_ref[...] = r.astype(o_ref.dtype)


def _matmul(a, b, *, out_dtype, act=None, res=None, name="matmul"):
    m, kdim = a.shape
    _, n = b.shape
    tm = _pick_tile(m, (1024, 512, 256, 128, 8))
    tn = _pick_tile(n, (1024, 512, 256, 128))
    tk = _pick_tile(kdim, (512, 256, 128))
    nk = kdim // tk
    in_specs = [pl.BlockSpec((tm, tk), lambda i, j, k: (i, k)),
                pl.BlockSpec((tk, tn), lambda i, j, k: (k, j))]
    args = [a, b]
    if res is not None:
        in_specs.append(pl.BlockSpec((tm, tn), lambda i, j, k: (i, j)))
        args.append(res)
    return pl.pallas_call(
        functools.partial(_mm_kernel, nk=nk, act=act, has_res=res is not None),
        grid=(m // tm, n // tn, nk),
        in_specs=in_specs,
        out_specs=pl.BlockSpec((tm, tn), lambda i, j, k: (i, j)),
        out_shape=jax.ShapeDtypeStruct((m, n), out_dtype),
        scratch_shapes=[pltpu.VMEM((tm, tn), F32)],
        compiler_params=_params("parallel", "parallel", "arbitrary"),
        name=name,
    )(*args)


_HGRN_LEVELS = 6


def _hgrn_constants():
    idx = np.arange(CHUNK)
    j = idx[None, :]
    r = idx[:, None]
    blocks = [j <= r, j > r]
    masks = []
    for lvl in range(_HGRN_LEVELS):
        half = (CHUNK // 2) >> lvl
        parent = idx // (2 * half)
        mid = parent * 2 * half + half
        is_right = (idx % (2 * half)) >= half
        right_rng = (j >= mid[:, None]) & (j <= r)
        left_rng = (j > r) & (j < mid[:, None])
        blocks.append(np.where(is_right[:, None], right_rng, left_rng))
        masks.append((parent[:, None] == parent[None, :]) & is_right[:, None] & (~is_right)[None, :])
    cm = np.concatenate(blocks, axis=0).astype(np.float32)
    cm3 = np.concatenate([cm, cm, cm, np.zeros_like(cm)], axis=1)
    mk = np.stack(masks, axis=0).astype(np.float32)
    return jnp.asarray(cm3, dtype=BF16), jnp.asarray(mk, dtype=F32)


def _hgrn_kernel(q_ref, f_ref, i_ref, g_ref, lb_ref, gn_ref, cm_ref, mk_ref, o_ref, st_ref, *, nchunk, scale):
    @pl.when(pl.program_id(2) == 0)
    def _():
        st_ref[...] = jnp.zeros_like(st_ref)

    lb = lb_ref[...]
    gn = gn_ref[...]

    def body(c, carry):
        rows = pl.ds(pl.multiple_of(c * CHUNK, CHUNK), CHUNK)
        f = lb + (1.0 - lb) * _sigmoid(f_ref[rows, :])
        logf = jnp.log(jnp.maximum(f, F_FLOOR))
        kk = 1.0 - f
        qr = q_ref[rows, :]
        qq = qr * _sigmoid(qr) * scale
        v = i_ref[rows, :]
        vb = v.astype(BF16)
        l3 = jnp.concatenate(_split3(logf) + (jnp.zeros_like(vb),), axis=0)
        sums = jnp.dot(cm_ref[...], l3, preferred_element_type=F32)
        b = sums[0:CHUNK]
        b_rest = sums[CHUNK:2 * CHUNK]
        scores = jnp.zeros((CHUNK, CHUNK), F32)
        for lvl in range(_HGRN_LEVELS):
            e = jnp.exp(sums[(2 + lvl) * CHUNK:(3 + lvl) * CHUNK])
            s = lax.dot_general((qq * e).astype(BF16), (kk * e).astype(BF16), NT_DIMS,
                                preferred_element_type=F32)
            scores = scores + s * mk_ref[lvl]
        diag = jnp.sum(qq * kk, axis=1, keepdims=True)
        st = st_ref[...]
        o = (jnp.dot(scores.astype(BF16), vb, preferred_element_type=F32) + diag * v
             + lax.dot_general((qq * jnp.exp(b)).astype(BF16), st.astype(BF16), NT_DIMS,
                               preferred_element_type=F32))
        ke = (kk * jnp.exp(b_rest)).astype(BF16)
        st_ref[...] = (st * jnp.exp(b[CHUNK - 1:CHUNK, :])
                       + lax.dot_general(vb, ke, TN_DIMS, preferred_element_type=F32))
        ms = jnp.mean(o * o, axis=1, keepdims=True)
        gr = g_ref[rows, :]
        y = o * lax.rsqrt(ms + EPS) * gn * (gr * _sigmoid(gr))
        o_ref[rows, :] = y.astype(o_ref.dtype)
        return carry

    lax.fori_loop(0, nchunk, body, 0)


def _hgrn(proj, lb, gnorm, *, batch, seq, group):
    m = batch * seq
    nh = group // HEAD_DIM
    t = _pick_tile(seq, (512, 256, 128, 64))
    nt = seq // t
    cm3, mk = _hgrn_constants()

    def col(off):
        return pl.BlockSpec((t, HEAD_DIM), lambda b, h, i: (b * nt + i, off + h))

    return pl.pallas_call(
        functools.partial(_hgrn_kernel, nchunk=t // CHUNK, scale=HEAD_DIM ** -0.5),
        grid=(batch, nh, nt),
        in_specs=[col(0), col(nh), col(2 * nh), col(3 * nh),
                  pl.BlockSpec((1, HEAD_DIM), lambda b, h, i: (0, h)),
                  pl.BlockSpec((1, HEAD_DIM), lambda b, h, i: (0, 0)),
                  pl.BlockSpec(cm3.shape, lambda b, h, i: (0, 0)),
                  pl.BlockSpec(mk.shape, lambda b, h, i: (0, 0, 0))],
        out_specs=pl.BlockSpec((t, HEAD_DIM), lambda b, h, i: (b * nt + i, h)),
        out_shape=jax.ShapeDtypeStruct((m, group), BF16),
        scratch_shapes=[pltpu.VMEM((HEAD_DIM, HEAD_DIM), F32)],
        compiler_params=_params("parallel", "parallel", "arbitrary"),
        name="hgrn2",
    )(proj, proj, proj, proj, lb.reshape(1, group).astype(F32), gnorm.reshape(1, HEAD_DIM).astype(F32), cm3, mk)


_CONV_PAD = 8


def _mlstm_constants():
    idx = np.arange(CHUNK)
    tri = (idx[None, :] <= idx[:, None]).astype(np.float32)
    zero = np.zeros_like(tri)
    tri3 = np.concatenate([tri, tri, tri, zero], axis=1)
    trit3 = np.concatenate([tri.T, tri.T, tri.T, zero], axis=0)
    trit3 = np.concatenate([trit3, np.zeros_like(trit3)], axis=1)
    return jnp.asarray(tri3, dtype=BF16), jnp.asarray(trit3, dtype=BF16)


def _mlstm_kernel(qk_ref, v_ref, og_ref, gc_ref, gr_ref, cw_ref, bc_ref, br_ref, hn_ref, tri_ref, trit_ref,
                  o_ref, xp_ref, qk_s, c_ref, m_ref, *, nchunk, nh, dqk, dv, t, g_off):
    @pl.when(pl.program_id(1) == 0)
    def _():
        xp_ref[0:_CONV_PAD, :] = jnp.zeros((_CONV_PAD, xp_ref.shape[1]), F32)
        c_ref[...] = jnp.zeros_like(c_ref)
        m_ref[...] = jnp.zeros_like(m_ref)

    xp_ref[_CONV_PAD:_CONV_PAD + t, :] = qk_ref[...]
    conv = jnp.zeros((t, 2 * nh * dqk), F32)
    for j in range(CONV_W):
        start = _CONV_PAD - (CONV_W - 1) + j
        conv = conv + cw_ref[j:j + 1, :] * xp_ref[start:start + t, :]
    qk_s[...] = conv * _sigmoid(conv)
    xp_ref[0:_CONV_PAD, :] = xp_ref[t:t + _CONV_PAD, :]

    hn = hn_ref[...]
    kscale = dqk ** -0.5
    r_i = lax.broadcasted_iota(jnp.int32, (CHUNK, CHUNK), 0)
    c_i = lax.broadcasted_iota(jnp.int32, (CHUNK, CHUNK), 1)
    causal = c_i <= r_i
    ones = jnp.ones((CHUNK, dv), BF16)
    zpad_c = jnp.zeros((CHUNK, LANE), BF16)
    zpad_r = jnp.zeros((2 * nh, CHUNK), BF16)

    def body(c, carry):
        base = pl.multiple_of(c * CHUNK, CHUNK)
        rows = pl.ds(base, CHUNK)
        qk = qk_s[rows, :]
        gcol = gc_ref[rows, :] + bc_ref[...]
        b_c = jnp.dot(tri_ref[...], jnp.concatenate(_split3(_log_sigmoid(gcol)) + (zpad_c,), axis=0),
                      preferred_element_type=F32)
        grow = gr_ref[c] + br_ref[...]
        ig_r = grow[0:nh, :]
        b_r = jnp.dot(jnp.concatenate(_split3(_log_sigmoid(grow)) + (zpad_r,), axis=1), trit_ref[...],
                      preferred_element_type=F32)[nh:2 * nh, 0:CHUNK]
        for h in range(nh):
            ig_col = gcol[:, g_off + h:g_off + h + 1]
            q_h = qk[:, h * dqk:(h + 1) * dqk].astype(BF16)
            k_f = qk[:, (nh + h) * dqk:(nh + h + 1) * dqk] * kscale
            k_h = k_f.astype(BF16)
            v_ext = jnp.concatenate([v_ref[rows, h * dv:(h + 1) * dv].astype(BF16), ones], axis=1)
            bcol = b_c[:, g_off + nh + h:g_off + nh + h + 1]
            m_prev = m_ref[h:h + 1, 0:1]
            log_d = jnp.where(causal, bcol - b_r[h:h + 1, :] + ig_r[h:h + 1, :], NEG)
            log_inter = bcol + m_prev
            m_t = jnp.maximum(jnp.max(log_d, axis=1, keepdims=True), log_inter)
            qk_w = (lax.dot_general(q_h, k_h, NT_DIMS, preferred_element_type=F32)
                    * jnp.exp(log_d - m_t))
            w_inter = jnp.exp(log_inter - m_t)
            c_ext = c_ref[h]
            numden = (jnp.dot(qk_w.astype(BF16), v_ext, preferred_element_type=F32)
                      + w_inter * jnp.dot(q_h, c_ext.astype(BF16), preferred_element_type=F32))
            num = numden[:, 0:dv]
            den = numden[:, dv:2 * dv]
            hh = num / jnp.maximum(jnp.abs(den), jnp.exp(-m_t))
            m_new = m_t[CHUNK - 1:CHUNK, :]
            b_last = bcol[CHUNK - 1:CHUNK, :]
            w_state = jnp.exp(b_last - bcol + ig_col - m_new)
            decay = jnp.exp(b_last + m_prev - m_new)
            c_ref[h] = decay * c_ext + lax.dot_general((k_f * w_state).astype(BF16), v_ext, TN_DIMS,
                                                       preferred_element_type=F32)
            m_ref[h:h + 1, :] = jnp.broadcast_to(m_new, (1, LANE))
            ms = jnp.mean(hh * hh, axis=1, keepdims=True)
            gate = _sigmoid(og_ref[rows, h * dv:(h + 1) * dv])
            o_ref[rows, h * dv:(h + 1) * dv] = (hh * lax.rsqrt(ms + EPS) * hn * gate).astype(o_ref.dtype)
        return carry

    lax.fori_loop(0, nchunk, body, 0)


def _mlstm(proj, tail, conv_w, gate_bias, hnorm, *, batch, seq, group, g_off):
    m = batch * seq
    nh = group // HEAD_DIM
    dqk = HEAD_DIM // 2
    dv = HEAD_DIM
    t = _pick_tile(seq, (256, 128, 64))
    nt = seq // t
    tri3, trit3 = _mlstm_constants()
    qk_blk = 4
    gates_row = (tail[:, g_off:g_off + 2 * nh].reshape(batch, seq // CHUNK, CHUNK, 2 * nh)
                 .transpose(0, 1, 3, 2))
    bias = gate_bias.astype(F32)
    bias_lane = jnp.zeros((1, LANE), F32).at[0, g_off:g_off + 2 * nh].set(bias)

    def wide(off):
        return pl.BlockSpec((t, group), lambda b, i: (b * nt + i, off))

    def const(shape):
        return pl.BlockSpec(shape, lambda b, i: tuple(0 for _ in shape))

    return pl.pallas_call(
        functools.partial(_mlstm_kernel, nchunk=t // CHUNK, nh=nh, dqk=dqk, dv=dv, t=t, g_off=g_off),
        grid=(batch, nt),
        in_specs=[wide(qk_blk), wide(qk_blk + 1), wide(qk_blk + 2),
                  pl.BlockSpec((t, LANE), lambda b, i: (b * nt + i, 0)),
                  pl.BlockSpec((None, t // CHUNK, 2 * nh, CHUNK), lambda b, i: (b, i, 0, 0)),
                  const((CONV_W, group)), const((1, LANE)), const((2 * nh, 1)), const((1, dv)),
                  const(tri3.shape), const(trit3.shape)],
        out_specs=pl.BlockSpec((t, group), lambda b, i: (b * nt + i, 0)),
        out_shape=jax.ShapeDtypeStruct((m, group), BF16),
        scratch_shapes=[pltpu.VMEM((t + _CONV_PAD, group), F32),
                        pltpu.VMEM((t, group), F32),
                        pltpu.VMEM((nh, dqk, 2 * dv), F32),
                        pltpu.VMEM((nh, LANE), F32)],
        compiler_params=_params("parallel", "arbitrary"),
        name="mlstm",
    )(proj, proj, proj, tail, gates_row, conv_w.astype(F32), bias_lane, bias.reshape(2 * nh, 1),
      hnorm.reshape(1, dv).astype(F32), tri3, trit3)


def _prep_kernel(x_ref, g_ref, u_ref, o_ref):
    x = x_ref[...]
    ms = jnp.mean(x * x, axis=-1, keepdims=True)
    inv = jnp.where(u_ref[...] > 0.0, lax.rsqrt(ms + EPS), 1.0)
    o_ref[...] = (x * inv * g_ref[...]).astype(o_ref.dtype)


def _prep_cd(proj, gains, use_norm, *, col_off):
    m = proj.shape[0]
    ncol = gains.shape[1] // LANE
    tm = _pick_tile(m, (512, 256, 128, 64))
    return pl.pallas_call(
        _prep_kernel,
        grid=(m // tm, ncol),
        in_specs=[pl.BlockSpec((tm, LANE), lambda i, j: (i, col_off + j)),
                  pl.BlockSpec((1, LANE), lambda i, j: (0, j)),
                  pl.BlockSpec((1, LANE), lambda i, j: (0, j))],
        out_specs=pl.BlockSpec((tm, LANE), lambda i, j: (i, j)),
        out_shape=jax.ShapeDtypeStruct((m, ncol * LANE), BF16),
        compiler_params=_params("parallel", "parallel"),
        name="prep_cd",
    )(proj, gains, use_norm)


def _ki_kernel(x_ref, g_ref, o_ref):
    x = x_ref[:, 0:IDX_DIM]
    ms = jnp.mean(x * x, axis=-1, keepdims=True)
    o_ref[...] = (x * lax.rsqrt(ms + EPS) * g_ref[...]).astype(o_ref.dtype)


def _ki_norm(tail, g):
    m = tail.shape[0]
    tm = _pick_tile(m, (512, 256, 128, 64))
    return pl.pallas_call(
        _ki_kernel,
        grid=(m // tm,),
        in_specs=[pl.BlockSpec((tm, LANE), lambda i: (i, 0)), pl.BlockSpec((1, IDX_DIM), lambda i: (0, 0))],
        out_specs=pl.BlockSpec((tm, IDX_DIM), lambda i: (i, 0)),
        out_shape=jax.ShapeDtypeStruct((m, IDX_DIM), BF16),
        compiler_params=_params("parallel"),
        name="ki_norm",
    )(tail, g.reshape(1, IDX_DIM).astype(F32))


def _diff_kernel(q_ref, k_ref, v_ref, lam_ref, sn_ref, o_ref, m_ref, l_ref, acc_ref, *, tq, tk, dqk, out_scale):
    qi = pl.program_id(2)
    ki = pl.program_id(3)
    last_k = ((qi + 1) * tq - 1) // tk

    @pl.when(ki == 0)
    def _():
        m_ref[...] = jnp.full_like(m_ref, NEG)
        l_ref[...] = jnp.zeros_like(l_ref)
        acc_ref[...] = jnp.zeros_like(acc_ref)

    @pl.when(ki <= last_k)
    def _():
        q = q_ref[...]
        k = k_ref[...]
        v = v_ref[...]
        q_chunk = (qi * tq + lax.broadcasted_iota(jnp.int32, (tq, tk), 0)) // CHUNK
        k_chunk = (ki * tk + lax.broadcasted_iota(jnp.int32, (tq, tk), 1)) // CHUNK
        mask = k_chunk <= q_chunk
        for half in range(2):
            s = lax.dot_general(q[:, half * dqk:(half + 1) * dqk], k[:, half * dqk:(half + 1) * dqk], NT_DIMS,
                                preferred_element_type=F32)
            s = jnp.where(mask, s, NEG)
            m_prev = m_ref[half]
            m_new = jnp.maximum(m_prev, jnp.max(s, axis=1, keepdims=True))
            p = jnp.exp(s - m_new)
            alpha = jnp.exp(m_prev - m_new)
            l_ref[half] = alpha * l_ref[half] + jnp.sum(p, axis=1, keepdims=True)
            acc_ref[half] = alpha * acc_ref[half] + jnp.dot(p.astype(BF16), v, preferred_element_type=F32)
            m_ref[half] = m_new

    @pl.when(ki == pl.num_programs(3) - 1)
    def _():
        o = acc_ref[0] / l_ref[0] - lam_ref[...] * (acc_ref[1] / l_ref[1])
        ms = jnp.mean(o * o, axis=1, keepdims=True)
        o_ref[...] = (o * lax.rsqrt(ms + EPS) * sn_ref[...] * out_scale).astype(o_ref.dtype)


def _diff_attention(cd, lam_full, subln, lambda_init, *, batch, seq, group):
    m = batch * seq
    dqk = HEAD_DIM
    dv = 2 * HEAD_DIM
    nh = group // dv
    tq = _pick_tile(seq, (256, 128))
    tk = _pick_tile(seq, (512, 256, 128))
    nq, nkb = seq // tq, seq // tk
    k_off, v_off = nh, 2 * nh

    def kv_map(off):
        def f(b, h, i, j):
            last = ((i + 1) * tq - 1) // tk
            return (b * nkb + jnp.minimum(j, last), off + h)
        return f

    return pl.pallas_call(
        functools.partial(_diff_kernel, tq=tq, tk=tk, dqk=dqk, out_scale=1.0 - lambda_init),
        grid=(batch, nh, nq, nkb),
        in_specs=[pl.BlockSpec((tq, dv), lambda b, h, i, j: (b * nq + i, h)),
                  pl.BlockSpec((tk, dv), kv_map(k_off)),
                  pl.BlockSpec((tk, dv), kv_map(v_off)),
                  pl.BlockSpec((1, 1), lambda b, h, i, j: (0, 0)),
                  pl.BlockSpec((1, dv), lambda b, h, i, j: (0, 0))],
        out_specs=pl.BlockSpec((tq, dv), lambda b, h, i, j: (b * nq + i, h)),
        out_shape=jax.ShapeDtypeStruct((m, group), BF16),
        scratch_shapes=[pltpu.VMEM((2, tq, 1), F32), pltpu.VMEM((2, tq, 1), F32), pltpu.VMEM((2, tq, dv), F32)],
        compiler_params=_params("parallel", "parallel", "parallel", "arbitrary"),
        name="diff_attn",
    )(cd, cd, cd, lam_full.reshape(1, 1).astype(F32), subln.reshape(1, dv).astype(F32))


_INT_MIN = -2 ** 31


def _dsa_kernel(q_ref, k_ref, v_ref, qi_ref, ki_ref, tail_ref, o_ref, sc_ref, *, tq, tk, topk, w_off, w_scale):
    qb = pl.program_id(1)
    h = pl.program_id(2)
    nkv = ((qb + 1) * tq + tk - 1) // tk

    @pl.when(h == 0)
    def _():
        w = tail_ref[:, w_off:w_off + IDX_HEADS] * w_scale
        q_chunk = (qb * tq + lax.broadcasted_iota(jnp.int32, (tq, tk), 0)) // CHUNK

        def score_body(kb, carry):
            cols = pl.ds(pl.multiple_of(kb * tk, tk), tk)
            ki = ki_ref[cols, :]
            acc = jnp.zeros((tq, tk), F32)
            for ih in range(IDX_HEADS):
                lg = lax.dot_general(qi_ref[:, ih * IDX_DIM:(ih + 1) * IDX_DIM], ki, NT_DIMS,
                                     preferred_element_type=F32)
                acc = acc + w[:, ih:ih + 1] * jnp.maximum(lg, 0.0)
            k_chunk = (kb * tk + lax.broadcasted_iota(jnp.int32, (tq, tk), 1)) // CHUNK
            sc_ref[kb] = jnp.where(k_chunk <= q_chunk, acc, NEG)
            return carry

        lax.fori_loop(0, nkv, score_body, 0)

        def bit_body(i, u):
            cand = u | jnp.left_shift(jnp.int32(1), 31 - i)
            key = cand ^ _INT_MIN
            bits = key ^ (jnp.right_shift(key, 31) & 0x7FFFFFFF)
            cf = lax.bitcast_convert_type(bits, F32)

            def cnt_body(kb, cnt):
                return cnt + jnp.sum(jnp.where(sc_ref[kb] >= cf, 1.0, 0.0), axis=1, keepdims=True)

            cnt = lax.fori_loop(0, nkv, cnt_body, jnp.zeros((tq, 1), F32))
            return jnp.where(cnt >= float(topk), cand, u)

        u = lax.fori_loop(0, 32, bit_body, jnp.zeros((tq, 1), jnp.int32))
        key = u ^ _INT_MIN
        thr = lax.bitcast_convert_type(key ^ (jnp.right_shift(key, 31) & 0x7FFFFFFF), F32)

        def bias_body(kb, carry):
            s = sc_ref[kb]
            sc_ref[kb] = jnp.where((s >= thr) & (s > 0.5 * NEG), 0.0, NEG)
            return carry

        lax.fori_loop(0, nkv, bias_body, 0)

    q = q_ref[...]

    def kv_body(kb, carry):
        m_prev, l_prev, acc = carry
        cols = pl.ds(pl.multiple_of(kb * tk, tk), tk)
        s = lax.dot_general(q, k_ref[cols, :], NT_DIMS, preferred_element_type=F32) + sc_ref[kb]
        m_new = jnp.maximum(m_prev, jnp.max(s, axis=1, keepdims=True))
        p = jnp.exp(s - m_new)
        alpha = jnp.exp(m_prev - m_new)
        l_new = alpha * l_prev + jnp.sum(p, axis=1, keepdims=True)
        acc = alpha * acc + jnp.dot(p.astype(BF16), v_ref[cols, :], preferred_element_type=F32)
        return m_new, l_new, acc

    init = (jnp.full((tq, 1), NEG, F32), jnp.zeros((tq, 1), F32), jnp.zeros((tq, HEAD_DIM), F32))
    _, l_fin, acc = lax.fori_loop(0, nkv, kv_body, init)
    o_ref[...] = (acc / l_fin).astype(o_ref.dtype)


def _dsa_attention(cd, ki_n, tail, *, batch, seq, group, w_off):
    m = batch * seq
    nh = group // HEAD_DIM
    tq = _pick_tile(seq, (256, 128))
    tk = _pick_tile(seq, (512, 256))
    topk = min(TOPK_MAX, seq // 4)
    assert tk >= topk, "the threshold search needs at least topk keys in range"
    nq = seq // tq
    nc = group // LANE
    q_off, k_off, v_off = 3 * nc, 4 * nc, 5 * nc
    qi_off = 6 * nc * LANE // (IDX_HEADS * IDX_DIM)
    assert qi_off * IDX_HEADS * IDX_DIM == 6 * nc * LANE

    return pl.pallas_call(
        functools.partial(_dsa_kernel, tq=tq, tk=tk, topk=topk, w_off=w_off,
                          w_scale=IDX_HEADS ** -0.5 * IDX_DIM ** -0.5),
        grid=(batch, nq, nh),
        in_specs=[pl.BlockSpec((tq, HEAD_DIM), lambda b, i, h: (b * nq + i, q_off + h)),
                  pl.BlockSpec((seq, HEAD_DIM), lambda b, i, h: (b, k_off + h)),
                  pl.BlockSpec((seq, HEAD_DIM), lambda b, i, h: (b, v_off + h)),
                  pl.BlockSpec((tq, IDX_HEADS * IDX_DIM), lambda b, i, h: (b * nq + i, qi_off)),
                  pl.BlockSpec((seq, IDX_DIM), lambda b, i, h: (b, 0)),
                  pl.BlockSpec((tq, LANE), lambda b, i, h: (b * nq + i, 0))],
        out_specs=pl.BlockSpec((tq, HEAD_DIM), lambda b, i, h: (b * nq + i, h)),
        out_shape=jax.ShapeDtypeStruct((m, group), BF16),
        scratch_shapes=[pltpu.VMEM((seq // tk, tq, tk), F32)],
        compiler_params=_params("parallel", "arbitrary", "arbitrary"),
        name="dsa",
    )(cd, cd, cd, cd, ki_n, tail)


def _rearrange_w_in(w, group, nhb):
    d = w.shape[0]
    p0 = 7 * group
    p1 = p0 + 2 * nhb
    p2 = p1 + 6 * group + IDX_HEADS * IDX_DIM
    main = jnp.concatenate([w[:, :p0], w[:, p1:p2]], axis=1)
    used = IDX_DIM + 2 * nhb + IDX_HEADS
    tail = jnp.concatenate([w[:, p2:p2 + IDX_DIM], w[:, p0:p1], w[:, p2 + IDX_DIM:],
                            jnp.zeros((d, LANE - used), w.dtype)], axis=1)
    return main.astype(BF16), tail.astype(BF16)


def kernel(x, norm_mix, w_in, hgrn_lb_logits, hgrn_gnorm, mlstm_conv, mlstm_gate_bias, mlstm_hnorm,
           diff_qk_norm, diff_lambda, diff_subln, dsa_qk_norm, dsa_idx_knorm, w_out, norm_mlp, w1, w2):
    batch, seq, d_model = x.shape
    depth = w_in.shape[0]
    group = d_model // 4
    nhb = group // HEAD_DIM
    nc = group // LANE
    m = batch * seq
    dims = dict(batch=batch, seq=seq, group=group)

    p_lb = jax.nn.softmax(hgrn_lb_logits.astype(F32), axis=0)
    lower_bounds = jnp.cumsum(p_lb, axis=0) - p_lb[0:1]

    x2 = x.reshape(m, d_model).astype(F32)
    ones_g = jnp.ones((group,), F32)
    for l in range(depth):
        lambda_init = 0.8 - 0.6 * math.exp(-0.3 * l)
        w_main, w_tail = _rearrange_w_in(w_in[l], group, nhb)
        h = _rmsnorm_rows(x2, norm_mix[l])
        proj = _matmul(h, w_main, out_dtype=F32, name="in_proj")
        tail = _matmul(h, w_tail, out_dtype=F32, name="in_proj_tail")

        y_a = _hgrn(proj, lower_bounds[l], hgrn_gnorm[l], **dims)

        y_b = _mlstm(proj, tail, mlstm_conv[l], mlstm_gate_bias[l], mlstm_hnorm[l], g_off=IDX_DIM, **dims)

        cqn = diff_qk_norm[l].astype(F32)
        dqn = dsa_qk_norm[l].astype(F32)
        gains = jnp.concatenate([
            jnp.tile(cqn[0], nc) * HEAD_DIM ** -0.5, jnp.tile(cqn[1], nc), ones_g,
            jnp.tile(dqn[0], nc) * HEAD_DIM ** -0.5, jnp.tile(dqn[1], nc), ones_g,
            jnp.ones((IDX_HEADS * IDX_DIM,), F32)]).reshape(1, -1)
        use_norm = jnp.concatenate([
            jnp.ones((2 * group,), F32), jnp.zeros((group,), F32),
            jnp.ones((2 * group,), F32), jnp.zeros((group + IDX_HEADS * IDX_DIM,), F32)]).reshape(1, -1)
        cd = _prep_cd(proj, gains, use_norm, col_off=7 * nc)
        ki_n = _ki_norm(tail, dsa_idx_knorm[l])

        lam = diff_lambda[l].astype(F32)
        lam_full = jnp.exp(jnp.sum(lam[0] * lam[1])) - jnp.exp(jnp.sum(lam[2] * lam[3])) + lambda_init
        y_c = _diff_attention(cd, lam_full, diff_subln[l], lambda_init, **dims)
        y_d = _dsa_attention(cd, ki_n, tail, w_off=IDX_DIM + 2 * nhb, **dims)

        y = jnp.concatenate([y_a, y_b, y_c, y_d], axis=1)
        x2 = _matmul(y, w_out[l].astype(BF16), out_dtype=F32, res=x2, name="out_proj")
        h = _rmsnorm_rows(x2, norm_mlp[l])
        hidden = _matmul(h, w1[l].astype(BF16), out_dtype=BF16, act="relu2", name="mlp_up")
        x2 = _matmul(hidden, w2[l].astype(BF16), out_dtype=F32, res=x2, name="mlp_down")
    return x2.reshape(batch, seq, d_model).astype(x.dtype)
```

```python
import functools
import math

import numpy as np
import jax
import jax.numpy as jnp
from jax import lax
from jax.experimental import pallas as pl
from jax.experimental.pallas import tpu as pltpu

CHUNK = 64
EPS = 1e-6
NEG = -1e30
F_FLOOR = 1e-20
HEAD_DIM = 128
CONV_W = 4
IDX_HEADS = 16
IDX_DIM = 64
TOPK_MAX = 256
LANE = 128
VMEM_LIMIT = 56 * 1024 * 1024

F32 = jnp.float32
BF16 = jnp.bfloat16
NT_DIMS = (((1,), (1,)), ((), ()))
TN_DIMS = (((0,), (0,)), ((), ()))


def _params(*sem):
    return pltpu.CompilerParams(dimension_semantics=sem, vmem_limit_bytes=VMEM_LIMIT)


def _sigmoid(x):
    return 1.0 / (1.0 + jnp.exp(-x))


def _log_sigmoid(x):
    return jnp.minimum(x, 0.0) - jnp.log1p(jnp.exp(-jnp.abs(x)))


def _split3(x):
    hi = x.astype(BF16)
    r1 = x - hi.astype(F32)
    mid = r1.astype(BF16)
    lo = (r1 - mid.astype(F32)).astype(BF16)
    return hi, mid, lo


def _pick_tile(n, candidates):
    for c in candidates:
        if n % c == 0:
            return c
    raise ValueError(f"no tile in {candidates} divides {n}")


def _rmsnorm_kernel(x_ref, g_ref, o_ref):
    x = x_ref[...]
    ms = jnp.mean(x * x, axis=-1, keepdims=True)
    o_ref[...] = (x * lax.rsqrt(ms + EPS) * g_ref[...]).astype(o_ref.dtype)


def _rmsnorm_rows(x2, g):
    m, d = x2.shape
    tm = _pick_tile(m, (256, 128, 64, 8))
    return pl.pallas_call(
        _rmsnorm_kernel,
        grid=(m // tm,),
        in_specs=[pl.BlockSpec((tm, d), lambda i: (i, 0)), pl.BlockSpec((1, d), lambda i: (0, 0))],
        out_specs=pl.BlockSpec((tm, d), lambda i: (i, 0)),
        out_shape=jax.ShapeDtypeStruct((m, d), BF16),
        compiler_params=_params("parallel"),
        name="rmsnorm",
    )(x2, g.reshape(1, d).astype(F32))


MM_MAX_TK = 4096


def _mm_epilogue(r, act, extra):
    if act == "relu2":
        r = jnp.square(jnp.maximum(r, 0.0))
    elif act == "groupnorm":
        g_ref, u_ref = extra
        segs = []
        for c in range(r.shape[1] // LANE):
            seg = r[:, c * LANE:(c + 1) * LANE]
            ms = jnp.mean(seg * seg, axis=1, keepdims=True)
            inv = jnp.where(u_ref[:, c * LANE:(c + 1) * LANE] > 0.0, lax.rsqrt(ms + EPS), 1.0)
            segs.append(seg * inv * g_ref[:, c * LANE:(c + 1) * LANE])
        r = jnp.concatenate(segs, axis=1)
    elif act == "residual":
        r = r + extra[0][...]
    return r


def _mm_kernel(a_ref, b_ref, *rest, nk, act, n_extra):
    extra = rest[:n_extra]
    o_ref = rest[n_extra]
    if nk == 1:
        r = jnp.dot(a_ref[...], b_ref[...], preferred_element_type=F32)
        o_ref[...] = _mm_epilogue(r, act, extra).astype(o_ref.dtype)
        return
    acc_ref = rest[n_extra + 1]
    k = pl.program_id(2)

    @pl.when(k == 0)
    def _():
        acc_ref[...] = jnp.zeros_like(acc_ref)

    acc_ref[...] += jnp.dot(a_ref[...], b_ref[...], preferred_element_type=F32)

    @pl.when(k == nk - 1)
    def _():
        o_ref[...] = _mm_epilogue(acc_ref[...], act, extra).astype(o_ref.dtype)


def _matmul(a, b, *, out_dtype, act=None, extra=(), tm_max=1024, tk_max=MM_MAX_TK, name="matmul"):
    m, kdim = a.shape
    _, n = b.shape
    tm = _pick_tile(m, tuple(t for t in (1024, 512, 256, 128, 8) if t <= tm_max))
    tn = _pick_tile(n, (1024, 512, 256, 128))
    tk = kdim if kdim <= tk_max else _pick_tile(kdim, tuple(t for t in (2048, 1024, 512, 256, 128) if t <= tk_max))
    nk = kdim // tk
    in_specs = [pl.BlockSpec((tm, tk), lambda i, j, k: (i, k)),
                pl.BlockSpec((tk, tn), lambda i, j, k: (k, j))]
    for e in extra:
        if e.shape[0] == 1:
            in_specs.append(pl.BlockSpec((1, tn), lambda i, j, k: (0, j)))
        else:
            in_specs.append(pl.BlockSpec((tm, tn), lambda i, j, k: (i, j)))
    return pl.pallas_call(
        functools.partial(_mm_kernel, nk=nk, act=act, n_extra=len(extra)),
        grid=(m // tm, n // tn, nk),
        in_specs=in_specs,
        out_specs=pl.BlockSpec((tm, tn), lambda i, j, k: (i, j)),
        out_shape=jax.ShapeDtypeStruct((m, n), out_dtype),
        scratch_shapes=[pltpu.VMEM((tm, tn), F32)] if nk > 1 else [],
        compiler_params=_params("parallel", "parallel", "arbitrary"),
        name=name,
    )(a, b, *extra)


_HGRN_LEVELS = 6


def _hgrn_constants():
    idx = np.arange(CHUNK)
    j = idx[None, :]
    r = idx[:, None]
    blocks = [j <= r, j > r]
    masks = []
    for lvl in range(_HGRN_LEVELS):
        half = (CHUNK // 2) >> lvl
        parent = idx // (2 * half)
        mid = parent * 2 * half + half
        is_right = (idx % (2 * half)) >= half
        right_rng = (j >= mid[:, None]) & (j <= r)
        left_rng = (j > r) & (j < mid[:, None])
        blocks.append(np.where(is_right[:, None], right_rng, left_rng))
        masks.append((parent[:, None] == parent[None, :]) & is_right[:, None] & (~is_right)[None, :])
    cm = np.concatenate(blocks, axis=0).astype(np.float32)
    cm3 = np.concatenate([cm, cm, cm, np.zeros_like(cm)], axis=1)
    mk = np.stack(masks, axis=0).astype(np.float32)
    return jnp.asarray(cm3, dtype=BF16), jnp.asarray(mk, dtype=F32)


def _hgrn_kernel(q_ref, f_ref, i_ref, g_ref, lb_ref, gn_ref, cm_ref, mk_ref, o_ref, st_ref, *, nchunk, scale):
    @pl.when(pl.program_id(2) == 0)
    def _():
        st_ref[...] = jnp.zeros_like(st_ref)

    lb = lb_ref[...]
    gn = gn_ref[...]

    def body(c, carry):
        rows = pl.ds(pl.multiple_of(c * CHUNK, CHUNK), CHUNK)
        f = lb + (1.0 - lb) * _sigmoid(f_ref[rows, :])
        logf = jnp.log(jnp.maximum(f, F_FLOOR))
        kk = 1.0 - f
        qr = q_ref[rows, :]
        qq = qr * _sigmoid(qr) * scale
        v = i_ref[rows, :]
        vb = v.astype(BF16)
        l3 = jnp.concatenate(_split3(logf) + (jnp.zeros_like(vb),), axis=0)
        sums = jnp.dot(cm_ref[...], l3, preferred_element_type=F32)
        b = sums[0:CHUNK]
        b_rest = sums[CHUNK:2 * CHUNK]
        scores = jnp.zeros((CHUNK, CHUNK), F32)
        for lvl in range(_HGRN_LEVELS):
            e = jnp.exp(sums[(2 + lvl) * CHUNK:(3 + lvl) * CHUNK])
            s = lax.dot_general((qq * e).astype(BF16), (kk * e).astype(BF16), NT_DIMS,
                                preferred_element_type=F32)
            scores = scores + s * mk_ref[lvl]
        diag = jnp.sum(qq * kk, axis=1, keepdims=True)
        st = st_ref[...]
        o = (jnp.dot(scores.astype(BF16), vb, preferred_element_type=F32) + diag * v
             + lax.dot_general((qq * jnp.exp(b)).astype(BF16), st.astype(BF16), NT_DIMS,
                               preferred_element_type=F32))
        ke = (kk * jnp.exp(b_rest)).astype(BF16)
        st_ref[...] = (st * jnp.exp(b[CHUNK - 1:CHUNK, :])
                       + lax.dot_general(vb, ke, TN_DIMS, preferred_element_type=F32))
        ms = jnp.mean(o * o, axis=1, keepdims=True)
        gr = g_ref[rows, :]
        y = o * lax.rsqrt(ms + EPS) * gn * (gr * _sigmoid(gr))
        o_ref[rows, :] = y.astype(o_ref.dtype)
        return carry

    lax.fori_loop(0, nchunk, body, 0, unroll=2)


def _hgrn(proj, lb, gnorm, *, batch, seq, group):
    m = batch * seq
    nh = group // HEAD_DIM
    t = _pick_tile(seq, (512, 256, 128, 64))
    nt = seq // t
    cm3, mk = _hgrn_constants()

    def col(off):
        return pl.BlockSpec((t, HEAD_DIM), lambda b, h, i: (b * nt + i, off + h))

    return pl.pallas_call(
        functools.partial(_hgrn_kernel, nchunk=t // CHUNK, scale=HEAD_DIM ** -0.5),
        grid=(batch, nh, nt),
        in_specs=[col(0), col(nh), col(2 * nh), col(3 * nh),
                  pl.BlockSpec((1, HEAD_DIM), lambda b, h, i: (0, h)),
                  pl.BlockSpec((1, HEAD_DIM), lambda b, h, i: (0, 0)),
                  pl.BlockSpec(cm3.shape, lambda b, h, i: (0, 0)),
                  pl.BlockSpec(mk.shape, lambda b, h, i: (0, 0, 0))],
        out_specs=pl.BlockSpec((t, HEAD_DIM), lambda b, h, i: (b * nt + i, h)),
        out_shape=jax.ShapeDtypeStruct((m, group), BF16),
        scratch_shapes=[pltpu.VMEM((HEAD_DIM, HEAD_DIM), F32)],
        compiler_params=_params("parallel", "parallel", "arbitrary"),
        name="hgrn2",
    )(proj, proj, proj, proj, lb.reshape(1, group).astype(F32), gnorm.reshape(1, HEAD_DIM).astype(F32), cm3, mk)


_CONV_PAD = 8


def _mlstm_constants():
    idx = np.arange(CHUNK)
    tri = (idx[None, :] <= idx[:, None]).astype(np.float32)
    zero = np.zeros_like(tri)
    tri3 = np.concatenate([tri, tri, tri, zero], axis=1)
    trit3 = np.concatenate([tri.T, tri.T, tri.T, zero], axis=0)
    trit3 = np.concatenate([trit3, np.zeros_like(trit3)], axis=1)
    return jnp.asarray(tri3, dtype=BF16), jnp.asarray(trit3, dtype=BF16)


def _mlstm_kernel(qk_ref, v_ref, og_ref, gc_ref, gr_ref, cw_ref, bc_ref, br_ref, hn_ref, tri_ref, trit_ref,
                  o_ref, xp_ref, qk_s, c_ref, m_ref, *, nchunk, nh, dqk, dv, t, g_off):
    @pl.when(pl.program_id(1) == 0)
    def _():
        xp_ref[0:_CONV_PAD, :] = jnp.zeros((_CONV_PAD, xp_ref.shape[1]), F32)
        c_ref[...] = jnp.zeros_like(c_ref)
        m_ref[...] = jnp.zeros_like(m_ref)

    xp_ref[_CONV_PAD:_CONV_PAD + t, :] = qk_ref[...]
    conv = jnp.zeros((t, 2 * nh * dqk), F32)
    for j in range(CONV_W):
        start = _CONV_PAD - (CONV_W - 1) + j
        conv = conv + cw_ref[j:j + 1, :] * xp_ref[start:start + t, :]
    qk_s[...] = conv * _sigmoid(conv)
    xp_ref[0:_CONV_PAD, :] = xp_ref[t:t + _CONV_PAD, :]

    hn = hn_ref[...]
    kscale = dqk ** -0.5
    r_i = lax.broadcasted_iota(jnp.int32, (CHUNK, CHUNK), 0)
    c_i = lax.broadcasted_iota(jnp.int32, (CHUNK, CHUNK), 1)
    causal = c_i <= r_i
    ones = jnp.ones((CHUNK, dv), BF16)
    zpad_c = jnp.zeros((CHUNK, LANE), BF16)
    zpad_r = jnp.zeros((2 * nh, CHUNK), BF16)

    def body(c, carry):
        base = pl.multiple_of(c * CHUNK, CHUNK)
        rows = pl.ds(base, CHUNK)
        qk = qk_s[rows, :]
        gcol = gc_ref[rows, :] + bc_ref[...]
        b_c = jnp.dot(tri_ref[...], jnp.concatenate(_split3(_log_sigmoid(gcol)) + (zpad_c,), axis=0),
                      preferred_element_type=F32)
        grow = gr_ref[c] + br_ref[...]
        ig_r = grow[0:nh, :]
        b_r = jnp.dot(jnp.concatenate(_split3(_log_sigmoid(grow)) + (zpad_r,), axis=1), trit_ref[...],
                      preferred_element_type=F32)[nh:2 * nh, 0:CHUNK]
        for h in range(nh):
            ig_col = gcol[:, g_off + h:g_off + h + 1]
            q_h = qk[:, h * dqk:(h + 1) * dqk].astype(BF16)
            k_f = qk[:, (nh + h) * dqk:(nh + h + 1) * dqk] * kscale
            k_h = k_f.astype(BF16)
            v_ext = jnp.concatenate([v_ref[rows, h * dv:(h + 1) * dv].astype(BF16), ones], axis=1)
            bcol = b_c[:, g_off + nh + h:g_off + nh + h + 1]
            m_prev = m_ref[h:h + 1, 0:1]
            log_d = jnp.where(causal, bcol - b_r[h:h + 1, :] + ig_r[h:h + 1, :], NEG)
            log_inter = bcol + m_prev
            m_t = jnp.maximum(jnp.max(log_d, axis=1, keepdims=True), log_inter)
            qk_w = (lax.dot_general(q_h, k_h, NT_DIMS, preferred_element_type=F32)
                    * jnp.exp(log_d - m_t))
            w_inter = jnp.exp(log_inter - m_t)
            c_ext = c_ref[h]
            numden = (jnp.dot(qk_w.astype(BF16), v_ext, preferred_element_type=F32)
                      + w_inter * jnp.dot(q_h, c_ext.astype(BF16), preferred_element_type=F32))
            num = numden[:, 0:dv]
            den = numden[:, dv:2 * dv]
            hh = num / jnp.maximum(jnp.abs(den), jnp.exp(-m_t))
            m_new = m_t[CHUNK - 1:CHUNK, :]
            b_last = bcol[CHUNK - 1:CHUNK, :]
            w_state = jnp.exp(b_last - bcol + ig_col - m_new)
            decay = jnp.exp(b_last + m_prev - m_new)
            c_ref[h] = decay * c_ext + lax.dot_general((k_f * w_state).astype(BF16), v_ext, TN_DIMS,
                                                       preferred_element_type=F32)
            m_ref[h:h + 1, :] = jnp.broadcast_to(m_new, (1, LANE))
            ms = jnp.mean(hh * hh, axis=1, keepdims=True)
            gate = _sigmoid(og_ref[rows, h * dv:(h + 1) * dv])
            o_ref[rows, h * dv:(h + 1) * dv] = (hh * lax.rsqrt(ms + EPS) * hn * gate).astype(o_ref.dtype)
        return carry

    lax.fori_loop(0, nchunk, body, 0)


def _mlstm(proj, tail, conv_w, gate_bias, hnorm, *, batch, seq, group, g_off):
    m = batch * seq
    nh = group // HEAD_DIM
    dqk = HEAD_DIM // 2
    dv = HEAD_DIM
    t = _pick_tile(seq, (256, 128, 64))
    nt = seq // t
    tri3, trit3 = _mlstm_constants()
    qk_blk = 4
    gates_row = (tail[:, g_off:g_off + 2 * nh].reshape(batch, seq // CHUNK, CHUNK, 2 * nh)
                 .transpose(0, 1, 3, 2))
    bias = gate_bias.astype(F32)
    bias_lane = jnp.zeros((1, LANE), F32).at[0, g_off:g_off + 2 * nh].set(bias)

    def wide(off):
        return pl.BlockSpec((t, group), lambda b, i: (b * nt + i, off))

    def const(shape):
        return pl.BlockSpec(shape, lambda b, i: tuple(0 for _ in shape))

    return pl.pallas_call(
        functools.partial(_mlstm_kernel, nchunk=t // CHUNK, nh=nh, dqk=dqk, dv=dv, t=t, g_off=g_off),
        grid=(batch, nt),
        in_specs=[wide(qk_blk), wide(qk_blk + 1), wide(qk_blk + 2),
                  pl.BlockSpec((t, LANE), lambda b, i: (b * nt + i, 0)),
                  pl.BlockSpec((None, t // CHUNK, 2 * nh, CHUNK), lambda b, i: (b, i, 0, 0)),
                  const((CONV_W, group)), const((1, LANE)), const((2 * nh, 1)), const((1, dv)),
                  const(tri3.shape), const(trit3.shape)],
        out_specs=pl.BlockSpec((t, group), lambda b, i: (b * nt + i, 0)),
        out_shape=jax.ShapeDtypeStruct((m, group), BF16),
        scratch_shapes=[pltpu.VMEM((t + _CONV_PAD, group), F32),
                        pltpu.VMEM((t, group), F32),
                        pltpu.VMEM((nh, dqk, 2 * dv), F32),
                        pltpu.VMEM((nh, LANE), F32)],
        compiler_params=_params("parallel", "arbitrary"),
        name="mlstm",
    )(proj, proj, proj, tail, gates_row, conv_w.astype(F32), bias_lane, bias.reshape(2 * nh, 1),
      hnorm.reshape(1, dv).astype(F32), tri3, trit3)


def _ki_kernel(x_ref, g_ref, o_ref):
    x = x_ref[:, 0:IDX_DIM]
    ms = jnp.mean(x * x, axis=-1, keepdims=True)
    o_ref[...] = (x * lax.rsqrt(ms + EPS) * g_ref[...]).astype(o_ref.dtype)


def _ki_norm(tail, g):
    m = tail.shape[0]
    tm = _pick_tile(m, (512, 256, 128, 64))
    return pl.pallas_call(
        _ki_kernel,
        grid=(m // tm,),
        in_specs=[pl.BlockSpec((tm, LANE), lambda i: (i, 0)), pl.BlockSpec((1, IDX_DIM), lambda i: (0, 0))],
        out_specs=pl.BlockSpec((tm, IDX_DIM), lambda i: (i, 0)),
        out_shape=jax.ShapeDtypeStruct((m, IDX_DIM), BF16),
        compiler_params=_params("parallel"),
        name="ki_norm",
    )(tail, g.reshape(1, IDX_DIM).astype(F32))


def _diff_kernel(q_ref, k_ref, v_ref, lam_ref, sn_ref, o_ref, m_ref, l_ref, acc_ref, *, tq, tk, dqk, out_scale):
    qi = pl.program_id(2)
    last_k = ((qi + 1) * tq - 1) // tk
    m_ref[...] = jnp.full_like(m_ref, NEG)
    l_ref[...] = jnp.zeros_like(l_ref)
    acc_ref[...] = jnp.zeros_like(acc_ref)
    q = q_ref[...]

    def step(kb, masked):
        rows = pl.ds(pl.multiple_of(kb * tk, tk), tk)
        k = k_ref[rows, :]
        v = v_ref[rows, :]
        if masked:
            q_chunk = (qi * tq + lax.broadcasted_iota(jnp.int32, (tq, tk), 0)) // CHUNK
            k_chunk = (kb * tk + lax.broadcasted_iota(jnp.int32, (tq, tk), 1)) // CHUNK
            mask = k_chunk <= q_chunk
        for half in range(2):
            s = lax.dot_general(q[:, half * dqk:(half + 1) * dqk], k[:, half * dqk:(half + 1) * dqk], NT_DIMS,
                                preferred_element_type=F32)
            if masked:
                s = jnp.where(mask, s, NEG)
            m_prev = m_ref[half]
            m_new = jnp.maximum(m_prev, jnp.max(s, axis=1, keepdims=True))
            p = jnp.exp(s - m_new)
            alpha = jnp.exp(m_prev - m_new)
            l_ref[half] = alpha * l_ref[half] + jnp.sum(p, axis=1, keepdims=True)
            acc_ref[half] = alpha * acc_ref[half] + jnp.dot(p.astype(BF16), v, preferred_element_type=F32)
            m_ref[half] = m_new

    def body(kb, carry):
        step(kb, False)
        return carry

    lax.fori_loop(0, last_k, body, 0)
    step(last_k, True)

    o = acc_ref[0] / l_ref[0] - lam_ref[...] * (acc_ref[1] / l_ref[1])
    ms = jnp.mean(o * o, axis=1, keepdims=True)
    o_ref[...] = (o * lax.rsqrt(ms + EPS) * sn_ref[...] * out_scale).astype(o_ref.dtype)


def _diff_attention(cd, lam_full, subln, lambda_init, *, batch, seq, group):
    m = batch * seq
    dqk = HEAD_DIM
    dv = 2 * HEAD_DIM
    nh = group // dv
    tq = _pick_tile(seq, (512, 256, 128))
    tk = _pick_tile(seq, (1024, 512, 256, 128))
    assert tk % tq == 0
    nq = seq // tq
    k_off, v_off = nh, 2 * nh

    return pl.pallas_call(
        functools.partial(_diff_kernel, tq=tq, tk=tk, dqk=dqk, out_scale=1.0 - lambda_init),
        grid=(batch, nh, nq),
        in_specs=[pl.BlockSpec((tq, dv), lambda b, h, i: (b * nq + i, h)),
                  pl.BlockSpec((seq, dv), lambda b, h, i: (b, k_off + h)),
                  pl.BlockSpec((seq, dv), lambda b, h, i: (b, v_off + h)),
                  pl.BlockSpec((1, 1), lambda b, h, i: (0, 0)),
                  pl.BlockSpec((1, dv), lambda b, h, i: (0, 0))],
        out_specs=pl.BlockSpec((tq, dv), lambda b, h, i: (b * nq + i, h)),
        out_shape=jax.ShapeDtypeStruct((m, group), BF16),
        scratch_shapes=[pltpu.VMEM((2, tq, 1), F32), pltpu.VMEM((2, tq, 1), F32), pltpu.VMEM((2, tq, dv), F32)],
        compiler_params=_params("parallel", "parallel", "arbitrary"),
        name="diff_attn",
    )(cd, cd, cd, lam_full.reshape(1, 1).astype(F32), subln.reshape(1, dv).astype(F32))


_INT_MIN = -2 ** 31


def _dsa_kernel(q_ref, k_ref, v_ref, qi_ref, ki_ref, tail_ref, o_ref, sc_ref, *, tq, tk, topk, w_off, w_scale):
    qb = pl.program_id(1)
    h = pl.program_id(2)
    nkv = ((qb + 1) * tq + tk - 1) // tk

    @pl.when(h == 0)
    def _():
        w = tail_ref[:, w_off:w_off + IDX_HEADS] * w_scale
        q_chunk = (qb * tq + lax.broadcasted_iota(jnp.int32, (tq, tk), 0)) // CHUNK

        def score_body(kb, carry):
            cols = pl.ds(pl.multiple_of(kb * tk, tk), tk)
            ki = ki_ref[cols, :]
            acc = jnp.zeros((tq, tk), F32)
            for ih in range(IDX_HEADS):
                lg = lax.dot_general(qi_ref[:, ih * IDX_DIM:(ih + 1) * IDX_DIM], ki, NT_DIMS,
                                     preferred_element_type=F32)
                acc = acc + w[:, ih:ih + 1] * jnp.maximum(lg, 0.0)
            k_chunk = (kb * tk + lax.broadcasted_iota(jnp.int32, (tq, tk), 1)) // CHUNK
            sc_ref[kb] = jnp.where(k_chunk <= q_chunk, acc, NEG)
            return carry

        lax.fori_loop(0, nkv, score_body, 0)

        def bit_body(i, u):
            cand = u | jnp.left_shift(jnp.int32(1), 31 - i)
            key = cand ^ _INT_MIN
            bits = key ^ (jnp.right_shift(key, 31) & 0x7FFFFFFF)
            cf = lax.bitcast_convert_type(bits, F32)

            def cnt_body(kb, cnt):
                hit = jnp.where(sc_ref[kb] >= cf, 1.0, 0.0)
                for c in range(tk // LANE):
                    cnt = cnt + hit[:, c * LANE:(c + 1) * LANE]
                return cnt

            cnt = jnp.sum(lax.fori_loop(0, nkv, cnt_body, jnp.zeros((tq, LANE), F32)), axis=1, keepdims=True)
            return jnp.where(cnt >= float(topk), cand, u)

        u = lax.fori_loop(0, 32, bit_body, jnp.zeros((tq, 1), jnp.int32))
        key = u ^ _INT_MIN
        thr = lax.bitcast_convert_type(key ^ (jnp.right_shift(key, 31) & 0x7FFFFFFF), F32)

        def bias_body(kb, carry):
            s = sc_ref[kb]
            sc_ref[kb] = jnp.where((s >= thr) & (s > 0.5 * NEG), 0.0, NEG)
            return carry

        lax.fori_loop(0, nkv, bias_body, 0)

    q = q_ref[...]

    def kv_body(kb, carry):
        m_prev, l_prev, acc = carry
        cols = pl.ds(pl.multiple_of(kb * tk, tk), tk)
        s = lax.dot_general(q, k_ref[cols, :], NT_DIMS, preferred_element_type=F32) + sc_ref[kb]
        m_new = jnp.maximum(m_prev, jnp.max(s, axis=1, keepdims=True))
        p = jnp.exp(s - m_new)
        alpha = jnp.exp(m_prev - m_new)
        l_new = alpha * l_prev + jnp.sum(p, axis=1, keepdims=True)
        acc = alpha * acc + jnp.dot(p.astype(BF16), v_ref[cols, :], preferred_element_type=F32)
        return m_new, l_new, acc

    init = (jnp.full((tq, 1), NEG, F32), jnp.zeros((tq, 1), F32), jnp.zeros((tq, HEAD_DIM), F32))
    _, l_fin, acc = lax.fori_loop(0, nkv, kv_body, init)
    o_ref[...] = (acc / l_fin).astype(o_ref.dtype)


def _dsa_attention(cd, ki_n, tail, *, batch, seq, group, w_off):
    m = batch * seq
    nh = group // HEAD_DIM
    tq = _pick_tile(seq, (512, 256, 128))
    tk = _pick_tile(seq, (1024, 512, 256))
    topk = min(TOPK_MAX, seq // 4)
    assert tk >= topk, "the threshold search needs at least topk keys in range"
    nq = seq // tq
    nc = group // LANE
    q_off, k_off, v_off = 3 * nc, 4 * nc, 5 * nc
    qi_off = 6 * nc * LANE // (IDX_HEADS * IDX_DIM)
    assert qi_off * IDX_HEADS * IDX_DIM == 6 * nc * LANE

    return pl.pallas_call(
        functools.partial(_dsa_kernel, tq=tq, tk=tk, topk=topk, w_off=w_off,
                          w_scale=IDX_HEADS ** -0.5 * IDX_DIM ** -0.5),
        grid=(batch, nq, nh),
        in_specs=[pl.BlockSpec((tq, HEAD_DIM), lambda b, i, h: (b * nq + i, q_off + h)),
                  pl.BlockSpec((seq, HEAD_DIM), lambda b, i, h: (b, k_off + h)),
                  pl.BlockSpec((seq, HEAD_DIM), lambda b, i, h: (b, v_off + h)),
                  pl.BlockSpec((tq, IDX_HEADS * IDX_DIM), lambda b, i, h: (b * nq + i, qi_off)),
                  pl.BlockSpec((seq, IDX_DIM), lambda b, i, h: (b, 0)),
                  pl.BlockSpec((tq, LANE), lambda b, i, h: (b * nq + i, 0))],
        out_specs=pl.BlockSpec((tq, HEAD_DIM), lambda b, i, h: (b * nq + i, h)),
        out_shape=jax.ShapeDtypeStruct((m, group), BF16),
        scratch_shapes=[pltpu.VMEM((seq // tk, tq, tk), F32)],
        compiler_params=_params("parallel", "arbitrary", "arbitrary"),
        name="dsa",
    )(cd, cd, cd, cd, ki_n, tail)


def _rearrange_w_in(w, group, nhb):
    d = w.shape[0]
    p0 = 7 * group
    p1 = p0 + 2 * nhb
    p2 = p1 + 6 * group + IDX_HEADS * IDX_DIM
    used = IDX_DIM + 2 * nhb + IDX_HEADS
    tail = jnp.concatenate([w[:, p2:p2 + IDX_DIM], w[:, p0:p1], w[:, p2 + IDX_DIM:],
                            jnp.zeros((d, LANE - used), w.dtype)], axis=1)
    return w[:, :p0].astype(BF16), w[:, p1:p2].astype(BF16), tail.astype(BF16)


def kernel(x, norm_mix, w_in, hgrn_lb_logits, hgrn_gnorm, mlstm_conv, mlstm_gate_bias, mlstm_hnorm,
           diff_qk_norm, diff_lambda, diff_subln, dsa_qk_norm, dsa_idx_knorm, w_out, norm_mlp, w1, w2):
    batch, seq, d_model = x.shape
    depth = w_in.shape[0]
    group = d_model // 4
    nhb = group // HEAD_DIM
    nc = group // LANE
    m = batch * seq
    dims = dict(batch=batch, seq=seq, group=group)

    p_lb = jax.nn.softmax(hgrn_lb_logits.astype(F32), axis=0)
    lower_bounds = jnp.cumsum(p_lb, axis=0) - p_lb[0:1]

    x2 = x.reshape(m, d_model).astype(F32)
    ones_g = jnp.ones((group,), F32)
    for l in range(depth):
        lambda_init = 0.8 - 0.6 * math.exp(-0.3 * l)
        w_ab, w_cd, w_tail = _rearrange_w_in(w_in[l], group, nhb)
        h = _rmsnorm_rows(x2, norm_mix[l])
        proj = _matmul(h, w_ab, out_dtype=F32, name="in_proj_ab")
        tail = _matmul(h, w_tail, out_dtype=F32, name="in_proj_tail")

        y_a = _hgrn(proj, lower_bounds[l], hgrn_gnorm[l], **dims)

        y_b = _mlstm(proj, tail, mlstm_conv[l], mlstm_gate_bias[l], mlstm_hnorm[l], g_off=IDX_DIM, **dims)

        cqn = diff_qk_norm[l].astype(F32)
        dqn = dsa_qk_norm[l].astype(F32)
        gains = jnp.concatenate([
            jnp.tile(cqn[0], nc) * HEAD_DIM ** -0.5, jnp.tile(cqn[1], nc), ones_g,
            jnp.tile(dqn[0], nc) * HEAD_DIM ** -0.5, jnp.tile(dqn[1], nc), ones_g,
            jnp.ones((IDX_HEADS * IDX_DIM,), F32)]).reshape(1, -1)
        use_norm = jnp.concatenate([
            jnp.ones((2 * group,), F32), jnp.zeros((group,), F32),
            jnp.ones((2 * group,), F32), jnp.zeros((group + IDX_HEADS * IDX_DIM,), F32)]).reshape(1, -1)
        cd = _matmul(h, w_cd, out_dtype=BF16, act="groupnorm", extra=(gains, use_norm), name="in_proj_cd")
        ki_n = _ki_norm(tail, dsa_idx_knorm[l])

        lam = diff_lambda[l].astype(F32)
        lam_full = jnp.exp(jnp.sum(lam[0] * lam[1])) - jnp.exp(jnp.sum(lam[2] * lam[3])) + lambda_init
        y_c = _diff_attention(cd, lam_full, diff_subln[l], lambda_init, **dims)
        y_d = _dsa_attention(cd, ki_n, tail, w_off=IDX_DIM + 2 * nhb, **dims)

        y = jnp.concatenate([y_a, y_b, y_c, y_d], axis=1)
        x2 = _matmul(y, w_out[l].astype(BF16), out_dtype=F32, act="residual", extra=(x2,), tm_max=512,
                     name="out_proj")
        h = _rmsnorm_rows(x2, norm_mlp[l])
        hidden = _matmul(h, w1[l].astype(BF16), out_dtype=BF16, act="relu2", name="mlp_up")
        x2 = _matmul(hidden, w2[l].astype(BF16), out_dtype=F32, act="residual", extra=(x2,), name="mlp_down")
    return x2.reshape(batch, seq, d_model).astype(x.dtype)
```

```python
import functools
import math

import numpy as np
import jax
import jax.numpy as jnp
from jax import lax
from jax.experimental import pallas as pl
from jax.experimental.pallas import tpu as pltpu

CHUNK = 64
EPS = 1e-6
NEG = -1e30
F_FLOOR = 1e-20
HEAD_DIM = 128
CONV_W = 4
IDX_HEADS = 16
IDX_DIM = 64
TOPK_MAX = 256
LOG2_E = math.log2(math.e)
LANE = 128
VMEM_LIMIT = 56 * 1024 * 1024

F32 = jnp.float32
BF16 = jnp.bfloat16
NT_DIMS = (((1,), (1,)), ((), ()))
TN_DIMS = (((0,), (0,)), ((), ()))


def _params(*sem):
    return pltpu.CompilerParams(dimension_semantics=sem, vmem_limit_bytes=VMEM_LIMIT)


def _sigmoid(x):
    return 1.0 / (1.0 + jnp.exp(-x))


def _log_sigmoid(x):
    return jnp.minimum(x, 0.0) - jnp.log1p(jnp.exp(-jnp.abs(x)))


def _split3(x):
    hi = x.astype(BF16)
    r1 = x - hi.astype(F32)
    mid = r1.astype(BF16)
    lo = (r1 - mid.astype(F32)).astype(BF16)
    return hi, mid, lo


def _pick_tile(n, candidates):
    for c in candidates:
        if n % c == 0:
            return c
    raise ValueError(f"no tile in {candidates} divides {n}")


def _rmsnorm_kernel(x_ref, g_ref, o_ref):
    x = x_ref[...]
    ms = jnp.mean(x * x, axis=-1, keepdims=True)
    o_ref[...] = (x * lax.rsqrt(ms + EPS) * g_ref[...]).astype(o_ref.dtype)


def _rmsnorm_rows(x2, g):
    m, d = x2.shape
    tm = _pick_tile(m, (256, 128, 64, 8))
    return pl.pallas_call(
        _rmsnorm_kernel,
        grid=(m // tm,),
        in_specs=[pl.BlockSpec((tm, d), lambda i: (i, 0)), pl.BlockSpec((1, d), lambda i: (0, 0))],
        out_specs=pl.BlockSpec((tm, d), lambda i: (i, 0)),
        out_shape=jax.ShapeDtypeStruct((m, d), BF16),
        compiler_params=_params("parallel"),
        name="rmsnorm",
    )(x2, g.reshape(1, d).astype(F32))


MM_MAX_TK = 4096


def _mm_epilogue(r, act, extra):
    if act == "relu2":
        r = jnp.square(jnp.maximum(r, 0.0))
    elif act == "groupnorm":
        g_ref, u_ref = extra
        segs = []
        for c in range(r.shape[1] // LANE):
            seg = r[:, c * LANE:(c + 1) * LANE]
            ms = jnp.mean(seg * seg, axis=1, keepdims=True)
            inv = jnp.where(u_ref[:, c * LANE:(c + 1) * LANE] > 0.0, lax.rsqrt(ms + EPS), 1.0)
            segs.append(seg * inv * g_ref[:, c * LANE:(c + 1) * LANE])
        r = jnp.concatenate(segs, axis=1)
    elif act == "residual":
        r = r + extra[0][...]
    return r


def _mm_kernel(a_ref, b_ref, *rest, nk, act, n_extra):
    extra = rest[:n_extra]
    o_ref = rest[n_extra]
    if nk == 1:
        r = jnp.dot(a_ref[...], b_ref[...], preferred_element_type=F32)
        o_ref[...] = _mm_epilogue(r, act, extra).astype(o_ref.dtype)
        return
    acc_ref = rest[n_extra + 1]
    k = pl.program_id(2)

    @pl.when(k == 0)
    def _():
        acc_ref[...] = jnp.zeros_like(acc_ref)

    acc_ref[...] += jnp.dot(a_ref[...], b_ref[...], preferred_element_type=F32)

    @pl.when(k == nk - 1)
    def _():
        o_ref[...] = _mm_epilogue(acc_ref[...], act, extra).astype(o_ref.dtype)


def _matmul(a, b, *, out_dtype, act=None, extra=(), tm_max=1024, tk_max=MM_MAX_TK, name="matmul"):
    m, kdim = a.shape
    _, n = b.shape
    tm = _pick_tile(m, tuple(t for t in (1024, 512, 256, 128, 8) if t <= tm_max))
    tn = _pick_tile(n, (1024, 512, 256, 128))
    tk = kdim if kdim <= tk_max else _pick_tile(kdim, tuple(t for t in (2048, 1024, 512, 256, 128) if t <= tk_max))
    nk = kdim // tk
    in_specs = [pl.BlockSpec((tm, tk), lambda i, j, k: (i, k)),
                pl.BlockSpec((tk, tn), lambda i, j, k: (k, j))]
    for e in extra:
        if e.shape[0] == 1:
            in_specs.append(pl.BlockSpec((1, tn), lambda i, j, k: (0, j)))
        else:
            in_specs.append(pl.BlockSpec((tm, tn), lambda i, j, k: (i, j)))
    return pl.pallas_call(
        functools.partial(_mm_kernel, nk=nk, act=act, n_extra=len(extra)),
        grid=(m // tm, n // tn, nk),
        in_specs=in_specs,
        out_specs=pl.BlockSpec((tm, tn), lambda i, j, k: (i, j)),
        out_shape=jax.ShapeDtypeStruct((m, n), out_dtype),
        scratch_shapes=[pltpu.VMEM((tm, tn), F32)] if nk > 1 else [],
        compiler_params=_params("parallel", "parallel", "arbitrary"),
        name=name,
    )(a, b, *extra)


_HGRN_LEVELS = 6


def _hgrn_constants():
    idx = np.arange(CHUNK)
    j = idx[None, :]
    r = idx[:, None]
    blocks = [j <= r, j > r]
    masks = []
    for lvl in range(_HGRN_LEVELS):
        half = (CHUNK // 2) >> lvl
        parent = idx // (2 * half)
        mid = parent * 2 * half + half
        is_right = (idx % (2 * half)) >= half
        right_rng = (j >= mid[:, None]) & (j <= r)
        left_rng = (j > r) & (j < mid[:, None])
        blocks.append(np.where(is_right[:, None], right_rng, left_rng))
        masks.append((parent[:, None] == parent[None, :]) & is_right[:, None] & (~is_right)[None, :])
    cm = np.concatenate(blocks, axis=0).astype(np.float32)
    cm3 = np.concatenate([cm, cm, cm, np.zeros_like(cm)], axis=1)
    mk = np.stack(masks, axis=0).astype(np.float32)
    return jnp.asarray(cm3, dtype=BF16), jnp.asarray(mk, dtype=F32)


def _hgrn_kernel(q_ref, f_ref, i_ref, g_ref, lb_ref, gn_ref, cm_ref, mk_ref, o_ref, st_ref, *, nchunk, scale):
    @pl.when(pl.program_id(2) == 0)
    def _():
        st_ref[...] = jnp.zeros_like(st_ref)

    lb = lb_ref[...]
    gn = gn_ref[...]

    def body(c, carry):
        rows = pl.ds(pl.multiple_of(c * CHUNK, CHUNK), CHUNK)
        f = lb + (1.0 - lb) * _sigmoid(f_ref[rows, :])
        logf = jnp.log(jnp.maximum(f, F_FLOOR))
        kk = 1.0 - f
        qr = q_ref[rows, :]
        qq = qr * _sigmoid(qr) * scale
        v = i_ref[rows, :]
        vb = v.astype(BF16)
        l3 = jnp.concatenate(_split3(logf) + (jnp.zeros_like(vb),), axis=0)
        sums = jnp.dot(cm_ref[...], l3, preferred_element_type=F32)
        b = sums[0:CHUNK]
        b_rest = sums[CHUNK:2 * CHUNK]
        scores = jnp.zeros((CHUNK, CHUNK), F32)
        for lvl in range(_HGRN_LEVELS):
            e = jnp.exp(sums[(2 + lvl) * CHUNK:(3 + lvl) * CHUNK])
            s = lax.dot_general((qq * e).astype(BF16), (kk * e).astype(BF16), NT_DIMS,
                                preferred_element_type=F32)
            scores = scores + s * mk_ref[lvl]
        diag = jnp.sum(qq * kk, axis=1, keepdims=True)
        st = st_ref[...]
        o = (jnp.dot(scores.astype(BF16), vb, preferred_element_type=F32) + diag * v
             + lax.dot_general((qq * jnp.exp(b)).astype(BF16), st.astype(BF16), NT_DIMS,
                               preferred_element_type=F32))
        ke = (kk * jnp.exp(b_rest)).astype(BF16)
        st_ref[...] = (st * jnp.exp(b[CHUNK - 1:CHUNK, :])
                       + lax.dot_general(vb, ke, TN_DIMS, preferred_element_type=F32))
        ms = jnp.mean(o * o, axis=1, keepdims=True)
        gr = g_ref[rows, :]
        y = o * lax.rsqrt(ms + EPS) * gn * (gr * _sigmoid(gr))
        o_ref[rows, :] = y.astype(o_ref.dtype)
        return carry

    lax.fori_loop(0, nchunk, body, 0, unroll=True)


def _hgrn(proj, lb, gnorm, *, batch, seq, group):
    m = batch * seq
    nh = group // HEAD_DIM
    t = _pick_tile(seq, (512, 256, 128, 64))
    nt = seq // t
    cm3, mk = _hgrn_constants()

    def col(off):
        return pl.BlockSpec((t, HEAD_DIM), lambda b, h, i: (b * nt + i, off + h))

    return pl.pallas_call(
        functools.partial(_hgrn_kernel, nchunk=t // CHUNK, scale=HEAD_DIM ** -0.5),
        grid=(batch, nh, nt),
        in_specs=[col(0), col(nh), col(2 * nh), col(3 * nh),
                  pl.BlockSpec((1, HEAD_DIM), lambda b, h, i: (0, h)),
                  pl.BlockSpec((1, HEAD_DIM), lambda b, h, i: (0, 0)),
                  pl.BlockSpec(cm3.shape, lambda b, h, i: (0, 0)),
                  pl.BlockSpec(mk.shape, lambda b, h, i: (0, 0, 0))],
        out_specs=pl.BlockSpec((t, HEAD_DIM), lambda b, h, i: (b * nt + i, h)),
        out_shape=jax.ShapeDtypeStruct((m, group), BF16),
        scratch_shapes=[pltpu.VMEM((HEAD_DIM, HEAD_DIM), F32)],
        compiler_params=_params("parallel", "parallel", "arbitrary"),
        name="hgrn2",
    )(proj, proj, proj, proj, lb.reshape(1, group).astype(F32), gnorm.reshape(1, HEAD_DIM).astype(F32), cm3, mk)


_CONV_PAD = 8


def _mlstm_constants():
    idx = np.arange(CHUNK)
    tri = (idx[None, :] <= idx[:, None]).astype(np.float32)
    zero = np.zeros_like(tri)
    tri3 = np.concatenate([tri, tri, tri, zero], axis=1)
    trit3 = np.concatenate([tri.T, tri.T, tri.T, zero], axis=0)
    trit3 = np.concatenate([trit3, np.zeros_like(trit3)], axis=1)
    return jnp.asarray(tri3, dtype=BF16), jnp.asarray(trit3, dtype=BF16)


def _mlstm_kernel(qk_ref, v_ref, og_ref, gc_ref, gr_ref, cw_ref, bc_ref, br_ref, hn_ref, tri_ref, trit_ref,
                  o_ref, xp_ref, qk_s, c_ref, m_ref, *, nchunk, nh, dqk, dv, t, g_off):
    @pl.when(pl.program_id(1) == 0)
    def _():
        xp_ref[0:_CONV_PAD, :] = jnp.zeros((_CONV_PAD, xp_ref.shape[1]), F32)
        c_ref[...] = jnp.zeros_like(c_ref)
        m_ref[...] = jnp.zeros_like(m_ref)

    xp_ref[_CONV_PAD:_CONV_PAD + t, :] = qk_ref[...]
    conv = jnp.zeros((t, 2 * nh * dqk), F32)
    for j in range(CONV_W):
        start = _CONV_PAD - (CONV_W - 1) + j
        conv = conv + cw_ref[j:j + 1, :] * xp_ref[start:start + t, :]
    qk_s[...] = conv * _sigmoid(conv)
    xp_ref[0:_CONV_PAD, :] = xp_ref[t:t + _CONV_PAD, :]

    hn = hn_ref[...]
    kscale = dqk ** -0.5
    r_i = lax.broadcasted_iota(jnp.int32, (CHUNK, CHUNK), 0)
    c_i = lax.broadcasted_iota(jnp.int32, (CHUNK, CHUNK), 1)
    causal = c_i <= r_i
    ones = jnp.ones((CHUNK, dv), BF16)
    zpad_c = jnp.zeros((CHUNK, LANE), BF16)
    zpad_r = jnp.zeros((2 * nh, CHUNK), BF16)

    def body(c, carry):
        base = pl.multiple_of(c * CHUNK, CHUNK)
        rows = pl.ds(base, CHUNK)
        qk = qk_s[rows, :]
        gcol = gc_ref[rows, :] + bc_ref[...]
        b_c = jnp.dot(tri_ref[...], jnp.concatenate(_split3(_log_sigmoid(gcol)) + (zpad_c,), axis=0),
                      preferred_element_type=F32)
        grow = gr_ref[c] + br_ref[...]
        ig_r = grow[0:nh, :]
        b_r = jnp.dot(jnp.concatenate(_split3(_log_sigmoid(grow)) + (zpad_r,), axis=1), trit_ref[...],
                      preferred_element_type=F32)[nh:2 * nh, 0:CHUNK]
        for h in range(nh):
            ig_col = gcol[:, g_off + h:g_off + h + 1]
            q_h = qk[:, h * dqk:(h + 1) * dqk].astype(BF16)
            k_f = qk[:, (nh + h) * dqk:(nh + h + 1) * dqk] * kscale
            k_h = k_f.astype(BF16)
            v_ext = jnp.concatenate([v_ref[rows, h * dv:(h + 1) * dv].astype(BF16), ones], axis=1)
            bcol = b_c[:, g_off + nh + h:g_off + nh + h + 1]
            m_prev = m_ref[h:h + 1, 0:1]
            log_d = jnp.where(causal, bcol - b_r[h:h + 1, :] + ig_r[h:h + 1, :], NEG)
            log_inter = bcol + m_prev
            m_t = jnp.maximum(jnp.max(log_d, axis=1, keepdims=True), log_inter)
            qk_w = (lax.dot_general(q_h, k_h, NT_DIMS, preferred_element_type=F32)
                    * jnp.exp(log_d - m_t))
            w_inter = jnp.exp(log_inter - m_t)
            c_ext = c_ref[h]
            numden = (jnp.dot(qk_w.astype(BF16), v_ext, preferred_element_type=F32)
                      + w_inter * jnp.dot(q_h, c_ext.astype(BF16), preferred_element_type=F32))
            num = numden[:, 0:dv]
            den = numden[:, dv:2 * dv]
            hh = num / jnp.maximum(jnp.abs(den), jnp.exp(-m_t))
            m_new = m_t[CHUNK - 1:CHUNK, :]
            b_last = bcol[CHUNK - 1:CHUNK, :]
            w_state = jnp.exp(b_last - bcol + ig_col - m_new)
            decay = jnp.exp(b_last + m_prev - m_new)
            c_ref[h] = decay * c_ext + lax.dot_general((k_f * w_state).astype(BF16), v_ext, TN_DIMS,
                                                       preferred_element_type=F32)
            m_ref[h:h + 1, :] = jnp.broadcast_to(m_new, (1, LANE))
            ms = jnp.mean(hh * hh, axis=1, keepdims=True)
            gate = _sigmoid(og_ref[rows, h * dv:(h + 1) * dv])
            o_ref[rows, h * dv:(h + 1) * dv] = (hh * lax.rsqrt(ms + EPS) * hn * gate).astype(o_ref.dtype)
        return carry

    lax.fori_loop(0, nchunk, body, 0)


def _mlstm(proj, tail, conv_w, gate_bias, hnorm, *, batch, seq, group, g_off):
    m = batch * seq
    nh = group // HEAD_DIM
    dqk = HEAD_DIM // 2
    dv = HEAD_DIM
    t = _pick_tile(seq, (256, 128, 64))
    nt = seq // t
    tri3, trit3 = _mlstm_constants()
    qk_blk = 4
    gates_row = (tail[:, g_off:g_off + 2 * nh].reshape(batch, seq // CHUNK, CHUNK, 2 * nh)
                 .transpose(0, 1, 3, 2))
    bias = gate_bias.astype(F32)
    bias_lane = jnp.zeros((1, LANE), F32).at[0, g_off:g_off + 2 * nh].set(bias)

    def wide(off):
        return pl.BlockSpec((t, group), lambda b, i: (b * nt + i, off))

    def const(shape):
        return pl.BlockSpec(shape, lambda b, i: tuple(0 for _ in shape))

    return pl.pallas_call(
        functools.partial(_mlstm_kernel, nchunk=t // CHUNK, nh=nh, dqk=dqk, dv=dv, t=t, g_off=g_off),
        grid=(batch, nt),
        in_specs=[wide(qk_blk), wide(qk_blk + 1), wide(qk_blk + 2),
                  pl.BlockSpec((t, LANE), lambda b, i: (b * nt + i, 0)),
                  pl.BlockSpec((None, t // CHUNK, 2 * nh, CHUNK), lambda b, i: (b, i, 0, 0)),
                  const((CONV_W, group)), const((1, LANE)), const((2 * nh, 1)), const((1, dv)),
                  const(tri3.shape), const(trit3.shape)],
        out_specs=pl.BlockSpec((t, group), lambda b, i: (b * nt + i, 0)),
        out_shape=jax.ShapeDtypeStruct((m, group), BF16),
        scratch_shapes=[pltpu.VMEM((t + _CONV_PAD, group), F32),
                        pltpu.VMEM((t, group), F32),
                        pltpu.VMEM((nh, dqk, 2 * dv), F32),
                        pltpu.VMEM((nh, LANE), F32)],
        compiler_params=_params("parallel", "arbitrary"),
        name="mlstm",
    )(proj, proj, proj, tail, gates_row, conv_w.astype(F32), bias_lane, bias.reshape(2 * nh, 1),
      hnorm.reshape(1, dv).astype(F32), tri3, trit3)


def _ki_kernel(x_ref, g_ref, o_ref):
    x = x_ref[:, 0:IDX_DIM]
    ms = jnp.mean(x * x, axis=-1, keepdims=True)
    o_ref[...] = (x * lax.rsqrt(ms + EPS) * g_ref[...]).astype(o_ref.dtype)


def _ki_norm(tail, g):
    m = tail.shape[0]
    tm = _pick_tile(m, (512, 256, 128, 64))
    return pl.pallas_call(
        _ki_kernel,
        grid=(m // tm,),
        in_specs=[pl.BlockSpec((tm, LANE), lambda i: (i, 0)), pl.BlockSpec((1, IDX_DIM), lambda i: (0, 0))],
        out_specs=pl.BlockSpec((tm, IDX_DIM), lambda i: (i, 0)),
        out_shape=jax.ShapeDtypeStruct((m, IDX_DIM), BF16),
        compiler_params=_params("parallel"),
        name="ki_norm",
    )(tail, g.reshape(1, IDX_DIM).astype(F32))


def _diff_kernel(q_ref, k_ref, v_ref, lam_ref, sn_ref, o_ref, m_ref, l_ref, acc_ref, *, tq, tk, dqk, out_scale):
    qi = pl.program_id(2)
    last_k = ((qi + 1) * tq - 1) // tk
    m_ref[...] = jnp.full_like(m_ref, NEG)
    l_ref[...] = jnp.zeros_like(l_ref)
    acc_ref[...] = jnp.zeros_like(acc_ref)
    q = q_ref[...]

    def qk(kb):
        rows = pl.ds(pl.multiple_of(kb * tk, tk), tk)
        k = k_ref[rows, :]
        return tuple(lax.dot_general(q[:, half * dqk:(half + 1) * dqk], k[:, half * dqk:(half + 1) * dqk],
                                     NT_DIMS, preferred_element_type=F32) for half in range(2))

    def step(kb, s_pair, masked):
        rows = pl.ds(pl.multiple_of(kb * tk, tk), tk)
        v = v_ref[rows, :]
        if masked:
            q_chunk = (qi * tq + lax.broadcasted_iota(jnp.int32, (tq, tk), 0)) // CHUNK
            k_chunk = (kb * tk + lax.broadcasted_iota(jnp.int32, (tq, tk), 1)) // CHUNK
            mask = k_chunk <= q_chunk
        for half in range(2):
            s = s_pair[half]
            if masked:
                s = jnp.where(mask, s, NEG)
            m_prev = m_ref[half]
            m_new = jnp.maximum(m_prev, jnp.max(s, axis=1, keepdims=True))
            p = jnp.exp2(s - m_new)
            alpha = jnp.exp2(m_prev - m_new)
            l_ref[half] = alpha * l_ref[half] + jnp.sum(p, axis=1, keepdims=True)
            acc_ref[half] = alpha * acc_ref[half] + jnp.dot(p.astype(BF16), v, preferred_element_type=F32)
            m_ref[half] = m_new

    def body(kb, s_pair):
        s_next = qk(kb + 1)
        step(kb, s_pair, False)
        return s_next

    s_last = lax.fori_loop(0, last_k, body, qk(0))
    step(last_k, s_last, True)

    o = acc_ref[0] / l_ref[0] - lam_ref[...] * (acc_ref[1] / l_ref[1])
    ms = jnp.mean(o * o, axis=1, keepdims=True)
    o_ref[...] = (o * lax.rsqrt(ms + EPS) * sn_ref[...] * out_scale).astype(o_ref.dtype)


def _diff_attention(cd, lam_full, subln, lambda_init, *, batch, seq, group):
    m = batch * seq
    dqk = HEAD_DIM
    dv = 2 * HEAD_DIM
    nh = group // dv
    tq = _pick_tile(seq, (512, 256, 128))
    tk = _pick_tile(seq, (1024, 512, 256, 128))
    assert tk % tq == 0
    nq = seq // tq
    k_off, v_off = nh, 2 * nh

    return pl.pallas_call(
        functools.partial(_diff_kernel, tq=tq, tk=tk, dqk=dqk, out_scale=1.0 - lambda_init),
        grid=(batch, nh, nq),
        in_specs=[pl.BlockSpec((tq, dv), lambda b, h, i: (b * nq + i, h)),
                  pl.BlockSpec((seq, dv), lambda b, h, i: (b, k_off + h)),
                  pl.BlockSpec((seq, dv), lambda b, h, i: (b, v_off + h)),
                  pl.BlockSpec((1, 1), lambda b, h, i: (0, 0)),
                  pl.BlockSpec((1, dv), lambda b, h, i: (0, 0))],
        out_specs=pl.BlockSpec((tq, dv), lambda b, h, i: (b * nq + i, h)),
        out_shape=jax.ShapeDtypeStruct((m, group), BF16),
        scratch_shapes=[pltpu.VMEM((2, tq, 1), F32), pltpu.VMEM((2, tq, 1), F32), pltpu.VMEM((2, tq, dv), F32)],
        compiler_params=_params("parallel", "parallel", "arbitrary"),
        name="diff_attn",
    )(cd, cd, cd, lam_full.reshape(1, 1).astype(F32), subln.reshape(1, dv).astype(F32))


_INT_MIN = -2 ** 31


def _dsa_kernel(q_ref, k_ref, v_ref, qi_ref, ki_ref, tail_ref, o_ref, sc_ref, *, tq, tk, topk, w_off, w_scale):
    qb = pl.program_id(1)
    h = pl.program_id(2)
    nkv = ((qb + 1) * tq + tk - 1) // tk

    @pl.when(h == 0)
    def _():
        w = tail_ref[:, w_off:w_off + IDX_HEADS] * w_scale
        q_chunk = (qb * tq + lax.broadcasted_iota(jnp.int32, (tq, tk), 0)) // CHUNK

        def score_body(kb, carry):
            cols = pl.ds(pl.multiple_of(kb * tk, tk), tk)
            ki = ki_ref[cols, :]
            acc = jnp.zeros((tq, tk), F32)
            for ih in range(IDX_HEADS):
                lg = lax.dot_general(qi_ref[:, ih * IDX_DIM:(ih + 1) * IDX_DIM], ki, NT_DIMS,
                                     preferred_element_type=F32)
                acc = acc + w[:, ih:ih + 1] * jnp.maximum(lg, 0.0)
            k_chunk = (kb * tk + lax.broadcasted_iota(jnp.int32, (tq, tk), 1)) // CHUNK
            sc_ref[kb] = jnp.where(k_chunk <= q_chunk, acc, NEG)
            return carry

        lax.fori_loop(0, nkv, score_body, 0)

        def bit_body(i, u):
            cand = u | jnp.left_shift(jnp.int32(1), 31 - i)
            key = cand ^ _INT_MIN
            bits = key ^ (jnp.right_shift(key, 31) & 0x7FFFFFFF)
            cf = lax.bitcast_convert_type(bits, F32)

            def cnt_body(kb, cnt):
                hit = jnp.where(sc_ref[kb] >= cf, 1.0, 0.0)
                for c in range(tk // LANE):
                    cnt = cnt + hit[:, c * LANE:(c + 1) * LANE]
                return cnt

            cnt = jnp.sum(lax.fori_loop(0, nkv, cnt_body, jnp.zeros((tq, LANE), F32)), axis=1, keepdims=True)
            return jnp.where(cnt >= float(topk), cand, u)

        u = lax.fori_loop(0, 32, bit_body, jnp.zeros((tq, 1), jnp.int32))
        key = u ^ _INT_MIN
        thr = lax.bitcast_convert_type(key ^ (jnp.right_shift(key, 31) & 0x7FFFFFFF), F32)

        def bias_body(kb, carry):
            s = sc_ref[kb]
            sc_ref[kb] = jnp.where((s >= thr) & (s > 0.5 * NEG), 0.0, NEG)
            return carry

        lax.fori_loop(0, nkv, bias_body, 0)

    q = q_ref[...]

    def kv_body(kb, carry):
        m_prev, l_prev, acc = carry
        cols = pl.ds(pl.multiple_of(kb * tk, tk), tk)
        s = lax.dot_general(q, k_ref[cols, :], NT_DIMS, preferred_element_type=F32) + sc_ref[kb]
        m_new = jnp.maximum(m_prev, jnp.max(s, axis=1, keepdims=True))
        p = jnp.exp2(s - m_new)
        alpha = jnp.exp2(m_prev - m_new)
        l_new = alpha * l_prev + jnp.sum(p, axis=1, keepdims=True)
        acc = alpha * acc + jnp.dot(p.astype(BF16), v_ref[cols, :], preferred_element_type=F32)
        return m_new, l_new, acc

    init = (jnp.full((tq, 1), NEG, F32), jnp.zeros((tq, 1), F32), jnp.zeros((tq, HEAD_DIM), F32))
    _, l_fin, acc = lax.fori_loop(0, nkv, kv_body, init)
    o_ref[...] = (acc / l_fin).astype(o_ref.dtype)


def _dsa_attention(cd, ki_n, tail, *, batch, seq, group, w_off):
    m = batch * seq
    nh = group // HEAD_DIM
    tq = _pick_tile(seq, (512, 256, 128))
    tk = _pick_tile(seq, (1024, 512, 256))
    topk = min(TOPK_MAX, seq // 4)
    assert tk >= topk, "the threshold search needs at least topk keys in range"
    nq = seq // tq
    nc = group // LANE
    q_off, k_off, v_off = 3 * nc, 4 * nc, 5 * nc
    qi_off = 6 * nc * LANE // (IDX_HEADS * IDX_DIM)
    assert qi_off * IDX_HEADS * IDX_DIM == 6 * nc * LANE

    return pl.pallas_call(
        functools.partial(_dsa_kernel, tq=tq, tk=tk, topk=topk, w_off=w_off,
                          w_scale=IDX_HEADS ** -0.5 * IDX_DIM ** -0.5),
        grid=(batch, nq, nh),
        in_specs=[pl.BlockSpec((tq, HEAD_DIM), lambda b, i, h: (b * nq + i, q_off + h)),
                  pl.BlockSpec((seq, HEAD_DIM), lambda b, i, h: (b, k_off + h)),
                  pl.BlockSpec((seq, HEAD_DIM), lambda b, i, h: (b, v_off + h)),
                  pl.BlockSpec((tq, IDX_HEADS * IDX_DIM), lambda b, i, h: (b * nq + i, qi_off)),
                  pl.BlockSpec((seq, IDX_DIM), lambda b, i, h: (b, 0)),
                  pl.BlockSpec((tq, LANE), lambda b, i, h: (b * nq + i, 0))],
        out_specs=pl.BlockSpec((tq, HEAD_DIM), lambda b, i, h: (b * nq + i, h)),
        out_shape=jax.ShapeDtypeStruct((m, group), BF16),
        scratch_shapes=[pltpu.VMEM((seq // tk, tq, tk), F32)],
        compiler_params=_params("parallel", "arbitrary", "arbitrary"),
        name="dsa",
    )(cd, cd, cd, cd, ki_n, tail)


def _rearrange_w_in(w, group, nhb):
    d = w.shape[0]
    p0 = 7 * group
    p1 = p0 + 2 * nhb
    p2 = p1 + 6 * group + IDX_HEADS * IDX_DIM
    used = IDX_DIM + 2 * nhb + IDX_HEADS
    tail = jnp.concatenate([w[:, p2:p2 + IDX_DIM], w[:, p0:p1], w[:, p2 + IDX_DIM:],
                            jnp.zeros((d, LANE - used), w.dtype)], axis=1)
    return w[:, :p0].astype(BF16), w[:, p1:p2].astype(BF16), tail.astype(BF16)


def kernel(x, norm_mix, w_in, hgrn_lb_logits, hgrn_gnorm, mlstm_conv, mlstm_gate_bias, mlstm_hnorm,
           diff_qk_norm, diff_lambda, diff_subln, dsa_qk_norm, dsa_idx_knorm, w_out, norm_mlp, w1, w2):
    batch, seq, d_model = x.shape
    depth = w_in.shape[0]
    group = d_model // 4
    nhb = group // HEAD_DIM
    nc = group // LANE
    m = batch * seq
    dims = dict(batch=batch, seq=seq, group=group)

    p_lb = jax.nn.softmax(hgrn_lb_logits.astype(F32), axis=0)
    lower_bounds = jnp.cumsum(p_lb, axis=0) - p_lb[0:1]

    x2 = x.reshape(m, d_model).astype(F32)
    ones_g = jnp.ones((group,), F32)
    for l in range(depth):
        lambda_init = 0.8 - 0.6 * math.exp(-0.3 * l)
        w_ab, w_cd, w_tail = _rearrange_w_in(w_in[l], group, nhb)
        h = _rmsnorm_rows(x2, norm_mix[l])
        proj = _matmul(h, w_ab, out_dtype=F32, name="in_proj_ab")
        tail = _matmul(h, w_tail, out_dtype=F32, name="in_proj_tail")

        y_a = _hgrn(proj, lower_bounds[l], hgrn_gnorm[l], **dims)

        y_b = _mlstm(proj, tail, mlstm_conv[l], mlstm_gate_bias[l], mlstm_hnorm[l], g_off=IDX_DIM, **dims)

        cqn = diff_qk_norm[l].astype(F32)
        dqn = dsa_qk_norm[l].astype(F32)
        q_scale = HEAD_DIM ** -0.5 * LOG2_E
        gains = jnp.concatenate([
            jnp.tile(cqn[0], nc) * q_scale, jnp.tile(cqn[1], nc), ones_g,
            jnp.tile(dqn[0], nc) * q_scale, jnp.tile(dqn[1], nc), ones_g,
            jnp.ones((IDX_HEADS * IDX_DIM,), F32)]).reshape(1, -1)
        use_norm = jnp.concatenate([
            jnp.ones((2 * group,), F32), jnp.zeros((group,), F32),
            jnp.ones((2 * group,), F32), jnp.zeros((group + IDX_HEADS * IDX_DIM,), F32)]).reshape(1, -1)
        cd = _matmul(h, w_cd, out_dtype=BF16, act="groupnorm", extra=(gains, use_norm), name="in_proj_cd")
        ki_n = _ki_norm(tail, dsa_idx_knorm[l])

        lam = diff_lambda[l].astype(F32)
        lam_full = jnp.exp(jnp.sum(lam[0] * lam[1])) - jnp.exp(jnp.sum(lam[2] * lam[3])) + lambda_init
        y_c = _diff_attention(cd, lam_full, diff_subln[l], lambda_init, **dims)
        y_d = _dsa_attention(cd, ki_n, tail, w_off=IDX_DIM + 2 * nhb, **dims)

        y = jnp.concatenate([y_a, y_b, y_c, y_d], axis=1)
        x2 = _matmul(y, w_out[l].astype(BF16), out_dtype=F32, act="residual", extra=(x2,), tm_max=512,
                     name="out_proj")
        h = _rmsnorm_rows(x2, norm_mlp[l])
        hidden = _matmul(h, w1[l].astype(BF16), out_dtype=BF16, act="relu2", name="mlp_up")
        x2 = _matmul(hidden, w2[l].astype(BF16), out_dtype=F32, act="residual", extra=(x2,), name="mlp_down")
    return x2.reshape(batch, seq, d_model).astype(x.dtype)
```

```python
import functools
import math

import numpy as np
import jax
import jax.numpy as jnp
from jax import lax
from jax.experimental import pallas as pl
from jax.experimental.pallas import tpu as pltpu

CHUNK = 64
EPS = 1e-6
NEG = -1e30
F_FLOOR = 1e-20
HEAD_DIM = 128
CONV_W = 4
IDX_HEADS = 16
IDX_DIM = 64
TOPK_MAX = 256
LOG2_E = math.log2(math.e)
LANE = 128
VMEM_LIMIT = 56 * 1024 * 1024

F32 = jnp.float32
BF16 = jnp.bfloat16
NT_DIMS = (((1,), (1,)), ((), ()))
TN_DIMS = (((0,), (0,)), ((), ()))


def _params(*sem):
    return pltpu.CompilerParams(dimension_semantics=sem, vmem_limit_bytes=VMEM_LIMIT)


def _sigmoid(x):
    return 1.0 / (1.0 + jnp.exp(-x))


def _log_sigmoid(x):
    return jnp.minimum(x, 0.0) - jnp.log1p(jnp.exp(-jnp.abs(x)))


def _split3(x):
    hi = x.astype(BF16)
    r1 = x - hi.astype(F32)
    mid = r1.astype(BF16)
    lo = (r1 - mid.astype(F32)).astype(BF16)
    return hi, mid, lo


def _pick_tile(n, candidates):
    for c in candidates:
        if n % c == 0:
            return c
    raise ValueError(f"no tile in {candidates} divides {n}")


def _rmsnorm_kernel(x_ref, g_ref, o_ref):
    x = x_ref[...]
    ms = jnp.mean(x * x, axis=-1, keepdims=True)
    o_ref[...] = (x * lax.rsqrt(ms + EPS) * g_ref[...]).astype(o_ref.dtype)


def _rmsnorm_rows(x2, g):
    m, d = x2.shape
    tm = _pick_tile(m, (256, 128, 64, 8))
    return pl.pallas_call(
        _rmsnorm_kernel,
        grid=(m // tm,),
        in_specs=[pl.BlockSpec((tm, d), lambda i: (i, 0)), pl.BlockSpec((1, d), lambda i: (0, 0))],
        out_specs=pl.BlockSpec((tm, d), lambda i: (i, 0)),
        out_shape=jax.ShapeDtypeStruct((m, d), BF16),
        compiler_params=_params("parallel"),
        name="rmsnorm",
    )(x2, g.reshape(1, d).astype(F32))


MM_MAX_TK = 4096


def _mm_epilogue(r, act, extra):
    if act == "relu2":
        r = jnp.square(jnp.maximum(r, 0.0))
    elif act == "groupnorm":
        g_ref, u_ref = extra
        segs = []
        for c in range(r.shape[1] // LANE):
            seg = r[:, c * LANE:(c + 1) * LANE]
            ms = jnp.mean(seg * seg, axis=1, keepdims=True)
            inv = jnp.where(u_ref[:, c * LANE:(c + 1) * LANE] > 0.0, lax.rsqrt(ms + EPS), 1.0)
            segs.append(seg * inv * g_ref[:, c * LANE:(c + 1) * LANE])
        r = jnp.concatenate(segs, axis=1)
    elif act == "residual":
        r = r + extra[0][...]
    return r


def _mm_kernel(a_ref, b_ref, *rest, nk, act, n_extra):
    extra = rest[:n_extra]
    o_ref = rest[n_extra]
    if nk == 1:
        r = jnp.dot(a_ref[...], b_ref[...], preferred_element_type=F32)
        o_ref[...] = _mm_epilogue(r, act, extra).astype(o_ref.dtype)
        return
    acc_ref = rest[n_extra + 1]
    k = pl.program_id(2)

    @pl.when(k == 0)
    def _():
        acc_ref[...] = jnp.zeros_like(acc_ref)

    acc_ref[...] += jnp.dot(a_ref[...], b_ref[...], preferred_element_type=F32)

    @pl.when(k == nk - 1)
    def _():
        o_ref[...] = _mm_epilogue(acc_ref[...], act, extra).astype(o_ref.dtype)


def _matmul(a, b, *, out_dtype, act=None, extra=(), tm_max=1024, tk_max=MM_MAX_TK, name="matmul"):
    m, kdim = a.shape
    _, n = b.shape
    tm = _pick_tile(m, tuple(t for t in (1024, 512, 256, 128, 8) if t <= tm_max))
    tn = _pick_tile(n, (1024, 512, 256, 128))
    tk = kdim if kdim <= tk_max else _pick_tile(kdim, tuple(t for t in (2048, 1024, 512, 256, 128) if t <= tk_max))
    nk = kdim // tk
    in_specs = [pl.BlockSpec((tm, tk), lambda i, j, k: (i, k)),
                pl.BlockSpec((tk, tn), lambda i, j, k: (k, j))]
    for e in extra:
        if e.shape[0] == 1:
            in_specs.append(pl.BlockSpec((1, tn), lambda i, j, k: (0, j)))
        else:
            in_specs.append(pl.BlockSpec((tm, tn), lambda i, j, k: (i, j)))
    return pl.pallas_call(
        functools.partial(_mm_kernel, nk=nk, act=act, n_extra=len(extra)),
        grid=(m // tm, n // tn, nk),
        in_specs=in_specs,
        out_specs=pl.BlockSpec((tm, tn), lambda i, j, k: (i, j)),
        out_shape=jax.ShapeDtypeStruct((m, n), out_dtype),
        scratch_shapes=[pltpu.VMEM((tm, tn), F32)] if nk > 1 else [],
        compiler_params=_params("parallel", "parallel", "arbitrary"),
        name=name,
    )(a, b, *extra)


_HGRN_LEVELS = 6


def _hgrn_constants():
    idx = np.arange(CHUNK)
    j = idx[None, :]
    r = idx[:, None]
    blocks = [j <= r, j > r]
    masks = []
    for lvl in range(_HGRN_LEVELS):
        half = (CHUNK // 2) >> lvl
        parent = idx // (2 * half)
        mid = parent * 2 * half + half
        is_right = (idx % (2 * half)) >= half
        right_rng = (j >= mid[:, None]) & (j <= r)
        left_rng = (j > r) & (j < mid[:, None])
        blocks.append(np.where(is_right[:, None], right_rng, left_rng))
        masks.append((parent[:, None] == parent[None, :]) & is_right[:, None] & (~is_right)[None, :])
    cm = np.concatenate(blocks, axis=0).astype(np.float32)
    cm3 = np.concatenate([cm, cm, cm, np.zeros_like(cm)], axis=1)
    mk = np.stack(masks, axis=0).astype(np.float32)
    return jnp.asarray(cm3, dtype=BF16), jnp.asarray(mk, dtype=F32)


def _hgrn_kernel(q_ref, f_ref, i_ref, g_ref, lb_ref, gn_ref, cm_ref, mk_ref, o_ref, st_ref, *, nchunk, scale):
    @pl.when(pl.program_id(2) == 0)
    def _():
        st_ref[...] = jnp.zeros_like(st_ref)

    lb = lb_ref[...]
    gn = gn_ref[...]

    def body(c, carry):
        rows = pl.ds(pl.multiple_of(c * CHUNK, CHUNK), CHUNK)
        f = lb + (1.0 - lb) * _sigmoid(f_ref[rows, :])
        logf = jnp.log(jnp.maximum(f, F_FLOOR))
        kk = 1.0 - f
        qr = q_ref[rows, :]
        qq = qr * _sigmoid(qr) * scale
        v = i_ref[rows, :]
        vb = v.astype(BF16)
        l3 = jnp.concatenate(_split3(logf) + (jnp.zeros_like(vb),), axis=0)
        sums = jnp.dot(cm_ref[...], l3, preferred_element_type=F32)
        b = sums[0:CHUNK]
        b_rest = sums[CHUNK:2 * CHUNK]
        scores = jnp.zeros((CHUNK, CHUNK), F32)
        for lvl in range(_HGRN_LEVELS):
            e = jnp.exp(sums[(2 + lvl) * CHUNK:(3 + lvl) * CHUNK])
            s = lax.dot_general((qq * e).astype(BF16), (kk * e).astype(BF16), NT_DIMS,
                                preferred_element_type=F32)
            scores = scores + s * mk_ref[lvl]
        diag = jnp.sum(qq * kk, axis=1, keepdims=True)
        st = st_ref[...]
        o = (jnp.dot(scores.astype(BF16), vb, preferred_element_type=F32) + diag * v
             + lax.dot_general((qq * jnp.exp(b)).astype(BF16), st.astype(BF16), NT_DIMS,
                               preferred_element_type=F32))
        ke = (kk * jnp.exp(b_rest)).astype(BF16)
        st_ref[...] = (st * jnp.exp(b[CHUNK - 1:CHUNK, :])
                       + lax.dot_general(vb, ke, TN_DIMS, preferred_element_type=F32))
        ms = jnp.mean(o * o, axis=1, keepdims=True)
        gr = g_ref[rows, :]
        y = o * lax.rsqrt(ms + EPS) * gn * (gr * _sigmoid(gr))
        o_ref[rows, :] = y.astype(o_ref.dtype)
        return carry

    lax.fori_loop(0, nchunk, body, 0, unroll=True)


def _hgrn(proj, lb, gnorm, *, batch, seq, group):
    m = batch * seq
    nh = group // HEAD_DIM
    t = _pick_tile(seq, (512, 256, 128, 64))
    nt = seq // t
    cm3, mk = _hgrn_constants()

    def col(off):
        return pl.BlockSpec((t, HEAD_DIM), lambda b, h, i: (b * nt + i, off + h))

    return pl.pallas_call(
        functools.partial(_hgrn_kernel, nchunk=t // CHUNK, scale=HEAD_DIM ** -0.5),
        grid=(batch, nh, nt),
        in_specs=[col(0), col(nh), col(2 * nh), col(3 * nh),
                  pl.BlockSpec((1, HEAD_DIM), lambda b, h, i: (0, h)),
                  pl.BlockSpec((1, HEAD_DIM), lambda b, h, i: (0, 0)),
                  pl.BlockSpec(cm3.shape, lambda b, h, i: (0, 0)),
                  pl.BlockSpec(mk.shape, lambda b, h, i: (0, 0, 0))],
        out_specs=pl.BlockSpec((t, HEAD_DIM), lambda b, h, i: (b * nt + i, h)),
        out_shape=jax.ShapeDtypeStruct((m, group), BF16),
        scratch_shapes=[pltpu.VMEM((HEAD_DIM, HEAD_DIM), F32)],
        compiler_params=_params("parallel", "parallel", "arbitrary"),
        name="hgrn2",
    )(proj, proj, proj, proj, lb.reshape(1, group).astype(F32), gnorm.reshape(1, HEAD_DIM).astype(F32), cm3, mk)


_CONV_PAD = 8


def _mlstm_constants():
    idx = np.arange(CHUNK)
    tri = (idx[None, :] <= idx[:, None]).astype(np.float32)
    zero = np.zeros_like(tri)
    tri3 = np.concatenate([tri, tri, tri, zero], axis=1)
    trit3 = np.concatenate([tri.T, tri.T, tri.T, zero], axis=0)
    trit3 = np.concatenate([trit3, np.zeros_like(trit3)], axis=1)
    return jnp.asarray(tri3, dtype=BF16), jnp.asarray(trit3, dtype=BF16)


def _mlstm_kernel(qk_ref, v_ref, og_ref, gc_ref, gr_ref, cw_ref, bc_ref, br_ref, hn_ref, tri_ref, trit_ref,
                  o_ref, xp_ref, qk_s, c_ref, m_ref, *, nchunk, nh, dqk, dv, t, g_off):
    @pl.when(pl.program_id(1) == 0)
    def _():
        xp_ref[0:_CONV_PAD, :] = jnp.zeros((_CONV_PAD, xp_ref.shape[1]), F32)
        c_ref[...] = jnp.zeros_like(c_ref)
        m_ref[...] = jnp.zeros_like(m_ref)

    xp_ref[_CONV_PAD:_CONV_PAD + t, :] = qk_ref[...]
    conv = jnp.zeros((t, 2 * nh * dqk), F32)
    for j in range(CONV_W):
        start = _CONV_PAD - (CONV_W - 1) + j
        conv = conv + cw_ref[j:j + 1, :] * xp_ref[start:start + t, :]
    qk_s[...] = conv * _sigmoid(conv)
    xp_ref[0:_CONV_PAD, :] = xp_ref[t:t + _CONV_PAD, :]

    hn = hn_ref[...]
    kscale = dqk ** -0.5
    r_i = lax.broadcasted_iota(jnp.int32, (CHUNK, CHUNK), 0)
    c_i = lax.broadcasted_iota(jnp.int32, (CHUNK, CHUNK), 1)
    causal = c_i <= r_i
    ones = jnp.ones((CHUNK, dv), BF16)
    zpad_c = jnp.zeros((CHUNK, LANE), BF16)
    zpad_r = jnp.zeros((2 * nh, CHUNK), BF16)

    def body(c, carry):
        base = pl.multiple_of(c * CHUNK, CHUNK)
        rows = pl.ds(base, CHUNK)
        qk = qk_s[rows, :]
        gcol = gc_ref[rows, :] + bc_ref[...]
        b_c = jnp.dot(tri_ref[...], jnp.concatenate(_split3(_log_sigmoid(gcol)) + (zpad_c,), axis=0),
                      preferred_element_type=F32)
        grow = gr_ref[c] + br_ref[...]
        ig_r = grow[0:nh, :]
        b_r = jnp.dot(jnp.concatenate(_split3(_log_sigmoid(grow)) + (zpad_r,), axis=1), trit_ref[...],
                      preferred_element_type=F32)[nh:2 * nh, 0:CHUNK]
        for h in range(nh):
            ig_col = gcol[:, g_off + h:g_off + h + 1]
            q_h = qk[:, h * dqk:(h + 1) * dqk].astype(BF16)
            k_f = qk[:, (nh + h) * dqk:(nh + h + 1) * dqk] * kscale
            k_h = k_f.astype(BF16)
            v_ext = jnp.concatenate([v_ref[rows, h * dv:(h + 1) * dv].astype(BF16), ones], axis=1)
            bcol = b_c[:, g_off + nh + h:g_off + nh + h + 1]
            m_prev = m_ref[h:h + 1, 0:1]
            log_d = jnp.where(causal, bcol - b_r[h:h + 1, :] + ig_r[h:h + 1, :], NEG)
            log_inter = bcol + m_prev
            m_t = jnp.maximum(jnp.max(log_d, axis=1, keepdims=True), log_inter)
            qk_w = (lax.dot_general(q_h, k_h, NT_DIMS, preferred_element_type=F32)
                    * jnp.exp(log_d - m_t))
            w_inter = jnp.exp(log_inter - m_t)
            c_ext = c_ref[h]
            numden = (jnp.dot(qk_w.astype(BF16), v_ext, preferred_element_type=F32)
                      + w_inter * jnp.dot(q_h, c_ext.astype(BF16), preferred_element_type=F32))
            num = numden[:, 0:dv]
            den = numden[:, dv:2 * dv]
            hh = num / jnp.maximum(jnp.abs(den), jnp.exp(-m_t))
            m_new = m_t[CHUNK - 1:CHUNK, :]
            b_last = bcol[CHUNK - 1:CHUNK, :]
            w_state = jnp.exp(b_last - bcol + ig_col - m_new)
            decay = jnp.exp(b_last + m_prev - m_new)
            c_ref[h] = decay * c_ext + lax.dot_general((k_f * w_state).astype(BF16), v_ext, TN_DIMS,
                                                       preferred_element_type=F32)
            m_ref[h:h + 1, :] = jnp.broadcast_to(m_new, (1, LANE))
            ms = jnp.mean(hh * hh, axis=1, keepdims=True)
            gate = _sigmoid(og_ref[rows, h * dv:(h + 1) * dv])
            o_ref[rows, h * dv:(h + 1) * dv] = (hh * lax.rsqrt(ms + EPS) * hn * gate).astype(o_ref.dtype)
        return carry

    lax.fori_loop(0, nchunk, body, 0)


def _mlstm(proj, tail, conv_w, gate_bias, hnorm, *, batch, seq, group, g_off):
    m = batch * seq
    nh = group // HEAD_DIM
    dqk = HEAD_DIM // 2
    dv = HEAD_DIM
    t = _pick_tile(seq, (256, 128, 64))
    nt = seq // t
    tri3, trit3 = _mlstm_constants()
    qk_blk = 4
    gates_row = (tail[:, g_off:g_off + 2 * nh].reshape(batch, seq // CHUNK, CHUNK, 2 * nh)
                 .transpose(0, 1, 3, 2))
    bias = gate_bias.astype(F32)
    bias_lane = jnp.zeros((1, LANE), F32).at[0, g_off:g_off + 2 * nh].set(bias)

    def wide(off):
        return pl.BlockSpec((t, group), lambda b, i: (b * nt + i, off))

    def const(shape):
        return pl.BlockSpec(shape, lambda b, i: tuple(0 for _ in shape))

    return pl.pallas_call(
        functools.partial(_mlstm_kernel, nchunk=t // CHUNK, nh=nh, dqk=dqk, dv=dv, t=t, g_off=g_off),
        grid=(batch, nt),
        in_specs=[wide(qk_blk), wide(qk_blk + 1), wide(qk_blk + 2),
                  pl.BlockSpec((t, LANE), lambda b, i: (b * nt + i, 0)),
                  pl.BlockSpec((None, t // CHUNK, 2 * nh, CHUNK), lambda b, i: (b, i, 0, 0)),
                  const((CONV_W, group)), const((1, LANE)), const((2 * nh, 1)), const((1, dv)),
                  const(tri3.shape), const(trit3.shape)],
        out_specs=pl.BlockSpec((t, group), lambda b, i: (b * nt + i, 0)),
        out_shape=jax.ShapeDtypeStruct((m, group), BF16),
        scratch_shapes=[pltpu.VMEM((t + _CONV_PAD, group), F32),
                        pltpu.VMEM((t, group), F32),
                        pltpu.VMEM((nh, dqk, 2 * dv), F32),
                        pltpu.VMEM((nh, LANE), F32)],
        compiler_params=_params("parallel", "arbitrary"),
        name="mlstm",
    )(proj, proj, proj, tail, gates_row, conv_w.astype(F32), bias_lane, bias.reshape(2 * nh, 1),
      hnorm.reshape(1, dv).astype(F32), tri3, trit3)


def _ki_kernel(x_ref, g_ref, o_ref):
    x = x_ref[:, 0:IDX_DIM]
    ms = jnp.mean(x * x, axis=-1, keepdims=True)
    o_ref[...] = (x * lax.rsqrt(ms + EPS) * g_ref[...]).astype(o_ref.dtype)


def _ki_norm(tail, g):
    m = tail.shape[0]
    tm = _pick_tile(m, (512, 256, 128, 64))
    return pl.pallas_call(
        _ki_kernel,
        grid=(m // tm,),
        in_specs=[pl.BlockSpec((tm, LANE), lambda i: (i, 0)), pl.BlockSpec((1, IDX_DIM), lambda i: (0, 0))],
        out_specs=pl.BlockSpec((tm, IDX_DIM), lambda i: (i, 0)),
        out_shape=jax.ShapeDtypeStruct((m, IDX_DIM), BF16),
        compiler_params=_params("parallel"),
        name="ki_norm",
    )(tail, g.reshape(1, IDX_DIM).astype(F32))


def _diff_kernel(q_ref, k_ref, vt_ref, lam_ref, sn_ref, o_ref, m_ref, l_ref, acc_ref, *, tq, tk, dqk, out_scale):
    qi = pl.program_id(2)
    last_k = ((qi + 1) * tq - 1) // tk
    m_ref[...] = jnp.full_like(m_ref, NEG)
    l_ref[...] = jnp.zeros_like(l_ref)
    acc_ref[...] = jnp.zeros_like(acc_ref)
    q = q_ref[...]

    def step(kb, masked):
        rows = pl.ds(pl.multiple_of(kb * tk, tk), tk)
        k = k_ref[rows, :]
        vt = vt_ref[kb]
        if masked:
            k_chunk = (kb * tk + lax.broadcasted_iota(jnp.int32, (tk, tq), 0)) // CHUNK
            q_chunk = (qi * tq + lax.broadcasted_iota(jnp.int32, (tk, tq), 1)) // CHUNK
            mask = k_chunk <= q_chunk
        for half in range(2):
            s = lax.dot_general(k[:, half * dqk:(half + 1) * dqk], q[:, half * dqk:(half + 1) * dqk], NT_DIMS,
                                preferred_element_type=F32)
            if masked:
                s = jnp.where(mask, s, NEG)
            m_prev = m_ref[half]
            m_new = jnp.maximum(m_prev, jnp.max(s, axis=0, keepdims=True))
            p = jnp.exp2(s - m_new)
            alpha = jnp.exp2(m_prev - m_new)
            l_ref[half] = alpha * l_ref[half] + jnp.sum(p, axis=0, keepdims=True)
            acc_ref[half] = alpha * acc_ref[half] + jnp.dot(vt, p.astype(BF16), preferred_element_type=F32)
            m_ref[half] = m_new

    def body(kb, carry):
        step(kb, False)
        return carry

    lax.fori_loop(0, last_k, body, 0)
    step(last_k, True)

    o = acc_ref[0] / l_ref[0] - lam_ref[...] * (acc_ref[1] / l_ref[1])
    ms = jnp.mean(o * o, axis=0, keepdims=True)
    o_ref[...] = (o * lax.rsqrt(ms + EPS) * (sn_ref[...] * out_scale)).T.astype(o_ref.dtype)


def _diff_attention(cd, lam_full, subln, lambda_init, *, batch, seq, group):
    m = batch * seq
    dqk = HEAD_DIM
    dv = 2 * HEAD_DIM
    nh = group // dv
    tq = _pick_tile(seq, (1024, 512, 256, 128))
    tk = _pick_tile(seq, (1024, 512, 256, 128))
    assert tk % tq == 0
    nq, nkb = seq // tq, seq // tk
    k_off = nh
    v_t = cd[:, 2 * group:3 * group].reshape(batch, nkb, tk, nh, dv).transpose(0, 1, 3, 4, 2)

    return pl.pallas_call(
        functools.partial(_diff_kernel, tq=tq, tk=tk, dqk=dqk, out_scale=1.0 - lambda_init),
        grid=(batch, nh, nq),
        in_specs=[pl.BlockSpec((tq, dv), lambda b, h, i: (b * nq + i, h)),
                  pl.BlockSpec((seq, dv), lambda b, h, i: (b, k_off + h)),
                  pl.BlockSpec((None, nkb, None, dv, tk), lambda b, h, i: (b, 0, h, 0, 0)),
                  pl.BlockSpec((1, 1), lambda b, h, i: (0, 0)),
                  pl.BlockSpec((dv, 1), lambda b, h, i: (0, 0))],
        out_specs=pl.BlockSpec((tq, dv), lambda b, h, i: (b * nq + i, h)),
        out_shape=jax.ShapeDtypeStruct((m, group), BF16),
        scratch_shapes=[pltpu.VMEM((2, 1, tq), F32), pltpu.VMEM((2, 1, tq), F32), pltpu.VMEM((2, dv, tq), F32)],
        compiler_params=_params("parallel", "parallel", "arbitrary"),
        name="diff_attn",
    )(cd, cd, v_t, lam_full.reshape(1, 1).astype(F32), subln.reshape(dv, 1).astype(F32))


_INT_MIN = -2 ** 31


_CNT_ROWS = 64


def _dsa_kernel(q_ref, k_ref, vt_ref, qi_ref, ki_ref, wt_ref, o_ref, sc_ref, *, tq, tk, topk, w_scale):
    qb = pl.program_id(1)
    h = pl.program_id(2)
    nkv = ((qb + 1) * tq + tk - 1) // tk

    @pl.when(h == 0)
    def _():
        w = wt_ref[...] * w_scale
        q_chunk = (qb * tq + lax.broadcasted_iota(jnp.int32, (tk, tq), 1)) // CHUNK

        def score_body(kb, carry):
            rows = pl.ds(pl.multiple_of(kb * tk, tk), tk)
            ki = ki_ref[rows, :]
            acc = jnp.zeros((tk, tq), F32)
            for ih in range(IDX_HEADS):
                lg = lax.dot_general(ki, qi_ref[:, ih * IDX_DIM:(ih + 1) * IDX_DIM], NT_DIMS,
                                     preferred_element_type=F32)
                acc = acc + w[ih:ih + 1, :] * jnp.maximum(lg, 0.0)
            k_chunk = (kb * tk + lax.broadcasted_iota(jnp.int32, (tk, tq), 0)) // CHUNK
            sc_ref[kb] = jnp.where(k_chunk <= q_chunk, acc, NEG)
            return carry

        lax.fori_loop(0, nkv, score_body, 0)

        def bit_body(i, u):
            cand = u | jnp.left_shift(jnp.int32(1), 31 - i)
            key = cand ^ _INT_MIN
            bits = key ^ (jnp.right_shift(key, 31) & 0x7FFFFFFF)
            cf = lax.bitcast_convert_type(bits, F32)

            def cnt_body(kb, cnt):
                for r in range(tk // _CNT_ROWS):
                    hit = jnp.where(sc_ref[kb, r * _CNT_ROWS:(r + 1) * _CNT_ROWS, :] >= cf, 1.0, 0.0)
                    for g in range(_CNT_ROWS // 8):
                        cnt = cnt + hit[g * 8:(g + 1) * 8, :]
                return cnt

            cnt = jnp.sum(lax.fori_loop(0, nkv, cnt_body, jnp.zeros((8, tq), F32)), axis=0, keepdims=True)
            return jnp.where(cnt >= float(topk), cand, u)

        u = lax.fori_loop(0, 32, bit_body, jnp.zeros((1, tq), jnp.int32))
        key = u ^ _INT_MIN
        thr = lax.bitcast_convert_type(key ^ (jnp.right_shift(key, 31) & 0x7FFFFFFF), F32)

        def bias_body(kb, carry):
            s = sc_ref[kb]
            sc_ref[kb] = jnp.where((s >= thr) & (s > 0.5 * NEG), 0.0, NEG)
            return carry

        lax.fori_loop(0, nkv, bias_body, 0)

    q = q_ref[...]

    def kv_body(kb, carry):
        m_prev, l_prev, acc = carry
        rows = pl.ds(pl.multiple_of(kb * tk, tk), tk)
        s = lax.dot_general(k_ref[rows, :], q, NT_DIMS, preferred_element_type=F32) + sc_ref[kb]
        m_new = jnp.maximum(m_prev, jnp.max(s, axis=0, keepdims=True))
        p = jnp.exp2(s - m_new)
        alpha = jnp.exp2(m_prev - m_new)
        l_new = alpha * l_prev + jnp.sum(p, axis=0, keepdims=True)
        acc = alpha * acc + jnp.dot(vt_ref[kb], p.astype(BF16), preferred_element_type=F32)
        return m_new, l_new, acc

    init = (jnp.full((1, tq), NEG, F32), jnp.zeros((1, tq), F32), jnp.zeros((HEAD_DIM, tq), F32))
    _, l_fin, acc = lax.fori_loop(0, nkv, kv_body, init)
    o_ref[...] = (acc / l_fin).T.astype(o_ref.dtype)


def _dsa_attention(cd, ki_n, tail, *, batch, seq, group, w_off):
    m = batch * seq
    nh = group // HEAD_DIM
    tq = _pick_tile(seq, (512, 256, 128))
    tk = _pick_tile(seq, (1024, 512, 256))
    topk = min(TOPK_MAX, seq // 4)
    assert tk >= topk, "the threshold search needs at least topk keys in range"
    nq = seq // tq
    nc = group // LANE
    nkb = seq // tk
    q_off, k_off, v_off = 3 * nc, 4 * nc, 5 * nc
    qi_off = 6 * nc * LANE // (IDX_HEADS * IDX_DIM)
    assert qi_off * IDX_HEADS * IDX_DIM == 6 * nc * LANE
    v_t = (cd[:, v_off * LANE:(v_off + nc) * LANE].reshape(batch, nkb, tk, nh, HEAD_DIM)
           .transpose(0, 1, 3, 4, 2))
    w_t = tail[:, w_off:w_off + IDX_HEADS].reshape(batch, seq, IDX_HEADS).transpose(0, 2, 1)

    return pl.pallas_call(
        functools.partial(_dsa_kernel, tq=tq, tk=tk, topk=topk, w_scale=IDX_HEADS ** -0.5 * IDX_DIM ** -0.5),
        grid=(batch, nq, nh),
        in_specs=[pl.BlockSpec((tq, HEAD_DIM), lambda b, i, h: (b * nq + i, q_off + h)),
                  pl.BlockSpec((seq, HEAD_DIM), lambda b, i, h: (b, k_off + h)),
                  pl.BlockSpec((None, nkb, None, HEAD_DIM, tk), lambda b, i, h: (b, 0, h, 0, 0)),
                  pl.BlockSpec((tq, IDX_HEADS * IDX_DIM), lambda b, i, h: (b * nq + i, qi_off)),
                  pl.BlockSpec((seq, IDX_DIM), lambda b, i, h: (b, 0)),
                  pl.BlockSpec((None, IDX_HEADS, tq), lambda b, i, h: (b, 0, i))],
        out_specs=pl.BlockSpec((tq, HEAD_DIM), lambda b, i, h: (b * nq + i, h)),
        out_shape=jax.ShapeDtypeStruct((m, group), BF16),
        scratch_shapes=[pltpu.VMEM((nkb, tk, tq), F32)],
        compiler_params=_params("parallel", "arbitrary", "arbitrary"),
        name="dsa",
    )(cd, cd, v_t, cd, ki_n, w_t)


def _rearrange_w_in(w, group, nhb):
    d = w.shape[0]
    p0 = 7 * group
    p1 = p0 + 2 * nhb
    p2 = p1 + 6 * group + IDX_HEADS * IDX_DIM
    used = IDX_DIM + 2 * nhb + IDX_HEADS
    tail = jnp.concatenate([w[:, p2:p2 + IDX_DIM], w[:, p0:p1], w[:, p2 + IDX_DIM:],
                            jnp.zeros((d, LANE - used), w.dtype)], axis=1)
    return w[:, :p0].astype(BF16), w[:, p1:p2].astype(BF16), tail.astype(BF16)


def kernel(x, norm_mix, w_in, hgrn_lb_logits, hgrn_gnorm, mlstm_conv, mlstm_gate_bias, mlstm_hnorm,
           diff_qk_norm, diff_lambda, diff_subln, dsa_qk_norm, dsa_idx_knorm, w_out, norm_mlp, w1, w2):
    batch, seq, d_model = x.shape
    depth = w_in.shape[0]
    group = d_model // 4
    nhb = group // HEAD_DIM
    nc = group // LANE
    m = batch * seq
    dims = dict(batch=batch, seq=seq, group=group)

    p_lb = jax.nn.softmax(hgrn_lb_logits.astype(F32), axis=0)
    lower_bounds = jnp.cumsum(p_lb, axis=0) - p_lb[0:1]

    x2 = x.reshape(m, d_model).astype(F32)
    ones_g = jnp.ones((group,), F32)
    for l in range(depth):
        lambda_init = 0.8 - 0.6 * math.exp(-0.3 * l)
        w_ab, w_cd, w_tail = _rearrange_w_in(w_in[l], group, nhb)
        h = _rmsnorm_rows(x2, norm_mix[l])
        proj = _matmul(h, w_ab, out_dtype=F32, name="in_proj_ab")
        tail = _matmul(h, w_tail, out_dtype=F32, name="in_proj_tail")

        y_a = _hgrn(proj, lower_bounds[l], hgrn_gnorm[l], **dims)

        y_b = _mlstm(proj, tail, mlstm_conv[l], mlstm_gate_bias[l], mlstm_hnorm[l], g_off=IDX_DIM, **dims)

        cqn = diff_qk_norm[l].astype(F32)
        dqn = dsa_qk_norm[l].astype(F32)
        q_scale = HEAD_DIM ** -0.5 * LOG2_E
        gains = jnp.concatenate([
            jnp.tile(cqn[0], nc) * q_scale, jnp.tile(cqn[1], nc), ones_g,
            jnp.tile(dqn[0], nc) * q_scale, jnp.tile(dqn[1], nc), ones_g,
            jnp.ones((IDX_HEADS * IDX_DIM,), F32)]).reshape(1, -1)
        use_norm = jnp.concatenate([
            jnp.ones((2 * group,), F32), jnp.zeros((group,), F32),
            jnp.ones((2 * group,), F32), jnp.zeros((group + IDX_HEADS * IDX_DIM,), F32)]).reshape(1, -1)
        cd = _matmul(h, w_cd, out_dtype=BF16, act="groupnorm", extra=(gains, use_norm), name="in_proj_cd")
        ki_n = _ki_norm(tail, dsa_idx_knorm[l])

        lam = diff_lambda[l].astype(F32)
        lam_full = jnp.exp(jnp.sum(lam[0] * lam[1])) - jnp.exp(jnp.sum(lam[2] * lam[3])) + lambda_init
        y_c = _diff_attention(cd, lam_full, diff_subln[l], lambda_init, **dims)
        y_d = _dsa_attention(cd, ki_n, tail, w_off=IDX_DIM + 2 * nhb, **dims)

        y = jnp.concatenate([y_a, y_b, y_c, y_d], axis=1)
        x2 = _matmul(y, w_out[l].astype(BF16), out_dtype=F32, act="residual", extra=(x2,), tm_max=512,
                     name="out_proj")
        h = _rmsnorm_rows(x2, norm_mlp[l])
        hidden = _matmul(h, w1[l].astype(BF16), out_dtype=BF16, act="relu2", name="mlp_up")
        x2 = _matmul(hidden, w2[l].astype(BF16), out_dtype=F32, act="residual", extra=(x2,), name="mlp_down")
    return x2.reshape(batch, seq, d_model).astype(x.dtype)
```

```python
import functools
import math

import numpy as np
import jax
import jax.numpy as jnp
from jax import lax
from jax.experimental import pallas as pl
from jax.experimental.pallas import tpu as pltpu

CHUNK = 64
EPS = 1e-6
NEG = -1e30
F_FLOOR = 1e-20
HEAD_DIM = 128
CONV_W = 4
IDX_HEADS = 16
IDX_DIM = 64
TOPK_MAX = 256
LOG2_E = math.log2(math.e)
LANE = 128
VMEM_LIMIT = 56 * 1024 * 1024

F32 = jnp.float32
BF16 = jnp.bfloat16
NT_DIMS = (((1,), (1,)), ((), ()))
TN_DIMS = (((0,), (0,)), ((), ()))


def _params(*sem):
    return pltpu.CompilerParams(dimension_semantics=sem, vmem_limit_bytes=VMEM_LIMIT)


def _sigmoid(x):
    return 1.0 / (1.0 + jnp.exp(-x))


def _log_sigmoid(x):
    return jnp.minimum(x, 0.0) - jnp.log1p(jnp.exp(-jnp.abs(x)))


def _split3(x):
    hi = x.astype(BF16)
    r1 = x - hi.astype(F32)
    mid = r1.astype(BF16)
    lo = (r1 - mid.astype(F32)).astype(BF16)
    return hi, mid, lo


def _pick_tile(n, candidates):
    for c in candidates:
        if n % c == 0:
            return c
    raise ValueError(f"no tile in {candidates} divides {n}")


def _rmsnorm_kernel(x_ref, g_ref, o_ref):
    x = x_ref[...]
    ms = jnp.mean(x * x, axis=-1, keepdims=True)
    o_ref[...] = (x * lax.rsqrt(ms + EPS) * g_ref[...]).astype(o_ref.dtype)


def _rmsnorm_rows(x2, g):
    m, d = x2.shape
    tm = _pick_tile(m, (256, 128, 64, 8))
    return pl.pallas_call(
        _rmsnorm_kernel,
        grid=(m // tm,),
        in_specs=[pl.BlockSpec((tm, d), lambda i: (i, 0)), pl.BlockSpec((1, d), lambda i: (0, 0))],
        out_specs=pl.BlockSpec((tm, d), lambda i: (i, 0)),
        out_shape=jax.ShapeDtypeStruct((m, d), BF16),
        compiler_params=_params("parallel"),
        name="rmsnorm",
    )(x2, g.reshape(1, d).astype(F32))


MM_MAX_TK = 4096
MM_TN_F32 = 512


def _mm_epilogue(r, act, extra):
    if act == "relu2":
        r = jnp.square(jnp.maximum(r, 0.0))
    elif act == "groupnorm":
        g_ref, u_ref = extra
        segs = []
        for c in range(r.shape[1] // LANE):
            seg = r[:, c * LANE:(c + 1) * LANE]
            ms = jnp.mean(seg * seg, axis=1, keepdims=True)
            inv = jnp.where(u_ref[:, c * LANE:(c + 1) * LANE] > 0.0, lax.rsqrt(ms + EPS), 1.0)
            segs.append(seg * inv * g_ref[:, c * LANE:(c + 1) * LANE])
        r = jnp.concatenate(segs, axis=1)
    elif act == "residual":
        r = r + extra[0][...]
    return r


def _mm_kernel(*refs, n_a, nk, act, n_extra):
    a_refs = refs[:n_a]
    b_ref = refs[n_a]
    extra = refs[n_a + 1:n_a + 1 + n_extra]
    o_ref = refs[n_a + 1 + n_extra]

    def product():
        r, off = None, 0
        for a_ref in a_refs:
            kw = a_ref.shape[1]
            d = jnp.dot(a_ref[...], b_ref[off:off + kw, :].astype(BF16), preferred_element_type=F32)
            r = d if r is None else r + d
            off += kw
        return r

    if nk == 1:
        o_ref[...] = _mm_epilogue(product(), act, extra).astype(o_ref.dtype)
        return
    acc_ref = refs[n_a + 2 + n_extra]
    k = pl.program_id(2)

    @pl.when(k == 0)
    def _():
        acc_ref[...] = jnp.zeros_like(acc_ref)

    acc_ref[...] += product()

    @pl.when(k == nk - 1)
    def _():
        o_ref[...] = _mm_epilogue(acc_ref[...], act, extra).astype(o_ref.dtype)


def _matmul(a, b, *, out_dtype, layer=None, col_blocks=None, act=None, extra=(), tm_max=1024, tn_max=1024,
            tk_max=MM_MAX_TK, name="matmul"):
    a_parts = a if isinstance(a, tuple) else (a,)
    m = a_parts[0].shape[0]
    kdim = sum(p.shape[1] for p in a_parts)
    n = b.shape[-1] if col_blocks is None else col_blocks * tn_max
    tm = _pick_tile(m, tuple(t for t in (1024, 512, 256, 128, 8) if t <= tm_max))
    tn = _pick_tile(n, tuple(t for t in (1024, 512, 256, 128) if t <= tn_max))
    tk = kdim if kdim <= tk_max else _pick_tile(kdim, tuple(t for t in (4096, 2048, 1024, 512, 256, 128)
                                                            if t <= tk_max))
    nk = kdim // tk
    assert nk == 1 or len(a_parts) == 1
    in_specs = [pl.BlockSpec((tm, tk if nk > 1 else p.shape[1]), lambda i, j, k: (i, k)) for p in a_parts]
    if layer is None:
        in_specs.append(pl.BlockSpec((tk, tn), lambda i, j, k: (k, j)))
    else:
        in_specs.append(pl.BlockSpec((None, tk, tn), lambda i, j, k: (layer, k, j)))
    for e in extra:
        if e.shape[0] == 1:
            in_specs.append(pl.BlockSpec((1, tn), lambda i, j, k: (0, j)))
        else:
            in_specs.append(pl.BlockSpec((tm, tn), lambda i, j, k: (i, j)))
    return pl.pallas_call(
        functools.partial(_mm_kernel, n_a=len(a_parts), nk=nk, act=act, n_extra=len(extra)),
        grid=(m // tm, n // tn, nk),
        in_specs=in_specs,
        out_specs=pl.BlockSpec((tm, tn), lambda i, j, k: (i, j)),
        out_shape=jax.ShapeDtypeStruct((m, n), out_dtype),
        scratch_shapes=[pltpu.VMEM((tm, tn), F32)] if nk > 1 else [],
        compiler_params=_params("parallel", "parallel", "arbitrary"),
        name=name,
    )(*a_parts, b, *extra)


_HGRN_LEVELS = 6


def _hgrn_constants():
    idx = np.arange(CHUNK)
    j = idx[None, :]
    r = idx[:, None]
    blocks = [j <= r, j > r]
    masks = []
    for lvl in range(_HGRN_LEVELS):
        half = (CHUNK // 2) >> lvl
        parent = idx // (2 * half)
        mid = parent * 2 * half + half
        is_right = (idx % (2 * half)) >= half
        right_rng = (j >= mid[:, None]) & (j <= r)
        left_rng = (j > r) & (j < mid[:, None])
        blocks.append(np.where(is_right[:, None], right_rng, left_rng))
        masks.append((parent[:, None] == parent[None, :]) & is_right[:, None] & (~is_right)[None, :])
    cm = np.concatenate(blocks, axis=0).astype(np.float32)
    cm3 = np.concatenate([cm, cm, cm, np.zeros_like(cm)], axis=1)
    mk = np.stack(masks, axis=0).astype(np.float32)
    return jnp.asarray(cm3, dtype=BF16), jnp.asarray(mk, dtype=F32)


def _hgrn_kernel(q_ref, f_ref, i_ref, g_ref, lb_ref, gn_ref, cm_ref, mk_ref, o_ref, st_ref, *, nchunk, scale):
    @pl.when(pl.program_id(2) == 0)
    def _():
        st_ref[...] = jnp.zeros_like(st_ref)

    lb = lb_ref[...]
    gn = gn_ref[...]

    def body(c, carry):
        rows = pl.ds(pl.multiple_of(c * CHUNK, CHUNK), CHUNK)
        f = lb + (1.0 - lb) * _sigmoid(f_ref[rows, :])
        logf = jnp.log(jnp.maximum(f, F_FLOOR))
        kk = 1.0 - f
        qr = q_ref[rows, :]
        qq = qr * _sigmoid(qr) * scale
        v = i_ref[rows, :]
        vb = v.astype(BF16)
        l3 = jnp.concatenate(_split3(logf) + (jnp.zeros_like(vb),), axis=0)
        sums = jnp.dot(cm_ref[...], l3, preferred_element_type=F32)
        b = sums[0:CHUNK]
        b_rest = sums[CHUNK:2 * CHUNK]
        scores = jnp.zeros((CHUNK, CHUNK), F32)
        for lvl in range(_HGRN_LEVELS):
            e = jnp.exp(sums[(2 + lvl) * CHUNK:(3 + lvl) * CHUNK])
            s = lax.dot_general((qq * e).astype(BF16), (kk * e).astype(BF16), NT_DIMS,
                                preferred_element_type=F32)
            scores = scores + s * mk_ref[lvl]
        diag = jnp.sum(qq * kk, axis=1, keepdims=True)
        st = st_ref[...]
        o = (jnp.dot(scores.astype(BF16), vb, preferred_element_type=F32) + diag * v
             + lax.dot_general((qq * jnp.exp(b)).astype(BF16), st.astype(BF16), NT_DIMS,
                               preferred_element_type=F32))
        ke = (kk * jnp.exp(b_rest)).astype(BF16)
        st_ref[...] = (st * jnp.exp(b[CHUNK - 1:CHUNK, :])
                       + lax.dot_general(vb, ke, TN_DIMS, preferred_element_type=F32))
        ms = jnp.mean(o * o, axis=1, keepdims=True)
        gr = g_ref[rows, :]
        y = o * lax.rsqrt(ms + EPS) * gn * (gr * _sigmoid(gr))
        o_ref[rows, :] = y.astype(o_ref.dtype)
        return carry

    lax.fori_loop(0, nchunk, body, 0, unroll=True)


def _hgrn(proj, lb, gnorm, *, batch, seq, group):
    m = batch * seq
    nh = group // HEAD_DIM
    t = _pick_tile(seq, (512, 256, 128, 64))
    nt = seq // t
    cm3, mk = _hgrn_constants()

    def col(off):
        return pl.BlockSpec((t, HEAD_DIM), lambda b, h, i: (b * nt + i, off + h))

    return pl.pallas_call(
        functools.partial(_hgrn_kernel, nchunk=t // CHUNK, scale=HEAD_DIM ** -0.5),
        grid=(batch, nh, nt),
        in_specs=[col(0), col(nh), col(2 * nh), col(3 * nh),
                  pl.BlockSpec((1, HEAD_DIM), lambda b, h, i: (0, h)),
                  pl.BlockSpec((1, HEAD_DIM), lambda b, h, i: (0, 0)),
                  pl.BlockSpec(cm3.shape, lambda b, h, i: (0, 0)),
                  pl.BlockSpec(mk.shape, lambda b, h, i: (0, 0, 0))],
        out_specs=pl.BlockSpec((t, HEAD_DIM), lambda b, h, i: (b * nt + i, h)),
        out_shape=jax.ShapeDtypeStruct((m, group), BF16),
        scratch_shapes=[pltpu.VMEM((HEAD_DIM, HEAD_DIM), F32)],
        compiler_params=_params("parallel", "parallel", "arbitrary"),
        name="hgrn2",
    )(proj, proj, proj, proj, lb.reshape(1, group).astype(F32), gnorm.reshape(1, HEAD_DIM).astype(F32), cm3, mk)


_CONV_PAD = 8


def _mlstm_constants():
    idx = np.arange(CHUNK)
    tri = (idx[None, :] <= idx[:, None]).astype(np.float32)
    zero = np.zeros_like(tri)
    tri3 = np.concatenate([tri, tri, tri, zero], axis=1)
    trit3 = np.concatenate([tri.T, tri.T, tri.T, zero], axis=0)
    trit3 = np.concatenate([trit3, np.zeros_like(trit3)], axis=1)
    return jnp.asarray(tri3, dtype=BF16), jnp.asarray(trit3, dtype=BF16)


def _mlstm_kernel(qk_ref, v_ref, og_ref, gc_ref, gr_ref, cw_ref, bc_ref, br_ref, hn_ref, tri_ref, trit_ref,
                  o_ref, xp_ref, qk_s, c_ref, m_ref, *, nchunk, nh, dqk, dv, t, g_off):
    @pl.when(pl.program_id(1) == 0)
    def _():
        xp_ref[0:_CONV_PAD, :] = jnp.zeros((_CONV_PAD, xp_ref.shape[1]), F32)
        c_ref[...] = jnp.zeros_like(c_ref)
        m_ref[...] = jnp.zeros_like(m_ref)

    xp_ref[_CONV_PAD:_CONV_PAD + t, :] = qk_ref[...]
    conv = jnp.zeros((t, 2 * nh * dqk), F32)
    for j in range(CONV_W):
        start = _CONV_PAD - (CONV_W - 1) + j
        conv = conv + cw_ref[j:j + 1, :] * xp_ref[start:start + t, :]
    qk_s[...] = conv * _sigmoid(conv)
    xp_ref[0:_CONV_PAD, :] = xp_ref[t:t + _CONV_PAD, :]

    hn = hn_ref[...]
    kscale = dqk ** -0.5
    r_i = lax.broadcasted_iota(jnp.int32, (CHUNK, CHUNK), 0)
    c_i = lax.broadcasted_iota(jnp.int32, (CHUNK, CHUNK), 1)
    causal = c_i <= r_i
    ones = jnp.ones((CHUNK, dv), BF16)
    zpad_c = jnp.zeros((CHUNK, LANE), BF16)
    zpad_r = jnp.zeros((2 * nh, CHUNK), BF16)

    def body(c, carry):
        base = pl.multiple_of(c * CHUNK, CHUNK)
        rows = pl.ds(base, CHUNK)
        qk = qk_s[rows, :]
        gcol = gc_ref[rows, :] + bc_ref[...]
        b_c = jnp.dot(tri_ref[...], jnp.concatenate(_split3(_log_sigmoid(gcol)) + (zpad_c,), axis=0),
                      preferred_element_type=F32)
        grow = gr_ref[c] + br_ref[...]
        ig_r = grow[0:nh, :]
        b_r = jnp.dot(jnp.concatenate(_split3(_log_sigmoid(grow)) + (zpad_r,), axis=1), trit_ref[...],
                      preferred_element_type=F32)[nh:2 * nh, 0:CHUNK]
        for h in range(nh):
            ig_col = gcol[:, g_off + h:g_off + h + 1]
            q_h = qk[:, h * dqk:(h + 1) * dqk].astype(BF16)
            k_f = qk[:, (nh + h) * dqk:(nh + h + 1) * dqk] * kscale
            k_h = k_f.astype(BF16)
            v_ext = jnp.concatenate([v_ref[rows, h * dv:(h + 1) * dv].astype(BF16), ones], axis=1)
            bcol = b_c[:, g_off + nh + h:g_off + nh + h + 1]
            m_prev = m_ref[h:h + 1, 0:1]
            log_d = jnp.where(causal, bcol - b_r[h:h + 1, :] + ig_r[h:h + 1, :], NEG)
            log_inter = bcol + m_prev
            m_t = jnp.maximum(jnp.max(log_d, axis=1, keepdims=True), log_inter)
            qk_w = (lax.dot_general(q_h, k_h, NT_DIMS, preferred_element_type=F32)
                    * jnp.exp(log_d - m_t))
            w_inter = jnp.exp(log_inter - m_t)
            c_ext = c_ref[h]
            numden = (jnp.dot(qk_w.astype(BF16), v_ext, preferred_element_type=F32)
                      + w_inter * jnp.dot(q_h, c_ext.astype(BF16), preferred_element_type=F32))
            num = numden[:, 0:dv]
            den = numden[:, dv:2 * dv]
            hh = num / jnp.maximum(jnp.abs(den), jnp.exp(-m_t))
            m_new = m_t[CHUNK - 1:CHUNK, :]
            b_last = bcol[CHUNK - 1:CHUNK, :]
            w_state = jnp.exp(b_last - bcol + ig_col - m_new)
            decay = jnp.exp(b_last + m_prev - m_new)
            c_ref[h] = decay * c_ext + lax.dot_general((k_f * w_state).astype(BF16), v_ext, TN_DIMS,
                                                       preferred_element_type=F32)
            m_ref[h:h + 1, :] = jnp.broadcast_to(m_new, (1, LANE))
            ms = jnp.mean(hh * hh, axis=1, keepdims=True)
            gate = _sigmoid(og_ref[rows, h * dv:(h + 1) * dv])
            o_ref[rows, h * dv:(h + 1) * dv] = (hh * lax.rsqrt(ms + EPS) * hn * gate).astype(o_ref.dtype)
        return carry

    lax.fori_loop(0, nchunk, body, 0)


def _mlstm(proj, tail, conv_w, gate_bias, hnorm, *, batch, seq, group, g_off):
    m = batch * seq
    nh = group // HEAD_DIM
    dqk = HEAD_DIM // 2
    dv = HEAD_DIM
    t = _pick_tile(seq, (256, 128, 64))
    nt = seq // t
    tri3, trit3 = _mlstm_constants()
    qk_blk = 4
    gates_row = (tail[:, g_off:g_off + 2 * nh].reshape(batch, seq // CHUNK, CHUNK, 2 * nh)
                 .transpose(0, 1, 3, 2))
    bias = gate_bias.astype(F32)
    bias_lane = jnp.zeros((1, LANE), F32).at[0, g_off:g_off + 2 * nh].set(bias)

    def wide(off):
        return pl.BlockSpec((t, group), lambda b, i: (b * nt + i, off))

    def const(shape):
        return pl.BlockSpec(shape, lambda b, i: tuple(0 for _ in shape))

    return pl.pallas_call(
        functools.partial(_mlstm_kernel, nchunk=t // CHUNK, nh=nh, dqk=dqk, dv=dv, t=t, g_off=g_off),
        grid=(batch, nt),
        in_specs=[wide(qk_blk), wide(qk_blk + 1), wide(qk_blk + 2),
                  pl.BlockSpec((t, LANE), lambda b, i: (b * nt + i, 0)),
                  pl.BlockSpec((None, t // CHUNK, 2 * nh, CHUNK), lambda b, i: (b, i, 0, 0)),
                  const((CONV_W, group)), const((1, LANE)), const((2 * nh, 1)), const((1, dv)),
                  const(tri3.shape), const(trit3.shape)],
        out_specs=pl.BlockSpec((t, group), lambda b, i: (b * nt + i, 0)),
        out_shape=jax.ShapeDtypeStruct((m, group), BF16),
        scratch_shapes=[pltpu.VMEM((t + _CONV_PAD, group), F32),
                        pltpu.VMEM((t, group), F32),
                        pltpu.VMEM((nh, dqk, 2 * dv), F32),
                        pltpu.VMEM((nh, LANE), F32)],
        compiler_params=_params("parallel", "arbitrary"),
        name="mlstm",
    )(proj, proj, proj, tail, gates_row, conv_w.astype(F32), bias_lane, bias.reshape(2 * nh, 1),
      hnorm.reshape(1, dv).astype(F32), tri3, trit3)


def _ki_kernel(x_ref, g_ref, o_ref):
    x = x_ref[:, 0:IDX_DIM]
    ms = jnp.mean(x * x, axis=-1, keepdims=True)
    o_ref[...] = (x * lax.rsqrt(ms + EPS) * g_ref[...]).astype(o_ref.dtype)


def _ki_norm(tail, g):
    m = tail.shape[0]
    tm = _pick_tile(m, (512, 256, 128, 64))
    return pl.pallas_call(
        _ki_kernel,
        grid=(m // tm,),
        in_specs=[pl.BlockSpec((tm, LANE), lambda i: (i, 0)), pl.BlockSpec((1, IDX_DIM), lambda i: (0, 0))],
        out_specs=pl.BlockSpec((tm, IDX_DIM), lambda i: (i, 0)),
        out_shape=jax.ShapeDtypeStruct((m, IDX_DIM), BF16),
        compiler_params=_params("parallel"),
        name="ki_norm",
    )(tail, g.reshape(1, IDX_DIM).astype(F32))


def _diff_kernel(q_ref, k_ref, vt_ref, lam_ref, sn_ref, o_ref, m_ref, l_ref, acc_ref, *, tq, tk, dqk, out_scale):
    qi = pl.program_id(2)
    last_k = ((qi + 1) * tq - 1) // tk
    m_ref[...] = jnp.full_like(m_ref, NEG)
    l_ref[...] = jnp.zeros_like(l_ref)
    acc_ref[...] = jnp.zeros_like(acc_ref)
    q = q_ref[...]

    def step(kb, masked):
        rows = pl.ds(pl.multiple_of(kb * tk, tk), tk)
        k = k_ref[rows, :]
        vt = vt_ref[kb]
        if masked:
            k_chunk = (kb * tk + lax.broadcasted_iota(jnp.int32, (tk, tq), 0)) // CHUNK
            q_chunk = (qi * tq + lax.broadcasted_iota(jnp.int32, (tk, tq), 1)) // CHUNK
            mask = k_chunk <= q_chunk
        for half in range(2):
            s = lax.dot_general(k[:, half * dqk:(half + 1) * dqk], q[:, half * dqk:(half + 1) * dqk], NT_DIMS,
                                preferred_element_type=F32)
            if masked:
                s = jnp.where(mask, s, NEG)
            m_prev = m_ref[half]
            m_new = jnp.maximum(m_prev, jnp.max(s, axis=0, keepdims=True))
            p = jnp.exp2(s - m_new)
            alpha = jnp.exp2(m_prev - m_new)
            l_ref[half] = alpha * l_ref[half] + jnp.sum(p, axis=0, keepdims=True)
            acc_ref[half] = alpha * acc_ref[half] + jnp.dot(vt, p.astype(BF16), preferred_element_type=F32)
            m_ref[half] = m_new

    def body(kb, carry):
        step(kb, False)
        return carry

    lax.fori_loop(0, last_k, body, 0)
    step(last_k, True)

    o = acc_ref[0] / l_ref[0] - lam_ref[...] * (acc_ref[1] / l_ref[1])
    ms = jnp.mean(o * o, axis=0, keepdims=True)
    o_ref[...] = (o * lax.rsqrt(ms + EPS) * (sn_ref[...] * out_scale)).T.astype(o_ref.dtype)


def _diff_attention(cd, lam_full, subln, lambda_init, *, batch, seq, group):
    m = batch * seq
    dqk = HEAD_DIM
    dv = 2 * HEAD_DIM
    nh = group // dv
    tq = _pick_tile(seq, (1024, 512, 256, 128))
    tk = _pick_tile(seq, (1024, 512, 256, 128))
    assert tk % tq == 0
    nq, nkb = seq // tq, seq // tk
    k_off = nh
    v_t = cd[:, 2 * group:3 * group].reshape(batch, nkb, tk, nh, dv).transpose(0, 1, 3, 4, 2)

    return pl.pallas_call(
        functools.partial(_diff_kernel, tq=tq, tk=tk, dqk=dqk, out_scale=1.0 - lambda_init),
        grid=(batch, nh, nq),
        in_specs=[pl.BlockSpec((tq, dv), lambda b, h, i: (b * nq + i, h)),
                  pl.BlockSpec((seq, dv), lambda b, h, i: (b, k_off + h)),
                  pl.BlockSpec((None, nkb, None, dv, tk), lambda b, h, i: (b, 0, h, 0, 0)),
                  pl.BlockSpec((1, 1), lambda b, h, i: (0, 0)),
                  pl.BlockSpec((dv, 1), lambda b, h, i: (0, 0))],
        out_specs=pl.BlockSpec((tq, dv), lambda b, h, i: (b * nq + i, h)),
        out_shape=jax.ShapeDtypeStruct((m, group), BF16),
        scratch_shapes=[pltpu.VMEM((2, 1, tq), F32), pltpu.VMEM((2, 1, tq), F32), pltpu.VMEM((2, dv, tq), F32)],
        compiler_params=_params("parallel", "parallel", "arbitrary"),
        name="diff_attn",
    )(cd, cd, v_t, lam_full.reshape(1, 1).astype(F32), subln.reshape(dv, 1).astype(F32))


_INT_MIN = -2 ** 31


_CNT_ROWS = 64


def _dsa_kernel(q_ref, k_ref, vt_ref, qi_ref, ki_ref, wt_ref, o_ref, sc_ref, *, tq, tk, topk, w_scale):
    qb = pl.program_id(1)
    h = pl.program_id(2)
    nkv = ((qb + 1) * tq + tk - 1) // tk

    @pl.when(h == 0)
    def _():
        w = wt_ref[...] * w_scale
        q_chunk = (qb * tq + lax.broadcasted_iota(jnp.int32, (tk, tq), 1)) // CHUNK

        def score_body(kb, carry):
            rows = pl.ds(pl.multiple_of(kb * tk, tk), tk)
            ki = ki_ref[rows, :]
            acc = jnp.zeros((tk, tq), F32)
            for ih in range(IDX_HEADS):
                lg = lax.dot_general(ki, qi_ref[:, ih * IDX_DIM:(ih + 1) * IDX_DIM], NT_DIMS,
                                     preferred_element_type=F32)
                acc = acc + w[ih:ih + 1, :] * jnp.maximum(lg, 0.0)
            k_chunk = (kb * tk + lax.broadcasted_iota(jnp.int32, (tk, tq), 0)) // CHUNK
            sc_ref[kb] = jnp.where(k_chunk <= q_chunk, acc, NEG)
            return carry

        lax.fori_loop(0, nkv, score_body, 0)

        def bit_body(i, u):
            cand = u | jnp.left_shift(jnp.int32(1), 31 - i)
            key = cand ^ _INT_MIN
            bits = key ^ (jnp.right_shift(key, 31) & 0x7FFFFFFF)
            cf = lax.bitcast_convert_type(bits, F32)

            def cnt_body(kb, cnt):
                for r in range(tk // _CNT_ROWS):
                    hit = jnp.where(sc_ref[kb, r * _CNT_ROWS:(r + 1) * _CNT_ROWS, :] >= cf, 1.0, 0.0)
                    for g in range(_CNT_ROWS // 8):
                        cnt = cnt + hit[g * 8:(g + 1) * 8, :]
                return cnt

            cnt = jnp.sum(lax.fori_loop(0, nkv, cnt_body, jnp.zeros((8, tq), F32)), axis=0, keepdims=True)
            return jnp.where(cnt >= float(topk), cand, u)

        u = lax.fori_loop(0, 32, bit_body, jnp.zeros((1, tq), jnp.int32))
        key = u ^ _INT_MIN
        thr = lax.bitcast_convert_type(key ^ (jnp.right_shift(key, 31) & 0x7FFFFFFF), F32)

        def bias_body(kb, carry):
            s = sc_ref[kb]
            sc_ref[kb] = jnp.where((s >= thr) & (s > 0.5 * NEG), 0.0, NEG)
            return carry

        lax.fori_loop(0, nkv, bias_body, 0)

    q = q_ref[...]

    def kv_body(kb, carry):
        m_prev, l_prev, acc = carry
        rows = pl.ds(pl.multiple_of(kb * tk, tk), tk)
        s = lax.dot_general(k_ref[rows, :], q, NT_DIMS, preferred_element_type=F32) + sc_ref[kb]
        m_new = jnp.maximum(m_prev, jnp.max(s, axis=0, keepdims=True))
        p = jnp.exp2(s - m_new)
        alpha = jnp.exp2(m_prev - m_new)
        l_new = alpha * l_prev + jnp.sum(p, axis=0, keepdims=True)
        acc = alpha * acc + jnp.dot(vt_ref[kb], p.astype(BF16), preferred_element_type=F32)
        return m_new, l_new, acc

    init = (jnp.full((1, tq), NEG, F32), jnp.zeros((1, tq), F32), jnp.zeros((HEAD_DIM, tq), F32))
    _, l_fin, acc = lax.fori_loop(0, nkv, kv_body, init)
    o_ref[...] = (acc / l_fin).T.astype(o_ref.dtype)


def _dsa_attention(cd, ki_n, tail, *, batch, seq, group, w_off):
    m = batch * seq
    nh = group // HEAD_DIM
    tq = _pick_tile(seq, (512, 256, 128))
    tk = _pick_tile(seq, (1024, 512, 256))
    topk = min(TOPK_MAX, seq // 4)
    assert tk >= topk, "the threshold search needs at least topk keys in range"
    nq = seq // tq
    nc = group // LANE
    nkb = seq // tk
    q_off, k_off, v_off = 3 * nc, 4 * nc, 5 * nc
    qi_off = 6 * nc * LANE // (IDX_HEADS * IDX_DIM)
    assert qi_off * IDX_HEADS * IDX_DIM == 6 * nc * LANE
    v_t = (cd[:, v_off * LANE:(v_off + nc) * LANE].reshape(batch, nkb, tk, nh, HEAD_DIM)
           .transpose(0, 1, 3, 4, 2))
    w_t = tail[:, w_off:w_off + IDX_HEADS].reshape(batch, seq, IDX_HEADS).transpose(0, 2, 1)

    return pl.pallas_call(
        functools.partial(_dsa_kernel, tq=tq, tk=tk, topk=topk, w_scale=IDX_HEADS ** -0.5 * IDX_DIM ** -0.5),
        grid=(batch, nq, nh),
        in_specs=[pl.BlockSpec((tq, HEAD_DIM), lambda b, i, h: (b * nq + i, q_off + h)),
                  pl.BlockSpec((seq, HEAD_DIM), lambda b, i, h: (b, k_off + h)),
                  pl.BlockSpec((None, nkb, None, HEAD_DIM, tk), lambda b, i, h: (b, 0, h, 0, 0)),
                  pl.BlockSpec((tq, IDX_HEADS * IDX_DIM), lambda b, i, h: (b * nq + i, qi_off)),
                  pl.BlockSpec((seq, IDX_DIM), lambda b, i, h: (b, 0)),
                  pl.BlockSpec((None, IDX_HEADS, tq), lambda b, i, h: (b, 0, i))],
        out_specs=pl.BlockSpec((tq, HEAD_DIM), lambda b, i, h: (b * nq + i, h)),
        out_shape=jax.ShapeDtypeStruct((m, group), BF16),
        scratch_shapes=[pltpu.VMEM((nkb, tk, tq), F32)],
        compiler_params=_params("parallel", "arbitrary", "arbitrary"),
        name="dsa",
    )(cd, cd, v_t, cd, ki_n, w_t)


def _rearrange_w_in(w, group, nhb):
    d = w.shape[0]
    p0 = 7 * group
    p1 = p0 + 2 * nhb
    p2 = p1 + 6 * group + IDX_HEADS * IDX_DIM
    used = IDX_DIM + 2 * nhb + IDX_HEADS
    tail = jnp.concatenate([w[:, p2:p2 + IDX_DIM], w[:, p0:p1], w[:, p2 + IDX_DIM:],
                            jnp.zeros((d, LANE - used), w.dtype)], axis=1)
    return w[:, p1:p2].astype(BF16), tail.astype(BF16)


def kernel(x, norm_mix, w_in, hgrn_lb_logits, hgrn_gnorm, mlstm_conv, mlstm_gate_bias, mlstm_hnorm,
           diff_qk_norm, diff_lambda, diff_subln, dsa_qk_norm, dsa_idx_knorm, w_out, norm_mlp, w1, w2):
    batch, seq, d_model = x.shape
    depth = w_in.shape[0]
    group = d_model // 4
    nhb = group // HEAD_DIM
    nc = group // LANE
    m = batch * seq
    dims = dict(batch=batch, seq=seq, group=group)

    p_lb = jax.nn.softmax(hgrn_lb_logits.astype(F32), axis=0)
    lower_bounds = jnp.cumsum(p_lb, axis=0) - p_lb[0:1]

    x2 = x.reshape(m, d_model).astype(F32)
    ones_g = jnp.ones((group,), F32)
    for l in range(depth):
        lambda_init = 0.8 - 0.6 * math.exp(-0.3 * l)
        w_cd, w_tail = _rearrange_w_in(w_in[l], group, nhb)
        h = _rmsnorm_rows(x2, norm_mix[l])
        proj = _matmul(h, w_in, layer=l, col_blocks=7 * group // MM_TN_F32, out_dtype=F32, tn_max=MM_TN_F32,
                       name="in_proj_ab")
        tail = _matmul(h, w_tail, out_dtype=F32, name="in_proj_tail")

        y_a = _hgrn(proj, lower_bounds[l], hgrn_gnorm[l], **dims)

        y_b = _mlstm(proj, tail, mlstm_conv[l], mlstm_gate_bias[l], mlstm_hnorm[l], g_off=IDX_DIM, **dims)

        cqn = diff_qk_norm[l].astype(F32)
        dqn = dsa_qk_norm[l].astype(F32)
        q_scale = HEAD_DIM ** -0.5 * LOG2_E
        gains = jnp.concatenate([
            jnp.tile(cqn[0], nc) * q_scale, jnp.tile(cqn[1], nc), ones_g,
            jnp.tile(dqn[0], nc) * q_scale, jnp.tile(dqn[1], nc), ones_g,
            jnp.ones((IDX_HEADS * IDX_DIM,), F32)]).reshape(1, -1)
        use_norm = jnp.concatenate([
            jnp.ones((2 * group,), F32), jnp.zeros((group,), F32),
            jnp.ones((2 * group,), F32), jnp.zeros((group + IDX_HEADS * IDX_DIM,), F32)]).reshape(1, -1)
        cd = _matmul(h, w_cd, out_dtype=BF16, act="groupnorm", extra=(gains, use_norm), name="in_proj_cd")
        ki_n = _ki_norm(tail, dsa_idx_knorm[l])

        lam = diff_lambda[l].astype(F32)
        lam_full = jnp.exp(jnp.sum(lam[0] * lam[1])) - jnp.exp(jnp.sum(lam[2] * lam[3])) + lambda_init
        y_c = _diff_attention(cd, lam_full, diff_subln[l], lambda_init, **dims)
        y_d = _dsa_attention(cd, ki_n, tail, w_off=IDX_DIM + 2 * nhb, **dims)

        x2 = _matmul((y_a, y_b, y_c, y_d), w_out, layer=l, out_dtype=F32, act="residual", extra=(x2,),
                     tn_max=MM_TN_F32, name="out_proj")
        h = _rmsnorm_rows(x2, norm_mlp[l])
        hidden = _matmul(h, w1, layer=l, out_dtype=BF16, act="relu2", tn_max=MM_TN_F32, name="mlp_up")
        x2 = _matmul(hidden, w2, layer=l, out_dtype=F32, act="residual", extra=(x2,), tn_max=MM_TN_F32,
                     name="mlp_down")
    return x2.reshape(batch, seq, d_model).astype(x.dtype)
```

```python
import functools
import math

import numpy as np
import jax
import jax.numpy as jnp
from jax import lax
from jax.experimental import pallas as pl
from jax.experimental.pallas import tpu as pltpu

CHUNK = 64
EPS = 1e-6
NEG = -1e30
F_FLOOR = 1e-20
HEAD_DIM = 128
CONV_W = 4
IDX_HEADS = 16
IDX_DIM = 64
TOPK_MAX = 256
LOG2_E = math.log2(math.e)
LANE = 128
VMEM_LIMIT = 56 * 1024 * 1024

F32 = jnp.float32
BF16 = jnp.bfloat16
NT_DIMS = (((1,), (1,)), ((), ()))
TN_DIMS = (((0,), (0,)), ((), ()))


def _params(*sem):
    return pltpu.CompilerParams(dimension_semantics=sem, vmem_limit_bytes=VMEM_LIMIT)


def _sigmoid(x):
    return 1.0 / (1.0 + jnp.exp(-x))


def _log_sigmoid(x):
    return jnp.minimum(x, 0.0) - jnp.log1p(jnp.exp(-jnp.abs(x)))


def _split3(x):
    hi = x.astype(BF16)
    r1 = x - hi.astype(F32)
    mid = r1.astype(BF16)
    lo = (r1 - mid.astype(F32)).astype(BF16)
    return hi, mid, lo


def _pick_tile(n, candidates):
    for c in candidates:
        if n % c == 0:
            return c
    raise ValueError(f"no tile in {candidates} divides {n}")


def _rmsnorm_kernel(x_ref, g_ref, o_ref):
    x = x_ref[...]
    ms = jnp.mean(x * x, axis=-1, keepdims=True)
    o_ref[...] = (x * lax.rsqrt(ms + EPS) * g_ref[...]).astype(o_ref.dtype)


def _rmsnorm_rows(x2, g):
    m, d = x2.shape
    tm = _pick_tile(m, (256, 128, 64, 8))
    return pl.pallas_call(
        _rmsnorm_kernel,
        grid=(m // tm,),
        in_specs=[pl.BlockSpec((tm, d), lambda i: (i, 0)), pl.BlockSpec((1, d), lambda i: (0, 0))],
        out_specs=pl.BlockSpec((tm, d), lambda i: (i, 0)),
        out_shape=jax.ShapeDtypeStruct((m, d), BF16),
        compiler_params=_params("parallel"),
        name="rmsnorm",
    )(x2, g.reshape(1, d).astype(F32))


MM_MAX_TK = 4096
MM_TN = 1024


def _mm_epilogue(r, act, extra):
    if act == "relu2":
        r = jnp.square(jnp.maximum(r, 0.0))
    elif act == "groupnorm":
        g_ref, u_ref = extra
        segs = []
        for c in range(r.shape[1] // LANE):
            seg = r[:, c * LANE:(c + 1) * LANE]
            ms = jnp.mean(seg * seg, axis=1, keepdims=True)
            inv = jnp.where(u_ref[:, c * LANE:(c + 1) * LANE] > 0.0, lax.rsqrt(ms + EPS), 1.0)
            segs.append(seg * inv * g_ref[:, c * LANE:(c + 1) * LANE])
        r = jnp.concatenate(segs, axis=1)
    elif act == "residual":
        r = r + extra[0][...]
    return r


def _mm_kernel(*refs, n_a, nk, act, n_extra):
    a_refs = refs[:n_a]
    b_ref = refs[n_a]
    extra = refs[n_a + 1:n_a + 1 + n_extra]
    o_ref = refs[n_a + 1 + n_extra]

    def product():
        r, off = None, 0
        for a_ref in a_refs:
            kw = a_ref.shape[1]
            d = jnp.dot(a_ref[...], b_ref[off:off + kw, :], preferred_element_type=F32)
            r = d if r is None else r + d
            off += kw
        return r

    if nk == 1:
        o_ref[...] = _mm_epilogue(product(), act, extra).astype(o_ref.dtype)
        return
    acc_ref = refs[n_a + 2 + n_extra]
    k = pl.program_id(2)

    @pl.when(k == 0)
    def _():
        acc_ref[...] = jnp.zeros_like(acc_ref)

    acc_ref[...] += product()

    @pl.when(k == nk - 1)
    def _():
        o_ref[...] = _mm_epilogue(acc_ref[...], act, extra).astype(o_ref.dtype)


def _matmul(a, b, *, out_dtype, layer=None, n_cols=None, act=None, extra=(), tm_max=1024, tn_max=MM_TN,
            tk_max=MM_MAX_TK, name="matmul"):
    a_parts = a if isinstance(a, tuple) else (a,)
    m = a_parts[0].shape[0]
    kdim = sum(p.shape[1] for p in a_parts)
    n = b.shape[-1] if n_cols is None else n_cols
    tm = _pick_tile(m, tuple(t for t in (1024, 512, 256, 128, 8) if t <= tm_max))
    tn = _pick_tile(n, tuple(t for t in (1024, 512, 256, 128) if t <= tn_max))
    tk = kdim if kdim <= tk_max else _pick_tile(kdim, tuple(t for t in (4096, 2048, 1024, 512, 256, 128)
                                                            if t <= tk_max))
    nk = kdim // tk
    assert nk == 1 or len(a_parts) == 1
    in_specs = [pl.BlockSpec((tm, tk if nk > 1 else p.shape[1]), lambda i, j, k: (i, k)) for p in a_parts]
    if layer is None:
        in_specs.append(pl.BlockSpec((tk, tn), lambda i, j, k: (k, j)))
    else:
        in_specs.append(pl.BlockSpec((None, tk, tn), lambda i, j, k: (layer, k, j)))
    for e in extra:
        if e.shape[0] == 1:
            in_specs.append(pl.BlockSpec((1, tn), lambda i, j, k: (0, j)))
        else:
            in_specs.append(pl.BlockSpec((tm, tn), lambda i, j, k: (i, j)))
    return pl.pallas_call(
        functools.partial(_mm_kernel, n_a=len(a_parts), nk=nk, act=act, n_extra=len(extra)),
        grid=(m // tm, n // tn, nk),
        in_specs=in_specs,
        out_specs=pl.BlockSpec((tm, tn), lambda i, j, k: (i, j)),
        out_shape=jax.ShapeDtypeStruct((m, n), out_dtype),
        scratch_shapes=[pltpu.VMEM((tm, tn), F32)] if nk > 1 else [],
        compiler_params=_params("parallel", "parallel", "arbitrary"),
        name=name,
    )(*a_parts, b, *extra)


_HGRN_LEVELS = 6


def _hgrn_constants():
    idx = np.arange(CHUNK)
    j = idx[None, :]
    r = idx[:, None]
    blocks = [j <= r, j > r]
    masks = []
    for lvl in range(_HGRN_LEVELS):
        half = (CHUNK // 2) >> lvl
        parent = idx // (2 * half)
        mid = parent * 2 * half + half
        is_right = (idx % (2 * half)) >= half
        right_rng = (j >= mid[:, None]) & (j <= r)
        left_rng = (j > r) & (j < mid[:, None])
        blocks.append(np.where(is_right[:, None], right_rng, left_rng))
        masks.append((parent[:, None] == parent[None, :]) & is_right[:, None] & (~is_right)[None, :])
    cm = np.concatenate(blocks, axis=0).astype(np.float32)
    cm3 = np.concatenate([cm, cm, cm, np.zeros_like(cm)], axis=1)
    mk = np.stack(masks, axis=0).astype(np.float32)
    return jnp.asarray(cm3, dtype=BF16), jnp.asarray(mk, dtype=F32)


def _hgrn_kernel(q_ref, f_ref, i_ref, g_ref, lb_ref, gn_ref, cm_ref, mk_ref, o_ref, st_ref, *, nchunk, scale):
    @pl.when(pl.program_id(2) == 0)
    def _():
        st_ref[...] = jnp.zeros_like(st_ref)

    lb = lb_ref[...]
    gn = gn_ref[...]

    def body(c, carry):
        rows = pl.ds(pl.multiple_of(c * CHUNK, CHUNK), CHUNK)
        f = lb + (1.0 - lb) * _sigmoid(f_ref[rows, :])
        logf = jnp.log(jnp.maximum(f, F_FLOOR))
        kk = 1.0 - f
        qr = q_ref[rows, :]
        qq = qr * _sigmoid(qr) * scale
        v = i_ref[rows, :]
        vb = v.astype(BF16)
        l3 = jnp.concatenate(_split3(logf) + (jnp.zeros_like(vb),), axis=0)
        sums = jnp.dot(cm_ref[...], l3, preferred_element_type=F32)
        b = sums[0:CHUNK]
        b_rest = sums[CHUNK:2 * CHUNK]
        scores = jnp.zeros((CHUNK, CHUNK), F32)
        for lvl in range(_HGRN_LEVELS):
            e = jnp.exp(sums[(2 + lvl) * CHUNK:(3 + lvl) * CHUNK])
            s = lax.dot_general((qq * e).astype(BF16), (kk * e).astype(BF16), NT_DIMS,
                                preferred_element_type=F32)
            scores = scores + s * mk_ref[lvl]
        diag = jnp.sum(qq * kk, axis=1, keepdims=True)
        st = st_ref[...]
        o = (jnp.dot(scores.astype(BF16), vb, preferred_element_type=F32) + diag * v
             + lax.dot_general((qq * jnp.exp(b)).astype(BF16), st.astype(BF16), NT_DIMS,
                               preferred_element_type=F32))
        ke = (kk * jnp.exp(b_rest)).astype(BF16)
        st_ref[...] = (st * jnp.exp(b[CHUNK - 1:CHUNK, :])
                       + lax.dot_general(vb, ke, TN_DIMS, preferred_element_type=F32))
        ms = jnp.mean(o * o, axis=1, keepdims=True)
        gr = g_ref[rows, :]
        y = o * lax.rsqrt(ms + EPS) * gn * (gr * _sigmoid(gr))
        o_ref[rows, :] = y.astype(o_ref.dtype)
        return carry

    lax.fori_loop(0, nchunk, body, 0, unroll=True)


def _hgrn(proj, lb, gnorm, *, batch, seq, group):
    m = batch * seq
    nh = group // HEAD_DIM
    t = _pick_tile(seq, (512, 256, 128, 64))
    nt = seq // t
    cm3, mk = _hgrn_constants()

    def col(off):
        return pl.BlockSpec((t, HEAD_DIM), lambda b, h, i: (b * nt + i, off + h))

    return pl.pallas_call(
        functools.partial(_hgrn_kernel, nchunk=t // CHUNK, scale=HEAD_DIM ** -0.5),
        grid=(batch, nh, nt),
        in_specs=[col(0), col(nh), col(2 * nh), col(3 * nh),
                  pl.BlockSpec((1, HEAD_DIM), lambda b, h, i: (0, h)),
                  pl.BlockSpec((1, HEAD_DIM), lambda b, h, i: (0, 0)),
                  pl.BlockSpec(cm3.shape, lambda b, h, i: (0, 0)),
                  pl.BlockSpec(mk.shape, lambda b, h, i: (0, 0, 0))],
        out_specs=pl.BlockSpec((t, HEAD_DIM), lambda b, h, i: (b * nt + i, h)),
        out_shape=jax.ShapeDtypeStruct((m, group), BF16),
        scratch_shapes=[pltpu.VMEM((HEAD_DIM, HEAD_DIM), F32)],
        compiler_params=_params("parallel", "parallel", "arbitrary"),
        name="hgrn2",
    )(proj, proj, proj, proj, lb.reshape(1, group).astype(F32), gnorm.reshape(1, HEAD_DIM).astype(F32), cm3, mk)


_CONV_PAD = 8


def _mlstm_constants():
    idx = np.arange(CHUNK)
    tri = (idx[None, :] <= idx[:, None]).astype(np.float32)
    zero = np.zeros_like(tri)
    tri3 = np.concatenate([tri, tri, tri, zero], axis=1)
    trit3 = np.concatenate([tri.T, tri.T, tri.T, zero], axis=0)
    trit3 = np.concatenate([trit3, np.zeros_like(trit3)], axis=1)
    return jnp.asarray(tri3, dtype=BF16), jnp.asarray(trit3, dtype=BF16)


def _mlstm_kernel(qk_ref, v_ref, og_ref, gc_ref, gr_ref, cw_ref, bc_ref, br_ref, hn_ref, tri_ref, trit_ref,
                  o_ref, xp_ref, qk_s, c_ref, m_ref, *, nchunk, nh, dqk, dv, t, g_off):
    @pl.when(pl.program_id(1) == 0)
    def _():
        xp_ref[0:_CONV_PAD, :] = jnp.zeros((_CONV_PAD, xp_ref.shape[1]), F32)
        c_ref[...] = jnp.zeros_like(c_ref)
        m_ref[...] = jnp.zeros_like(m_ref)

    xp_ref[_CONV_PAD:_CONV_PAD + t, :] = qk_ref[...]
    conv = jnp.zeros((t, 2 * nh * dqk), F32)
    for j in range(CONV_W):
        start = _CONV_PAD - (CONV_W - 1) + j
        conv = conv + cw_ref[j:j + 1, :] * xp_ref[start:start + t, :]
    qk_s[...] = conv * _sigmoid(conv)
    xp_ref[0:_CONV_PAD, :] = xp_ref[t:t + _CONV_PAD, :]

    hn = hn_ref[...]
    kscale = dqk ** -0.5
    r_i = lax.broadcasted_iota(jnp.int32, (CHUNK, CHUNK), 0)
    c_i = lax.broadcasted_iota(jnp.int32, (CHUNK, CHUNK), 1)
    causal = c_i <= r_i
    ones = jnp.ones((CHUNK, dv), BF16)
    zpad_c = jnp.zeros((CHUNK, LANE), BF16)
    zpad_r = jnp.zeros((2 * nh, CHUNK), BF16)

    def body(c, carry):
        base = pl.multiple_of(c * CHUNK, CHUNK)
        rows = pl.ds(base, CHUNK)
        qk = qk_s[rows, :]
        gcol = gc_ref[rows, :] + bc_ref[...]
        b_c = jnp.dot(tri_ref[...], jnp.concatenate(_split3(_log_sigmoid(gcol)) + (zpad_c,), axis=0),
                      preferred_element_type=F32)
        grow = gr_ref[c] + br_ref[...]
        ig_r = grow[0:nh, :]
        b_r = jnp.dot(jnp.concatenate(_split3(_log_sigmoid(grow)) + (zpad_r,), axis=1), trit_ref[...],
                      preferred_element_type=F32)[nh:2 * nh, 0:CHUNK]
        for h in range(nh):
            ig_col = gcol[:, g_off + h:g_off + h + 1]
            q_h = qk[:, h * dqk:(h + 1) * dqk].astype(BF16)
            k_f = qk[:, (nh + h) * dqk:(nh + h + 1) * dqk] * kscale
            k_h = k_f.astype(BF16)
            v_ext = jnp.concatenate([v_ref[rows, h * dv:(h + 1) * dv].astype(BF16), ones], axis=1)
            bcol = b_c[:, g_off + nh + h:g_off + nh + h + 1]
            m_prev = m_ref[h:h + 1, 0:1]
            log_d = jnp.where(causal, bcol - b_r[h:h + 1, :] + ig_r[h:h + 1, :], NEG)
            log_inter = bcol + m_prev
            m_t = jnp.maximum(jnp.max(log_d, axis=1, keepdims=True), log_inter)
            qk_w = (lax.dot_general(q_h, k_h, NT_DIMS, preferred_element_type=F32)
                    * jnp.exp(log_d - m_t))
            w_inter = jnp.exp(log_inter - m_t)
            c_ext = c_ref[h]
            numden = (jnp.dot(qk_w.astype(BF16), v_ext, preferred_element_type=F32)
                      + w_inter * jnp.dot(q_h, c_ext.astype(BF16), preferred_element_type=F32))
            num = numden[:, 0:dv]
            den = numden[:, dv:2 * dv]
            hh = num / jnp.maximum(jnp.abs(den), jnp.exp(-m_t))
            m_new = m_t[CHUNK - 1:CHUNK, :]
            b_last = bcol[CHUNK - 1:CHUNK, :]
            w_state = jnp.exp(b_last - bcol + ig_col - m_new)
            decay = jnp.exp(b_last + m_prev - m_new)
            c_ref[h] = decay * c_ext + lax.dot_general((k_f * w_state).astype(BF16), v_ext, TN_DIMS,
                                                       preferred_element_type=F32)
            m_ref[h:h + 1, :] = jnp.broadcast_to(m_new, (1, LANE))
            ms = jnp.mean(hh * hh, axis=1, keepdims=True)
            gate = _sigmoid(og_ref[rows, h * dv:(h + 1) * dv])
            o_ref[rows, h * dv:(h + 1) * dv] = (hh * lax.rsqrt(ms + EPS) * hn * gate).astype(o_ref.dtype)
        return carry

    lax.fori_loop(0, nchunk, body, 0)


def _mlstm(proj, tail, conv_w, gate_bias, hnorm, *, batch, seq, group, g_off):
    m = batch * seq
    nh = group // HEAD_DIM
    dqk = HEAD_DIM // 2
    dv = HEAD_DIM
    t = _pick_tile(seq, (256, 128, 64))
    nt = seq // t
    tri3, trit3 = _mlstm_constants()
    qk_blk = 4
    gates_row = (tail[:, g_off:g_off + 2 * nh].reshape(batch, seq // CHUNK, CHUNK, 2 * nh)
                 .transpose(0, 1, 3, 2))
    bias = gate_bias.astype(F32)
    bias_lane = jnp.zeros((1, LANE), F32).at[0, g_off:g_off + 2 * nh].set(bias)

    def wide(off):
        return pl.BlockSpec((t, group), lambda b, i: (b * nt + i, off))

    def const(shape):
        return pl.BlockSpec(shape, lambda b, i: tuple(0 for _ in shape))

    return pl.pallas_call(
        functools.partial(_mlstm_kernel, nchunk=t // CHUNK, nh=nh, dqk=dqk, dv=dv, t=t, g_off=g_off),
        grid=(batch, nt),
        in_specs=[wide(qk_blk), wide(qk_blk + 1), wide(qk_blk + 2),
                  pl.BlockSpec((t, LANE), lambda b, i: (b * nt + i, 0)),
                  pl.BlockSpec((None, t // CHUNK, 2 * nh, CHUNK), lambda b, i: (b, i, 0, 0)),
                  const((CONV_W, group)), const((1, LANE)), const((2 * nh, 1)), const((1, dv)),
                  const(tri3.shape), const(trit3.shape)],
        out_specs=pl.BlockSpec((t, group), lambda b, i: (b * nt + i, 0)),
        out_shape=jax.ShapeDtypeStruct((m, group), BF16),
        scratch_shapes=[pltpu.VMEM((t + _CONV_PAD, group), F32),
                        pltpu.VMEM((t, group), F32),
                        pltpu.VMEM((nh, dqk, 2 * dv), F32),
                        pltpu.VMEM((nh, LANE), F32)],
        compiler_params=_params("parallel", "arbitrary"),
        name="mlstm",
    )(proj, proj, proj, tail, gates_row, conv_w.astype(F32), bias_lane, bias.reshape(2 * nh, 1),
      hnorm.reshape(1, dv).astype(F32), tri3, trit3)


def _ki_kernel(x_ref, g_ref, o_ref):
    x = x_ref[:, 0:IDX_DIM]
    ms = jnp.mean(x * x, axis=-1, keepdims=True)
    o_ref[...] = (x * lax.rsqrt(ms + EPS) * g_ref[...]).astype(o_ref.dtype)


def _ki_norm(tail, g):
    m = tail.shape[0]
    tm = _pick_tile(m, (512, 256, 128, 64))
    return pl.pallas_call(
        _ki_kernel,
        grid=(m // tm,),
        in_specs=[pl.BlockSpec((tm, LANE), lambda i: (i, 0)), pl.BlockSpec((1, IDX_DIM), lambda i: (0, 0))],
        out_specs=pl.BlockSpec((tm, IDX_DIM), lambda i: (i, 0)),
        out_shape=jax.ShapeDtypeStruct((m, IDX_DIM), BF16),
        compiler_params=_params("parallel"),
        name="ki_norm",
    )(tail, g.reshape(1, IDX_DIM).astype(F32))


def _diff_kernel(q_ref, k_ref, vt_ref, lam_ref, sn_ref, o_ref, m_ref, l_ref, acc_ref, *, tq, tk, dqk, out_scale):
    qi = pl.program_id(2)
    last_k = ((qi + 1) * tq - 1) // tk
    m_ref[...] = jnp.full_like(m_ref, NEG)
    l_ref[...] = jnp.zeros_like(l_ref)
    acc_ref[...] = jnp.zeros_like(acc_ref)
    q = q_ref[...]

    def step(kb, masked):
        rows = pl.ds(pl.multiple_of(kb * tk, tk), tk)
        k = k_ref[rows, :]
        vt = vt_ref[kb]
        if masked:
            k_chunk = (kb * tk + lax.broadcasted_iota(jnp.int32, (tk, tq), 0)) // CHUNK
            q_chunk = (qi * tq + lax.broadcasted_iota(jnp.int32, (tk, tq), 1)) // CHUNK
            mask = k_chunk <= q_chunk
        for half in range(2):
            s = lax.dot_general(k[:, half * dqk:(half + 1) * dqk], q[:, half * dqk:(half + 1) * dqk], NT_DIMS,
                                preferred_element_type=F32)
            if masked:
                s = jnp.where(mask, s, NEG)
            m_prev = m_ref[half]
            m_new = jnp.maximum(m_prev, jnp.max(s, axis=0, keepdims=True))
            p = jnp.exp2(s - m_new)
            alpha = jnp.exp2(m_prev - m_new)
            l_ref[half] = alpha * l_ref[half] + jnp.sum(p, axis=0, keepdims=True)
            acc_ref[half] = alpha * acc_ref[half] + jnp.dot(vt, p.astype(BF16), preferred_element_type=F32)
            m_ref[half] = m_new

    def body(kb, carry):
        step(kb, False)
        return carry

    lax.fori_loop(0, last_k, body, 0)
    step(last_k, True)

    o = acc_ref[0] / l_ref[0] - lam_ref[...] * (acc_ref[1] / l_ref[1])
    ms = jnp.mean(o * o, axis=0, keepdims=True)
    o_ref[...] = (o * lax.rsqrt(ms + EPS) * (sn_ref[...] * out_scale)).T.astype(o_ref.dtype)


def _diff_attention(cd, lam_full, subln, lambda_init, *, batch, seq, group):
    m = batch * seq
    dqk = HEAD_DIM
    dv = 2 * HEAD_DIM
    nh = group // dv
    tq = _pick_tile(seq, (1024, 512, 256, 128))
    tk = _pick_tile(seq, (1024, 512, 256, 128))
    assert tk % tq == 0
    nq, nkb = seq // tq, seq // tk
    k_off = nh
    v_t = cd[:, 2 * group:3 * group].reshape(batch, nkb, tk, nh, dv).transpose(0, 1, 3, 4, 2)

    return pl.pallas_call(
        functools.partial(_diff_kernel, tq=tq, tk=tk, dqk=dqk, out_scale=1.0 - lambda_init),
        grid=(batch, nh, nq),
        in_specs=[pl.BlockSpec((tq, dv), lambda b, h, i: (b * nq + i, h)),
                  pl.BlockSpec((seq, dv), lambda b, h, i: (b, k_off + h)),
                  pl.BlockSpec((None, nkb, None, dv, tk), lambda b, h, i: (b, 0, h, 0, 0)),
                  pl.BlockSpec((1, 1), lambda b, h, i: (0, 0)),
                  pl.BlockSpec((dv, 1), lambda b, h, i: (0, 0))],
        out_specs=pl.BlockSpec((tq, dv), lambda b, h, i: (b * nq + i, h)),
        out_shape=jax.ShapeDtypeStruct((m, group), BF16),
        scratch_shapes=[pltpu.VMEM((2, 1, tq), F32), pltpu.VMEM((2, 1, tq), F32), pltpu.VMEM((2, dv, tq), F32)],
        compiler_params=_params("parallel", "parallel", "arbitrary"),
        name="diff_attn",
    )(cd, cd, v_t, lam_full.reshape(1, 1).astype(F32), subln.reshape(dv, 1).astype(F32))


_INT_MIN = -2 ** 31


_CNT_ROWS = 64
_DSA_HEADS_PER_STEP = 2


def _dsa_kernel(q_ref, k_ref, vt_ref, qi_ref, ki_ref, wt_ref, o_ref, sc_ref, *, tq, tk, topk, hpb, w_scale):
    qb = pl.program_id(1)
    h = pl.program_id(2)
    nkv = ((qb + 1) * tq + tk - 1) // tk

    @pl.when(h == 0)
    def _():
        w = wt_ref[...] * w_scale
        q_chunk = (qb * tq + lax.broadcasted_iota(jnp.int32, (tk, tq), 1)) // CHUNK

        def score_body(kb, carry):
            rows = pl.ds(pl.multiple_of(kb * tk, tk), tk)
            ki = ki_ref[rows, :]
            acc = jnp.zeros((tk, tq), F32)
            for ih in range(IDX_HEADS):
                lg = lax.dot_general(ki, qi_ref[:, ih * IDX_DIM:(ih + 1) * IDX_DIM], NT_DIMS,
                                     preferred_element_type=F32)
                acc = acc + w[ih:ih + 1, :] * jnp.maximum(lg, 0.0)
            k_chunk = (kb * tk + lax.broadcasted_iota(jnp.int32, (tk, tq), 0)) // CHUNK
            sc_ref[kb] = jnp.where(k_chunk <= q_chunk, acc, NEG)
            return carry

        lax.fori_loop(0, nkv, score_body, 0)

        def bit_body(i, u):
            cand = u | jnp.left_shift(jnp.int32(1), 31 - i)
            key = cand ^ _INT_MIN
            bits = key ^ (jnp.right_shift(key, 31) & 0x7FFFFFFF)
            cf = lax.bitcast_convert_type(bits, F32)

            def cnt_body(kb, cnt):
                for r in range(tk // _CNT_ROWS):
                    hit = jnp.where(sc_ref[kb, r * _CNT_ROWS:(r + 1) * _CNT_ROWS, :] >= cf, 1.0, 0.0)
                    for g in range(_CNT_ROWS // 8):
                        cnt = cnt + hit[g * 8:(g + 1) * 8, :]
                return cnt

            cnt = jnp.sum(lax.fori_loop(0, nkv, cnt_body, jnp.zeros((8, tq), F32)), axis=0, keepdims=True)
            return jnp.where(cnt >= float(topk), cand, u)

        u = lax.fori_loop(0, 32, bit_body, jnp.zeros((1, tq), jnp.int32))
        key = u ^ _INT_MIN
        thr = lax.bitcast_convert_type(key ^ (jnp.right_shift(key, 31) & 0x7FFFFFFF), F32)

        def bias_body(kb, carry):
            s = sc_ref[kb]
            sc_ref[kb] = jnp.where((s >= thr) & (s > 0.5 * NEG), 0.0, NEG)
            return carry

        lax.fori_loop(0, nkv, bias_body, 0)

    q = q_ref[...]

    def kv_body(kb, carry):
        rows = pl.ds(pl.multiple_of(kb * tk, tk), tk)
        k = k_ref[rows, :]
        bias = sc_ref[kb]
        out = []
        for a in range(hpb):
            m_prev, l_prev, acc = carry[a]
            hs = slice(a * HEAD_DIM, (a + 1) * HEAD_DIM)
            s = lax.dot_general(k[:, hs], q[:, hs], NT_DIMS, preferred_element_type=F32) + bias
            m_new = jnp.maximum(m_prev, jnp.max(s, axis=0, keepdims=True))
            p = jnp.exp2(s - m_new)
            alpha = jnp.exp2(m_prev - m_new)
            l_new = alpha * l_prev + jnp.sum(p, axis=0, keepdims=True)
            acc = alpha * acc + jnp.dot(vt_ref[kb, a], p.astype(BF16), preferred_element_type=F32)
            out.append((m_new, l_new, acc))
        return tuple(out)

    init = tuple((jnp.full((1, tq), NEG, F32), jnp.zeros((1, tq), F32), jnp.zeros((HEAD_DIM, tq), F32))
                 for _ in range(hpb))
    fin = lax.fori_loop(0, nkv, kv_body, init)
    for a in range(hpb):
        _, l_fin, acc = fin[a]
        o_ref[:, a * HEAD_DIM:(a + 1) * HEAD_DIM] = (acc / l_fin).T.astype(o_ref.dtype)


def _dsa_attention(cd, ki_n, tail, *, batch, seq, group, w_off):
    m = batch * seq
    nh = group // HEAD_DIM
    tq = _pick_tile(seq, (512, 256, 128))
    tk = _pick_tile(seq, (1024, 512, 256))
    topk = min(TOPK_MAX, seq // 4)
    assert tk >= topk, "the threshold search needs at least topk keys in range"
    nq = seq // tq
    nc = group // LANE
    nkb = seq // tk
    q_off, k_off, v_off = 3 * nc, 4 * nc, 5 * nc
    qi_off = 6 * nc * LANE // (IDX_HEADS * IDX_DIM)
    assert qi_off * IDX_HEADS * IDX_DIM == 6 * nc * LANE
    v_t = (cd[:, v_off * LANE:(v_off + nc) * LANE].reshape(batch, nkb, tk, nh, HEAD_DIM)
           .transpose(0, 1, 3, 4, 2))
    w_t = tail[:, w_off:w_off + IDX_HEADS].reshape(batch, seq, IDX_HEADS).transpose(0, 2, 1)

    hpb = _DSA_HEADS_PER_STEP
    assert nh % hpb == 0
    wide = hpb * HEAD_DIM
    return pl.pallas_call(
        functools.partial(_dsa_kernel, tq=tq, tk=tk, topk=topk, hpb=hpb,
                          w_scale=IDX_HEADS ** -0.5 * IDX_DIM ** -0.5),
        grid=(batch, nq, nh // hpb),
        in_specs=[pl.BlockSpec((tq, wide), lambda b, i, h: (b * nq + i, q_off // hpb + h)),
                  pl.BlockSpec((seq, wide), lambda b, i, h: (b, k_off // hpb + h)),
                  pl.BlockSpec((None, nkb, hpb, HEAD_DIM, tk), lambda b, i, h: (b, 0, h, 0, 0)),
                  pl.BlockSpec((tq, IDX_HEADS * IDX_DIM), lambda b, i, h: (b * nq + i, qi_off)),
                  pl.BlockSpec((seq, IDX_DIM), lambda b, i, h: (b, 0)),
                  pl.BlockSpec((None, IDX_HEADS, tq), lambda b, i, h: (b, 0, i))],
        out_specs=pl.BlockSpec((tq, wide), lambda b, i, h: (b * nq + i, h)),
        out_shape=jax.ShapeDtypeStruct((m, group), BF16),
        scratch_shapes=[pltpu.VMEM((nkb, tk, tq), F32)],
        compiler_params=_params("parallel", "arbitrary", "arbitrary"),
        name="dsa",
    )(cd, cd, v_t, cd, ki_n, w_t)


def _rearrange_w_in(w, group, nhb):
    d = w.shape[0]
    p0 = 7 * group
    p1 = p0 + 2 * nhb
    p2 = p1 + 6 * group + IDX_HEADS * IDX_DIM
    used = IDX_DIM + 2 * nhb + IDX_HEADS
    tail = jnp.concatenate([w[:, p2:p2 + IDX_DIM], w[:, p0:p1], w[:, p2 + IDX_DIM:],
                            jnp.zeros((d, LANE - used), w.dtype)], axis=1)
    return w[:, p1:p2].astype(BF16), tail.astype(BF16)


def kernel(x, norm_mix, w_in, hgrn_lb_logits, hgrn_gnorm, mlstm_conv, mlstm_gate_bias, mlstm_hnorm,
           diff_qk_norm, diff_lambda, diff_subln, dsa_qk_norm, dsa_idx_knorm, w_out, norm_mlp, w1, w2):
    batch, seq, d_model = x.shape
    depth = w_in.shape[0]
    group = d_model // 4
    nhb = group // HEAD_DIM
    nc = group // LANE
    m = batch * seq
    dims = dict(batch=batch, seq=seq, group=group)

    p_lb = jax.nn.softmax(hgrn_lb_logits.astype(F32), axis=0)
    lower_bounds = jnp.cumsum(p_lb, axis=0) - p_lb[0:1]

    w_in_b, w_out_b, w1_b, w2_b = (w.astype(BF16) for w in (w_in, w_out, w1, w2))

    x2 = x.reshape(m, d_model).astype(F32)
    ones_g = jnp.ones((group,), F32)
    for l in range(depth):
        lambda_init = 0.8 - 0.6 * math.exp(-0.3 * l)
        w_cd, w_tail = _rearrange_w_in(w_in_b[l], group, nhb)
        h = _rmsnorm_rows(x2, norm_mix[l])
        proj = _matmul(h, w_in_b, layer=l, n_cols=7 * group, out_dtype=F32, name="in_proj_ab")
        tail = _matmul(h, w_tail, out_dtype=F32, name="in_proj_tail")

        y_a = _hgrn(proj, lower_bounds[l], hgrn_gnorm[l], **dims)

        y_b = _mlstm(proj, tail, mlstm_conv[l], mlstm_gate_bias[l], mlstm_hnorm[l], g_off=IDX_DIM, **dims)

        cqn = diff_qk_norm[l].astype(F32)
        dqn = dsa_qk_norm[l].astype(F32)
        q_scale = HEAD_DIM ** -0.5 * LOG2_E
        gains = jnp.concatenate([
            jnp.tile(cqn[0], nc) * q_scale, jnp.tile(cqn[1], nc), ones_g,
            jnp.tile(dqn[0], nc) * q_scale, jnp.tile(dqn[1], nc), ones_g,
            jnp.ones((IDX_HEADS * IDX_DIM,), F32)]).reshape(1, -1)
        use_norm = jnp.concatenate([
            jnp.ones((2 * group,), F32), jnp.zeros((group,), F32),
            jnp.ones((2 * group,), F32), jnp.zeros((group + IDX_HEADS * IDX_DIM,), F32)]).reshape(1, -1)
        cd = _matmul(h, w_cd, out_dtype=BF16, act="groupnorm", extra=(gains, use_norm), name="in_proj_cd")
        ki_n = _ki_norm(tail, dsa_idx_knorm[l])

        lam = diff_lambda[l].astype(F32)
        lam_full = jnp.exp(jnp.sum(lam[0] * lam[1])) - jnp.exp(jnp.sum(lam[2] * lam[3])) + lambda_init
        y_c = _diff_attention(cd, lam_full, diff_subln[l], lambda_init, **dims)
        y_d = _dsa_attention(cd, ki_n, tail, w_off=IDX_DIM + 2 * nhb, **dims)

        x2 = _matmul((y_a, y_b, y_c, y_d), w_out_b, layer=l, out_dtype=F32, act="residual", extra=(x2,),
                     tm_max=512, name="out_proj")
        h = _rmsnorm_rows(x2, norm_mlp[l])
        hidden = _matmul(h, w1_b, layer=l, out_dtype=BF16, act="relu2", name="mlp_up")
        x2 = _matmul(hidden, w2_b, layer=l, out_dtype=F32, act="residual", extra=(x2,), tk_max=2048,
                     name="mlp_down")
    return x2.reshape(batch, seq, d_model).astype(x.dtype)
```

```python
import functools
import math

import numpy as np
import jax
import jax.numpy as jnp
from jax import lax
from jax.experimental import pallas as pl
from jax.experimental.pallas import tpu as pltpu

CHUNK = 64
EPS = 1e-6
NEG = -1e30
F_FLOOR = 1e-20
HEAD_DIM = 128
CONV_W = 4
IDX_HEADS = 16
IDX_DIM = 64
TOPK_MAX = 256
LOG2_E = math.log2(math.e)
LANE = 128
VMEM_LIMIT = 56 * 1024 * 1024

F32 = jnp.float32
BF16 = jnp.bfloat16
NT_DIMS = (((1,), (1,)), ((), ()))
TN_DIMS = (((0,), (0,)), ((), ()))


def _params(*sem):
    return pltpu.CompilerParams(dimension_semantics=sem, vmem_limit_bytes=VMEM_LIMIT)


def _sigmoid(x):
    return 1.0 / (1.0 + jnp.exp(-x))


def _log_sigmoid(x):
    return jnp.minimum(x, 0.0) - jnp.log1p(jnp.exp(-jnp.abs(x)))


def _split3(x):
    hi = x.astype(BF16)
    r1 = x - hi.astype(F32)
    mid = r1.astype(BF16)
    lo = (r1 - mid.astype(F32)).astype(BF16)
    return hi, mid, lo


def _pick_tile(n, candidates):
    for c in candidates:
        if n % c == 0:
            return c
    raise ValueError(f"no tile in {candidates} divides {n}")


def _rmsnorm_kernel(x_ref, g_ref, o_ref):
    x = x_ref[...]
    ms = jnp.mean(x * x, axis=-1, keepdims=True)
    o_ref[...] = (x * lax.rsqrt(ms + EPS) * g_ref[...]).astype(o_ref.dtype)


def _rmsnorm_rows(x2, g):
    m, d = x2.shape
    tm = _pick_tile(m, (256, 128, 64, 8))
    return pl.pallas_call(
        _rmsnorm_kernel,
        grid=(m // tm,),
        in_specs=[pl.BlockSpec((tm, d), lambda i: (i, 0)), pl.BlockSpec((1, d), lambda i: (0, 0))],
        out_specs=pl.BlockSpec((tm, d), lambda i: (i, 0)),
        out_shape=jax.ShapeDtypeStruct((m, d), BF16),
        compiler_params=_params("parallel"),
        name="rmsnorm",
    )(x2, g.reshape(1, d).astype(F32))


MM_MAX_TK = 4096
MM_TN = 1024


def _mm_epilogue(r, act, extra):
    if act == "relu2":
        r = jnp.square(jnp.maximum(r, 0.0))
    elif act == "groupnorm":
        g_ref, u_ref = extra
        segs = []
        for c in range(r.shape[1] // LANE):
            seg = r[:, c * LANE:(c + 1) * LANE]
            ms = jnp.mean(seg * seg, axis=1, keepdims=True)
            inv = jnp.where(u_ref[:, c * LANE:(c + 1) * LANE] > 0.0, lax.rsqrt(ms + EPS), 1.0)
            segs.append(seg * inv * g_ref[:, c * LANE:(c + 1) * LANE])
        r = jnp.concatenate(segs, axis=1)
    elif act == "residual":
        r = r + extra[0][...]
    return r


def _mm_kernel(*refs, n_a, nk, act, n_extra):
    a_refs = refs[:n_a]
    b_ref = refs[n_a]
    extra = refs[n_a + 1:n_a + 1 + n_extra]
    o_ref = refs[n_a + 1 + n_extra]

    def product():
        r, off = None, 0
        for a_ref in a_refs:
            kw = a_ref.shape[1]
            d = jnp.dot(a_ref[...], b_ref[off:off + kw, :], preferred_element_type=F32)
            r = d if r is None else r + d
            off += kw
        return r

    if nk == 1:
        o_ref[...] = _mm_epilogue(product(), act, extra).astype(o_ref.dtype)
        return
    acc_ref = refs[n_a + 2 + n_extra]
    k = pl.program_id(2)

    @pl.when(k == 0)
    def _():
        acc_ref[...] = jnp.zeros_like(acc_ref)

    acc_ref[...] += product()

    @pl.when(k == nk - 1)
    def _():
        o_ref[...] = _mm_epilogue(acc_ref[...], act, extra).astype(o_ref.dtype)


def _matmul(a, b, *, out_dtype, layer=None, act=None, extra=(), tm_max=1024, tn_max=MM_TN,
            tk_max=MM_MAX_TK, name="matmul"):
    a_parts = a if isinstance(a, tuple) else (a,)
    m = a_parts[0].shape[0]
    kdim = sum(p.shape[1] for p in a_parts)
    n = b.shape[-1]
    tm = _pick_tile(m, tuple(t for t in (1024, 512, 256, 128, 8) if t <= tm_max))
    tn = _pick_tile(n, tuple(t for t in (1024, 512, 256, 128) if t <= tn_max))
    tk = kdim if kdim <= tk_max else _pick_tile(kdim, tuple(t for t in (4096, 2048, 1024, 512, 256, 128)
                                                            if t <= tk_max))
    nk = kdim // tk
    assert nk == 1 or len(a_parts) == 1
    in_specs = [pl.BlockSpec((tm, tk if nk > 1 else p.shape[1]), lambda i, j, k: (i, k)) for p in a_parts]
    if layer is None:
        in_specs.append(pl.BlockSpec((tk, tn), lambda i, j, k: (k, j)))
    else:
        in_specs.append(pl.BlockSpec((None, tk, tn), lambda i, j, k: (layer, k, j)))
    for e in extra:
        if e.shape[0] == 1:
            in_specs.append(pl.BlockSpec((1, tn), lambda i, j, k: (0, j)))
        else:
            in_specs.append(pl.BlockSpec((tm, tn), lambda i, j, k: (i, j)))
    return pl.pallas_call(
        functools.partial(_mm_kernel, n_a=len(a_parts), nk=nk, act=act, n_extra=len(extra)),
        grid=(m // tm, n // tn, nk),
        in_specs=in_specs,
        out_specs=pl.BlockSpec((tm, tn), lambda i, j, k: (i, j)),
        out_shape=jax.ShapeDtypeStruct((m, n), out_dtype),
        scratch_shapes=[pltpu.VMEM((tm, tn), F32)] if nk > 1 else [],
        compiler_params=_params("parallel", "parallel", "arbitrary"),
        name=name,
    )(*a_parts, b, *extra)


_HGRN_LEVELS = 6


def _hgrn_constants():
    idx = np.arange(CHUNK)
    j = idx[None, :]
    r = idx[:, None]
    blocks = [j <= r, j > r]
    masks = []
    for lvl in range(_HGRN_LEVELS):
        half = (CHUNK // 2) >> lvl
        parent = idx // (2 * half)
        mid = parent * 2 * half + half
        is_right = (idx % (2 * half)) >= half
        right_rng = (j >= mid[:, None]) & (j <= r)
        left_rng = (j > r) & (j < mid[:, None])
        blocks.append(np.where(is_right[:, None], right_rng, left_rng))
        masks.append((parent[:, None] == parent[None, :]) & is_right[:, None] & (~is_right)[None, :])
    cm = np.concatenate(blocks, axis=0).astype(np.float32)
    cm3 = np.concatenate([cm, cm, cm, np.zeros_like(cm)], axis=1)
    mk = np.stack(masks, axis=0).astype(np.float32)
    return jnp.asarray(cm3, dtype=BF16), jnp.asarray(mk, dtype=F32)


def _hgrn_kernel(q_ref, f_ref, i_ref, g_ref, lb_ref, gn_ref, cm_ref, mk_ref, o_ref, st_ref, *, nchunk, scale):
    @pl.when(pl.program_id(2) == 0)
    def _():
        st_ref[...] = jnp.zeros_like(st_ref)

    lb = lb_ref[...]
    gn = gn_ref[...]

    def body(c, carry):
        rows = pl.ds(pl.multiple_of(c * CHUNK, CHUNK), CHUNK)
        f = lb + (1.0 - lb) * _sigmoid(f_ref[rows, :])
        logf = jnp.log(jnp.maximum(f, F_FLOOR))
        kk = 1.0 - f
        qr = q_ref[rows, :]
        qq = qr * _sigmoid(qr) * scale
        v = i_ref[rows, :]
        vb = v.astype(BF16)
        l3 = jnp.concatenate(_split3(logf) + (jnp.zeros_like(vb),), axis=0)
        sums = jnp.dot(cm_ref[...], l3, preferred_element_type=F32)
        b = sums[0:CHUNK]
        b_rest = sums[CHUNK:2 * CHUNK]
        scores = jnp.zeros((CHUNK, CHUNK), F32)
        for lvl in range(_HGRN_LEVELS):
            e = jnp.exp(sums[(2 + lvl) * CHUNK:(3 + lvl) * CHUNK])
            s = lax.dot_general((qq * e).astype(BF16), (kk * e).astype(BF16), NT_DIMS,
                                preferred_element_type=F32)
            scores = scores + s * mk_ref[lvl]
        diag = jnp.sum(qq * kk, axis=1, keepdims=True)
        st = st_ref[...]
        o = (jnp.dot(scores.astype(BF16), vb, preferred_element_type=F32) + diag * v
             + lax.dot_general((qq * jnp.exp(b)).astype(BF16), st.astype(BF16), NT_DIMS,
                               preferred_element_type=F32))
        ke = (kk * jnp.exp(b_rest)).astype(BF16)
        st_ref[...] = (st * jnp.exp(b[CHUNK - 1:CHUNK, :])
                       + lax.dot_general(vb, ke, TN_DIMS, preferred_element_type=F32))
        ms = jnp.mean(o * o, axis=1, keepdims=True)
        gr = g_ref[rows, :]
        y = o * lax.rsqrt(ms + EPS) * gn * (gr * _sigmoid(gr))
        o_ref[rows, :] = y.astype(o_ref.dtype)
        return carry

    lax.fori_loop(0, nchunk, body, 0, unroll=True)


def _hgrn(proj, lb, gnorm, *, batch, seq, group):
    m = batch * seq
    nh = group // HEAD_DIM
    t = _pick_tile(seq, (512, 256, 128, 64))
    nt = seq // t
    cm3, mk = _hgrn_constants()

    def col(off):
        return pl.BlockSpec((t, HEAD_DIM), lambda b, h, i: (b * nt + i, off + h))

    return pl.pallas_call(
        functools.partial(_hgrn_kernel, nchunk=t // CHUNK, scale=HEAD_DIM ** -0.5),
        grid=(batch, nh, nt),
        in_specs=[col(0), col(nh), col(2 * nh), col(3 * nh),
                  pl.BlockSpec((1, HEAD_DIM), lambda b, h, i: (0, h)),
                  pl.BlockSpec((1, HEAD_DIM), lambda b, h, i: (0, 0)),
                  pl.BlockSpec(cm3.shape, lambda b, h, i: (0, 0)),
                  pl.BlockSpec(mk.shape, lambda b, h, i: (0, 0, 0))],
        out_specs=pl.BlockSpec((t, HEAD_DIM), lambda b, h, i: (b * nt + i, h)),
        out_shape=jax.ShapeDtypeStruct((m, group), BF16),
        scratch_shapes=[pltpu.VMEM((HEAD_DIM, HEAD_DIM), F32)],
        compiler_params=_params("parallel", "parallel", "arbitrary"),
        name="hgrn2",
    )(proj, proj, proj, proj, lb.reshape(1, group).astype(F32), gnorm.reshape(1, HEAD_DIM).astype(F32), cm3, mk)


_CONV_PAD = 8


def _mlstm_constants():
    idx = np.arange(CHUNK)
    tri = (idx[None, :] <= idx[:, None]).astype(np.float32)
    zero = np.zeros_like(tri)
    tri3 = np.concatenate([tri, tri, tri, zero], axis=1)
    trit3 = np.concatenate([tri.T, tri.T, tri.T, zero], axis=0)
    trit3 = np.concatenate([trit3, np.zeros_like(trit3)], axis=1)
    return jnp.asarray(tri3, dtype=BF16), jnp.asarray(trit3, dtype=BF16)


def _mlstm_kernel(qk_ref, v_ref, og_ref, gc_ref, gr_ref, cw_ref, bc_ref, br_ref, hn_ref, tri_ref, trit_ref,
                  o_ref, xp_ref, qk_s, c_ref, m_ref, *, nchunk, nh, dqk, dv, t, g_off):
    @pl.when(pl.program_id(1) == 0)
    def _():
        xp_ref[0:_CONV_PAD, :] = jnp.zeros((_CONV_PAD, xp_ref.shape[1]), F32)
        c_ref[...] = jnp.zeros_like(c_ref)
        m_ref[...] = jnp.zeros_like(m_ref)

    xp_ref[_CONV_PAD:_CONV_PAD + t, :] = qk_ref[...]
    conv = jnp.zeros((t, 2 * nh * dqk), F32)
    for j in range(CONV_W):
        start = _CONV_PAD - (CONV_W - 1) + j
        conv = conv + cw_ref[j:j + 1, :] * xp_ref[start:start + t, :]
    qk_s[...] = conv * _sigmoid(conv)
    xp_ref[0:_CONV_PAD, :] = xp_ref[t:t + _CONV_PAD, :]

    hn = hn_ref[...]
    kscale = dqk ** -0.5
    r_i = lax.broadcasted_iota(jnp.int32, (CHUNK, CHUNK), 0)
    c_i = lax.broadcasted_iota(jnp.int32, (CHUNK, CHUNK), 1)
    causal = c_i <= r_i
    ones = jnp.ones((CHUNK, dv), BF16)
    zpad_c = jnp.zeros((CHUNK, LANE), BF16)
    zpad_r = jnp.zeros((2 * nh, CHUNK), BF16)

    def body(c, carry):
        base = pl.multiple_of(c * CHUNK, CHUNK)
        rows = pl.ds(base, CHUNK)
        qk = qk_s[rows, :]
        gcol = gc_ref[rows, :] + bc_ref[...]
        b_c = jnp.dot(tri_ref[...], jnp.concatenate(_split3(_log_sigmoid(gcol)) + (zpad_c,), axis=0),
                      preferred_element_type=F32)
        grow = gr_ref[c] + br_ref[...]
        ig_r = grow[0:nh, :]
        b_r = jnp.dot(jnp.concatenate(_split3(_log_sigmoid(grow)) + (zpad_r,), axis=1), trit_ref[...],
                      preferred_element_type=F32)[nh:2 * nh, 0:CHUNK]
        for h in range(nh):
            ig_col = gcol[:, g_off + h:g_off + h + 1]
            q_h = qk[:, h * dqk:(h + 1) * dqk].astype(BF16)
            k_f = qk[:, (nh + h) * dqk:(nh + h + 1) * dqk] * kscale
            k_h = k_f.astype(BF16)
            v_ext = jnp.concatenate([v_ref[rows, h * dv:(h + 1) * dv].astype(BF16), ones], axis=1)
            bcol = b_c[:, g_off + nh + h:g_off + nh + h + 1]
            m_prev = m_ref[h:h + 1, 0:1]
            log_d = jnp.where(causal, bcol - b_r[h:h + 1, :] + ig_r[h:h + 1, :], NEG)
            log_inter = bcol + m_prev
            m_t = jnp.maximum(jnp.max(log_d, axis=1, keepdims=True), log_inter)
            qk_w = (lax.dot_general(q_h, k_h, NT_DIMS, preferred_element_type=F32)
                    * jnp.exp(log_d - m_t))
            w_inter = jnp.exp(log_inter - m_t)
            c_ext = c_ref[h]
            numden = (jnp.dot(qk_w.astype(BF16), v_ext, preferred_element_type=F32)
                      + w_inter * jnp.dot(q_h, c_ext.astype(BF16), preferred_element_type=F32))
            num = numden[:, 0:dv]
            den = numden[:, dv:2 * dv]
            hh = num / jnp.maximum(jnp.abs(den), jnp.exp(-m_t))
            m_new = m_t[CHUNK - 1:CHUNK, :]
            b_last = bcol[CHUNK - 1:CHUNK, :]
            w_state = jnp.exp(b_last - bcol + ig_col - m_new)
            decay = jnp.exp(b_last + m_prev - m_new)
            c_ref[h] = decay * c_ext + lax.dot_general((k_f * w_state).astype(BF16), v_ext, TN_DIMS,
                                                       preferred_element_type=F32)
            m_ref[h:h + 1, :] = jnp.broadcast_to(m_new, (1, LANE))
            ms = jnp.mean(hh * hh, axis=1, keepdims=True)
            gate = _sigmoid(og_ref[rows, h * dv:(h + 1) * dv])
            o_ref[rows, h * dv:(h + 1) * dv] = (hh * lax.rsqrt(ms + EPS) * hn * gate).astype(o_ref.dtype)
        return carry

    lax.fori_loop(0, nchunk, body, 0)


def _mlstm(proj, tail, conv_w, gate_bias, hnorm, *, batch, seq, group, g_off):
    m = batch * seq
    nh = group // HEAD_DIM
    dqk = HEAD_DIM // 2
    dv = HEAD_DIM
    t = _pick_tile(seq, (256, 128, 64))
    nt = seq // t
    tri3, trit3 = _mlstm_constants()
    qk_blk = 4
    gates_row = (tail[:, g_off:g_off + 2 * nh].reshape(batch, seq // CHUNK, CHUNK, 2 * nh)
                 .transpose(0, 1, 3, 2))
    bias = gate_bias.astype(F32)
    bias_lane = jnp.zeros((1, LANE), F32).at[0, g_off:g_off + 2 * nh].set(bias)

    def wide(off):
        return pl.BlockSpec((t, group), lambda b, i: (b * nt + i, off))

    def const(shape):
        return pl.BlockSpec(shape, lambda b, i: tuple(0 for _ in shape))

    return pl.pallas_call(
        functools.partial(_mlstm_kernel, nchunk=t // CHUNK, nh=nh, dqk=dqk, dv=dv, t=t, g_off=g_off),
        grid=(batch, nt),
        in_specs=[wide(qk_blk), wide(qk_blk + 1), wide(qk_blk + 2),
                  pl.BlockSpec((t, LANE), lambda b, i: (b * nt + i, 0)),
                  pl.BlockSpec((None, t // CHUNK, 2 * nh, CHUNK), lambda b, i: (b, i, 0, 0)),
                  const((CONV_W, group)), const((1, LANE)), const((2 * nh, 1)), const((1, dv)),
                  const(tri3.shape), const(trit3.shape)],
        out_specs=pl.BlockSpec((t, group), lambda b, i: (b * nt + i, 0)),
        out_shape=jax.ShapeDtypeStruct((m, group), BF16),
        scratch_shapes=[pltpu.VMEM((t + _CONV_PAD, group), F32),
                        pltpu.VMEM((t, group), F32),
                        pltpu.VMEM((nh, dqk, 2 * dv), F32),
                        pltpu.VMEM((nh, LANE), F32)],
        compiler_params=_params("parallel", "arbitrary"),
        name="mlstm",
    )(proj, proj, proj, tail, gates_row, conv_w.astype(F32), bias_lane, bias.reshape(2 * nh, 1),
      hnorm.reshape(1, dv).astype(F32), tri3, trit3)


def _ki_kernel(x_ref, g_ref, o_ref):
    x = x_ref[:, 0:IDX_DIM]
    ms = jnp.mean(x * x, axis=-1, keepdims=True)
    o_ref[...] = (x * lax.rsqrt(ms + EPS) * g_ref[...]).astype(o_ref.dtype)


def _ki_norm(tail, g):
    m = tail.shape[0]
    tm = _pick_tile(m, (512, 256, 128, 64))
    return pl.pallas_call(
        _ki_kernel,
        grid=(m // tm,),
        in_specs=[pl.BlockSpec((tm, LANE), lambda i: (i, 0)), pl.BlockSpec((1, IDX_DIM), lambda i: (0, 0))],
        out_specs=pl.BlockSpec((tm, IDX_DIM), lambda i: (i, 0)),
        out_shape=jax.ShapeDtypeStruct((m, IDX_DIM), BF16),
        compiler_params=_params("parallel"),
        name="ki_norm",
    )(tail, g.reshape(1, IDX_DIM).astype(F32))


def _diff_kernel(q_ref, k_ref, vt_ref, lam_ref, sn_ref, o_ref, m_ref, l_ref, acc_ref, *, tq, tk, dqk, out_scale):
    qi = pl.program_id(2)
    last_k = ((qi + 1) * tq - 1) // tk
    m_ref[...] = jnp.full_like(m_ref, NEG)
    l_ref[...] = jnp.zeros_like(l_ref)
    acc_ref[...] = jnp.zeros_like(acc_ref)
    q = q_ref[...]

    def step(kb, masked):
        rows = pl.ds(pl.multiple_of(kb * tk, tk), tk)
        k = k_ref[rows, :]
        vt = vt_ref[kb]
        if masked:
            k_chunk = (kb * tk + lax.broadcasted_iota(jnp.int32, (tk, tq), 0)) // CHUNK
            q_chunk = (qi * tq + lax.broadcasted_iota(jnp.int32, (tk, tq), 1)) // CHUNK
            mask = k_chunk <= q_chunk
        for half in range(2):
            s = lax.dot_general(k[:, half * dqk:(half + 1) * dqk], q[:, half * dqk:(half + 1) * dqk], NT_DIMS,
                                preferred_element_type=F32)
            if masked:
                s = jnp.where(mask, s, NEG)
            m_prev = m_ref[half]
            m_new = jnp.maximum(m_prev, jnp.max(s, axis=0, keepdims=True))
            p = jnp.exp2(s - m_new)
            alpha = jnp.exp2(m_prev - m_new)
            l_ref[half] = alpha * l_ref[half] + jnp.sum(p, axis=0, keepdims=True)
            acc_ref[half] = alpha * acc_ref[half] + jnp.dot(vt, p.astype(BF16), preferred_element_type=F32)
            m_ref[half] = m_new

    def body(kb, carry):
        step(kb, False)
        return carry

    lax.fori_loop(0, last_k, body, 0)
    step(last_k, True)

    o = acc_ref[0] / l_ref[0] - lam_ref[...] * (acc_ref[1] / l_ref[1])
    ms = jnp.mean(o * o, axis=0, keepdims=True)
    o_ref[...] = (o * lax.rsqrt(ms + EPS) * (sn_ref[...] * out_scale)).T.astype(o_ref.dtype)


def _diff_attention(cd, lam_full, subln, lambda_init, *, batch, seq, group):
    m = batch * seq
    dqk = HEAD_DIM
    dv = 2 * HEAD_DIM
    nh = group // dv
    tq = _pick_tile(seq, (1024, 512, 256, 128))
    tk = _pick_tile(seq, (1024, 512, 256, 128))
    assert tk % tq == 0
    nq, nkb = seq // tq, seq // tk
    k_off = nh
    v_t = cd[:, 2 * group:3 * group].reshape(batch, nkb, tk, nh, dv).transpose(0, 1, 3, 4, 2)

    return pl.pallas_call(
        functools.partial(_diff_kernel, tq=tq, tk=tk, dqk=dqk, out_scale=1.0 - lambda_init),
        grid=(batch, nh, nq),
        in_specs=[pl.BlockSpec((tq, dv), lambda b, h, i: (b * nq + i, h)),
                  pl.BlockSpec((seq, dv), lambda b, h, i: (b, k_off + h)),
                  pl.BlockSpec((None, nkb, None, dv, tk), lambda b, h, i: (b, 0, h, 0, 0)),
                  pl.BlockSpec((1, 1), lambda b, h, i: (0, 0)),
                  pl.BlockSpec((dv, 1), lambda b, h, i: (0, 0))],
        out_specs=pl.BlockSpec((tq, dv), lambda b, h, i: (b * nq + i, h)),
        out_shape=jax.ShapeDtypeStruct((m, group), BF16),
        scratch_shapes=[pltpu.VMEM((2, 1, tq), F32), pltpu.VMEM((2, 1, tq), F32), pltpu.VMEM((2, dv, tq), F32)],
        compiler_params=_params("parallel", "parallel", "arbitrary"),
        name="diff_attn",
    )(cd, cd, v_t, lam_full.reshape(1, 1).astype(F32), subln.reshape(dv, 1).astype(F32))


_INT_MIN = -2 ** 31
_HI16 = -(1 << 16)


_CNT_ROWS = 64
_DSA_HEADS_PER_STEP = 2


def _dsa_kernel(q_ref, k_ref, vt_ref, qi_ref, ki_ref, wt_ref, o_ref, sc_ref, sc16_ref, *, tq, tk, topk, hpb,
                w_scale):
    qb = pl.program_id(1)
    h = pl.program_id(2)
    nkv = ((qb + 1) * tq + tk - 1) // tk

    @pl.when(h == 0)
    def _():
        w = wt_ref[...] * w_scale
        q_chunk = (qb * tq + lax.broadcasted_iota(jnp.int32, (tk, tq), 1)) // CHUNK

        def score_body(kb, carry):
            rows = pl.ds(pl.multiple_of(kb * tk, tk), tk)
            ki = ki_ref[rows, :]
            acc = jnp.zeros((tk, tq), F32)
            for ih in range(IDX_HEADS):
                lg = lax.dot_general(ki, qi_ref[:, ih * IDX_DIM:(ih + 1) * IDX_DIM], NT_DIMS,
                                     preferred_element_type=F32)
                acc = acc + w[ih:ih + 1, :] * jnp.maximum(lg, 0.0)
            k_chunk = (kb * tk + lax.broadcasted_iota(jnp.int32, (tk, tq), 0)) // CHUNK
            x = jnp.where(k_chunk <= q_chunk, acc, NEG)
            sc_ref[kb] = x
            hi = lax.bitcast_convert_type(x, jnp.int32) & _HI16
            sc16_ref[kb] = lax.bitcast_convert_type(hi, F32).astype(BF16)
            return carry

        lax.fori_loop(0, nkv, score_body, 0)

        def count_ge(cf):
            def cnt_body(kb, cnt):
                for r in range(tk // _CNT_ROWS):
                    hit = jnp.where(sc_ref[kb, r * _CNT_ROWS:(r + 1) * _CNT_ROWS, :] >= cf, 1.0, 0.0)
                    for g in range(_CNT_ROWS // 8):
                        cnt = cnt + hit[g * 8:(g + 1) * 8, :]
                return cnt

            return jnp.sum(lax.fori_loop(0, nkv, cnt_body, jnp.zeros((8, tq), F32)), axis=0, keepdims=True)

        def count_ge_hi(cfb):
            one, zero = jnp.ones((), BF16), jnp.zeros((), BF16)

            def cnt_body(kb, cnt):
                part = jnp.zeros((16, tq), BF16)
                for r in range(tk // _CNT_ROWS):
                    hit = jnp.where(sc16_ref[kb, r * _CNT_ROWS:(r + 1) * _CNT_ROWS, :] >= cfb, one, zero)
                    for g in range(_CNT_ROWS // 16):
                        part = part + hit[g * 16:(g + 1) * 16, :]
                return cnt + part.astype(F32)

            return jnp.sum(lax.fori_loop(0, nkv, cnt_body, jnp.zeros((16, tq), F32)), axis=0, keepdims=True)

        def bit_body(i, u, *, coarse):
            cand = u | jnp.left_shift(jnp.int32(1), 31 - i)
            key = cand ^ _INT_MIN
            if coarse:
                key = jnp.where(key < 0, key | 0xFFFF, key)
            cf = lax.bitcast_convert_type(key ^ (jnp.right_shift(key, 31) & 0x7FFFFFFF), F32)
            cnt = count_ge_hi(cf.astype(BF16)) if coarse else count_ge(cf)
            return jnp.where(cnt >= float(topk), cand, u)

        u = lax.fori_loop(0, 16, functools.partial(bit_body, coarse=True), jnp.zeros((1, tq), jnp.int32))
        u = lax.fori_loop(16, 32, functools.partial(bit_body, coarse=False), u)
        key = u ^ _INT_MIN
        thr = lax.bitcast_convert_type(key ^ (jnp.right_shift(key, 31) & 0x7FFFFFFF), F32)

        def bias_body(kb, carry):
            s = sc_ref[kb]
            sc_ref[kb] = jnp.where((s >= thr) & (s > 0.5 * NEG), 0.0, NEG)
            return carry

        lax.fori_loop(0, nkv, bias_body, 0)

    q = q_ref[...]

    def kv_body(kb, carry):
        rows = pl.ds(pl.multiple_of(kb * tk, tk), tk)
        k = k_ref[rows, :]
        bias = sc_ref[kb]
        out = []
        for a in range(hpb):
            m_prev, l_prev, acc = carry[a]
            hs = slice(a * HEAD_DIM, (a + 1) * HEAD_DIM)
            s = lax.dot_general(k[:, hs], q[:, hs], NT_DIMS, preferred_element_type=F32) + bias
            m_new = jnp.maximum(m_prev, jnp.max(s, axis=0, keepdims=True))
            p = jnp.exp2(s - m_new)
            alpha = jnp.exp2(m_prev - m_new)
            l_new = alpha * l_prev + jnp.sum(p, axis=0, keepdims=True)
            acc = alpha * acc + jnp.dot(vt_ref[kb, a], p.astype(BF16), preferred_element_type=F32)
            out.append((m_new, l_new, acc))
        return tuple(out)

    init = tuple((jnp.full((1, tq), NEG, F32), jnp.zeros((1, tq), F32), jnp.zeros((HEAD_DIM, tq), F32))
                 for _ in range(hpb))
    fin = lax.fori_loop(0, nkv, kv_body, init)
    for a in range(hpb):
        _, l_fin, acc = fin[a]
        o_ref[:, a * HEAD_DIM:(a + 1) * HEAD_DIM] = (acc / l_fin).T.astype(o_ref.dtype)


def _dsa_attention(cd, ki_n, tail, *, batch, seq, group, w_off):
    m = batch * seq
    nh = group // HEAD_DIM
    tq = _pick_tile(seq, (512, 256, 128))
    tk = _pick_tile(seq, (1024, 512, 256))
    topk = min(TOPK_MAX, seq // 4)
    assert tk >= topk, "the threshold search needs at least topk keys in range"
    nq = seq // tq
    nc = group // LANE
    nkb = seq // tk
    q_off, k_off, v_off = 3 * nc, 4 * nc, 5 * nc
    qi_off = 6 * nc * LANE // (IDX_HEADS * IDX_DIM)
    assert qi_off * IDX_HEADS * IDX_DIM == 6 * nc * LANE
    v_t = (cd[:, v_off * LANE:(v_off + nc) * LANE].reshape(batch, nkb, tk, nh, HEAD_DIM)
           .transpose(0, 1, 3, 4, 2))
    w_t = tail[:, w_off:w_off + IDX_HEADS].reshape(batch, seq, IDX_HEADS).transpose(0, 2, 1)

    hpb = _DSA_HEADS_PER_STEP
    assert nh % hpb == 0
    wide = hpb * HEAD_DIM
    return pl.pallas_call(
        functools.partial(_dsa_kernel, tq=tq, tk=tk, topk=topk, hpb=hpb,
                          w_scale=IDX_HEADS ** -0.5 * IDX_DIM ** -0.5),
        grid=(batch, nq, nh // hpb),
        in_specs=[pl.BlockSpec((tq, wide), lambda b, i, h: (b * nq + i, q_off // hpb + h)),
                  pl.BlockSpec((seq, wide), lambda b, i, h: (b, k_off // hpb + h)),
                  pl.BlockSpec((None, nkb, hpb, HEAD_DIM, tk), lambda b, i, h: (b, 0, h, 0, 0)),
                  pl.BlockSpec((tq, IDX_HEADS * IDX_DIM), lambda b, i, h: (b * nq + i, qi_off)),
                  pl.BlockSpec((seq, IDX_DIM), lambda b, i, h: (b, 0)),
                  pl.BlockSpec((None, IDX_HEADS, tq), lambda b, i, h: (b, 0, i))],
        out_specs=pl.BlockSpec((tq, wide), lambda b, i, h: (b * nq + i, h)),
        out_shape=jax.ShapeDtypeStruct((m, group), BF16),
        scratch_shapes=[pltpu.VMEM((nkb, tk, tq), F32), pltpu.VMEM((nkb, tk, tq), BF16)],
        compiler_params=_params("parallel", "arbitrary", "arbitrary"),
        name="dsa",
    )(cd, cd, v_t, cd, ki_n, w_t)


def _rearrange_w_in(w, group, nhb):
    d = w.shape[0]
    p0 = 7 * group
    p1 = p0 + 2 * nhb
    p2 = p1 + 6 * group + IDX_HEADS * IDX_DIM
    used = IDX_DIM + 2 * nhb + IDX_HEADS
    tail = jnp.concatenate([w[:, p2:p2 + IDX_DIM], w[:, p0:p1], w[:, p2 + IDX_DIM:],
                            jnp.zeros((d, LANE - used), w.dtype)], axis=1)
    return w[:, p1:p2].astype(BF16), tail.astype(BF16)


def kernel(x, norm_mix, w_in, hgrn_lb_logits, hgrn_gnorm, mlstm_conv, mlstm_gate_bias, mlstm_hnorm,
           diff_qk_norm, diff_lambda, diff_subln, dsa_qk_norm, dsa_idx_knorm, w_out, norm_mlp, w1, w2):
    batch, seq, d_model = x.shape
    depth = w_in.shape[0]
    group = d_model // 4
    nhb = group // HEAD_DIM
    nc = group // LANE
    m = batch * seq
    dims = dict(batch=batch, seq=seq, group=group)

    p_lb = jax.nn.softmax(hgrn_lb_logits.astype(F32), axis=0)
    lower_bounds = jnp.cumsum(p_lb, axis=0) - p_lb[0:1]

    w_ab_b = w_in[:, :, :7 * group].astype(BF16)
    w_out_b, w1_b, w2_b = (w.astype(BF16) for w in (w_out, w1, w2))

    x2 = x.reshape(m, d_model).astype(F32)
    ones_g = jnp.ones((group,), F32)
    for l in range(depth):
        lambda_init = 0.8 - 0.6 * math.exp(-0.3 * l)
        w_cd, w_tail = _rearrange_w_in(w_in[l], group, nhb)
        h = _rmsnorm_rows(x2, norm_mix[l])
        proj = _matmul(h, w_ab_b, layer=l, out_dtype=F32, name="in_proj_ab")
        tail = _matmul(h, w_tail, out_dtype=F32, name="in_proj_tail")

        y_a = _hgrn(proj, lower_bounds[l], hgrn_gnorm[l], **dims)

        y_b = _mlstm(proj, tail, mlstm_conv[l], mlstm_gate_bias[l], mlstm_hnorm[l], g_off=IDX_DIM, **dims)

        cqn = diff_qk_norm[l].astype(F32)
        dqn = dsa_qk_norm[l].astype(F32)
        q_scale = HEAD_DIM ** -0.5 * LOG2_E
        gains = jnp.concatenate([
            jnp.tile(cqn[0], nc) * q_scale, jnp.tile(cqn[1], nc), ones_g,
            jnp.tile(dqn[0], nc) * q_scale, jnp.tile(dqn[1], nc), ones_g,
            jnp.ones((IDX_HEADS * IDX_DIM,), F32)]).reshape(1, -1)
        use_norm = jnp.concatenate([
            jnp.ones((2 * group,), F32), jnp.zeros((group,), F32),
            jnp.ones((2 * group,), F32), jnp.zeros((group + IDX_HEADS * IDX_DIM,), F32)]).reshape(1, -1)
        cd = _matmul(h, w_cd, out_dtype=BF16, act="groupnorm", extra=(gains, use_norm), name="in_proj_cd")
        ki_n = _ki_norm(tail, dsa_idx_knorm[l])

        lam = diff_lambda[l].astype(F32)
        lam_full = jnp.exp(jnp.sum(lam[0] * lam[1])) - jnp.exp(jnp.sum(lam[2] * lam[3])) + lambda_init
        y_c = _diff_attention(cd, lam_full, diff_subln[l], lambda_init, **dims)
        y_d = _dsa_attention(cd, ki_n, tail, w_off=IDX_DIM + 2 * nhb, **dims)

        x2 = _matmul((y_a, y_b, y_c, y_d), w_out_b, layer=l, out_dtype=F32, act="residual", extra=(x2,),
                     tm_max=512, name="out_proj")
        h = _rmsnorm_rows(x2, norm_mlp[l])
        hidden = _matmul(h, w1_b, layer=l, out_dtype=BF16, act="relu2", name="mlp_up")
        x2 = _matmul(hidden, w2_b, layer=l, out_dtype=F32, act="residual", extra=(x2,), tn_max=512,
                     name="mlp_down")
    return x2.reshape(batch, seq, d_model).astype(x.dtype)
```

```python
import functools
import math

import numpy as np
import jax
import jax.numpy as jnp
from jax import lax
from jax.experimental import pallas as pl
from jax.experimental.pallas import tpu as pltpu

CHUNK = 64
EPS = 1e-6
NEG = -1e30
F_FLOOR = 1e-20
HEAD_DIM = 128
CONV_W = 4
IDX_HEADS = 16
IDX_DIM = 64
TOPK_MAX = 256
LOG2_E = math.log2(math.e)
LANE = 128
VMEM_LIMIT = 56 * 1024 * 1024

F32 = jnp.float32
BF16 = jnp.bfloat16
NT_DIMS = (((1,), (1,)), ((), ()))
TN_DIMS = (((0,), (0,)), ((), ()))


def _params(*sem):
    return pltpu.CompilerParams(dimension_semantics=sem, vmem_limit_bytes=VMEM_LIMIT)


def _sigmoid(x):
    return 1.0 / (1.0 + jnp.exp(-x))


def _log_sigmoid(x):
    return jnp.minimum(x, 0.0) - jnp.log1p(jnp.exp(-jnp.abs(x)))


def _split3(x):
    hi = x.astype(BF16)
    r1 = x - hi.astype(F32)
    mid = r1.astype(BF16)
    lo = (r1 - mid.astype(F32)).astype(BF16)
    return hi, mid, lo


def _pick_tile(n, candidates):
    for c in candidates:
        if n % c == 0:
            return c
    raise ValueError(f"no tile in {candidates} divides {n}")


def _rmsnorm_kernel(x_ref, g_ref, o_ref):
    x = x_ref[...]
    ms = jnp.mean(x * x, axis=-1, keepdims=True)
    o_ref[...] = (x * lax.rsqrt(ms + EPS) * g_ref[...]).astype(o_ref.dtype)


def _rmsnorm_rows(x2, g):
    m, d = x2.shape
    tm = _pick_tile(m, (256, 128, 64, 8))
    return pl.pallas_call(
        _rmsnorm_kernel,
        grid=(m // tm,),
        in_specs=[pl.BlockSpec((tm, d), lambda i: (i, 0)), pl.BlockSpec((1, d), lambda i: (0, 0))],
        out_specs=pl.BlockSpec((tm, d), lambda i: (i, 0)),
        out_shape=jax.ShapeDtypeStruct((m, d), BF16),
        compiler_params=_params("parallel"),
        name="rmsnorm",
    )(x2, g.reshape(1, d).astype(F32))


MM_MAX_TK = 4096
MM_TN = 1024


def _mm_epilogue(r, act, extra):
    if act == "relu2":
        r = jnp.square(jnp.maximum(r, 0.0))
    elif act == "groupnorm":
        g_ref, u_ref = extra
        segs = []
        for c in range(r.shape[1] // LANE):
            seg = r[:, c * LANE:(c + 1) * LANE]
            ms = jnp.mean(seg * seg, axis=1, keepdims=True)
            inv = jnp.where(u_ref[:, c * LANE:(c + 1) * LANE] > 0.0, lax.rsqrt(ms + EPS), 1.0)
            segs.append(seg * inv * g_ref[:, c * LANE:(c + 1) * LANE])
        r = jnp.concatenate(segs, axis=1)
    elif act == "residual":
        r = r + extra[0][...]
    return r


def _mm_kernel(*refs, n_a, nk, act, n_extra, b_rows):
    a_refs = refs[:n_a]
    b_ref = refs[n_a]
    extra = refs[n_a + 1:n_a + 1 + n_extra]
    o_ref = refs[n_a + 1 + n_extra]

    def product():
        r, off = None, 0
        for a_ref in a_refs:
            kw = a_ref.shape[1]
            if b_rows:
                d = lax.dot_general(a_ref[...], b_ref[:, off:off + kw], NT_DIMS, preferred_element_type=F32)
            else:
                d = jnp.dot(a_ref[...], b_ref[off:off + kw, :], preferred_element_type=F32)
            r = d if r is None else r + d
            off += kw
        return r

    if nk == 1:
        o_ref[...] = _mm_epilogue(product(), act, extra).astype(o_ref.dtype)
        return
    acc_ref = refs[n_a + 2 + n_extra]
    k = pl.program_id(2)

    @pl.when(k == 0)
    def _():
        acc_ref[...] = jnp.zeros_like(acc_ref)

    acc_ref[...] += product()

    @pl.when(k == nk - 1)
    def _():
        o_ref[...] = _mm_epilogue(acc_ref[...], act, extra).astype(o_ref.dtype)


def _matmul(a, b, *, out_dtype, layer, cols=None, b_rows=False, act=None, extra=(), tm_max=1024, tn_max=MM_TN,
            tk_max=MM_MAX_TK, name="matmul"):
    a_parts = a if isinstance(a, tuple) else (a,)
    m = a_parts[0].shape[0]
    kdim = sum(p.shape[1] for p in a_parts)
    col0, n = (0, b.shape[1 if b_rows else 2]) if cols is None else cols
    tm = _pick_tile(m, tuple(t for t in (1024, 512, 256, 128, 8) if t <= tm_max))
    tn = _pick_tile(math.gcd(n, col0), tuple(t for t in (1024, 512, 256, 128) if t <= tn_max))
    j0 = col0 // tn
    tk = kdim if kdim <= tk_max else _pick_tile(kdim, tuple(t for t in (4096, 2048, 1024, 512, 256, 128)
                                                            if t <= tk_max))
    nk = kdim // tk
    assert nk == 1 or len(a_parts) == 1
    in_specs = [pl.BlockSpec((tm, tk if nk > 1 else p.shape[1]), lambda i, j, k: (i, k)) for p in a_parts]
    if b_rows:
        in_specs.append(pl.BlockSpec((None, tn, tk), lambda i, j, k: (layer, j0 + j, k)))
    else:
        in_specs.append(pl.BlockSpec((None, tk, tn), lambda i, j, k: (layer, k, j0 + j)))
    for e in extra:
        if e.shape[0] == 1:
            in_specs.append(pl.BlockSpec((1, tn), lambda i, j, k: (0, j)))
        else:
            in_specs.append(pl.BlockSpec((tm, tn), lambda i, j, k: (i, j)))
    return pl.pallas_call(
        functools.partial(_mm_kernel, n_a=len(a_parts), nk=nk, act=act, n_extra=len(extra), b_rows=b_rows),
        grid=(m // tm, n // tn, nk),
        in_specs=in_specs,
        out_specs=pl.BlockSpec((tm, tn), lambda i, j, k: (i, j)),
        out_shape=jax.ShapeDtypeStruct((m, n), out_dtype),
        scratch_shapes=[pltpu.VMEM((tm, tn), F32)] if nk > 1 else [],
        compiler_params=_params("parallel", "parallel", "arbitrary"),
        name=name,
    )(*a_parts, b, *extra)


_HGRN_LEVELS = 6


def _hgrn_constants():
    idx = np.arange(CHUNK)
    j = idx[None, :]
    r = idx[:, None]
    blocks = [j <= r, j > r]
    masks = []
    for lvl in range(_HGRN_LEVELS):
        half = (CHUNK // 2) >> lvl
        parent = idx // (2 * half)
        mid = parent * 2 * half + half
        is_right = (idx % (2 * half)) >= half
        right_rng = (j >= mid[:, None]) & (j <= r)
        left_rng = (j > r) & (j < mid[:, None])
        blocks.append(np.where(is_right[:, None], right_rng, left_rng))
        masks.append((parent[:, None] == parent[None, :]) & is_right[:, None] & (~is_right)[None, :])
    cm = np.concatenate(blocks, axis=0).astype(np.float32)
    cm3 = np.concatenate([cm, cm, cm, np.zeros_like(cm)], axis=1)
    mk = np.stack(masks, axis=0).astype(np.float32)
    return jnp.asarray(cm3, dtype=BF16), jnp.asarray(mk, dtype=F32)


def _hgrn_kernel(q_ref, f_ref, i_ref, g_ref, lb_ref, gn_ref, cm_ref, mk_ref, o_ref, st_ref, *, nchunk, scale):
    @pl.when(pl.program_id(2) == 0)
    def _():
        st_ref[...] = jnp.zeros_like(st_ref)

    lb = lb_ref[...]
    gn = gn_ref[...]

    def body(c, carry):
        rows = pl.ds(pl.multiple_of(c * CHUNK, CHUNK), CHUNK)
        f = lb + (1.0 - lb) * _sigmoid(f_ref[rows, :])
        logf = jnp.log(jnp.maximum(f, F_FLOOR))
        kk = 1.0 - f
        qr = q_ref[rows, :]
        qq = qr * _sigmoid(qr) * scale
        v = i_ref[rows, :]
        vb = v.astype(BF16)
        l3 = jnp.concatenate(_split3(logf) + (jnp.zeros_like(vb),), axis=0)
        sums = jnp.dot(cm_ref[...], l3, preferred_element_type=F32)
        b = sums[0:CHUNK]
        b_rest = sums[CHUNK:2 * CHUNK]
        scores = jnp.zeros((CHUNK, CHUNK), F32)
        for lvl in range(_HGRN_LEVELS):
            e = jnp.exp(sums[(2 + lvl) * CHUNK:(3 + lvl) * CHUNK])
            s = lax.dot_general((qq * e).astype(BF16), (kk * e).astype(BF16), NT_DIMS,
                                preferred_element_type=F32)
            scores = scores + s * mk_ref[lvl]
        diag = jnp.sum(qq * kk, axis=1, keepdims=True)
        st = st_ref[...]
        o = (jnp.dot(scores.astype(BF16), vb, preferred_element_type=F32) + diag * v
             + lax.dot_general((qq * jnp.exp(b)).astype(BF16), st.astype(BF16), NT_DIMS,
                               preferred_element_type=F32))
        ke = (kk * jnp.exp(b_rest)).astype(BF16)
        st_ref[...] = (st * jnp.exp(b[CHUNK - 1:CHUNK, :])
                       + lax.dot_general(vb, ke, TN_DIMS, preferred_element_type=F32))
        ms = jnp.mean(o * o, axis=1, keepdims=True)
        gr = g_ref[rows, :]
        y = o * lax.rsqrt(ms + EPS) * gn * (gr * _sigmoid(gr))
        o_ref[rows, :] = y.astype(o_ref.dtype)
        return carry

    lax.fori_loop(0, nchunk, body, 0, unroll=True)


def _hgrn(proj, lb, gnorm, *, batch, seq, group):
    m = batch * seq
    nh = group // HEAD_DIM
    t = _pick_tile(seq, (512, 256, 128, 64))
    nt = seq // t
    cm3, mk = _hgrn_constants()

    def col(off):
        return pl.BlockSpec((t, HEAD_DIM), lambda b, h, i: (b * nt + i, off + h))

    return pl.pallas_call(
        functools.partial(_hgrn_kernel, nchunk=t // CHUNK, scale=HEAD_DIM ** -0.5),
        grid=(batch, nh, nt),
        in_specs=[col(0), col(nh), col(2 * nh), col(3 * nh),
                  pl.BlockSpec((1, HEAD_DIM), lambda b, h, i: (0, h)),
                  pl.BlockSpec((1, HEAD_DIM), lambda b, h, i: (0, 0)),
                  pl.BlockSpec(cm3.shape, lambda b, h, i: (0, 0)),
                  pl.BlockSpec(mk.shape, lambda b, h, i: (0, 0, 0))],
        out_specs=pl.BlockSpec((t, HEAD_DIM), lambda b, h, i: (b * nt + i, h)),
        out_shape=jax.ShapeDtypeStruct((m, group), BF16),
        scratch_shapes=[pltpu.VMEM((HEAD_DIM, HEAD_DIM), F32)],
        compiler_params=_params("parallel", "parallel", "arbitrary"),
        name="hgrn2",
    )(proj, proj, proj, proj, lb.reshape(1, group).astype(F32), gnorm.reshape(1, HEAD_DIM).astype(F32), cm3, mk)


_CONV_PAD = 8


def _mlstm_constants():
    idx = np.arange(CHUNK)
    tri = (idx[None, :] <= idx[:, None]).astype(np.float32)
    zero = np.zeros_like(tri)
    tri3 = np.concatenate([tri, tri, tri, zero], axis=1)
    trit3 = np.concatenate([tri.T, tri.T, tri.T, zero], axis=0)
    trit3 = np.concatenate([trit3, np.zeros_like(trit3)], axis=1)
    return jnp.asarray(tri3, dtype=BF16), jnp.asarray(trit3, dtype=BF16)


def _mlstm_kernel(qk_ref, v_ref, og_ref, gc_ref, gr_ref, cw_ref, bc_ref, br_ref, hn_ref, tri_ref, trit_ref,
                  o_ref, xp_ref, qk_s, c_ref, m_ref, *, nchunk, nh, dqk, dv, t, g_off):
    @pl.when(pl.program_id(1) == 0)
    def _():
        xp_ref[0:_CONV_PAD, :] = jnp.zeros((_CONV_PAD, xp_ref.shape[1]), F32)
        c_ref[...] = jnp.zeros_like(c_ref)
        m_ref[...] = jnp.zeros_like(m_ref)

    xp_ref[_CONV_PAD:_CONV_PAD + t, :] = qk_ref[...]
    conv = jnp.zeros((t, 2 * nh * dqk), F32)
    for j in range(CONV_W):
        start = _CONV_PAD - (CONV_W - 1) + j
        conv = conv + cw_ref[j:j + 1, :] * xp_ref[start:start + t, :]
    qk_s[...] = conv * _sigmoid(conv)
    xp_ref[0:_CONV_PAD, :] = xp_ref[t:t + _CONV_PAD, :]

    hn = hn_ref[...]
    kscale = dqk ** -0.5
    r_i = lax.broadcasted_iota(jnp.int32, (CHUNK, CHUNK), 0)
    c_i = lax.broadcasted_iota(jnp.int32, (CHUNK, CHUNK), 1)
    causal = c_i <= r_i
    ones = jnp.ones((CHUNK, dv), BF16)
    zpad_c = jnp.zeros((CHUNK, LANE), BF16)
    zpad_r = jnp.zeros((2 * nh, CHUNK), BF16)

    def body(c, carry):
        base = pl.multiple_of(c * CHUNK, CHUNK)
        rows = pl.ds(base, CHUNK)
        qk = qk_s[rows, :]
        gcol = gc_ref[rows, :] + bc_ref[...]
        b_c = jnp.dot(tri_ref[...], jnp.concatenate(_split3(_log_sigmoid(gcol)) + (zpad_c,), axis=0),
                      preferred_element_type=F32)
        grow = gr_ref[c] + br_ref[...]
        ig_r = grow[0:nh, :]
        b_r = jnp.dot(jnp.concatenate(_split3(_log_sigmoid(grow)) + (zpad_r,), axis=1), trit_ref[...],
                      preferred_element_type=F32)[nh:2 * nh, 0:CHUNK]
        for h in range(nh):
            ig_col = gcol[:, g_off + h:g_off + h + 1]
            q_h = qk[:, h * dqk:(h + 1) * dqk].astype(BF16)
            k_f = qk[:, (nh + h) * dqk:(nh + h + 1) * dqk] * kscale
            k_h = k_f.astype(BF16)
            v_ext = jnp.concatenate([v_ref[rows, h * dv:(h + 1) * dv].astype(BF16), ones], axis=1)
            bcol = b_c[:, g_off + nh + h:g_off + nh + h + 1]
            m_prev = m_ref[h:h + 1, 0:1]
            log_d = jnp.where(causal, bcol - b_r[h:h + 1, :] + ig_r[h:h + 1, :], NEG)
            log_inter = bcol + m_prev
            m_t = jnp.maximum(jnp.max(log_d, axis=1, keepdims=True), log_inter)
            qk_w = (lax.dot_general(q_h, k_h, NT_DIMS, preferred_element_type=F32)
                    * jnp.exp(log_d - m_t))
            w_inter = jnp.exp(log_inter - m_t)
            c_ext = c_ref[h]
            numden = (jnp.dot(qk_w.astype(BF16), v_ext, preferred_element_type=F32)
                      + w_inter * jnp.dot(q_h, c_ext.astype(BF16), preferred_element_type=F32))
            num = numden[:, 0:dv]
            den = numden[:, dv:2 * dv]
            hh = num / jnp.maximum(jnp.abs(den), jnp.exp(-m_t))
            m_new = m_t[CHUNK - 1:CHUNK, :]
            b_last = bcol[CHUNK - 1:CHUNK, :]
            w_state = jnp.exp(b_last - bcol + ig_col - m_new)
            decay = jnp.exp(b_last + m_prev - m_new)
            c_ref[h] = decay * c_ext + lax.dot_general((k_f * w_state).astype(BF16), v_ext, TN_DIMS,
                                                       preferred_element_type=F32)
            m_ref[h:h + 1, :] = jnp.broadcast_to(m_new, (1, LANE))
            ms = jnp.mean(hh * hh, axis=1, keepdims=True)
            gate = _sigmoid(og_ref[rows, h * dv:(h + 1) * dv])
            o_ref[rows, h * dv:(h + 1) * dv] = (hh * lax.rsqrt(ms + EPS) * hn * gate).astype(o_ref.dtype)
        return carry

    lax.fori_loop(0, nchunk, body, 0)


def _mlstm(proj, tail, conv_w, gate_bias, hnorm, *, batch, seq, group, g_off):
    m = batch * seq
    nh = group // HEAD_DIM
    dqk = HEAD_DIM // 2
    dv = HEAD_DIM
    t = _pick_tile(seq, (256, 128, 64))
    nt = seq // t
    tri3, trit3 = _mlstm_constants()
    qk_blk = 4
    gates_row = (tail[:, g_off:g_off + 2 * nh].reshape(batch, seq // CHUNK, CHUNK, 2 * nh)
                 .transpose(0, 1, 3, 2))
    bias = gate_bias.astype(F32)
    bias_lane = jnp.zeros((1, LANE), F32).at[0, g_off:g_off + 2 * nh].set(bias)

    def wide(off):
        return pl.BlockSpec((t, group), lambda b, i: (b * nt + i, off))

    def const(shape):
        return pl.BlockSpec(shape, lambda b, i: tuple(0 for _ in shape))

    return pl.pallas_call(
        functools.partial(_mlstm_kernel, nchunk=t // CHUNK, nh=nh, dqk=dqk, dv=dv, t=t, g_off=g_off),
        grid=(batch, nt),
        in_specs=[wide(qk_blk), wide(qk_blk + 1), wide(qk_blk + 2),
                  pl.BlockSpec((t, LANE), lambda b, i: (b * nt + i, 0)),
                  pl.BlockSpec((None, t // CHUNK, 2 * nh, CHUNK), lambda b, i: (b, i, 0, 0)),
                  const((CONV_W, group)), const((1, LANE)), const((2 * nh, 1)), const((1, dv)),
                  const(tri3.shape), const(trit3.shape)],
        out_specs=pl.BlockSpec((t, group), lambda b, i: (b * nt + i, 0)),
        out_shape=jax.ShapeDtypeStruct((m, group), BF16),
        scratch_shapes=[pltpu.VMEM((t + _CONV_PAD, group), F32),
                        pltpu.VMEM((t, group), F32),
                        pltpu.VMEM((nh, dqk, 2 * dv), F32),
                        pltpu.VMEM((nh, LANE), F32)],
        compiler_params=_params("parallel", "arbitrary"),
        name="mlstm",
    )(proj, proj, proj, tail, gates_row, conv_w.astype(F32), bias_lane, bias.reshape(2 * nh, 1),
      hnorm.reshape(1, dv).astype(F32), tri3, trit3)


def _ki_kernel(x_ref, g_ref, o_ref):
    x = x_ref[:, 0:IDX_DIM]
    ms = jnp.mean(x * x, axis=-1, keepdims=True)
    o_ref[...] = (x * lax.rsqrt(ms + EPS) * g_ref[...]).astype(o_ref.dtype)


def _ki_norm(tail, g):
    m = tail.shape[0]
    tm = _pick_tile(m, (512, 256, 128, 64))
    return pl.pallas_call(
        _ki_kernel,
        grid=(m // tm,),
        in_specs=[pl.BlockSpec((tm, LANE), lambda i: (i, 0)), pl.BlockSpec((1, IDX_DIM), lambda i: (0, 0))],
        out_specs=pl.BlockSpec((tm, IDX_DIM), lambda i: (i, 0)),
        out_shape=jax.ShapeDtypeStruct((m, IDX_DIM), BF16),
        compiler_params=_params("parallel"),
        name="ki_norm",
    )(tail, g.reshape(1, IDX_DIM).astype(F32))


def _diff_kernel(q_ref, k_ref, vt_ref, lam_ref, sn_ref, o_ref, m_ref, l_ref, acc_ref, *, tq, tk, dqk, out_scale):
    qi = pl.program_id(2)
    last_k = ((qi + 1) * tq - 1) // tk
    m_ref[...] = jnp.full_like(m_ref, NEG)
    l_ref[...] = jnp.zeros_like(l_ref)
    acc_ref[...] = jnp.zeros_like(acc_ref)
    q = q_ref[...]

    def step(kb, masked):
        rows = pl.ds(pl.multiple_of(kb * tk, tk), tk)
        k = k_ref[rows, :]
        vt = vt_ref[kb]
        if masked:
            k_chunk = (kb * tk + lax.broadcasted_iota(jnp.int32, (tk, tq), 0)) // CHUNK
            q_chunk = (qi * tq + lax.broadcasted_iota(jnp.int32, (tk, tq), 1)) // CHUNK
            mask = k_chunk <= q_chunk
        for half in range(2):
            s = lax.dot_general(k[:, half * dqk:(half + 1) * dqk], q[:, half * dqk:(half + 1) * dqk], NT_DIMS,
                                preferred_element_type=F32)
            if masked:
                s = jnp.where(mask, s, NEG)
            m_prev = m_ref[half]
            m_new = jnp.maximum(m_prev, jnp.max(s, axis=0, keepdims=True))
            p = jnp.exp2(s - m_new)
            alpha = jnp.exp2(m_prev - m_new)
            l_ref[half] = alpha * l_ref[half] + jnp.sum(p, axis=0, keepdims=True)
            acc_ref[half] = alpha * acc_ref[half] + jnp.dot(vt, p.astype(BF16), preferred_element_type=F32)
            m_ref[half] = m_new

    def body(kb, carry):
        step(kb, False)
        return carry

    lax.fori_loop(0, last_k, body, 0)
    step(last_k, True)

    o = acc_ref[0] / l_ref[0] - lam_ref[...] * (acc_ref[1] / l_ref[1])
    ms = jnp.mean(o * o, axis=0, keepdims=True)
    o_ref[...] = (o * lax.rsqrt(ms + EPS) * (sn_ref[...] * out_scale)).T.astype(o_ref.dtype)


def _diff_attention(cd, lam_full, subln, lambda_init, *, batch, seq, group):
    m = batch * seq
    dqk = HEAD_DIM
    dv = 2 * HEAD_DIM
    nh = group // dv
    tq = _pick_tile(seq, (1024, 512, 256, 128))
    tk = _pick_tile(seq, (1024, 512, 256, 128))
    assert tk % tq == 0
    nq, nkb = seq // tq, seq // tk
    k_off = nh
    v_t = cd[:, 2 * group:3 * group].reshape(batch, nkb, tk, nh, dv).transpose(0, 1, 3, 4, 2)

    return pl.pallas_call(
        functools.partial(_diff_kernel, tq=tq, tk=tk, dqk=dqk, out_scale=1.0 - lambda_init),
        grid=(batch, nh, nq),
        in_specs=[pl.BlockSpec((tq, dv), lambda b, h, i: (b * nq + i, h)),
                  pl.BlockSpec((seq, dv), lambda b, h, i: (b, k_off + h)),
                  pl.BlockSpec((None, nkb, None, dv, tk), lambda b, h, i: (b, 0, h, 0, 0)),
                  pl.BlockSpec((1, 1), lambda b, h, i: (0, 0)),
                  pl.BlockSpec((dv, 1), lambda b, h, i: (0, 0))],
        out_specs=pl.BlockSpec((tq, dv), lambda b, h, i: (b * nq + i, h)),
        out_shape=jax.ShapeDtypeStruct((m, group), BF16),
        scratch_shapes=[pltpu.VMEM((2, 1, tq), F32), pltpu.VMEM((2, 1, tq), F32), pltpu.VMEM((2, dv, tq), F32)],
        compiler_params=_params("parallel", "parallel", "arbitrary"),
        name="diff_attn",
    )(cd, cd, v_t, lam_full.reshape(1, 1).astype(F32), subln.reshape(dv, 1).astype(F32))


_INT_MIN = -2 ** 31
_HI16 = -(1 << 16)


_CNT_ROWS = 64
_DSA_HEADS_PER_STEP = 2


def _dsa_kernel(q_ref, k_ref, vt_ref, qi_ref, ki_ref, wt_ref, o_ref, sc_ref, sc16_ref, *, tq, tk, topk, hpb,
                w_scale):
    qb = pl.program_id(1)
    h = pl.program_id(2)
    nkv = ((qb + 1) * tq + tk - 1) // tk

    @pl.when(h == 0)
    def _():
        w = wt_ref[...] * w_scale
        q_chunk = (qb * tq + lax.broadcasted_iota(jnp.int32, (tk, tq), 1)) // CHUNK

        def score_body(kb, carry):
            rows = pl.ds(pl.multiple_of(kb * tk, tk), tk)
            ki = ki_ref[rows, :]
            acc = jnp.zeros((tk, tq), F32)
            for ih in range(IDX_HEADS):
                lg = lax.dot_general(ki, qi_ref[:, ih * IDX_DIM:(ih + 1) * IDX_DIM], NT_DIMS,
                                     preferred_element_type=F32)
                acc = acc + w[ih:ih + 1, :] * jnp.maximum(lg, 0.0)
            k_chunk = (kb * tk + lax.broadcasted_iota(jnp.int32, (tk, tq), 0)) // CHUNK
            x = jnp.where(k_chunk <= q_chunk, acc, NEG)
            sc_ref[kb] = x
            hi = lax.bitcast_convert_type(x, jnp.int32) & _HI16
            sc16_ref[kb] = lax.bitcast_convert_type(hi, F32).astype(BF16)
            return carry

        lax.fori_loop(0, nkv, score_body, 0)

        def count_ge(cf):
            def cnt_body(kb, cnt):
                for r in range(tk // _CNT_ROWS):
                    hit = jnp.where(sc_ref[kb, r * _CNT_ROWS:(r + 1) * _CNT_ROWS, :] >= cf, 1.0, 0.0)
                    for g in range(_CNT_ROWS // 8):
                        cnt = cnt + hit[g * 8:(g + 1) * 8, :]
                return cnt

            return jnp.sum(lax.fori_loop(0, nkv, cnt_body, jnp.zeros((8, tq), F32)), axis=0, keepdims=True)

        def count_ge_hi(cfb):
            one, zero = jnp.ones((), BF16), jnp.zeros((), BF16)

            def cnt_body(kb, cnt):
                part = jnp.zeros((16, tq), BF16)
                for r in range(tk // _CNT_ROWS):
                    hit = jnp.where(sc16_ref[kb, r * _CNT_ROWS:(r + 1) * _CNT_ROWS, :] >= cfb, one, zero)
                    for g in range(_CNT_ROWS // 16):
                        part = part + hit[g * 16:(g + 1) * 16, :]
                return cnt + part.astype(F32)

            return jnp.sum(lax.fori_loop(0, nkv, cnt_body, jnp.zeros((16, tq), F32)), axis=0, keepdims=True)

        def bit_body(i, u, *, coarse):
            cand = u | jnp.left_shift(jnp.int32(1), 31 - i)
            key = cand ^ _INT_MIN
            if coarse:
                key = jnp.where(key < 0, key | 0xFFFF, key)
            cf = lax.bitcast_convert_type(key ^ (jnp.right_shift(key, 31) & 0x7FFFFFFF), F32)
            cnt = count_ge_hi(cf.astype(BF16)) if coarse else count_ge(cf)
            return jnp.where(cnt >= float(topk), cand, u)

        u = lax.fori_loop(0, 16, functools.partial(bit_body, coarse=True), jnp.zeros((1, tq), jnp.int32))
        u = lax.fori_loop(16, 32, functools.partial(bit_body, coarse=False), u)
        key = u ^ _INT_MIN
        thr = lax.bitcast_convert_type(key ^ (jnp.right_shift(key, 31) & 0x7FFFFFFF), F32)

        def bias_body(kb, carry):
            s = sc_ref[kb]
            sc_ref[kb] = jnp.where((s >= thr) & (s > 0.5 * NEG), 0.0, NEG)
            return carry

        lax.fori_loop(0, nkv, bias_body, 0)

    q = q_ref[...]

    def kv_body(kb, carry):
        rows = pl.ds(pl.multiple_of(kb * tk, tk), tk)
        k = k_ref[rows, :]
        bias = sc_ref[kb]
        out = []
        for a in range(hpb):
            m_prev, l_prev, acc = carry[a]
            hs = slice(a * HEAD_DIM, (a + 1) * HEAD_DIM)
            s = lax.dot_general(k[:, hs], q[:, hs], NT_DIMS, preferred_element_type=F32) + bias
            m_new = jnp.maximum(m_prev, jnp.max(s, axis=0, keepdims=True))
            p = jnp.exp2(s - m_new)
            alpha = jnp.exp2(m_prev - m_new)
            l_new = alpha * l_prev + jnp.sum(p, axis=0, keepdims=True)
            acc = alpha * acc + jnp.dot(vt_ref[kb, a], p.astype(BF16), preferred_element_type=F32)
            out.append((m_new, l_new, acc))
        return tuple(out)

    init = tuple((jnp.full((1, tq), NEG, F32), jnp.zeros((1, tq), F32), jnp.zeros((HEAD_DIM, tq), F32))
                 for _ in range(hpb))
    fin = lax.fori_loop(0, nkv, kv_body, init)
    for a in range(hpb):
        _, l_fin, acc = fin[a]
        o_ref[:, a * HEAD_DIM:(a + 1) * HEAD_DIM] = (acc / l_fin).T.astype(o_ref.dtype)


def _dsa_attention(cd, ki_n, tail, *, batch, seq, group, w_off):
    m = batch * seq
    nh = group // HEAD_DIM
    tq = _pick_tile(seq, (512, 256, 128))
    tk = _pick_tile(seq, (1024, 512, 256))
    topk = min(TOPK_MAX, seq // 4)
    assert tk >= topk, "the threshold search needs at least topk keys in range"
    nq = seq // tq
    nc = group // LANE
    nkb = seq // tk
    q_off, k_off, v_off = 3 * nc, 4 * nc, 5 * nc
    qi_off = 6 * nc * LANE // (IDX_HEADS * IDX_DIM)
    assert qi_off * IDX_HEADS * IDX_DIM == 6 * nc * LANE
    v_t = (cd[:, v_off * LANE:(v_off + nc) * LANE].reshape(batch, nkb, tk, nh, HEAD_DIM)
           .transpose(0, 1, 3, 4, 2))
    w_t = tail[:, w_off:w_off + IDX_HEADS].reshape(batch, seq, IDX_HEADS).transpose(0, 2, 1)

    hpb = _DSA_HEADS_PER_STEP
    assert nh % hpb == 0
    wide = hpb * HEAD_DIM
    return pl.pallas_call(
        functools.partial(_dsa_kernel, tq=tq, tk=tk, topk=topk, hpb=hpb,
                          w_scale=IDX_HEADS ** -0.5 * IDX_DIM ** -0.5),
        grid=(batch, nq, nh // hpb),
        in_specs=[pl.BlockSpec((tq, wide), lambda b, i, h: (b * nq + i, q_off // hpb + h)),
                  pl.BlockSpec((seq, wide), lambda b, i, h: (b, k_off // hpb + h)),
                  pl.BlockSpec((None, nkb, hpb, HEAD_DIM, tk), lambda b, i, h: (b, 0, h, 0, 0)),
                  pl.BlockSpec((tq, IDX_HEADS * IDX_DIM), lambda b, i, h: (b * nq + i, qi_off)),
                  pl.BlockSpec((seq, IDX_DIM), lambda b, i, h: (b, 0)),
                  pl.BlockSpec((None, IDX_HEADS, tq), lambda b, i, h: (b, 0, i))],
        out_specs=pl.BlockSpec((tq, wide), lambda b, i, h: (b * nq + i, h)),
        out_shape=jax.ShapeDtypeStruct((m, group), BF16),
        scratch_shapes=[pltpu.VMEM((nkb, tk, tq), F32), pltpu.VMEM((nkb, tk, tq), BF16)],
        compiler_params=_params("parallel", "arbitrary", "arbitrary"),
        name="dsa",
    )(cd, cd, v_t, cd, ki_n, w_t)


def _repack_w_in(w, group, nhb):
    p0 = 7 * group
    p1 = p0 + 2 * nhb
    p2 = p1 + 6 * group + IDX_HEADS * IDX_DIM
    used = IDX_DIM + 2 * nhb + IDX_HEADS
    wt = jnp.swapaxes(w, 1, 2)
    pad = jnp.zeros((w.shape[0], LANE - used, w.shape[1]), w.dtype)
    return jnp.concatenate([wt[:, :p0], wt[:, p1:p2], wt[:, p2:p2 + IDX_DIM], wt[:, p0:p1],
                            wt[:, p2 + IDX_DIM:], pad], axis=1).astype(BF16)


def kernel(x, norm_mix, w_in, hgrn_lb_logits, hgrn_gnorm, mlstm_conv, mlstm_gate_bias, mlstm_hnorm,
           diff_qk_norm, diff_lambda, diff_subln, dsa_qk_norm, dsa_idx_knorm, w_out, norm_mlp, w1, w2):
    batch, seq, d_model = x.shape
    depth = w_in.shape[0]
    group = d_model // 4
    nhb = group // HEAD_DIM
    nc = group // LANE
    m = batch * seq
    dims = dict(batch=batch, seq=seq, group=group)

    p_lb = jax.nn.softmax(hgrn_lb_logits.astype(F32), axis=0)
    lower_bounds = jnp.cumsum(p_lb, axis=0) - p_lb[0:1]

    w_in_b = _repack_w_in(w_in, group, nhb)
    w_out_b, w1_b, w2_b = (w.astype(BF16) for w in (w_out, w1, w2))
    n_ab, n_cd = 7 * group, 6 * group + IDX_HEADS * IDX_DIM

    x2 = x.reshape(m, d_model).astype(F32)
    ones_g = jnp.ones((group,), F32)
    for l in range(depth):
        lambda_init = 0.8 - 0.6 * math.exp(-0.3 * l)
        h = _rmsnorm_rows(x2, norm_mix[l])
        proj = _matmul(h, w_in_b, layer=l, cols=(0, n_ab), b_rows=True, out_dtype=F32, name="in_proj_ab")
        tail = _matmul(h, w_in_b, layer=l, cols=(n_ab + n_cd, LANE), b_rows=True, out_dtype=F32,
                       name="in_proj_tail")

        y_a = _hgrn(proj, lower_bounds[l], hgrn_gnorm[l], **dims)

        y_b = _mlstm(proj, tail, mlstm_conv[l], mlstm_gate_bias[l], mlstm_hnorm[l], g_off=IDX_DIM, **dims)

        cqn = diff_qk_norm[l].astype(F32)
        dqn = dsa_qk_norm[l].astype(F32)
        q_scale = HEAD_DIM ** -0.5 * LOG2_E
        gains = jnp.concatenate([
            jnp.tile(cqn[0], nc) * q_scale, jnp.tile(cqn[1], nc), ones_g,
            jnp.tile(dqn[0], nc) * q_scale, jnp.tile(dqn[1], nc), ones_g,
            jnp.ones((IDX_HEADS * IDX_DIM,), F32)]).reshape(1, -1)
        use_norm = jnp.concatenate([
            jnp.ones((2 * group,), F32), jnp.zeros((group,), F32),
            jnp.ones((2 * group,), F32), jnp.zeros((group + IDX_HEADS * IDX_DIM,), F32)]).reshape(1, -1)
        cd = _matmul(h, w_in_b, layer=l, cols=(n_ab, n_cd), b_rows=True, out_dtype=BF16, act="groupnorm",
                     extra=(gains, use_norm), name="in_proj_cd")
        ki_n = _ki_norm(tail, dsa_idx_knorm[l])

        lam = diff_lambda[l].astype(F32)
        lam_full = jnp.exp(jnp.sum(lam[0] * lam[1])) - jnp.exp(jnp.sum(lam[2] * lam[3])) + lambda_init
        y_c = _diff_attention(cd, lam_full, diff_subln[l], lambda_init, **dims)
        y_d = _dsa_attention(cd, ki_n, tail, w_off=IDX_DIM + 2 * nhb, **dims)

        x2 = _matmul((y_a, y_b, y_c, y_d), w_out_b, layer=l, out_dtype=F32, act="residual", extra=(x2,),
                     tm_max=512, name="out_proj")
        h = _rmsnorm_rows(x2, norm_mlp[l])
        hidden = _matmul(h, w1_b, layer=l, out_dtype=BF16, act="relu2", name="mlp_up")
        x2 = _matmul(hidden, w2_b, layer=l, out_dtype=F32, act="residual", extra=(x2,), tk_max=2048,
                     name="mlp_down")
    return x2.reshape(batch, seq, d_model).astype(x.dtype)
```

```python
import functools
import math

import numpy as np
import jax
import jax.numpy as jnp
from jax import lax
from jax.experimental import pallas as pl
from jax.experimental.pallas import tpu as pltpu

CHUNK = 64
EPS = 1e-6
NEG = -1e30
F_FLOOR = 1e-20
HEAD_DIM = 128
CONV_W = 4
IDX_HEADS = 16
IDX_DIM = 64
TOPK_MAX = 256
LOG2_E = math.log2(math.e)
LANE = 128
VMEM_LIMIT = 56 * 1024 * 1024

F32 = jnp.float32
BF16 = jnp.bfloat16
NT_DIMS = (((1,), (1,)), ((), ()))
TN_DIMS = (((0,), (0,)), ((), ()))


def _params(*sem):
    return pltpu.CompilerParams(dimension_semantics=sem, vmem_limit_bytes=VMEM_LIMIT)


def _sigmoid(x):
    return 1.0 / (1.0 + jnp.exp(-x))


def _log_sigmoid(x):
    return jnp.minimum(x, 0.0) - jnp.log1p(jnp.exp(-jnp.abs(x)))


def _split3(x):
    hi = x.astype(BF16)
    r1 = x - hi.astype(F32)
    mid = r1.astype(BF16)
    lo = (r1 - mid.astype(F32)).astype(BF16)
    return hi, mid, lo


def _pick_tile(n, candidates):
    for c in candidates:
        if n % c == 0:
            return c
    raise ValueError(f"no tile in {candidates} divides {n}")


def _rmsnorm_kernel(x_ref, g_ref, o_ref):
    x = x_ref[...]
    ms = jnp.mean(x * x, axis=-1, keepdims=True)
    o_ref[...] = (x * lax.rsqrt(ms + EPS) * g_ref[...]).astype(o_ref.dtype)


def _rmsnorm_rows(x2, g):
    m, d = x2.shape
    tm = _pick_tile(m, (256, 128, 64, 8))
    return pl.pallas_call(
        _rmsnorm_kernel,
        grid=(m // tm,),
        in_specs=[pl.BlockSpec((tm, d), lambda i: (i, 0)), pl.BlockSpec((1, d), lambda i: (0, 0))],
        out_specs=pl.BlockSpec((tm, d), lambda i: (i, 0)),
        out_shape=jax.ShapeDtypeStruct((m, d), BF16),
        compiler_params=_params("parallel"),
        name="rmsnorm",
    )(x2, g.reshape(1, d).astype(F32))


MM_MAX_TK = 4096
MM_TN = 1024


def _mm_epilogue(r, act, extra):
    if act == "relu2":
        r = jnp.square(jnp.maximum(r, 0.0))
    elif act == "groupnorm":
        g_ref, u_ref = extra
        segs = []
        for c in range(r.shape[1] // LANE):
            seg = r[:, c * LANE:(c + 1) * LANE]
            ms = jnp.mean(seg * seg, axis=1, keepdims=True)
            inv = jnp.where(u_ref[:, c * LANE:(c + 1) * LANE] > 0.0, lax.rsqrt(ms + EPS), 1.0)
            segs.append(seg * inv * g_ref[:, c * LANE:(c + 1) * LANE])
        r = jnp.concatenate(segs, axis=1)
    elif act == "residual":
        r = r + extra[0][...]
    return r


def _mm_kernel(*refs, n_a, nk, act, n_extra, b_rows):
    a_refs = refs[:n_a]
    b_ref = refs[n_a]
    extra = refs[n_a + 1:n_a + 1 + n_extra]
    o_ref = refs[n_a + 1 + n_extra]

    def product():
        r, off = None, 0
        for a_ref in a_refs:
            kw = a_ref.shape[1]
            if b_rows:
                d = lax.dot_general(a_ref[...], b_ref[:, off:off + kw], NT_DIMS, preferred_element_type=F32)
            else:
                d = jnp.dot(a_ref[...], b_ref[off:off + kw, :], preferred_element_type=F32)
            r = d if r is None else r + d
            off += kw
        return r

    if nk == 1:
        o_ref[...] = _mm_epilogue(product(), act, extra).astype(o_ref.dtype)
        return
    acc_ref = refs[n_a + 2 + n_extra]
    k = pl.program_id(2)

    @pl.when(k == 0)
    def _():
        acc_ref[...] = jnp.zeros_like(acc_ref)

    acc_ref[...] += product()

    @pl.when(k == nk - 1)
    def _():
        o_ref[...] = _mm_epilogue(acc_ref[...], act, extra).astype(o_ref.dtype)


def _matmul(a, b, *, out_dtype, layer, cols=None, b_rows=False, act=None, extra=(), tm_max=1024, tn_max=MM_TN,
            tk_max=MM_MAX_TK, name="matmul"):
    a_parts = a if isinstance(a, tuple) else (a,)
    m = a_parts[0].shape[0]
    kdim = sum(p.shape[1] for p in a_parts)
    col0, n = (0, b.shape[1 if b_rows else 2]) if cols is None else cols
    tm = _pick_tile(m, tuple(t for t in (1024, 512, 256, 128, 8) if t <= tm_max))
    tn = _pick_tile(math.gcd(n, col0), tuple(t for t in (1024, 512, 256, 128) if t <= tn_max))
    j0 = col0 // tn
    tk = kdim if kdim <= tk_max else _pick_tile(kdim, tuple(t for t in (4096, 2048, 1024, 512, 256, 128)
                                                            if t <= tk_max))
    nk = kdim // tk
    assert nk == 1 or len(a_parts) == 1
    in_specs = [pl.BlockSpec((tm, tk if nk > 1 else p.shape[1]), lambda i, j, k: (i, k)) for p in a_parts]
    if b_rows:
        in_specs.append(pl.BlockSpec((None, tn, tk), lambda i, j, k: (layer, j0 + j, k)))
    else:
        in_specs.append(pl.BlockSpec((None, tk, tn), lambda i, j, k: (layer, k, j0 + j)))
    for e in extra:
        if e.shape[0] == 1:
            in_specs.append(pl.BlockSpec((1, tn), lambda i, j, k: (0, j)))
        else:
            in_specs.append(pl.BlockSpec((tm, tn), lambda i, j, k: (i, j)))
    return pl.pallas_call(
        functools.partial(_mm_kernel, n_a=len(a_parts), nk=nk, act=act, n_extra=len(extra), b_rows=b_rows),
        grid=(m // tm, n // tn, nk),
        in_specs=in_specs,
        out_specs=pl.BlockSpec((tm, tn), lambda i, j, k: (i, j)),
        out_shape=jax.ShapeDtypeStruct((m, n), out_dtype),
        scratch_shapes=[pltpu.VMEM((tm, tn), F32)] if nk > 1 else [],
        compiler_params=_params("parallel", "parallel", "arbitrary"),
        name=name,
    )(*a_parts, b, *extra)


_HGRN_LEVELS = 6
_HGRN_HEADS_PER_STEP = 4


def _hgrn_constants():
    idx = np.arange(CHUNK)
    j = idx[None, :]
    r = idx[:, None]
    blocks = [j <= r, j > r]
    masks = []
    for lvl in range(_HGRN_LEVELS):
        half = (CHUNK // 2) >> lvl
        parent = idx // (2 * half)
        mid = parent * 2 * half + half
        is_right = (idx % (2 * half)) >= half
        right_rng = (j >= mid[:, None]) & (j <= r)
        left_rng = (j > r) & (j < mid[:, None])
        blocks.append(np.where(is_right[:, None], right_rng, left_rng))
        masks.append((parent[:, None] == parent[None, :]) & is_right[:, None] & (~is_right)[None, :])
    cm = np.concatenate(blocks, axis=0).astype(np.float32)
    cm3 = np.concatenate([cm, cm, cm, np.zeros_like(cm)], axis=1)
    mk = np.stack(masks, axis=0).astype(np.float32)
    return jnp.asarray(cm3, dtype=BF16), jnp.asarray(mk, dtype=F32)


def _hgrn_kernel(q_ref, f_ref, i_ref, g_ref, lb_ref, gn_ref, cm_ref, mk_ref, o_ref, st_ref, *, nchunk, hb, scale):
    @pl.when(pl.program_id(2) == 0)
    def _():
        st_ref[...] = jnp.zeros_like(st_ref)

    lb = lb_ref[...]
    gn = gn_ref[...]

    def body(c, carry):
        rows = pl.ds(pl.multiple_of(c * CHUNK, CHUNK), CHUNK)
        f = lb + (1.0 - lb) * _sigmoid(f_ref[rows, :])
        logf = jnp.log(jnp.maximum(f, F_FLOOR))
        kk_w = 1.0 - f
        qr = q_ref[rows, :]
        qq_w = qr * _sigmoid(qr) * scale
        v_w = i_ref[rows, :]
        gr = g_ref[rows, :]
        gate_w = gr * _sigmoid(gr)
        l3 = jnp.concatenate(_split3(logf) + (jnp.zeros(logf.shape, BF16),), axis=0)
        sums_w = jnp.dot(cm_ref[...], l3, preferred_element_type=F32)
        for a in range(hb):
            hs = slice(a * HEAD_DIM, (a + 1) * HEAD_DIM)
            sums, qq, kk, v = sums_w[:, hs], qq_w[:, hs], kk_w[:, hs], v_w[:, hs]
            vb = v.astype(BF16)
            b = sums[0:CHUNK]
            b_rest = sums[CHUNK:2 * CHUNK]
            scores = jnp.zeros((CHUNK, CHUNK), F32)
            for lvl in range(_HGRN_LEVELS):
                e = jnp.exp(sums[(2 + lvl) * CHUNK:(3 + lvl) * CHUNK])
                s = lax.dot_general((qq * e).astype(BF16), (kk * e).astype(BF16), NT_DIMS,
                                    preferred_element_type=F32)
                scores = scores + s * mk_ref[lvl]
            diag = jnp.sum(qq * kk, axis=1, keepdims=True)
            st = st_ref[a]
            o = (jnp.dot(scores.astype(BF16), vb, preferred_element_type=F32) + diag * v
                 + lax.dot_general((qq * jnp.exp(b)).astype(BF16), st.astype(BF16), NT_DIMS,
                                   preferred_element_type=F32))
            ke = (kk * jnp.exp(b_rest)).astype(BF16)
            st_ref[a] = (st * jnp.exp(b[CHUNK - 1:CHUNK, :])
                         + lax.dot_general(vb, ke, TN_DIMS, preferred_element_type=F32))
            ms = jnp.mean(o * o, axis=1, keepdims=True)
            o_ref[rows, hs] = (o * lax.rsqrt(ms + EPS) * gn * gate_w[:, hs]).astype(o_ref.dtype)
        return carry

    lax.fori_loop(0, nchunk, body, 0, unroll=True)


def _hgrn(proj, lb, gnorm, *, batch, seq, group):
    m = batch * seq
    nh = group // HEAD_DIM
    t = _pick_tile(seq, (512, 256, 128, 64))
    nt = seq // t
    cm3, mk = _hgrn_constants()

    hb = _HGRN_HEADS_PER_STEP
    assert nh % hb == 0
    wide = hb * HEAD_DIM
    ng = nh // hb

    def col(off):
        return pl.BlockSpec((t, wide), lambda b, h, i: (b * nt + i, off + h))

    return pl.pallas_call(
        functools.partial(_hgrn_kernel, nchunk=t // CHUNK, hb=hb, scale=HEAD_DIM ** -0.5),
        grid=(batch, ng, nt),
        in_specs=[col(0), col(ng), col(2 * ng), col(3 * ng),
                  pl.BlockSpec((1, wide), lambda b, h, i: (0, h)),
                  pl.BlockSpec((1, HEAD_DIM), lambda b, h, i: (0, 0)),
                  pl.BlockSpec(cm3.shape, lambda b, h, i: (0, 0)),
                  pl.BlockSpec(mk.shape, lambda b, h, i: (0, 0, 0))],
        out_specs=pl.BlockSpec((t, wide), lambda b, h, i: (b * nt + i, h)),
        out_shape=jax.ShapeDtypeStruct((m, group), BF16),
        scratch_shapes=[pltpu.VMEM((hb, HEAD_DIM, HEAD_DIM), F32)],
        compiler_params=_params("parallel", "parallel", "arbitrary"),
        name="hgrn2",
    )(proj, proj, proj, proj, lb.reshape(1, group).astype(F32), gnorm.reshape(1, HEAD_DIM).astype(F32), cm3, mk)


_CONV_PAD = 8


def _mlstm_constants():
    idx = np.arange(CHUNK)
    tri = (idx[None, :] <= idx[:, None]).astype(np.float32)
    zero = np.zeros_like(tri)
    tri3 = np.concatenate([tri, tri, tri, zero], axis=1)
    trit3 = np.concatenate([tri.T, tri.T, tri.T, zero], axis=0)
    trit3 = np.concatenate([trit3, np.zeros_like(trit3)], axis=1)
    return jnp.asarray(tri3, dtype=BF16), jnp.asarray(trit3, dtype=BF16)


def _mlstm_kernel(qk_ref, v_ref, og_ref, gc_ref, gr_ref, cw_ref, bc_ref, br_ref, hn_ref, tri_ref, trit_ref,
                  o_ref, xp_ref, qk_s, c_ref, m_ref, *, nchunk, nh, dqk, dv, t, g_off):
    @pl.when(pl.program_id(1) == 0)
    def _():
        xp_ref[0:_CONV_PAD, :] = jnp.zeros((_CONV_PAD, xp_ref.shape[1]), F32)
        c_ref[...] = jnp.zeros_like(c_ref)
        m_ref[...] = jnp.zeros_like(m_ref)

    xp_ref[_CONV_PAD:_CONV_PAD + t, :] = qk_ref[...]
    conv = jnp.zeros((t, 2 * nh * dqk), F32)
    for j in range(CONV_W):
        start = _CONV_PAD - (CONV_W - 1) + j
        conv = conv + cw_ref[j:j + 1, :] * xp_ref[start:start + t, :]
    qk_s[...] = conv * _sigmoid(conv)
    xp_ref[0:_CONV_PAD, :] = xp_ref[t:t + _CONV_PAD, :]

    hn = hn_ref[...]
    kscale = dqk ** -0.5
    r_i = lax.broadcasted_iota(jnp.int32, (CHUNK, CHUNK), 0)
    c_i = lax.broadcasted_iota(jnp.int32, (CHUNK, CHUNK), 1)
    causal = c_i <= r_i
    ones = jnp.ones((CHUNK, dv), BF16)
    zpad_c = jnp.zeros((CHUNK, LANE), BF16)
    zpad_r = jnp.zeros((2 * nh, CHUNK), BF16)

    def body(c, carry):
        base = pl.multiple_of(c * CHUNK, CHUNK)
        rows = pl.ds(base, CHUNK)
        qk = qk_s[rows, :]
        gcol = gc_ref[rows, :] + bc_ref[...]
        b_c = jnp.dot(tri_ref[...], jnp.concatenate(_split3(_log_sigmoid(gcol)) + (zpad_c,), axis=0),
                      preferred_element_type=F32)
        grow = gr_ref[c] + br_ref[...]
        ig_r = grow[0:nh, :]
        b_r = jnp.dot(jnp.concatenate(_split3(_log_sigmoid(grow)) + (zpad_r,), axis=1), trit_ref[...],
                      preferred_element_type=F32)[nh:2 * nh, 0:CHUNK]
        for h in range(nh):
            ig_col = gcol[:, g_off + h:g_off + h + 1]
            q_h = qk[:, h * dqk:(h + 1) * dqk].astype(BF16)
            k_f = qk[:, (nh + h) * dqk:(nh + h + 1) * dqk] * kscale
            k_h = k_f.astype(BF16)
            v_ext = jnp.concatenate([v_ref[rows, h * dv:(h + 1) * dv].astype(BF16), ones], axis=1)
            bcol = b_c[:, g_off + nh + h:g_off + nh + h + 1]
            m_prev = m_ref[h:h + 1, 0:1]
            log_d = jnp.where(causal, bcol - b_r[h:h + 1, :] + ig_r[h:h + 1, :], NEG)
            log_inter = bcol + m_prev
            m_t = jnp.maximum(jnp.max(log_d, axis=1, keepdims=True), log_inter)
            qk_w = (lax.dot_general(q_h, k_h, NT_DIMS, preferred_element_type=F32)
                    * jnp.exp(log_d - m_t))
            w_inter = jnp.exp(log_inter - m_t)
            c_ext = c_ref[h]
            numden = (jnp.dot(qk_w.astype(BF16), v_ext, preferred_element_type=F32)
                      + w_inter * jnp.dot(q_h, c_ext.astype(BF16), preferred_element_type=F32))
            num = numden[:, 0:dv]
            den = numden[:, dv:2 * dv]
            hh = num / jnp.maximum(jnp.abs(den), jnp.exp(-m_t))
            m_new = m_t[CHUNK - 1:CHUNK, :]
            b_last = bcol[CHUNK - 1:CHUNK, :]
            w_state = jnp.exp(b_last - bcol + ig_col - m_new)
            decay = jnp.exp(b_last + m_prev - m_new)
            c_ref[h] = decay * c_ext + lax.dot_general((k_f * w_state).astype(BF16), v_ext, TN_DIMS,
                                                       preferred_element_type=F32)
            m_ref[h:h + 1, :] = jnp.broadcast_to(m_new, (1, LANE))
            ms = jnp.mean(hh * hh, axis=1, keepdims=True)
            gate = _sigmoid(og_ref[rows, h * dv:(h + 1) * dv])
            o_ref[rows, h * dv:(h + 1) * dv] = (hh * lax.rsqrt(ms + EPS) * hn * gate).astype(o_ref.dtype)
        return carry

    lax.fori_loop(0, nchunk, body, 0, unroll=True)


def _mlstm(proj, tail, conv_w, gate_bias, hnorm, *, batch, seq, group, g_off):
    m = batch * seq
    nh = group // HEAD_DIM
    dqk = HEAD_DIM // 2
    dv = HEAD_DIM
    t = _pick_tile(seq, (256, 128, 64))
    nt = seq // t
    tri3, trit3 = _mlstm_constants()
    qk_blk = 4
    gates_row = (tail[:, g_off:g_off + 2 * nh].reshape(batch, seq // CHUNK, CHUNK, 2 * nh)
                 .transpose(0, 1, 3, 2))
    bias = gate_bias.astype(F32)
    bias_lane = jnp.zeros((1, LANE), F32).at[0, g_off:g_off + 2 * nh].set(bias)

    def wide(off):
        return pl.BlockSpec((t, group), lambda b, i: (b * nt + i, off))

    def const(shape):
        return pl.BlockSpec(shape, lambda b, i: tuple(0 for _ in shape))

    return pl.pallas_call(
        functools.partial(_mlstm_kernel, nchunk=t // CHUNK, nh=nh, dqk=dqk, dv=dv, t=t, g_off=g_off),
        grid=(batch, nt),
        in_specs=[wide(qk_blk), wide(qk_blk + 1), wide(qk_blk + 2),
                  pl.BlockSpec((t, LANE), lambda b, i: (b * nt + i, 0)),
                  pl.BlockSpec((None, t // CHUNK, 2 * nh, CHUNK), lambda b, i: (b, i, 0, 0)),
                  const((CONV_W, group)), const((1, LANE)), const((2 * nh, 1)), const((1, dv)),
                  const(tri3.shape), const(trit3.shape)],
        out_specs=pl.BlockSpec((t, group), lambda b, i: (b * nt + i, 0)),
        out_shape=jax.ShapeDtypeStruct((m, group), BF16),
        scratch_shapes=[pltpu.VMEM((t + _CONV_PAD, group), F32),
                        pltpu.VMEM((t, group), F32),
                        pltpu.VMEM((nh, dqk, 2 * dv), F32),
                        pltpu.VMEM((nh, LANE), F32)],
        compiler_params=_params("parallel", "arbitrary"),
        name="mlstm",
    )(proj, proj, proj, tail, gates_row, conv_w.astype(F32), bias_lane, bias.reshape(2 * nh, 1),
      hnorm.reshape(1, dv).astype(F32), tri3, trit3)


def _ki_kernel(x_ref, g_ref, o_ref):
    x = x_ref[:, 0:IDX_DIM]
    ms = jnp.mean(x * x, axis=-1, keepdims=True)
    o_ref[...] = (x * lax.rsqrt(ms + EPS) * g_ref[...]).astype(o_ref.dtype)


def _ki_norm(tail, g):
    m = tail.shape[0]
    tm = _pick_tile(m, (512, 256, 128, 64))
    return pl.pallas_call(
        _ki_kernel,
        grid=(m // tm,),
        in_specs=[pl.BlockSpec((tm, LANE), lambda i: (i, 0)), pl.BlockSpec((1, IDX_DIM), lambda i: (0, 0))],
        out_specs=pl.BlockSpec((tm, IDX_DIM), lambda i: (i, 0)),
        out_shape=jax.ShapeDtypeStruct((m, IDX_DIM), BF16),
        compiler_params=_params("parallel"),
        name="ki_norm",
    )(tail, g.reshape(1, IDX_DIM).astype(F32))


def _diff_kernel(q_ref, k_ref, vt_ref, lam_ref, sn_ref, o_ref, m_ref, l_ref, acc_ref, *, tq, tk, dqk, out_scale):
    qi = pl.program_id(2)
    last_k = ((qi + 1) * tq - 1) // tk
    m_ref[...] = jnp.full_like(m_ref, NEG)
    l_ref[...] = jnp.zeros_like(l_ref)
    acc_ref[...] = jnp.zeros_like(acc_ref)
    q = q_ref[...]

    def step(kb, masked):
        rows = pl.ds(pl.multiple_of(kb * tk, tk), tk)
        k = k_ref[rows, :]
        vt = vt_ref[kb]
        if masked:
            k_chunk = (kb * tk + lax.broadcasted_iota(jnp.int32, (tk, tq), 0)) // CHUNK
            q_chunk = (qi * tq + lax.broadcasted_iota(jnp.int32, (tk, tq), 1)) // CHUNK
            mask = k_chunk <= q_chunk
        for half in range(2):
            s = lax.dot_general(k[:, half * dqk:(half + 1) * dqk], q[:, half * dqk:(half + 1) * dqk], NT_DIMS,
                                preferred_element_type=F32)
            if masked:
                s = jnp.where(mask, s, NEG)
            m_prev = m_ref[half]
            m_new = jnp.maximum(m_prev, jnp.max(s, axis=0, keepdims=True))
            p = jnp.exp2(s - m_new)
            alpha = jnp.exp2(m_prev - m_new)
            l_ref[half] = alpha * l_ref[half] + jnp.sum(p, axis=0, keepdims=True)
            acc_ref[half] = alpha * acc_ref[half] + jnp.dot(vt, p.astype(BF16), preferred_element_type=F32)
            m_ref[half] = m_new

    def body(kb, carry):
        step(kb, False)
        return carry

    lax.fori_loop(0, last_k, body, 0)
    step(last_k, True)

    o = acc_ref[0] / l_ref[0] - lam_ref[...] * (acc_ref[1] / l_ref[1])
    ms = jnp.mean(o * o, axis=0, keepdims=True)
    o_ref[...] = (o * lax.rsqrt(ms + EPS) * (sn_ref[...] * out_scale)).T.astype(o_ref.dtype)


def _diff_attention(cd, lam_full, subln, lambda_init, *, batch, seq, group):
    m = batch * seq
    dqk = HEAD_DIM
    dv = 2 * HEAD_DIM
    nh = group // dv
    tq = _pick_tile(seq, (1024, 512, 256, 128))
    tk = _pick_tile(seq, (1024, 512, 256, 128))
    assert tk % tq == 0
    nq, nkb = seq // tq, seq // tk
    k_off = nh
    v_t = cd[:, 2 * group:3 * group].reshape(batch, nkb, tk, nh, dv).transpose(0, 1, 3, 4, 2)

    return pl.pallas_call(
        functools.partial(_diff_kernel, tq=tq, tk=tk, dqk=dqk, out_scale=1.0 - lambda_init),
        grid=(batch, nh, nq),
        in_specs=[pl.BlockSpec((tq, dv), lambda b, h, i: (b * nq + i, h)),
                  pl.BlockSpec((seq, dv), lambda b, h, i: (b, k_off + h)),
                  pl.BlockSpec((None, nkb, None, dv, tk), lambda b, h, i: (b, 0, h, 0, 0)),
                  pl.BlockSpec((1, 1), lambda b, h, i: (0, 0)),
                  pl.BlockSpec((dv, 1), lambda b, h, i: (0, 0))],
        out_specs=pl.BlockSpec((tq, dv), lambda b, h, i: (b * nq + i, h)),
        out_shape=jax.ShapeDtypeStruct((m, group), BF16),
        scratch_shapes=[pltpu.VMEM((2, 1, tq), F32), pltpu.VMEM((2, 1, tq), F32), pltpu.VMEM((2, dv, tq), F32)],
        compiler_params=_params("parallel", "parallel", "arbitrary"),
        name="diff_attn",
    )(cd, cd, v_t, lam_full.reshape(1, 1).astype(F32), subln.reshape(dv, 1).astype(F32))


_INT_MIN = -2 ** 31
_HI16 = -(1 << 16)


_CNT_ROWS = 64
_DSA_HEADS_PER_STEP = 2


def _dsa_kernel(q_ref, k_ref, vt_ref, qi_ref, ki_ref, wt_ref, o_ref, sc_ref, sc16_ref, *, tq, tk, topk, hpb,
                w_scale):
    qb = pl.program_id(1)
    h = pl.program_id(2)
    nkv = ((qb + 1) * tq + tk - 1) // tk

    @pl.when(h == 0)
    def _():
        w = wt_ref[...] * w_scale
        q_chunk = (qb * tq + lax.broadcasted_iota(jnp.int32, (tk, tq), 1)) // CHUNK

        def score_body(kb, carry):
            rows = pl.ds(pl.multiple_of(kb * tk, tk), tk)
            ki = ki_ref[rows, :]
            acc = jnp.zeros((tk, tq), F32)
            for ih in range(IDX_HEADS):
                lg = lax.dot_general(ki, qi_ref[:, ih * IDX_DIM:(ih + 1) * IDX_DIM], NT_DIMS,
                                     preferred_element_type=F32)
                acc = acc + w[ih:ih + 1, :] * jnp.maximum(lg, 0.0)
            k_chunk = (kb * tk + lax.broadcasted_iota(jnp.int32, (tk, tq), 0)) // CHUNK
            x = jnp.where(k_chunk <= q_chunk, acc, NEG)
            sc_ref[kb] = x
            hi = lax.bitcast_convert_type(x, jnp.int32) & _HI16
            sc16_ref[kb] = lax.bitcast_convert_type(hi, F32).astype(BF16)
            return carry

        lax.fori_loop(0, nkv, score_body, 0)

        def count_ge(cf):
            def cnt_body(kb, cnt):
                for r in range(tk // _CNT_ROWS):
                    hit = jnp.where(sc_ref[kb, r * _CNT_ROWS:(r + 1) * _CNT_ROWS, :] >= cf, 1.0, 0.0)
                    for g in range(_CNT_ROWS // 8):
                        cnt = cnt + hit[g * 8:(g + 1) * 8, :]
                return cnt

            return jnp.sum(lax.fori_loop(0, nkv, cnt_body, jnp.zeros((8, tq), F32)), axis=0, keepdims=True)

        def count_ge_hi(cfb):
            one, zero = jnp.ones((), BF16), jnp.zeros((), BF16)

            def cnt_body(kb, cnt):
                part = jnp.zeros((16, tq), BF16)
                for r in range(tk // _CNT_ROWS):
                    hit = jnp.where(sc16_ref[kb, r * _CNT_ROWS:(r + 1) * _CNT_ROWS, :] >= cfb, one, zero)
                    for g in range(_CNT_ROWS // 16):
                        part = part + hit[g * 16:(g + 1) * 16, :]
                return cnt + part.astype(F32)

            return jnp.sum(lax.fori_loop(0, nkv, cnt_body, jnp.zeros((16, tq), F32)), axis=0, keepdims=True)

        def bit_body(i, u, *, coarse):
            cand = u | jnp.left_shift(jnp.int32(1), 31 - i)
            key = cand ^ _INT_MIN
            if coarse:
                key = jnp.where(key < 0, key | 0xFFFF, key)
            cf = lax.bitcast_convert_type(key ^ (jnp.right_shift(key, 31) & 0x7FFFFFFF), F32)
            cnt = count_ge_hi(cf.astype(BF16)) if coarse else count_ge(cf)
            return jnp.where(cnt >= float(topk), cand, u)

        u = lax.fori_loop(0, 16, functools.partial(bit_body, coarse=True), jnp.zeros((1, tq), jnp.int32))
        u = lax.fori_loop(16, 32, functools.partial(bit_body, coarse=False), u)
        key = u ^ _INT_MIN
        thr = lax.bitcast_convert_type(key ^ (jnp.right_shift(key, 31) & 0x7FFFFFFF), F32)

        def bias_body(kb, carry):
            s = sc_ref[kb]
            sc_ref[kb] = jnp.where((s >= thr) & (s > 0.5 * NEG), 0.0, NEG)
            return carry

        lax.fori_loop(0, nkv, bias_body, 0)

    q = q_ref[...]

    def kv_body(kb, carry):
        rows = pl.ds(pl.multiple_of(kb * tk, tk), tk)
        k = k_ref[rows, :]
        bias = sc_ref[kb]
        out = []
        for a in range(hpb):
            m_prev, l_prev, acc = carry[a]
            hs = slice(a * HEAD_DIM, (a + 1) * HEAD_DIM)
            s = lax.dot_general(k[:, hs], q[:, hs], NT_DIMS, preferred_element_type=F32) + bias
            m_new = jnp.maximum(m_prev, jnp.max(s, axis=0, keepdims=True))
            p = jnp.exp2(s - m_new)
            alpha = jnp.exp2(m_prev - m_new)
            l_new = alpha * l_prev + jnp.sum(p, axis=0, keepdims=True)
            acc = alpha * acc + jnp.dot(vt_ref[kb, a], p.astype(BF16), preferred_element_type=F32)
            out.append((m_new, l_new, acc))
        return tuple(out)

    init = tuple((jnp.full((1, tq), NEG, F32), jnp.zeros((1, tq), F32), jnp.zeros((HEAD_DIM, tq), F32))
                 for _ in range(hpb))
    fin = lax.fori_loop(0, nkv, kv_body, init)
    for a in range(hpb):
        _, l_fin, acc = fin[a]
        o_ref[:, a * HEAD_DIM:(a + 1) * HEAD_DIM] = (acc / l_fin).T.astype(o_ref.dtype)


def _dsa_attention(cd, ki_n, tail, *, batch, seq, group, w_off):
    m = batch * seq
    nh = group // HEAD_DIM
    tq = _pick_tile(seq, (512, 256, 128))
    tk = _pick_tile(seq, (1024, 512, 256))
    topk = min(TOPK_MAX, seq // 4)
    assert tk >= topk, "the threshold search needs at least topk keys in range"
    nq = seq // tq
    nc = group // LANE
    nkb = seq // tk
    q_off, k_off, v_off = 3 * nc, 4 * nc, 5 * nc
    qi_off = 6 * nc * LANE // (IDX_HEADS * IDX_DIM)
    assert qi_off * IDX_HEADS * IDX_DIM == 6 * nc * LANE
    v_t = (cd[:, v_off * LANE:(v_off + nc) * LANE].reshape(batch, nkb, tk, nh, HEAD_DIM)
           .transpose(0, 1, 3, 4, 2))
    w_t = tail[:, w_off:w_off + IDX_HEADS].reshape(batch, seq, IDX_HEADS).transpose(0, 2, 1)

    hpb = _DSA_HEADS_PER_STEP
    assert nh % hpb == 0
    wide = hpb * HEAD_DIM
    return pl.pallas_call(
        functools.partial(_dsa_kernel, tq=tq, tk=tk, topk=topk, hpb=hpb,
                          w_scale=IDX_HEADS ** -0.5 * IDX_DIM ** -0.5),
        grid=(batch, nq, nh // hpb),
        in_specs=[pl.BlockSpec((tq, wide), lambda b, i, h: (b * nq + i, q_off // hpb + h)),
                  pl.BlockSpec((seq, wide), lambda b, i, h: (b, k_off // hpb + h)),
                  pl.BlockSpec((None, nkb, hpb, HEAD_DIM, tk), lambda b, i, h: (b, 0, h, 0, 0)),
                  pl.BlockSpec((tq, IDX_HEADS * IDX_DIM), lambda b, i, h: (b * nq + i, qi_off)),
                  pl.BlockSpec((seq, IDX_DIM), lambda b, i, h: (b, 0)),
                  pl.BlockSpec((None, IDX_HEADS, tq), lambda b, i, h: (b, 0, i))],
        out_specs=pl.BlockSpec((tq, wide), lambda b, i, h: (b * nq + i, h)),
        out_shape=jax.ShapeDtypeStruct((m, group), BF16),
        scratch_shapes=[pltpu.VMEM((nkb, tk, tq), F32), pltpu.VMEM((nkb, tk, tq), BF16)],
        compiler_params=_params("parallel", "arbitrary", "arbitrary"),
        name="dsa",
    )(cd, cd, v_t, cd, ki_n, w_t)


def _repack_w_in(w, group, nhb):
    p0 = 7 * group
    p1 = p0 + 2 * nhb
    p2 = p1 + 6 * group + IDX_HEADS * IDX_DIM
    used = IDX_DIM + 2 * nhb + IDX_HEADS
    wt = jnp.swapaxes(w, 1, 2)
    pad = jnp.zeros((w.shape[0], LANE - used, w.shape[1]), w.dtype)
    return jnp.concatenate([wt[:, :p0], wt[:, p1:p2], wt[:, p2:p2 + IDX_DIM], wt[:, p0:p1],
                            wt[:, p2 + IDX_DIM:], pad], axis=1).astype(BF16)


def kernel(x, norm_mix, w_in, hgrn_lb_logits, hgrn_gnorm, mlstm_conv, mlstm_gate_bias, mlstm_hnorm,
           diff_qk_norm, diff_lambda, diff_subln, dsa_qk_norm, dsa_idx_knorm, w_out, norm_mlp, w1, w2):
    batch, seq, d_model = x.shape
    depth = w_in.shape[0]
    group = d_model // 4
    nhb = group // HEAD_DIM
    nc = group // LANE
    m = batch * seq
    dims = dict(batch=batch, seq=seq, group=group)

    p_lb = jax.nn.softmax(hgrn_lb_logits.astype(F32), axis=0)
    lower_bounds = jnp.cumsum(p_lb, axis=0) - p_lb[0:1]

    w_in_b = _repack_w_in(w_in, group, nhb)
    w_out_b, w1_b, w2_b = (w.astype(BF16) for w in (w_out, w1, w2))
    n_ab, n_cd = 7 * group, 6 * group + IDX_HEADS * IDX_DIM

    x2 = x.reshape(m, d_model).astype(F32)
    ones_g = jnp.ones((group,), F32)
    for l in range(depth):
        lambda_init = 0.8 - 0.6 * math.exp(-0.3 * l)
        h = _rmsnorm_rows(x2, norm_mix[l])
        proj = _matmul(h, w_in_b, layer=l, cols=(0, n_ab), b_rows=True, out_dtype=F32, name="in_proj_ab")
        tail = _matmul(h, w_in_b, layer=l, cols=(n_ab + n_cd, LANE), b_rows=True, out_dtype=F32,
                       name="in_proj_tail")

        y_a = _hgrn(proj, lower_bounds[l], hgrn_gnorm[l], **dims)

        y_b = _mlstm(proj, tail, mlstm_conv[l], mlstm_gate_bias[l], mlstm_hnorm[l], g_off=IDX_DIM, **dims)

        cqn = diff_qk_norm[l].astype(F32)
        dqn = dsa_qk_norm[l].astype(F32)
        q_scale = HEAD_DIM ** -0.5 * LOG2_E
        gains = jnp.concatenate([
            jnp.tile(cqn[0], nc) * q_scale, jnp.tile(cqn[1], nc), ones_g,
            jnp.tile(dqn[0], nc) * q_scale, jnp.tile(dqn[1], nc), ones_g,
            jnp.ones((IDX_HEADS * IDX_DIM,), F32)]).reshape(1, -1)
        use_norm = jnp.concatenate([
            jnp.ones((2 * group,), F32), jnp.zeros((group,), F32),
            jnp.ones((2 * group,), F32), jnp.zeros((group + IDX_HEADS * IDX_DIM,), F32)]).reshape(1, -1)
        cd = _matmul(h, w_in_b, layer=l, cols=(n_ab, n_cd), b_rows=True, out_dtype=BF16, act="groupnorm",
                     extra=(gains, use_norm), name="in_proj_cd")
        ki_n = _ki_norm(tail, dsa_idx_knorm[l])

        lam = diff_lambda[l].astype(F32)
        lam_full = jnp.exp(jnp.sum(lam[0] * lam[1])) - jnp.exp(jnp.sum(lam[2] * lam[3])) + lambda_init
        y_c = _diff_attention(cd, lam_full, diff_subln[l], lambda_init, **dims)
        y_d = _dsa_attention(cd, ki_n, tail, w_off=IDX_DIM + 2 * nhb, **dims)

        x2 = _matmul((y_a, y_b, y_c, y_d), w_out_b, layer=l, out_dtype=F32, act="residual", extra=(x2,),
                     tm_max=512, name="out_proj")
        h = _rmsnorm_rows(x2, norm_mlp[l])
        hidden = _matmul(h, w1_b, layer=l, out_dtype=BF16, act="relu2", name="mlp_up")
        x2 = _matmul(hidden, w2_b, layer=l, out_dtype=F32, act="residual", extra=(x2,), tk_max=2048,
                     name="mlp_down")
    return x2.reshape(batch, seq, d_model).astype(x.dtype)
```

```python
import functools
import math

import numpy as np
import jax
import jax.numpy as jnp
from jax import lax
from jax.experimental import pallas as pl
from jax.experimental.pallas import tpu as pltpu

CHUNK = 64
EPS = 1e-6
NEG = -1e30
F_FLOOR = 1e-20
HEAD_DIM = 128
CONV_W = 4
IDX_HEADS = 16
IDX_DIM = 64
TOPK_MAX = 256
LOG2_E = math.log2(math.e)
LANE = 128
VMEM_LIMIT = 56 * 1024 * 1024

F32 = jnp.float32
BF16 = jnp.bfloat16
NT_DIMS = (((1,), (1,)), ((), ()))
TN_DIMS = (((0,), (0,)), ((), ()))


def _params(*sem):
    return pltpu.CompilerParams(dimension_semantics=sem, vmem_limit_bytes=VMEM_LIMIT)


def _sigmoid(x):
    return 1.0 / (1.0 + jnp.exp(-x))


def _log_sigmoid(x):
    return jnp.minimum(x, 0.0) - jnp.log1p(jnp.exp(-jnp.abs(x)))


def _split3(x):
    hi = x.astype(BF16)
    r1 = x - hi.astype(F32)
    mid = r1.astype(BF16)
    lo = (r1 - mid.astype(F32)).astype(BF16)
    return hi, mid, lo


def _pick_tile(n, candidates):
    for c in candidates:
        if n % c == 0:
            return c
    raise ValueError(f"no tile in {candidates} divides {n}")


def _rmsnorm_kernel(x_ref, g_ref, o_ref):
    x = x_ref[...]
    ms = jnp.mean(x * x, axis=-1, keepdims=True)
    o_ref[...] = (x * lax.rsqrt(ms + EPS) * g_ref[...]).astype(o_ref.dtype)


def _rmsnorm_rows(x2, g):
    m, d = x2.shape
    tm = _pick_tile(m, (256, 128, 64, 8))
    return pl.pallas_call(
        _rmsnorm_kernel,
        grid=(m // tm,),
        in_specs=[pl.BlockSpec((tm, d), lambda i: (i, 0)), pl.BlockSpec((1, d), lambda i: (0, 0))],
        out_specs=pl.BlockSpec((tm, d), lambda i: (i, 0)),
        out_shape=jax.ShapeDtypeStruct((m, d), BF16),
        compiler_params=_params("parallel"),
        name="rmsnorm",
    )(x2, g.reshape(1, d).astype(F32))


MM_MAX_TK = 4096
MM_TN = 1024


def _mm_epilogue(r, act, extra):
    if act == "relu2":
        r = jnp.square(jnp.maximum(r, 0.0))
    elif act == "groupnorm":
        g_ref, u_ref = extra
        segs = []
        for c in range(r.shape[1] // LANE):
            seg = r[:, c * LANE:(c + 1) * LANE]
            ms = jnp.mean(seg * seg, axis=1, keepdims=True)
            inv = jnp.where(u_ref[:, c * LANE:(c + 1) * LANE] > 0.0, lax.rsqrt(ms + EPS), 1.0)
            segs.append(seg * inv * g_ref[:, c * LANE:(c + 1) * LANE])
        r = jnp.concatenate(segs, axis=1)
    elif act == "residual":
        r = r + extra[0][...]
    return r


def _mm_kernel(*refs, n_a, nk, act, n_extra, b_rows):
    a_refs = refs[:n_a]
    b_ref = refs[n_a]
    extra = refs[n_a + 1:n_a + 1 + n_extra]
    o_ref = refs[n_a + 1 + n_extra]

    def product():
        r, off = None, 0
        for a_ref in a_refs:
            kw = a_ref.shape[1]
            if b_rows:
                d = lax.dot_general(a_ref[...], b_ref[:, off:off + kw], NT_DIMS, preferred_element_type=F32)
            else:
                d = jnp.dot(a_ref[...], b_ref[off:off + kw, :], preferred_element_type=F32)
            r = d if r is None else r + d
            off += kw
        return r

    if nk == 1:
        o_ref[...] = _mm_epilogue(product(), act, extra).astype(o_ref.dtype)
        return
    acc_ref = refs[n_a + 2 + n_extra]
    k = pl.program_id(2)

    @pl.when(k == 0)
    def _():
        acc_ref[...] = jnp.zeros_like(acc_ref)

    acc_ref[...] += product()

    @pl.when(k == nk - 1)
    def _():
        o_ref[...] = _mm_epilogue(acc_ref[...], act, extra).astype(o_ref.dtype)


def _matmul(a, b, *, out_dtype, layer, cols=None, b_rows=False, act=None, extra=(), tm_max=1024, tn_max=MM_TN,
            tk_max=MM_MAX_TK, name="matmul"):
    a_parts = a if isinstance(a, tuple) else (a,)
    m = a_parts[0].shape[0]
    kdim = sum(p.shape[1] for p in a_parts)
    col0, n = (0, b.shape[1 if b_rows else 2]) if cols is None else cols
    tm = _pick_tile(m, tuple(t for t in (1024, 512, 256, 128, 8) if t <= tm_max))
    tn = _pick_tile(math.gcd(n, col0), tuple(t for t in (1024, 512, 256, 128) if t <= tn_max))
    j0 = col0 // tn
    tk = kdim if kdim <= tk_max else _pick_tile(kdim, tuple(t for t in (4096, 2048, 1024, 512, 256, 128)
                                                            if t <= tk_max))
    nk = kdim // tk
    assert nk == 1 or len(a_parts) == 1
    in_specs = [pl.BlockSpec((tm, tk if nk > 1 else p.shape[1]), lambda i, j, k: (i, k)) for p in a_parts]
    if b_rows:
        in_specs.append(pl.BlockSpec((None, tn, tk), lambda i, j, k: (layer, j0 + j, k)))
    else:
        in_specs.append(pl.BlockSpec((None, tk, tn), lambda i, j, k: (layer, k, j0 + j)))
    for e in extra:
        if e.shape[0] == 1:
            in_specs.append(pl.BlockSpec((1, tn), lambda i, j, k: (0, j)))
        else:
            in_specs.append(pl.BlockSpec((tm, tn), lambda i, j, k: (i, j)))
    return pl.pallas_call(
        functools.partial(_mm_kernel, n_a=len(a_parts), nk=nk, act=act, n_extra=len(extra), b_rows=b_rows),
        grid=(m // tm, n // tn, nk),
        in_specs=in_specs,
        out_specs=pl.BlockSpec((tm, tn), lambda i, j, k: (i, j)),
        out_shape=jax.ShapeDtypeStruct((m, n), out_dtype),
        scratch_shapes=[pltpu.VMEM((tm, tn), F32)] if nk > 1 else [],
        compiler_params=_params("parallel", "parallel", "arbitrary"),
        name=name,
    )(*a_parts, b, *extra)


_HGRN_LEVELS = 6
_HGRN_HEADS_PER_STEP = 4


def _hgrn_constants():
    idx = np.arange(CHUNK)
    j = idx[None, :]
    r = idx[:, None]
    blocks = [j <= r, j > r]
    masks = []
    for lvl in range(_HGRN_LEVELS):
        half = (CHUNK // 2) >> lvl
        parent = idx // (2 * half)
        mid = parent * 2 * half + half
        is_right = (idx % (2 * half)) >= half
        right_rng = (j >= mid[:, None]) & (j <= r)
        left_rng = (j > r) & (j < mid[:, None])
        blocks.append(np.where(is_right[:, None], right_rng, left_rng))
        masks.append((parent[:, None] == parent[None, :]) & is_right[:, None] & (~is_right)[None, :])
    cm = np.concatenate(blocks, axis=0).astype(np.float32)
    cm3 = np.concatenate([cm, cm, cm, np.zeros_like(cm)], axis=1)
    mk = np.stack(masks, axis=0).astype(np.float32)
    return jnp.asarray(cm3, dtype=BF16), jnp.asarray(mk, dtype=F32)


def _hgrn_kernel(q_ref, f_ref, i_ref, g_ref, lb_ref, gn_ref, cm_ref, mk_ref, o_ref, st_ref, *, nchunk, hb, scale):
    @pl.when(pl.program_id(2) == 0)
    def _():
        st_ref[...] = jnp.zeros_like(st_ref)

    lb = lb_ref[...]
    gn = gn_ref[...]

    def body(c, carry):
        rows = pl.ds(pl.multiple_of(c * CHUNK, CHUNK), CHUNK)
        f = lb + (1.0 - lb) * _sigmoid(f_ref[rows, :])
        logf = jnp.log(jnp.maximum(f, F_FLOOR))
        kk_w = 1.0 - f
        qr = q_ref[rows, :]
        qq_w = qr * _sigmoid(qr) * scale
        v_w = i_ref[rows, :]
        gr = g_ref[rows, :]
        gate_w = gr * _sigmoid(gr)
        l3 = jnp.concatenate(_split3(logf) + (jnp.zeros(logf.shape, BF16),), axis=0)
        sums_w = jnp.dot(cm_ref[...], l3, preferred_element_type=F32)
        for a in range(hb):
            hs = slice(a * HEAD_DIM, (a + 1) * HEAD_DIM)
            sums, qq, kk, v = sums_w[:, hs], qq_w[:, hs], kk_w[:, hs], v_w[:, hs]
            vb = v.astype(BF16)
            b = sums[0:CHUNK]
            b_rest = sums[CHUNK:2 * CHUNK]
            scores = jnp.zeros((CHUNK, CHUNK), F32)
            for lvl in range(_HGRN_LEVELS):
                e = jnp.exp(sums[(2 + lvl) * CHUNK:(3 + lvl) * CHUNK])
                s = lax.dot_general((qq * e).astype(BF16), (kk * e).astype(BF16), NT_DIMS,
                                    preferred_element_type=F32)
                scores = scores + s * mk_ref[lvl]
            diag = jnp.sum(qq * kk, axis=1, keepdims=True)
            st = st_ref[a]
            o = (jnp.dot(scores.astype(BF16), vb, preferred_element_type=F32) + diag * v
                 + lax.dot_general((qq * jnp.exp(b)).astype(BF16), st.astype(BF16), NT_DIMS,
                                   preferred_element_type=F32))
            ke = (kk * jnp.exp(b_rest)).astype(BF16)
            st_ref[a] = (st * jnp.exp(b[CHUNK - 1:CHUNK, :])
                         + lax.dot_general(vb, ke, TN_DIMS, preferred_element_type=F32))
            ms = jnp.mean(o * o, axis=1, keepdims=True)
            o_ref[rows, hs] = (o * lax.rsqrt(ms + EPS) * gn * gate_w[:, hs]).astype(o_ref.dtype)
        return carry

    lax.fori_loop(0, nchunk, body, 0, unroll=True)


def _hgrn(proj, lb, gnorm, *, batch, seq, group):
    m = batch * seq
    nh = group // HEAD_DIM
    t = _pick_tile(seq, (512, 256, 128, 64))
    nt = seq // t
    cm3, mk = _hgrn_constants()

    hb = min(_HGRN_HEADS_PER_STEP, nh)
    assert nh % hb == 0
    wide = hb * HEAD_DIM
    ng = nh // hb

    def col(off):
        return pl.BlockSpec((t, wide), lambda b, h, i: (b * nt + i, off + h))

    return pl.pallas_call(
        functools.partial(_hgrn_kernel, nchunk=t // CHUNK, hb=hb, scale=HEAD_DIM ** -0.5),
        grid=(batch, ng, nt),
        in_specs=[col(0), col(ng), col(2 * ng), col(3 * ng),
                  pl.BlockSpec((1, wide), lambda b, h, i: (0, h)),
                  pl.BlockSpec((1, HEAD_DIM), lambda b, h, i: (0, 0)),
                  pl.BlockSpec(cm3.shape, lambda b, h, i: (0, 0)),
                  pl.BlockSpec(mk.shape, lambda b, h, i: (0, 0, 0))],
        out_specs=pl.BlockSpec((t, wide), lambda b, h, i: (b * nt + i, h)),
        out_shape=jax.ShapeDtypeStruct((m, group), BF16),
        scratch_shapes=[pltpu.VMEM((hb, HEAD_DIM, HEAD_DIM), F32)],
        compiler_params=_params("parallel", "parallel", "arbitrary"),
        name="hgrn2",
    )(proj, proj, proj, proj, lb.reshape(1, group).astype(F32), gnorm.reshape(1, HEAD_DIM).astype(F32), cm3, mk)


_CONV_PAD = 8


def _mlstm_constants():
    idx = np.arange(CHUNK)
    tri = (idx[None, :] <= idx[:, None]).astype(np.float32)
    zero = np.zeros_like(tri)
    tri3 = np.concatenate([tri, tri, tri, zero], axis=1)
    trit3 = np.concatenate([tri.T, tri.T, tri.T, zero], axis=0)
    trit3 = np.concatenate([trit3, np.zeros_like(trit3)], axis=1)
    return jnp.asarray(tri3, dtype=BF16), jnp.asarray(trit3, dtype=BF16)


def _mlstm_kernel(qk_ref, v_ref, og_ref, gc_ref, gr_ref, cw_ref, bc_ref, br_ref, hn_ref, tri_ref, trit_ref,
                  o_ref, xp_ref, qk_s, c_ref, m_ref, *, nchunk, nh, dqk, dv, t, g_off):
    @pl.when(pl.program_id(1) == 0)
    def _():
        xp_ref[0:_CONV_PAD, :] = jnp.zeros((_CONV_PAD, xp_ref.shape[1]), F32)
        c_ref[...] = jnp.zeros_like(c_ref)
        m_ref[...] = jnp.zeros_like(m_ref)

    xp_ref[_CONV_PAD:_CONV_PAD + t, :] = qk_ref[...]
    conv = jnp.zeros((t, 2 * nh * dqk), F32)
    for j in range(CONV_W):
        start = _CONV_PAD - (CONV_W - 1) + j
        conv = conv + cw_ref[j:j + 1, :] * xp_ref[start:start + t, :]
    qk_s[...] = conv * _sigmoid(conv)
    xp_ref[0:_CONV_PAD, :] = xp_ref[t:t + _CONV_PAD, :]

    hn = hn_ref[...]
    kscale = dqk ** -0.5
    r_i = lax.broadcasted_iota(jnp.int32, (CHUNK, CHUNK), 0)
    c_i = lax.broadcasted_iota(jnp.int32, (CHUNK, CHUNK), 1)
    causal = c_i <= r_i
    ones = jnp.ones((CHUNK, dv), BF16)
    zpad_c = jnp.zeros((CHUNK, LANE), BF16)
    zpad_r = jnp.zeros((2 * nh, CHUNK), BF16)

    def body(c, carry):
        base = pl.multiple_of(c * CHUNK, CHUNK)
        rows = pl.ds(base, CHUNK)
        qk = qk_s[rows, :]
        gcol = gc_ref[rows, :] + bc_ref[...]
        b_c = jnp.dot(tri_ref[...], jnp.concatenate(_split3(_log_sigmoid(gcol)) + (zpad_c,), axis=0),
                      preferred_element_type=F32)
        grow = gr_ref[c] + br_ref[...]
        ig_r = grow[0:nh, :]
        b_r = jnp.dot(jnp.concatenate(_split3(_log_sigmoid(grow)) + (zpad_r,), axis=1), trit_ref[...],
                      preferred_element_type=F32)[nh:2 * nh, 0:CHUNK]
        for h in range(nh):
            ig_col = gcol[:, g_off + h:g_off + h + 1]
            q_h = qk[:, h * dqk:(h + 1) * dqk].astype(BF16)
            k_f = qk[:, (nh + h) * dqk:(nh + h + 1) * dqk] * kscale
            k_h = k_f.astype(BF16)
            v_ext = jnp.concatenate([v_ref[rows, h * dv:(h + 1) * dv].astype(BF16), ones], axis=1)
            bcol = b_c[:, g_off + nh + h:g_off + nh + h + 1]
            m_prev = m_ref[h:h + 1, 0:1]
            log_d = jnp.where(causal, bcol - b_r[h:h + 1, :] + ig_r[h:h + 1, :], NEG)
            log_inter = bcol + m_prev
            m_t = jnp.maximum(jnp.max(log_d, axis=1, keepdims=True), log_inter)
            qk_w = (lax.dot_general(q_h, k_h, NT_DIMS, preferred_element_type=F32)
                    * jnp.exp(log_d - m_t))
            w_inter = jnp.exp(log_inter - m_t)
            c_ext = c_ref[h]
            numden = (jnp.dot(qk_w.astype(BF16), v_ext, preferred_element_type=F32)
                      + w_inter * jnp.dot(q_h, c_ext.astype(BF16), preferred_element_type=F32))
            num = numden[:, 0:dv]
            den = numden[:, dv:2 * dv]
            hh = num / jnp.maximum(jnp.abs(den), jnp.exp(-m_t))
            m_new = m_t[CHUNK - 1:CHUNK, :]
            b_last = bcol[CHUNK - 1:CHUNK, :]
            w_state = jnp.exp(b_last - bcol + ig_col - m_new)
            decay = jnp.exp(b_last + m_prev - m_new)
            c_ref[h] = decay * c_ext + lax.dot_general((k_f * w_state).astype(BF16), v_ext, TN_DIMS,
                                                       preferred_element_type=F32)
            m_ref[h:h + 1, :] = jnp.broadcast_to(m_new, (1, LANE))
            ms = jnp.mean(hh * hh, axis=1, keepdims=True)
            gate = _sigmoid(og_ref[rows, h * dv:(h + 1) * dv])
            o_ref[rows, h * dv:(h + 1) * dv] = (hh * lax.rsqrt(ms + EPS) * hn * gate).astype(o_ref.dtype)
        return carry

    lax.fori_loop(0, nchunk, body, 0, unroll=True)


def _mlstm(proj, tail, conv_w, gate_bias, hnorm, *, batch, seq, group, g_off):
    m = batch * seq
    nh = group // HEAD_DIM
    dqk = HEAD_DIM // 2
    dv = HEAD_DIM
    t = _pick_tile(seq, (256, 128, 64))
    nt = seq // t
    tri3, trit3 = _mlstm_constants()
    qk_blk = 4
    gates_row = (tail[:, g_off:g_off + 2 * nh].reshape(batch, seq // CHUNK, CHUNK, 2 * nh)
                 .transpose(0, 1, 3, 2))
    bias = gate_bias.astype(F32)
    bias_lane = jnp.zeros((1, LANE), F32).at[0, g_off:g_off + 2 * nh].set(bias)

    def wide(off):
        return pl.BlockSpec((t, group), lambda b, i: (b * nt + i, off))

    def const(shape):
        return pl.BlockSpec(shape, lambda b, i: tuple(0 for _ in shape))

    return pl.pallas_call(
        functools.partial(_mlstm_kernel, nchunk=t // CHUNK, nh=nh, dqk=dqk, dv=dv, t=t, g_off=g_off),
        grid=(batch, nt),
        in_specs=[wide(qk_blk), wide(qk_blk + 1), wide(qk_blk + 2),
                  pl.BlockSpec((t, LANE), lambda b, i: (b * nt + i, 0)),
                  pl.BlockSpec((None, t // CHUNK, 2 * nh, CHUNK), lambda b, i: (b, i, 0, 0)),
                  const((CONV_W, group)), const((1, LANE)), const((2 * nh, 1)), const((1, dv)),
                  const(tri3.shape), const(trit3.shape)],
        out_specs=pl.BlockSpec((t, group), lambda b, i: (b * nt + i, 0)),
        out_shape=jax.ShapeDtypeStruct((m, group), BF16),
        scratch_shapes=[pltpu.VMEM((t + _CONV_PAD, group), F32),
                        pltpu.VMEM((t, group), F32),
                        pltpu.VMEM((nh, dqk, 2 * dv), F32),
                        pltpu.VMEM((nh, LANE), F32)],
        compiler_params=_params("parallel", "arbitrary"),
        name="mlstm",
    )(proj, proj, proj, tail, gates_row, conv_w.astype(F32), bias_lane, bias.reshape(2 * nh, 1),
      hnorm.reshape(1, dv).astype(F32), tri3, trit3)


def _ki_kernel(x_ref, g_ref, o_ref):
    x = x_ref[:, 0:IDX_DIM]
    ms = jnp.mean(x * x, axis=-1, keepdims=True)
    o_ref[...] = (x * lax.rsqrt(ms + EPS) * g_ref[...]).astype(o_ref.dtype)


def _ki_norm(tail, g):
    m = tail.shape[0]
    tm = _pick_tile(m, (512, 256, 128, 64))
    return pl.pallas_call(
        _ki_kernel,
        grid=(m // tm,),
        in_specs=[pl.BlockSpec((tm, LANE), lambda i: (i, 0)), pl.BlockSpec((1, IDX_DIM), lambda i: (0, 0))],
        out_specs=pl.BlockSpec((tm, IDX_DIM), lambda i: (i, 0)),
        out_shape=jax.ShapeDtypeStruct((m, IDX_DIM), BF16),
        compiler_params=_params("parallel"),
        name="ki_norm",
    )(tail, g.reshape(1, IDX_DIM).astype(F32))


def _diff_kernel(q_ref, k_ref, vt_ref, lam_ref, sn_ref, o_ref, m_ref, l_ref, acc_ref, *, tq, tk, dqk, out_scale):
    qi = pl.program_id(2)
    last_k = ((qi + 1) * tq - 1) // tk
    m_ref[...] = jnp.full_like(m_ref, NEG)
    l_ref[...] = jnp.zeros_like(l_ref)
    acc_ref[...] = jnp.zeros_like(acc_ref)
    q = q_ref[...]

    def step(kb, masked):
        rows = pl.ds(pl.multiple_of(kb * tk, tk), tk)
        k = k_ref[rows, :]
        vt = vt_ref[kb]
        if masked:
            k_chunk = (kb * tk + lax.broadcasted_iota(jnp.int32, (tk, tq), 0)) // CHUNK
            q_chunk = (qi * tq + lax.broadcasted_iota(jnp.int32, (tk, tq), 1)) // CHUNK
            mask = k_chunk <= q_chunk
        for half in range(2):
            s = lax.dot_general(k[:, half * dqk:(half + 1) * dqk], q[:, half * dqk:(half + 1) * dqk], NT_DIMS,
                                preferred_element_type=F32)
            if masked:
                s = jnp.where(mask, s, NEG)
            m_prev = m_ref[half]
            m_new = jnp.maximum(m_prev, jnp.max(s, axis=0, keepdims=True))
            p = jnp.exp2(s - m_new)
            alpha = jnp.exp2(m_prev - m_new)
            l_ref[half] = alpha * l_ref[half] + jnp.sum(p, axis=0, keepdims=True)
            acc_ref[half] = alpha * acc_ref[half] + jnp.dot(vt, p.astype(BF16), preferred_element_type=F32)
            m_ref[half] = m_new

    def body(kb, carry):
        step(kb, False)
        return carry

    lax.fori_loop(0, last_k, body, 0)
    step(last_k, True)

    o = acc_ref[0] / l_ref[0] - lam_ref[...] * (acc_ref[1] / l_ref[1])
    ms = jnp.mean(o * o, axis=0, keepdims=True)
    o_ref[...] = (o * lax.rsqrt(ms + EPS) * (sn_ref[...] * out_scale)).T.astype(o_ref.dtype)


def _diff_attention(cd, lam_full, subln, lambda_init, *, batch, seq, group):
    m = batch * seq
    dqk = HEAD_DIM
    dv = 2 * HEAD_DIM
    nh = group // dv
    tq = _pick_tile(seq, (1024, 512, 256, 128))
    tk = _pick_tile(seq, (1024, 512, 256, 128))
    assert tk % tq == 0
    nq, nkb = seq // tq, seq // tk
    k_off = nh
    v_t = cd[:, 2 * group:3 * group].reshape(batch, nkb, tk, nh, dv).transpose(0, 1, 3, 4, 2)

    return pl.pallas_call(
        functools.partial(_diff_kernel, tq=tq, tk=tk, dqk=dqk, out_scale=1.0 - lambda_init),
        grid=(batch, nh, nq),
        in_specs=[pl.BlockSpec((tq, dv), lambda b, h, i: (b * nq + i, h)),
                  pl.BlockSpec((seq, dv), lambda b, h, i: (b, k_off + h)),
                  pl.BlockSpec((None, nkb, None, dv, tk), lambda b, h, i: (b, 0, h, 0, 0)),
                  pl.BlockSpec((1, 1), lambda b, h, i: (0, 0)),
                  pl.BlockSpec((dv, 1), lambda b, h, i: (0, 0))],
        out_specs=pl.BlockSpec((tq, dv), lambda b, h, i: (b * nq + i, h)),
        out_shape=jax.ShapeDtypeStruct((m, group), BF16),
        scratch_shapes=[pltpu.VMEM((2, 1, tq), F32), pltpu.VMEM((2, 1, tq), F32), pltpu.VMEM((2, dv, tq), F32)],
        compiler_params=_params("parallel", "parallel", "arbitrary"),
        name="diff_attn",
    )(cd, cd, v_t, lam_full.reshape(1, 1).astype(F32), subln.reshape(dv, 1).astype(F32))


_INT_MIN = -2 ** 31
_HI16 = -(1 << 16)


_CNT_ROWS = 64
_DSA_HEADS_PER_STEP = 2


def _dsa_kernel(q_ref, k_ref, vt_ref, qi_ref, ki_ref, wt_ref, o_ref, sc_ref, sc16_ref, *, tq, tk, topk, hpb,
                w_scale):
    qb = pl.program_id(1)
    h = pl.program_id(2)
    nkv = ((qb + 1) * tq + tk - 1) // tk

    @pl.when(h == 0)
    def _():
        w = wt_ref[...] * w_scale
        q_chunk = (qb * tq + lax.broadcasted_iota(jnp.int32, (tk, tq), 1)) // CHUNK

        def score_body(kb, carry):
            rows = pl.ds(pl.multiple_of(kb * tk, tk), tk)
            ki = ki_ref[rows, :]
            acc = jnp.zeros((tk, tq), F32)
            for ih in range(IDX_HEADS):
                lg = lax.dot_general(ki, qi_ref[:, ih * IDX_DIM:(ih + 1) * IDX_DIM], NT_DIMS,
                                     preferred_element_type=F32)
                acc = acc + w[ih:ih + 1, :] * jnp.maximum(lg, 0.0)
            k_chunk = (kb * tk + lax.broadcasted_iota(jnp.int32, (tk, tq), 0)) // CHUNK
            x = jnp.where(k_chunk <= q_chunk, acc, NEG)
            sc_ref[kb] = x
            hi = lax.bitcast_convert_type(x, jnp.int32) & _HI16
            sc16_ref[kb] = lax.bitcast_convert_type(hi, F32).astype(BF16)
            return carry

        lax.fori_loop(0, nkv, score_body, 0)

        def count_ge(cf):
            def cnt_body(kb, cnt):
                for r in range(tk // _CNT_ROWS):
                    hit = jnp.where(sc_ref[kb, r * _CNT_ROWS:(r + 1) * _CNT_ROWS, :] >= cf, 1.0, 0.0)
                    for g in range(_CNT_ROWS // 8):
                        cnt = cnt + hit[g * 8:(g + 1) * 8, :]
                return cnt

            return jnp.sum(lax.fori_loop(0, nkv, cnt_body, jnp.zeros((8, tq), F32)), axis=0, keepdims=True)

        def count_ge_hi(cfb):
            one, zero = jnp.ones((), BF16), jnp.zeros((), BF16)

            def cnt_body(kb, cnt):
                part = jnp.zeros((16, tq), BF16)
                for r in range(tk // _CNT_ROWS):
                    hit = jnp.where(sc16_ref[kb, r * _CNT_ROWS:(r + 1) * _CNT_ROWS, :] >= cfb, one, zero)
                    for g in range(_CNT_ROWS // 16):
                        part = part + hit[g * 16:(g + 1) * 16, :]
                return cnt + part.astype(F32)

            return jnp.sum(lax.fori_loop(0, nkv, cnt_body, jnp.zeros((16, tq), F32)), axis=0, keepdims=True)

        def bit_body(state, *, coarse):
            i, u, kept = state
            cand = u | jnp.left_shift(jnp.int32(1), 31 - i)
            key = cand ^ _INT_MIN
            if coarse:
                key = jnp.where(key < 0, key | 0xFFFF, key)
            cf = lax.bitcast_convert_type(key ^ (jnp.right_shift(key, 31) & 0x7FFFFFFF), F32)
            cnt = count_ge_hi(cf.astype(BF16)) if coarse else count_ge(cf)
            ok = cnt >= float(topk)
            return i + 1, jnp.where(ok, cand, u), jnp.where(ok, cnt, kept)

        def unsettled(state, *, last):
            i, _, kept = state
            return (i < last) & (jnp.max(jnp.where(kept == float(topk), 0.0, 1.0)) > 0.0)

        in_range = jnp.zeros((1, tq), F32) + (nkv * tk).astype(F32)
        state = (jnp.int32(0), jnp.zeros((1, tq), jnp.int32), in_range)
        state = lax.while_loop(functools.partial(unsettled, last=16), functools.partial(bit_body, coarse=True), state)
        state = lax.while_loop(functools.partial(unsettled, last=32), functools.partial(bit_body, coarse=False),
                               (jnp.maximum(state[0], 16),) + state[1:])
        key = state[1] ^ _INT_MIN
        thr = lax.bitcast_convert_type(key ^ (jnp.right_shift(key, 31) & 0x7FFFFFFF), F32)

        def bias_body(kb, carry):
            s = sc_ref[kb]
            sc_ref[kb] = jnp.where((s >= thr) & (s > 0.5 * NEG), 0.0, NEG)
            return carry

        lax.fori_loop(0, nkv, bias_body, 0)

    q = q_ref[...]

    def kv_body(kb, carry):
        rows = pl.ds(pl.multiple_of(kb * tk, tk), tk)
        k = k_ref[rows, :]
        bias = sc_ref[kb]
        out = []
        for a in range(hpb):
            m_prev, l_prev, acc = carry[a]
            hs = slice(a * HEAD_DIM, (a + 1) * HEAD_DIM)
            s = lax.dot_general(k[:, hs], q[:, hs], NT_DIMS, preferred_element_type=F32) + bias
            m_new = jnp.maximum(m_prev, jnp.max(s, axis=0, keepdims=True))
            p = jnp.exp2(s - m_new)
            alpha = jnp.exp2(m_prev - m_new)
            l_new = alpha * l_prev + jnp.sum(p, axis=0, keepdims=True)
            acc = alpha * acc + jnp.dot(vt_ref[kb, a], p.astype(BF16), preferred_element_type=F32)
            out.append((m_new, l_new, acc))
        return tuple(out)

    init = tuple((jnp.full((1, tq), NEG, F32), jnp.zeros((1, tq), F32), jnp.zeros((HEAD_DIM, tq), F32))
                 for _ in range(hpb))
    fin = lax.fori_loop(0, nkv, kv_body, init)
    for a in range(hpb):
        _, l_fin, acc = fin[a]
        o_ref[:, a * HEAD_DIM:(a + 1) * HEAD_DIM] = (acc / l_fin).T.astype(o_ref.dtype)


def _dsa_attention(cd, ki_n, tail, *, batch, seq, group, w_off):
    m = batch * seq
    nh = group // HEAD_DIM
    tq = _pick_tile(seq, (512, 256, 128))
    tk = _pick_tile(seq, (1024, 512, 256))
    topk = min(TOPK_MAX, seq // 4)
    assert tk >= topk, "the threshold search needs at least topk keys in range"
    nq = seq // tq
    nc = group // LANE
    nkb = seq // tk
    q_off, k_off, v_off = 3 * nc, 4 * nc, 5 * nc
    qi_off = 6 * nc * LANE // (IDX_HEADS * IDX_DIM)
    assert qi_off * IDX_HEADS * IDX_DIM == 6 * nc * LANE
    v_t = (cd[:, v_off * LANE:(v_off + nc) * LANE].reshape(batch, nkb, tk, nh, HEAD_DIM)
           .transpose(0, 1, 3, 4, 2))
    w_t = tail[:, w_off:w_off + IDX_HEADS].reshape(batch, seq, IDX_HEADS).transpose(0, 2, 1)

    hpb = _DSA_HEADS_PER_STEP
    assert nh % hpb == 0
    wide = hpb * HEAD_DIM
    return pl.pallas_call(
        functools.partial(_dsa_kernel, tq=tq, tk=tk, topk=topk, hpb=hpb,
                          w_scale=IDX_HEADS ** -0.5 * IDX_DIM ** -0.5),
        grid=(batch, nq, nh // hpb),
        in_specs=[pl.BlockSpec((tq, wide), lambda b, i, h: (b * nq + i, q_off // hpb + h)),
                  pl.BlockSpec((seq, wide), lambda b, i, h: (b, k_off // hpb + h)),
                  pl.BlockSpec((None, nkb, hpb, HEAD_DIM, tk), lambda b, i, h: (b, 0, h, 0, 0)),
                  pl.BlockSpec((tq, IDX_HEADS * IDX_DIM), lambda b, i, h: (b * nq + i, qi_off)),
                  pl.BlockSpec((seq, IDX_DIM), lambda b, i, h: (b, 0)),
                  pl.BlockSpec((None, IDX_HEADS, tq), lambda b, i, h: (b, 0, i))],
        out_specs=pl.BlockSpec((tq, wide), lambda b, i, h: (b * nq + i, h)),
        out_shape=jax.ShapeDtypeStruct((m, group), BF16),
        scratch_shapes=[pltpu.VMEM((nkb, tk, tq), F32), pltpu.VMEM((nkb, tk, tq), BF16)],
        compiler_params=_params("parallel", "arbitrary", "arbitrary"),
        name="dsa",
    )(cd, cd, v_t, cd, ki_n, w_t)


def _repack_w_in(w, group, nhb):
    p0 = 7 * group
    p1 = p0 + 2 * nhb
    p2 = p1 + 6 * group + IDX_HEADS * IDX_DIM
    used = IDX_DIM + 2 * nhb + IDX_HEADS
    wt = jnp.swapaxes(w, 1, 2)
    pad = jnp.zeros((w.shape[0], LANE - used, w.shape[1]), w.dtype)
    return jnp.concatenate([wt[:, :p0], wt[:, p1:p2], wt[:, p2:p2 + IDX_DIM], wt[:, p0:p1],
                            wt[:, p2 + IDX_DIM:], pad], axis=1).astype(BF16)


def kernel(x, norm_mix, w_in, hgrn_lb_logits, hgrn_gnorm, mlstm_conv, mlstm_gate_bias, mlstm_hnorm,
           diff_qk_norm, diff_lambda, diff_subln, dsa_qk_norm, dsa_idx_knorm, w_out, norm_mlp, w1, w2):
    batch, seq, d_model = x.shape
    depth = w_in.shape[0]
    group = d_model // 4
    nhb = group // HEAD_DIM
    nc = group // LANE
    m = batch * seq
    dims = dict(batch=batch, seq=seq, group=group)

    p_lb = jax.nn.softmax(hgrn_lb_logits.astype(F32), axis=0)
    lower_bounds = jnp.cumsum(p_lb, axis=0) - p_lb[0:1]

    w_in_b = _repack_w_in(w_in, group, nhb)
    w_out_b, w1_b, w2_b = (w.astype(BF16) for w in (w_out, w1, w2))
    n_ab, n_cd = 7 * group, 6 * group + IDX_HEADS * IDX_DIM

    x2 = x.reshape(m, d_model).astype(F32)
    ones_g = jnp.ones((group,), F32)
    for l in range(depth):
        lambda_init = 0.8 - 0.6 * math.exp(-0.3 * l)
        h = _rmsnorm_rows(x2, norm_mix[l])
        proj = _matmul(h, w_in_b, layer=l, cols=(0, n_ab), b_rows=True, out_dtype=F32, name="in_proj_ab")
        tail = _matmul(h, w_in_b, layer=l, cols=(n_ab + n_cd, LANE), b_rows=True, out_dtype=F32,
                       name="in_proj_tail")

        y_a = _hgrn(proj, lower_bounds[l], hgrn_gnorm[l], **dims)

        y_b = _mlstm(proj, tail, mlstm_conv[l], mlstm_gate_bias[l], mlstm_hnorm[l], g_off=IDX_DIM, **dims)

        cqn = diff_qk_norm[l].astype(F32)
        dqn = dsa_qk_norm[l].astype(F32)
        q_scale = HEAD_DIM ** -0.5 * LOG2_E
        gains = jnp.concatenate([
            jnp.tile(cqn[0], nc) * q_scale, jnp.tile(cqn[1], nc), ones_g,
            jnp.tile(dqn[0], nc) * q_scale, jnp.tile(dqn[1], nc), ones_g,
            jnp.ones((IDX_HEADS * IDX_DIM,), F32)]).reshape(1, -1)
        use_norm = jnp.concatenate([
            jnp.ones((2 * group,), F32), jnp.zeros((group,), F32),
            jnp.ones((2 * group,), F32), jnp.zeros((group + IDX_HEADS * IDX_DIM,), F32)]).reshape(1, -1)
        cd = _matmul(h, w_in_b, layer=l, cols=(n_ab, n_cd), b_rows=True, out_dtype=BF16, act="groupnorm",
                     extra=(gains, use_norm), name="in_proj_cd")
        ki_n = _ki_norm(tail, dsa_idx_knorm[l])

        lam = diff_lambda[l].astype(F32)
        lam_full = jnp.exp(jnp.sum(lam[0] * lam[1])) - jnp.exp(jnp.sum(lam[2] * lam[3])) + lambda_init
        y_c = _diff_attention(cd, lam_full, diff_subln[l], lambda_init, **dims)
        y_d = _dsa_attention(cd, ki_n, tail, w_off=IDX_DIM + 2 * nhb, **dims)

        x2 = _matmul((y_a, y_b, y_c, y_d), w_out_b, layer=l, out_dtype=F32, act="residual", extra=(x2,),
                     tm_max=512, name="out_proj")
        h = _rmsnorm_rows(x2, norm_mlp[l])
        hidden = _matmul(h, w1_b, layer=l, out_dtype=BF16, act="relu2", name="mlp_up")
        x2 = _matmul(hidden, w2_b, layer=l, out_dtype=F32, act="residual", extra=(x2,), tk_max=2048,
                     name="mlp_down")
    return x2.reshape(batch, seq, d_model).astype(x.dtype)
```

```python
import functools
import math

import numpy as np
import jax
import jax.numpy as jnp
from jax import lax
from jax.experimental import pallas as pl
from jax.experimental.pallas import tpu as pltpu

CHUNK = 64
EPS = 1e-6
NEG = -1e30
F_FLOOR = 1e-20
HEAD_DIM = 128
CONV_W = 4
IDX_HEADS = 16
IDX_DIM = 64
TOPK_MAX = 256
LOG2_E = math.log2(math.e)
LANE = 128
VMEM_LIMIT = 56 * 1024 * 1024

F32 = jnp.float32
BF16 = jnp.bfloat16
NT_DIMS = (((1,), (1,)), ((), ()))
TN_DIMS = (((0,), (0,)), ((), ()))


def _params(*sem):
    return pltpu.CompilerParams(dimension_semantics=sem, vmem_limit_bytes=VMEM_LIMIT)


def _sigmoid(x):
    return 1.0 / (1.0 + jnp.exp(-x))


def _log_sigmoid(x):
    return jnp.minimum(x, 0.0) - jnp.log1p(jnp.exp(-jnp.abs(x)))


def _split3(x):
    hi = x.astype(BF16)
    r1 = x - hi.astype(F32)
    mid = r1.astype(BF16)
    lo = (r1 - mid.astype(F32)).astype(BF16)
    return hi, mid, lo


def _pick_tile(n, candidates):
    for c in candidates:
        if n % c == 0:
            return c
    raise ValueError(f"no tile in {candidates} divides {n}")


def _rmsnorm_kernel(x_ref, g_ref, o_ref):
    x = x_ref[...]
    ms = jnp.mean(x * x, axis=-1, keepdims=True)
    o_ref[...] = (x * lax.rsqrt(ms + EPS) * g_ref[...]).astype(o_ref.dtype)


def _rmsnorm_rows(x2, g):
    m, d = x2.shape
    tm = _pick_tile(m, (256, 128, 64, 8))
    return pl.pallas_call(
        _rmsnorm_kernel,
        grid=(m // tm,),
        in_specs=[pl.BlockSpec((tm, d), lambda i: (i, 0)), pl.BlockSpec((1, d), lambda i: (0, 0))],
        out_specs=pl.BlockSpec((tm, d), lambda i: (i, 0)),
        out_shape=jax.ShapeDtypeStruct((m, d), BF16),
        compiler_params=_params("parallel"),
        name="rmsnorm",
    )(x2, g.reshape(1, d).astype(F32))


MM_MAX_TK = 4096
MM_TN = 1024


def _mm_epilogue(r, act, extra):
    if act == "relu2":
        r = jnp.square(jnp.maximum(r, 0.0))
    elif act == "groupnorm":
        g_ref, u_ref = extra
        segs = []
        for c in range(r.shape[1] // LANE):
            seg = r[:, c * LANE:(c + 1) * LANE]
            ms = jnp.mean(seg * seg, axis=1, keepdims=True)
            inv = jnp.where(u_ref[:, c * LANE:(c + 1) * LANE] > 0.0, lax.rsqrt(ms + EPS), 1.0)
            segs.append(seg * inv * g_ref[:, c * LANE:(c + 1) * LANE])
        r = jnp.concatenate(segs, axis=1)
    elif act == "residual":
        r = r + extra[0][...]
    return r


def _mm_kernel(*refs, n_a, nk, act, n_extra, b_rows):
    a_refs = refs[:n_a]
    b_ref = refs[n_a]
    extra = refs[n_a + 1:n_a + 1 + n_extra]
    o_ref = refs[n_a + 1 + n_extra]

    def product():
        r, off = None, 0
        for a_ref in a_refs:
            kw = a_ref.shape[1]
            if b_rows:
                d = lax.dot_general(a_ref[...], b_ref[:, off:off + kw], NT_DIMS, preferred_element_type=F32)
            else:
                d = jnp.dot(a_ref[...], b_ref[off:off + kw, :], preferred_element_type=F32)
            r = d if r is None else r + d
            off += kw
        return r

    if nk == 1:
        o_ref[...] = _mm_epilogue(product(), act, extra).astype(o_ref.dtype)
        return
    acc_ref = refs[n_a + 2 + n_extra]
    k = pl.program_id(2)

    @pl.when(k == 0)
    def _():
        acc_ref[...] = jnp.zeros_like(acc_ref)

    acc_ref[...] += product()

    @pl.when(k == nk - 1)
    def _():
        o_ref[...] = _mm_epilogue(acc_ref[...], act, extra).astype(o_ref.dtype)


def _matmul(a, b, *, out_dtype, layer, cols=None, b_rows=False, act=None, extra=(), tm_max=1024, tn_max=MM_TN,
            tk_max=MM_MAX_TK, name="matmul"):
    a_parts = a if isinstance(a, tuple) else (a,)
    m = a_parts[0].shape[0]
    kdim = sum(p.shape[1] for p in a_parts)
    col0, n = (0, b.shape[1 if b_rows else 2]) if cols is None else cols
    tm = _pick_tile(m, tuple(t for t in (1024, 512, 256, 128, 8) if t <= tm_max))
    tn = _pick_tile(math.gcd(n, col0), tuple(t for t in (1024, 512, 256, 128) if t <= tn_max))
    j0 = col0 // tn
    tk = kdim if kdim <= tk_max else _pick_tile(kdim, tuple(t for t in (4096, 2048, 1024, 512, 256, 128)
                                                            if t <= tk_max))
    nk = kdim // tk
    assert nk == 1 or len(a_parts) == 1
    in_specs = [pl.BlockSpec((tm, tk if nk > 1 else p.shape[1]), lambda i, j, k: (i, k)) for p in a_parts]
    if b_rows:
        in_specs.append(pl.BlockSpec((None, tn, tk), lambda i, j, k: (layer, j0 + j, k)))
    else:
        in_specs.append(pl.BlockSpec((None, tk, tn), lambda i, j, k: (layer, k, j0 + j)))
    for e in extra:
        if e.shape[0] == 1:
            in_specs.append(pl.BlockSpec((1, tn), lambda i, j, k: (0, j)))
        else:
            in_specs.append(pl.BlockSpec((tm, tn), lambda i, j, k: (i, j)))
    return pl.pallas_call(
        functools.partial(_mm_kernel, n_a=len(a_parts), nk=nk, act=act, n_extra=len(extra), b_rows=b_rows),
        grid=(m // tm, n // tn, nk),
        in_specs=in_specs,
        out_specs=pl.BlockSpec((tm, tn), lambda i, j, k: (i, j)),
        out_shape=jax.ShapeDtypeStruct((m, n), out_dtype),
        scratch_shapes=[pltpu.VMEM((tm, tn), F32)] if nk > 1 else [],
        compiler_params=_params("parallel", "parallel", "arbitrary"),
        name=name,
    )(*a_parts, b, *extra)


_HGRN_LEVELS = 6
_HGRN_HEADS_PER_STEP = 4


def _hgrn_constants():
    idx = np.arange(CHUNK)
    j = idx[None, :]
    r = idx[:, None]
    blocks = [j <= r, j > r]
    masks = []
    for lvl in range(_HGRN_LEVELS):
        half = (CHUNK // 2) >> lvl
        parent = idx // (2 * half)
        mid = parent * 2 * half + half
        is_right = (idx % (2 * half)) >= half
        right_rng = (j >= mid[:, None]) & (j <= r)
        left_rng = (j > r) & (j < mid[:, None])
        blocks.append(np.where(is_right[:, None], right_rng, left_rng))
        masks.append((parent[:, None] == parent[None, :]) & is_right[:, None] & (~is_right)[None, :])
    cm = np.concatenate(blocks, axis=0).astype(np.float32)
    cm3 = np.concatenate([cm, cm, cm, np.zeros_like(cm)], axis=1)
    mk = np.stack(masks, axis=0).astype(np.float32)
    return jnp.asarray(cm3, dtype=BF16), jnp.asarray(mk, dtype=F32)


def _hgrn_kernel(q_ref, f_ref, i_ref, g_ref, lb_ref, gn_ref, cm_ref, mk_ref, o_ref, st_ref, *, nchunk, hb, scale):
    @pl.when(pl.program_id(2) == 0)
    def _():
        st_ref[...] = jnp.zeros_like(st_ref)

    lb = lb_ref[...]
    gn = gn_ref[...]

    def body(c, carry):
        rows = pl.ds(pl.multiple_of(c * CHUNK, CHUNK), CHUNK)
        f = lb + (1.0 - lb) * _sigmoid(f_ref[rows, :])
        logf = jnp.log(jnp.maximum(f, F_FLOOR))
        kk_w = 1.0 - f
        qr = q_ref[rows, :]
        qq_w = qr * _sigmoid(qr) * scale
        v_w = i_ref[rows, :]
        gr = g_ref[rows, :]
        gate_w = gr * _sigmoid(gr)
        l3 = jnp.concatenate(_split3(logf) + (jnp.zeros(logf.shape, BF16),), axis=0)
        sums_w = jnp.dot(cm_ref[...], l3, preferred_element_type=F32)
        for a in range(hb):
            hs = slice(a * HEAD_DIM, (a + 1) * HEAD_DIM)
            sums, qq, kk, v = sums_w[:, hs], qq_w[:, hs], kk_w[:, hs], v_w[:, hs]
            vb = v.astype(BF16)
            b = sums[0:CHUNK]
            b_rest = sums[CHUNK:2 * CHUNK]
            scores = jnp.zeros((CHUNK, CHUNK), F32)
            for lvl in range(_HGRN_LEVELS):
                e = jnp.exp(sums[(2 + lvl) * CHUNK:(3 + lvl) * CHUNK])
                s = lax.dot_general((qq * e).astype(BF16), (kk * e).astype(BF16), NT_DIMS,
                                    preferred_element_type=F32)
                scores = scores + s * mk_ref[lvl]
            diag = jnp.sum(qq * kk, axis=1, keepdims=True)
            st = st_ref[a]
            o = (jnp.dot(scores.astype(BF16), vb, preferred_element_type=F32) + diag * v
                 + lax.dot_general((qq * jnp.exp(b)).astype(BF16), st.astype(BF16), NT_DIMS,
                                   preferred_element_type=F32))
            ke = (kk * jnp.exp(b_rest)).astype(BF16)
            st_ref[a] = (st * jnp.exp(b[CHUNK - 1:CHUNK, :])
                         + lax.dot_general(vb, ke, TN_DIMS, preferred_element_type=F32))
            ms = jnp.mean(o * o, axis=1, keepdims=True)
            o_ref[rows, hs] = (o * lax.rsqrt(ms + EPS) * gn * gate_w[:, hs]).astype(o_ref.dtype)
        return carry

    lax.fori_loop(0, nchunk, body, 0, unroll=True)


def _hgrn(proj, lb, gnorm, *, batch, seq, group):
    m = batch * seq
    nh = group // HEAD_DIM
    t = _pick_tile(seq, (512, 256, 128, 64))
    nt = seq // t
    cm3, mk = _hgrn_constants()

    hb = min(_HGRN_HEADS_PER_STEP, nh)
    assert nh % hb == 0
    wide = hb * HEAD_DIM
    ng = nh // hb

    def col(off):
        return pl.BlockSpec((t, wide), lambda b, h, i: (b * nt + i, off + h))

    return pl.pallas_call(
        functools.partial(_hgrn_kernel, nchunk=t // CHUNK, hb=hb, scale=HEAD_DIM ** -0.5),
        grid=(batch, ng, nt),
        in_specs=[col(0), col(ng), col(2 * ng), col(3 * ng),
                  pl.BlockSpec((1, wide), lambda b, h, i: (0, h)),
                  pl.BlockSpec((1, HEAD_DIM), lambda b, h, i: (0, 0)),
                  pl.BlockSpec(cm3.shape, lambda b, h, i: (0, 0)),
                  pl.BlockSpec(mk.shape, lambda b, h, i: (0, 0, 0))],
        out_specs=pl.BlockSpec((t, wide), lambda b, h, i: (b * nt + i, h)),
        out_shape=jax.ShapeDtypeStruct((m, group), BF16),
        scratch_shapes=[pltpu.VMEM((hb, HEAD_DIM, HEAD_DIM), F32)],
        compiler_params=_params("parallel", "parallel", "arbitrary"),
        name="hgrn2",
    )(proj, proj, proj, proj, lb.reshape(1, group).astype(F32), gnorm.reshape(1, HEAD_DIM).astype(F32), cm3, mk)


_CONV_PAD = 8


def _mlstm_constants():
    idx = np.arange(CHUNK)
    tri = (idx[None, :] <= idx[:, None]).astype(np.float32)
    zero = np.zeros_like(tri)
    tri3 = np.concatenate([tri, tri, tri, zero], axis=1)
    trit3 = np.concatenate([tri.T, tri.T, tri.T, zero], axis=0)
    trit3 = np.concatenate([trit3, np.zeros_like(trit3)], axis=1)
    return jnp.asarray(tri3, dtype=BF16), jnp.asarray(trit3, dtype=BF16)


def _mlstm_kernel(qk_ref, v_ref, og_ref, gc_ref, gr_ref, cw_ref, bc_ref, br_ref, hn_ref, tri_ref, trit_ref,
                  o_ref, xp_ref, qk_s, c_ref, m_ref, *, nchunk, nh, dqk, dv, t, g_off):
    @pl.when(pl.program_id(1) == 0)
    def _():
        xp_ref[0:_CONV_PAD, :] = jnp.zeros((_CONV_PAD, xp_ref.shape[1]), F32)
        c_ref[...] = jnp.zeros_like(c_ref)
        m_ref[...] = jnp.zeros_like(m_ref)

    xp_ref[_CONV_PAD:_CONV_PAD + t, :] = qk_ref[...]
    conv = jnp.zeros((t, 2 * nh * dqk), F32)
    for j in range(CONV_W):
        start = _CONV_PAD - (CONV_W - 1) + j
        conv = conv + cw_ref[j:j + 1, :] * xp_ref[start:start + t, :]
    qk_s[...] = conv * _sigmoid(conv)
    xp_ref[0:_CONV_PAD, :] = xp_ref[t:t + _CONV_PAD, :]

    hn = hn_ref[...]
    kscale = dqk ** -0.5
    r_i = lax.broadcasted_iota(jnp.int32, (CHUNK, CHUNK), 0)
    c_i = lax.broadcasted_iota(jnp.int32, (CHUNK, CHUNK), 1)
    causal = c_i <= r_i
    ones = jnp.ones((CHUNK, dv), BF16)
    zpad_c = jnp.zeros((CHUNK, LANE), BF16)
    zpad_r = jnp.zeros((2 * nh, CHUNK), BF16)

    def body(c, carry):
        base = pl.multiple_of(c * CHUNK, CHUNK)
        rows = pl.ds(base, CHUNK)
        qk = qk_s[rows, :]
        gcol = gc_ref[rows, :] + bc_ref[...]
        b_c = jnp.dot(tri_ref[...], jnp.concatenate(_split3(_log_sigmoid(gcol)) + (zpad_c,), axis=0),
                      preferred_element_type=F32)
        grow = gr_ref[c] + br_ref[...]
        ig_r = grow[0:nh, :]
        b_r = jnp.dot(jnp.concatenate(_split3(_log_sigmoid(grow)) + (zpad_r,), axis=1), trit_ref[...],
                      preferred_element_type=F32)[nh:2 * nh, 0:CHUNK]
        for h in range(nh):
            ig_col = gcol[:, g_off + h:g_off + h + 1]
            q_h = qk[:, h * dqk:(h + 1) * dqk].astype(BF16)
            k_f = qk[:, (nh + h) * dqk:(nh + h + 1) * dqk] * kscale
            k_h = k_f.astype(BF16)
            v_ext = jnp.concatenate([v_ref[rows, h * dv:(h + 1) * dv].astype(BF16), ones], axis=1)
            bcol = b_c[:, g_off + nh + h:g_off + nh + h + 1]
            m_prev = m_ref[h:h + 1, 0:1]
            log_d = jnp.where(causal, bcol - b_r[h:h + 1, :] + ig_r[h:h + 1, :], NEG)
            log_inter = bcol + m_prev
            m_t = jnp.maximum(jnp.max(log_d, axis=1, keepdims=True), log_inter)
            qk_w = (lax.dot_general(q_h, k_h, NT_DIMS, preferred_element_type=F32)
                    * jnp.exp(log_d - m_t))
            w_inter = jnp.exp(log_inter - m_t)
            c_ext = c_ref[h]
            numden = (jnp.dot(qk_w.astype(BF16), v_ext, preferred_element_type=F32)
                      + w_inter * jnp.dot(q_h, c_ext.astype(BF16), preferred_element_type=F32))
            num = numden[:, 0:dv]
            den = numden[:, dv:2 * dv]
            hh = num / jnp.maximum(jnp.abs(den), jnp.exp(-m_t))
            m_new = m_t[CHUNK - 1:CHUNK, :]
            b_last = bcol[CHUNK - 1:CHUNK, :]
            w_state = jnp.exp(b_last - bcol + ig_col - m_new)
            decay = jnp.exp(b_last + m_prev - m_new)
            c_ref[h] = decay * c_ext + lax.dot_general((k_f * w_state).astype(BF16), v_ext, TN_DIMS,
                                                       preferred_element_type=F32)
            m_ref[h:h + 1, :] = jnp.broadcast_to(m_new, (1, LANE))
            ms = jnp.mean(hh * hh, axis=1, keepdims=True)
            gate = _sigmoid(og_ref[rows, h * dv:(h + 1) * dv])
            o_ref[rows, h * dv:(h + 1) * dv] = (hh * lax.rsqrt(ms + EPS) * hn * gate).astype(o_ref.dtype)
        return carry

    lax.fori_loop(0, nchunk, body, 0, unroll=True)


def _mlstm(proj, tail, conv_w, gate_bias, hnorm, *, batch, seq, group, g_off):
    m = batch * seq
    nh = group // HEAD_DIM
    dqk = HEAD_DIM // 2
    dv = HEAD_DIM
    t = _pick_tile(seq, (256, 128, 64))
    nt = seq // t
    tri3, trit3 = _mlstm_constants()
    qk_blk = 4
    gates_row = (tail[:, g_off:g_off + 2 * nh].reshape(batch, seq // CHUNK, CHUNK, 2 * nh)
                 .transpose(0, 1, 3, 2))
    bias = gate_bias.astype(F32)
    bias_lane = jnp.zeros((1, LANE), F32).at[0, g_off:g_off + 2 * nh].set(bias)

    def wide(off):
        return pl.BlockSpec((t, group), lambda b, i: (b * nt + i, off))

    def const(shape):
        return pl.BlockSpec(shape, lambda b, i: tuple(0 for _ in shape))

    return pl.pallas_call(
        functools.partial(_mlstm_kernel, nchunk=t // CHUNK, nh=nh, dqk=dqk, dv=dv, t=t, g_off=g_off),
        grid=(batch, nt),
        in_specs=[wide(qk_blk), wide(qk_blk + 1), wide(qk_blk + 2),
                  pl.BlockSpec((t, LANE), lambda b, i: (b * nt + i, 0)),
                  pl.BlockSpec((None, t // CHUNK, 2 * nh, CHUNK), lambda b, i: (b, i, 0, 0)),
                  const((CONV_W, group)), const((1, LANE)), const((2 * nh, 1)), const((1, dv)),
                  const(tri3.shape), const(trit3.shape)],
        out_specs=pl.BlockSpec((t, group), lambda b, i: (b * nt + i, 0)),
        out_shape=jax.ShapeDtypeStruct((m, group), BF16),
        scratch_shapes=[pltpu.VMEM((t + _CONV_PAD, group), F32),
                        pltpu.VMEM((t, group), F32),
                        pltpu.VMEM((nh, dqk, 2 * dv), F32),
                        pltpu.VMEM((nh, LANE), F32)],
        compiler_params=_params("parallel", "arbitrary"),
        name="mlstm",
    )(proj, proj, proj, tail, gates_row, conv_w.astype(F32), bias_lane, bias.reshape(2 * nh, 1),
      hnorm.reshape(1, dv).astype(F32), tri3, trit3)


def _ki_kernel(x_ref, g_ref, o_ref):
    x = x_ref[:, 0:IDX_DIM]
    ms = jnp.mean(x * x, axis=-1, keepdims=True)
    o_ref[...] = (x * lax.rsqrt(ms + EPS) * g_ref[...]).astype(o_ref.dtype)


def _ki_norm(tail, g):
    m = tail.shape[0]
    tm = _pick_tile(m, (512, 256, 128, 64))
    return pl.pallas_call(
        _ki_kernel,
        grid=(m // tm,),
        in_specs=[pl.BlockSpec((tm, LANE), lambda i: (i, 0)), pl.BlockSpec((1, IDX_DIM), lambda i: (0, 0))],
        out_specs=pl.BlockSpec((tm, IDX_DIM), lambda i: (i, 0)),
        out_shape=jax.ShapeDtypeStruct((m, IDX_DIM), BF16),
        compiler_params=_params("parallel"),
        name="ki_norm",
    )(tail, g.reshape(1, IDX_DIM).astype(F32))


def _diff_kernel(q_ref, k_ref, vt_ref, lam_ref, sn_ref, o_ref, m_ref, l_ref, acc_ref, *, tq, tk, dqk, out_scale):
    qi = pl.program_id(2)
    last_k = ((qi + 1) * tq - 1) // tk
    m_ref[...] = jnp.full_like(m_ref, NEG)
    l_ref[...] = jnp.zeros_like(l_ref)
    acc_ref[...] = jnp.zeros_like(acc_ref)
    q = q_ref[...]

    def step(kb, masked):
        rows = pl.ds(pl.multiple_of(kb * tk, tk), tk)
        k = k_ref[rows, :]
        vt = vt_ref[kb]
        if masked:
            k_chunk = (kb * tk + lax.broadcasted_iota(jnp.int32, (tk, tq), 0)) // CHUNK
            q_chunk = (qi * tq + lax.broadcasted_iota(jnp.int32, (tk, tq), 1)) // CHUNK
            mask = k_chunk <= q_chunk
        for half in range(2):
            s = lax.dot_general(k[:, half * dqk:(half + 1) * dqk], q[:, half * dqk:(half + 1) * dqk], NT_DIMS,
                                preferred_element_type=F32)
            if masked:
                s = jnp.where(mask, s, NEG)
            m_prev = m_ref[half]
            m_new = jnp.maximum(m_prev, jnp.max(s, axis=0, keepdims=True))
            p = jnp.exp2(s - m_new)
            alpha = jnp.exp2(m_prev - m_new)
            l_ref[half] = alpha * l_ref[half] + jnp.sum(p, axis=0, keepdims=True)
            acc_ref[half] = alpha * acc_ref[half] + jnp.dot(vt, p.astype(BF16), preferred_element_type=F32)
            m_ref[half] = m_new

    def body(kb, carry):
        step(kb, False)
        return carry

    lax.fori_loop(0, last_k, body, 0)
    step(last_k, True)

    o = acc_ref[0] / l_ref[0] - lam_ref[...] * (acc_ref[1] / l_ref[1])
    ms = jnp.mean(o * o, axis=0, keepdims=True)
    o_ref[...] = (o * lax.rsqrt(ms + EPS) * (sn_ref[...] * out_scale)).T.astype(o_ref.dtype)


def _diff_attention(cd, lam_full, subln, lambda_init, *, batch, seq, group):
    m = batch * seq
    dqk = HEAD_DIM
    dv = 2 * HEAD_DIM
    nh = group // dv
    tq = _pick_tile(seq, (1024, 512, 256, 128))
    tk = _pick_tile(seq, (1024, 512, 256, 128))
    assert tk % tq == 0
    nq, nkb = seq // tq, seq // tk
    k_off = nh
    v_t = cd[:, 2 * group:3 * group].reshape(batch, nkb, tk, nh, dv).transpose(0, 1, 3, 4, 2)

    return pl.pallas_call(
        functools.partial(_diff_kernel, tq=tq, tk=tk, dqk=dqk, out_scale=1.0 - lambda_init),
        grid=(batch, nh, nq),
        in_specs=[pl.BlockSpec((tq, dv), lambda b, h, i: (b * nq + i, h)),
                  pl.BlockSpec((seq, dv), lambda b, h, i: (b, k_off + h)),
                  pl.BlockSpec((None, nkb, None, dv, tk), lambda b, h, i: (b, 0, h, 0, 0)),
                  pl.BlockSpec((1, 1), lambda b, h, i: (0, 0)),
                  pl.BlockSpec((dv, 1), lambda b, h, i: (0, 0))],
        out_specs=pl.BlockSpec((tq, dv), lambda b, h, i: (b * nq + i, h)),
        out_shape=jax.ShapeDtypeStruct((m, group), BF16),
        scratch_shapes=[pltpu.VMEM((2, 1, tq), F32), pltpu.VMEM((2, 1, tq), F32), pltpu.VMEM((2, dv, tq), F32)],
        compiler_params=_params("parallel", "parallel", "arbitrary"),
        name="diff_attn",
    )(cd, cd, v_t, lam_full.reshape(1, 1).astype(F32), subln.reshape(dv, 1).astype(F32))


_INT_MIN = -2 ** 31
_HI16 = -(1 << 16)


_CNT_ROWS = 64
_DSA_HEADS_PER_STEP = 2


def _dsa_kernel(q_ref, k_ref, vt_ref, qi_ref, ki_ref, wt_ref, o_ref, sc_ref, sc16_ref, *, tq, tk, topk, hpb,
                w_scale):
    qb = pl.program_id(1)
    h = pl.program_id(2)
    nkv = ((qb + 1) * tq + tk - 1) // tk

    @pl.when(h == 0)
    def _():
        w = wt_ref[...] * w_scale
        q_chunk = (qb * tq + lax.broadcasted_iota(jnp.int32, (tk, tq), 1)) // CHUNK

        def score_body(kb, carry):
            rows = pl.ds(pl.multiple_of(kb * tk, tk), tk)
            ki = ki_ref[rows, :]
            acc = jnp.zeros((tk, tq), F32)
            for ih in range(IDX_HEADS):
                lg = lax.dot_general(ki, qi_ref[:, ih * IDX_DIM:(ih + 1) * IDX_DIM], NT_DIMS,
                                     preferred_element_type=F32)
                acc = acc + w[ih:ih + 1, :] * jnp.maximum(lg, 0.0)
            k_chunk = (kb * tk + lax.broadcasted_iota(jnp.int32, (tk, tq), 0)) // CHUNK
            x = jnp.where(k_chunk <= q_chunk, acc, NEG)
            sc_ref[kb] = x
            hi = lax.bitcast_convert_type(x, jnp.int32) & _HI16
            sc16_ref[kb] = lax.bitcast_convert_type(hi, F32).astype(BF16)
            return carry

        lax.fori_loop(0, nkv, score_body, 0)

        def count_ge(cf):
            def cnt_body(kb, cnt):
                for r in range(tk // _CNT_ROWS):
                    hit = jnp.where(sc_ref[kb, r * _CNT_ROWS:(r + 1) * _CNT_ROWS, :] >= cf, 1.0, 0.0)
                    for g in range(_CNT_ROWS // 8):
                        cnt = cnt + hit[g * 8:(g + 1) * 8, :]
                return cnt

            return jnp.sum(lax.fori_loop(0, nkv, cnt_body, jnp.zeros((8, tq), F32)), axis=0, keepdims=True)

        def count_ge_hi(cfb):
            one, zero = jnp.ones((), BF16), jnp.zeros((), BF16)

            def cnt_body(kb, cnt):
                part = jnp.zeros((16, tq), BF16)
                for r in range(tk // _CNT_ROWS):
                    hit = jnp.where(sc16_ref[kb, r * _CNT_ROWS:(r + 1) * _CNT_ROWS, :] >= cfb, one, zero)
                    for g in range(_CNT_ROWS // 16):
                        part = part + hit[g * 16:(g + 1) * 16, :]
                return cnt + part.astype(F32)

            return jnp.sum(lax.fori_loop(0, nkv, cnt_body, jnp.zeros((16, tq), F32)), axis=0, keepdims=True)

        def bit_body(state, *, coarse):
            i, u, kept = state
            cand = u | jnp.left_shift(jnp.int32(1), 31 - i)
            key = cand ^ _INT_MIN
            if coarse:
                key = jnp.where(key < 0, key | 0xFFFF, key)
            cf = lax.bitcast_convert_type(key ^ (jnp.right_shift(key, 31) & 0x7FFFFFFF), F32)
            cnt = count_ge_hi(cf.astype(BF16)) if coarse else count_ge(cf)
            ok = cnt >= float(topk)
            return i + 1, jnp.where(ok, cand, u), jnp.where(ok, cnt, kept)

        def unsettled(state, *, last):
            i, _, kept = state
            return (i < last) & (jnp.max(jnp.where(kept == float(topk), 0.0, 1.0)) > 0.0)

        in_range = jnp.zeros((1, tq), F32) + (nkv * tk).astype(F32)
        state = (jnp.int32(0), jnp.zeros((1, tq), jnp.int32), in_range)
        state = lax.while_loop(functools.partial(unsettled, last=16), functools.partial(bit_body, coarse=True), state)
        state = lax.while_loop(functools.partial(unsettled, last=32), functools.partial(bit_body, coarse=False),
                               (jnp.maximum(state[0], 16),) + state[1:])
        key = state[1] ^ _INT_MIN
        thr = lax.bitcast_convert_type(key ^ (jnp.right_shift(key, 31) & 0x7FFFFFFF), F32)

        tied = (state[2] > float(topk)) & (thr > 0.5 * NEG)
        has_tie = jnp.max(jnp.where(tied, 1.0, 0.0)) > 0.0

        @pl.when(jnp.logical_not(has_tie))
        def _():
            def bias_body(kb, carry):
                s = sc_ref[kb]
                sc_ref[kb] = jnp.where((s >= thr) & (s > 0.5 * NEG), 0.0, NEG)
                return carry

            lax.fori_loop(0, nkv, bias_body, 0)

        @pl.when(has_tie)
        def _():
            def gt_body(kb, cnt):
                for r in range(tk // _CNT_ROWS):
                    hit = jnp.where(sc_ref[kb, r * _CNT_ROWS:(r + 1) * _CNT_ROWS, :] > thr, 1.0, 0.0)
                    for g in range(_CNT_ROWS // 8):
                        cnt = cnt + hit[g * 8:(g + 1) * 8, :]
                return cnt

            above = jnp.sum(lax.fori_loop(0, nkv, gt_body, jnp.zeros((8, tq), F32)), axis=0, keepdims=True)
            room = float(topk) - above
            lower = (lax.broadcasted_iota(jnp.int32, (LANE, LANE), 1)
                     <= lax.broadcasted_iota(jnp.int32, (LANE, LANE), 0)).astype(BF16)

            def tie_body(kb, seen):
                for r in range(tk // LANE):
                    strip = slice(r * LANE, (r + 1) * LANE)
                    s = sc_ref[kb, strip, :]
                    eq = s == thr
                    rank = seen + jnp.dot(lower, jnp.where(eq, 1.0, 0.0).astype(BF16), preferred_element_type=F32)
                    keep = ((s > thr) | (eq & (rank <= room))) & (s > 0.5 * NEG)
                    sc_ref[kb, strip, :] = jnp.where(keep, 0.0, NEG)
                    seen = rank[LANE - 1:LANE, :]
                return seen

            lax.fori_loop(0, nkv, tie_body, jnp.zeros((1, tq), F32))

    q = q_ref[...]

    def kv_body(kb, carry):
        rows = pl.ds(pl.multiple_of(kb * tk, tk), tk)
        k = k_ref[rows, :]
        bias = sc_ref[kb]
        out = []
        for a in range(hpb):
            m_prev, l_prev, acc = carry[a]
            hs = slice(a * HEAD_DIM, (a + 1) * HEAD_DIM)
            s = lax.dot_general(k[:, hs], q[:, hs], NT_DIMS, preferred_element_type=F32) + bias
            m_new = jnp.maximum(m_prev, jnp.max(s, axis=0, keepdims=True))
            p = jnp.exp2(s - m_new)
            alpha = jnp.exp2(m_prev - m_new)
            l_new = alpha * l_prev + jnp.sum(p, axis=0, keepdims=True)
            acc = alpha * acc + jnp.dot(vt_ref[kb, a], p.astype(BF16), preferred_element_type=F32)
            out.append((m_new, l_new, acc))
        return tuple(out)

    init = tuple((jnp.full((1, tq), NEG, F32), jnp.zeros((1, tq), F32), jnp.zeros((HEAD_DIM, tq), F32))
                 for _ in range(hpb))
    fin = lax.fori_loop(0, nkv, kv_body, init)
    for a in range(hpb):
        _, l_fin, acc = fin[a]
        o_ref[:, a * HEAD_DIM:(a + 1) * HEAD_DIM] = (acc / l_fin).T.astype(o_ref.dtype)


def _dsa_attention(cd, ki_n, tail, *, batch, seq, group, w_off):
    m = batch * seq
    nh = group // HEAD_DIM
    tq = _pick_tile(seq, (512, 256, 128))
    tk = _pick_tile(seq, (1024, 512, 256))
    topk = min(TOPK_MAX, seq // 4)
    assert tk >= topk, "the threshold search needs at least topk keys in range"
    nq = seq // tq
    nc = group // LANE
    nkb = seq // tk
    q_off, k_off, v_off = 3 * nc, 4 * nc, 5 * nc
    qi_off = 6 * nc * LANE // (IDX_HEADS * IDX_DIM)
    assert qi_off * IDX_HEADS * IDX_DIM == 6 * nc * LANE
    v_t = (cd[:, v_off * LANE:(v_off + nc) * LANE].reshape(batch, nkb, tk, nh, HEAD_DIM)
           .transpose(0, 1, 3, 4, 2))
    w_t = tail[:, w_off:w_off + IDX_HEADS].reshape(batch, seq, IDX_HEADS).transpose(0, 2, 1)

    hpb = _DSA_HEADS_PER_STEP
    assert nh % hpb == 0
    wide = hpb * HEAD_DIM
    return pl.pallas_call(
        functools.partial(_dsa_kernel, tq=tq, tk=tk, topk=topk, hpb=hpb,
                          w_scale=IDX_HEADS ** -0.5 * IDX_DIM ** -0.5),
        grid=(batch, nq, nh // hpb),
        in_specs=[pl.BlockSpec((tq, wide), lambda b, i, h: (b * nq + i, q_off // hpb + h)),
                  pl.BlockSpec((seq, wide), lambda b, i, h: (b, k_off // hpb + h)),
                  pl.BlockSpec((None, nkb, hpb, HEAD_DIM, tk), lambda b, i, h: (b, 0, h, 0, 0)),
                  pl.BlockSpec((tq, IDX_HEADS * IDX_DIM), lambda b, i, h: (b * nq + i, qi_off)),
                  pl.BlockSpec((seq, IDX_DIM), lambda b, i, h: (b, 0)),
                  pl.BlockSpec((None, IDX_HEADS, tq), lambda b, i, h: (b, 0, i))],
        out_specs=pl.BlockSpec((tq, wide), lambda b, i, h: (b * nq + i, h)),
        out_shape=jax.ShapeDtypeStruct((m, group), BF16),
        scratch_shapes=[pltpu.VMEM((nkb, tk, tq), F32), pltpu.VMEM((nkb, tk, tq), BF16)],
        compiler_params=_params("parallel", "arbitrary", "arbitrary"),
        name="dsa",
    )(cd, cd, v_t, cd, ki_n, w_t)


def _repack_w_in(w, group, nhb):
    p0 = 7 * group
    p1 = p0 + 2 * nhb
    p2 = p1 + 6 * group + IDX_HEADS * IDX_DIM
    used = IDX_DIM + 2 * nhb + IDX_HEADS
    wt = jnp.swapaxes(w, 1, 2)
    pad = jnp.zeros((w.shape[0], LANE - used, w.shape[1]), w.dtype)
    return jnp.concatenate([wt[:, :p0], wt[:, p1:p2], wt[:, p2:p2 + IDX_DIM], wt[:, p0:p1],
                            wt[:, p2 + IDX_DIM:], pad], axis=1).astype(BF16)


def kernel(x, norm_mix, w_in, hgrn_lb_logits, hgrn_gnorm, mlstm_conv, mlstm_gate_bias, mlstm_hnorm,
           diff_qk_norm, diff_lambda, diff_subln, dsa_qk_norm, dsa_idx_knorm, w_out, norm_mlp, w1, w2):
    batch, seq, d_model = x.shape
    depth = w_in.shape[0]
    group = d_model // 4
    nhb = group // HEAD_DIM
    nc = group // LANE
    m = batch * seq
    dims = dict(batch=batch, seq=seq, group=group)

    p_lb = jax.nn.softmax(hgrn_lb_logits.astype(F32), axis=0)
    lower_bounds = jnp.cumsum(p_lb, axis=0) - p_lb[0:1]

    w_in_b = _repack_w_in(w_in, group, nhb)
    w_out_b, w1_b, w2_b = (w.astype(BF16) for w in (w_out, w1, w2))
    n_ab, n_cd = 7 * group, 6 * group + IDX_HEADS * IDX_DIM

    x2 = x.reshape(m, d_model).astype(F32)
    ones_g = jnp.ones((group,), F32)
    for l in range(depth):
        lambda_init = 0.8 - 0.6 * math.exp(-0.3 * l)
        h = _rmsnorm_rows(x2, norm_mix[l])
        proj = _matmul(h, w_in_b, layer=l, cols=(0, n_ab), b_rows=True, out_dtype=F32, name="in_proj_ab")
        tail = _matmul(h, w_in_b, layer=l, cols=(n_ab + n_cd, LANE), b_rows=True, out_dtype=F32,
                       name="in_proj_tail")

        y_a = _hgrn(proj, lower_bounds[l], hgrn_gnorm[l], **dims)

        y_b = _mlstm(proj, tail, mlstm_conv[l], mlstm_gate_bias[l], mlstm_hnorm[l], g_off=IDX_DIM, **dims)

        cqn = diff_qk_norm[l].astype(F32)
        dqn = dsa_qk_norm[l].astype(F32)
        q_scale = HEAD_DIM ** -0.5 * LOG2_E
        gains = jnp.concatenate([
            jnp.tile(cqn[0], nc) * q_scale, jnp.tile(cqn[1], nc), ones_g,
            jnp.tile(dqn[0], nc) * q_scale, jnp.tile(dqn[1], nc), ones_g,
            jnp.ones((IDX_HEADS * IDX_DIM,), F32)]).reshape(1, -1)
        use_norm = jnp.concatenate([
            jnp.ones((2 * group,), F32), jnp.zeros((group,), F32),
            jnp.ones((2 * group,), F32), jnp.zeros((group + IDX_HEADS * IDX_DIM,), F32)]).reshape(1, -1)
        cd = _matmul(h, w_in_b, layer=l, cols=(n_ab, n_cd), b_rows=True, out_dtype=BF16, act="groupnorm",
                     extra=(gains, use_norm), name="in_proj_cd")
        ki_n = _ki_norm(tail, dsa_idx_knorm[l])

        lam = diff_lambda[l].astype(F32)
        lam_full = jnp.exp(jnp.sum(lam[0] * lam[1])) - jnp.exp(jnp.sum(lam[2] * lam[3])) + lambda_init
        y_c = _diff_attention(cd, lam_full, diff_subln[l], lambda_init, **dims)
        y_d = _dsa_attention(cd, ki_n, tail, w_off=IDX_DIM + 2 * nhb, **dims)

        x2 = _matmul((y_a, y_b, y_c, y_d), w_out_b, layer=l, out_dtype=F32, act="residual", extra=(x2,),
                     tm_max=512, name="out_proj")
        h = _rmsnorm_rows(x2, norm_mlp[l])
        hidden = _matmul(h, w1_b, layer=l, out_dtype=BF16, act="relu2", name="mlp_up")
        x2 = _matmul(hidden, w2_b, layer=l, out_dtype=F32, act="residual", extra=(x2,), tk_max=2048,
                     name="mlp_down")
    return x2.reshape(batch, seq, d_model).astype(x.dtype)
```

```python
import functools
import math

import numpy as np
import jax
import jax.numpy as jnp
from jax import lax
from jax.experimental import pallas as pl
from jax.experimental.pallas import tpu as pltpu

CHUNK = 64
EPS = 1e-6
NEG = -1e30
F_FLOOR = 1e-20
HEAD_DIM = 128
CONV_W = 4
IDX_HEADS = 16
IDX_DIM = 64
TOPK_MAX = 256
LOG2_E = math.log2(math.e)
LANE = 128
VMEM_LIMIT = 56 * 1024 * 1024

F32 = jnp.float32
BF16 = jnp.bfloat16
NT_DIMS = (((1,), (1,)), ((), ()))
TN_DIMS = (((0,), (0,)), ((), ()))


def _params(*sem):
    return pltpu.CompilerParams(dimension_semantics=sem, vmem_limit_bytes=VMEM_LIMIT)


def _sigmoid(x):
    return 1.0 / (1.0 + jnp.exp(-x))


def _log_sigmoid(x):
    return jnp.minimum(x, 0.0) - jnp.log1p(jnp.exp(-jnp.abs(x)))


def _split3(x):
    hi = x.astype(BF16)
    r1 = x - hi.astype(F32)
    mid = r1.astype(BF16)
    lo = (r1 - mid.astype(F32)).astype(BF16)
    return hi, mid, lo


def _pick_tile(n, candidates):
    for c in candidates:
        if n % c == 0:
            return c
    raise ValueError(f"no tile in {candidates} divides {n}")


def _rmsnorm_kernel(x_ref, g_ref, o_ref):
    x = x_ref[...]
    ms = jnp.mean(x * x, axis=-1, keepdims=True)
    o_ref[...] = (x * lax.rsqrt(ms + EPS) * g_ref[...]).astype(o_ref.dtype)


def _rmsnorm_rows(x2, g):
    m, d = x2.shape
    tm = _pick_tile(m, (256, 128, 64, 8))
    return pl.pallas_call(
        _rmsnorm_kernel,
        grid=(m // tm,),
        in_specs=[pl.BlockSpec((tm, d), lambda i: (i, 0)), pl.BlockSpec((1, d), lambda i: (0, 0))],
        out_specs=pl.BlockSpec((tm, d), lambda i: (i, 0)),
        out_shape=jax.ShapeDtypeStruct((m, d), BF16),
        compiler_params=_params("parallel"),
        name="rmsnorm",
    )(x2, g.reshape(1, d).astype(F32))


MM_MAX_TK = 4096
MM_TN = 1024


def _mm_epilogue(r, act, extra):
    if act == "relu2":
        r = jnp.square(jnp.maximum(r, 0.0))
    elif act == "groupnorm":
        g_ref, u_ref = extra
        segs = []
        for c in range(r.shape[1] // LANE):
            seg = r[:, c * LANE:(c + 1) * LANE]
            ms = jnp.mean(seg * seg, axis=1, keepdims=True)
            inv = jnp.where(u_ref[:, c * LANE:(c + 1) * LANE] > 0.0, lax.rsqrt(ms + EPS), 1.0)
            segs.append(seg * inv * g_ref[:, c * LANE:(c + 1) * LANE])
        r = jnp.concatenate(segs, axis=1)
    elif act == "residual":
        r = r + extra[0][...]
    return r


def _mm_kernel(*refs, n_a, nk, act, n_extra, b_rows):
    a_refs = refs[:n_a]
    b_ref = refs[n_a]
    extra = refs[n_a + 1:n_a + 1 + n_extra]
    o_ref = refs[n_a + 1 + n_extra]

    def product():
        r, off = None, 0
        for a_ref in a_refs:
            kw = a_ref.shape[1]
            if b_rows:
                d = lax.dot_general(a_ref[...], b_ref[:, off:off + kw], NT_DIMS, preferred_element_type=F32)
            else:
                d = jnp.dot(a_ref[...], b_ref[off:off + kw, :], preferred_element_type=F32)
            r = d if r is None else r + d
            off += kw
        return r

    if nk == 1:
        o_ref[...] = _mm_epilogue(product(), act, extra).astype(o_ref.dtype)
        return
    acc_ref = refs[n_a + 2 + n_extra]
    k = pl.program_id(2)

    @pl.when(k == 0)
    def _():
        acc_ref[...] = jnp.zeros_like(acc_ref)

    acc_ref[...] += product()

    @pl.when(k == nk - 1)
    def _():
        o_ref[...] = _mm_epilogue(acc_ref[...], act, extra).astype(o_ref.dtype)


def _matmul(a, b, *, out_dtype, layer, cols=None, b_rows=False, act=None, extra=(), tm_max=1024, tn_max=MM_TN,
            tk_max=MM_MAX_TK, name="matmul"):
    a_parts = a if isinstance(a, tuple) else (a,)
    m = a_parts[0].shape[0]
    kdim = sum(p.shape[1] for p in a_parts)
    col0, n = (0, b.shape[1 if b_rows else 2]) if cols is None else cols
    tm = _pick_tile(m, tuple(t for t in (1024, 512, 256, 128, 8) if t <= tm_max))
    tn = _pick_tile(math.gcd(n, col0), tuple(t for t in (1024, 512, 256, 128) if t <= tn_max))
    j0 = col0 // tn
    tk = kdim if kdim <= tk_max else _pick_tile(kdim, tuple(t for t in (4096, 2048, 1024, 512, 256, 128)
                                                            if t <= tk_max))
    nk = kdim // tk
    assert nk == 1 or len(a_parts) == 1
    in_specs = [pl.BlockSpec((tm, tk if nk > 1 else p.shape[1]), lambda i, j, k: (i, k)) for p in a_parts]
    if b_rows:
        in_specs.append(pl.BlockSpec((None, tn, tk), lambda i, j, k: (layer, j0 + j, k)))
    else:
        in_specs.append(pl.BlockSpec((None, tk, tn), lambda i, j, k: (layer, k, j0 + j)))
    for e in extra:
        if e.shape[0] == 1:
            in_specs.append(pl.BlockSpec((1, tn), lambda i, j, k: (0, j)))
        else:
            in_specs.append(pl.BlockSpec((tm, tn), lambda i, j, k: (i, j)))
    return pl.pallas_call(
        functools.partial(_mm_kernel, n_a=len(a_parts), nk=nk, act=act, n_extra=len(extra), b_rows=b_rows),
        grid=(m // tm, n // tn, nk),
        in_specs=in_specs,
        out_specs=pl.BlockSpec((tm, tn), lambda i, j, k: (i, j)),
        out_shape=jax.ShapeDtypeStruct((m, n), out_dtype),
        scratch_shapes=[pltpu.VMEM((tm, tn), F32)] if nk > 1 else [],
        compiler_params=_params("parallel", "parallel", "arbitrary"),
        name=name,
    )(*a_parts, b, *extra)


_HGRN_LEVELS = 6
_HGRN_HEADS_PER_STEP = 4


def _hgrn_constants():
    idx = np.arange(CHUNK)
    j = idx[None, :]
    r = idx[:, None]
    blocks = [j <= r, j > r]
    masks = []
    for lvl in range(_HGRN_LEVELS):
        half = (CHUNK // 2) >> lvl
        parent = idx // (2 * half)
        mid = parent * 2 * half + half
        is_right = (idx % (2 * half)) >= half
        right_rng = (j >= mid[:, None]) & (j <= r)
        left_rng = (j > r) & (j < mid[:, None])
        blocks.append(np.where(is_right[:, None], right_rng, left_rng))
        masks.append((parent[:, None] == parent[None, :]) & is_right[:, None] & (~is_right)[None, :])
    cm = np.concatenate(blocks, axis=0).astype(np.float32)
    cm3 = np.concatenate([cm, cm, cm, np.zeros_like(cm)], axis=1)
    mk = np.stack(masks, axis=0).astype(np.float32)
    return jnp.asarray(cm3, dtype=BF16), jnp.asarray(mk, dtype=F32)


def _hgrn_kernel(q_ref, f_ref, i_ref, g_ref, lb_ref, gn_ref, cm_ref, mk_ref, o_ref, st_ref, *, nchunk, hb, scale):
    @pl.when(pl.program_id(2) == 0)
    def _():
        st_ref[...] = jnp.zeros_like(st_ref)

    lb = lb_ref[...]
    gn = gn_ref[...]

    def body(c, carry):
        rows = pl.ds(pl.multiple_of(c * CHUNK, CHUNK), CHUNK)
        f = lb + (1.0 - lb) * _sigmoid(f_ref[rows, :])
        logf = jnp.log(jnp.maximum(f, F_FLOOR))
        kk_w = 1.0 - f
        qr = q_ref[rows, :]
        qq_w = qr * _sigmoid(qr) * scale
        v_w = i_ref[rows, :]
        gr = g_ref[rows, :]
        gate_w = gr * _sigmoid(gr)
        l3 = jnp.concatenate(_split3(logf) + (jnp.zeros(logf.shape, BF16),), axis=0)
        sums_w = jnp.dot(cm_ref[...], l3, preferred_element_type=F32)
        for a in range(hb):
            hs = slice(a * HEAD_DIM, (a + 1) * HEAD_DIM)
            sums, qq, kk, v = sums_w[:, hs], qq_w[:, hs], kk_w[:, hs], v_w[:, hs]
            vb = v.astype(BF16)
            b = sums[0:CHUNK]
            b_rest = sums[CHUNK:2 * CHUNK]
            scores = jnp.zeros((CHUNK, CHUNK), F32)
            for lvl in range(_HGRN_LEVELS):
                e = jnp.exp(sums[(2 + lvl) * CHUNK:(3 + lvl) * CHUNK])
                s = lax.dot_general((qq * e).astype(BF16), (kk * e).astype(BF16), NT_DIMS,
                                    preferred_element_type=F32)
                scores = scores + s * mk_ref[lvl]
            diag = jnp.sum(qq * kk, axis=1, keepdims=True)
            st = st_ref[a]
            o = (jnp.dot(scores.astype(BF16), vb, preferred_element_type=F32) + diag * v
                 + lax.dot_general((qq * jnp.exp(b)).astype(BF16), st.astype(BF16), NT_DIMS,
                                   preferred_element_type=F32))
            ke = (kk * jnp.exp(b_rest)).astype(BF16)
            st_ref[a] = (st * jnp.exp(b[CHUNK - 1:CHUNK, :])
                         + lax.dot_general(vb, ke, TN_DIMS, preferred_element_type=F32))
            ms = jnp.mean(o * o, axis=1, keepdims=True)
            o_ref[rows, hs] = (o * lax.rsqrt(ms + EPS) * gn * gate_w[:, hs]).astype(o_ref.dtype)
        return carry

    lax.fori_loop(0, nchunk, body, 0, unroll=True)


def _hgrn(proj, lb, gnorm, *, batch, seq, group):
    m = batch * seq
    nh = group // HEAD_DIM
    t = _pick_tile(seq, (512, 256, 128, 64))
    nt = seq // t
    cm3, mk = _hgrn_constants()

    hb = min(_HGRN_HEADS_PER_STEP, nh)
    assert nh % hb == 0
    wide = hb * HEAD_DIM
    ng = nh // hb

    def col(off):
        return pl.BlockSpec((t, wide), lambda b, h, i: (b * nt + i, off + h))

    return pl.pallas_call(
        functools.partial(_hgrn_kernel, nchunk=t // CHUNK, hb=hb, scale=HEAD_DIM ** -0.5),
        grid=(batch, ng, nt),
        in_specs=[col(0), col(ng), col(2 * ng), col(3 * ng),
                  pl.BlockSpec((1, wide), lambda b, h, i: (0, h)),
                  pl.BlockSpec((1, HEAD_DIM), lambda b, h, i: (0, 0)),
                  pl.BlockSpec(cm3.shape, lambda b, h, i: (0, 0)),
                  pl.BlockSpec(mk.shape, lambda b, h, i: (0, 0, 0))],
        out_specs=pl.BlockSpec((t, wide), lambda b, h, i: (b * nt + i, h)),
        out_shape=jax.ShapeDtypeStruct((m, group), BF16),
        scratch_shapes=[pltpu.VMEM((hb, HEAD_DIM, HEAD_DIM), F32)],
        compiler_params=_params("parallel", "parallel", "arbitrary"),
        name="hgrn2",
    )(proj, proj, proj, proj, lb.reshape(1, group).astype(F32), gnorm.reshape(1, HEAD_DIM).astype(F32), cm3, mk)


_CONV_PAD = 8


def _mlstm_constants():
    idx = np.arange(CHUNK)
    tri = (idx[None, :] <= idx[:, None]).astype(np.float32)
    zero = np.zeros_like(tri)
    tri3 = np.concatenate([tri, tri, tri, zero], axis=1)
    trit3 = np.concatenate([tri.T, tri.T, tri.T, zero], axis=0)
    trit3 = np.concatenate([trit3, np.zeros_like(trit3)], axis=1)
    return jnp.asarray(tri3, dtype=BF16), jnp.asarray(trit3, dtype=BF16)


def _mlstm_kernel(qk_ref, v_ref, og_ref, gc_ref, gr_ref, cw_ref, bc_ref, br_ref, hn_ref, tri_ref, trit_ref,
                  o_ref, xp_ref, qk_s, c_ref, m_ref, *, nchunk, nh, dqk, dv, t, g_off):
    @pl.when(pl.program_id(1) == 0)
    def _():
        xp_ref[0:_CONV_PAD, :] = jnp.zeros((_CONV_PAD, xp_ref.shape[1]), F32)
        c_ref[...] = jnp.zeros_like(c_ref)
        m_ref[...] = jnp.zeros_like(m_ref)

    xp_ref[_CONV_PAD:_CONV_PAD + t, :] = qk_ref[...]
    conv = jnp.zeros((t, 2 * nh * dqk), F32)
    for j in range(CONV_W):
        start = _CONV_PAD - (CONV_W - 1) + j
        conv = conv + cw_ref[j:j + 1, :] * xp_ref[start:start + t, :]
    qk_s[...] = conv * _sigmoid(conv)
    xp_ref[0:_CONV_PAD, :] = xp_ref[t:t + _CONV_PAD, :]

    hn = hn_ref[...]
    kscale = dqk ** -0.5
    r_i = lax.broadcasted_iota(jnp.int32, (CHUNK, CHUNK), 0)
    c_i = lax.broadcasted_iota(jnp.int32, (CHUNK, CHUNK), 1)
    causal = c_i <= r_i
    ones = jnp.ones((CHUNK, dv), BF16)
    zpad_c = jnp.zeros((CHUNK, LANE), BF16)
    zpad_r = jnp.zeros((2 * nh, CHUNK), BF16)

    def body(c, carry):
        base = pl.multiple_of(c * CHUNK, CHUNK)
        rows = pl.ds(base, CHUNK)
        qk = qk_s[rows, :]
        gcol = gc_ref[rows, :] + bc_ref[...]
        b_c = jnp.dot(tri_ref[...], jnp.concatenate(_split3(_log_sigmoid(gcol)) + (zpad_c,), axis=0),
                      preferred_element_type=F32)
        grow = gr_ref[c] + br_ref[...]
        ig_r = grow[0:nh, :]
        b_r = jnp.dot(jnp.concatenate(_split3(_log_sigmoid(grow)) + (zpad_r,), axis=1), trit_ref[...],
                      preferred_element_type=F32)[nh:2 * nh, 0:CHUNK]
        for h in range(nh):
            ig_col = gcol[:, g_off + h:g_off + h + 1]
            q_h = qk[:, h * dqk:(h + 1) * dqk].astype(BF16)
            k_f = qk[:, (nh + h) * dqk:(nh + h + 1) * dqk] * kscale
            k_h = k_f.astype(BF16)
            v_ext = jnp.concatenate([v_ref[rows, h * dv:(h + 1) * dv].astype(BF16), ones], axis=1)
            bcol = b_c[:, g_off + nh + h:g_off + nh + h + 1]
            m_prev = m_ref[h:h + 1, 0:1]
            log_d = jnp.where(causal, bcol - b_r[h:h + 1, :] + ig_r[h:h + 1, :], NEG)
            log_inter = bcol + m_prev
            m_t = jnp.maximum(jnp.max(log_d, axis=1, keepdims=True), log_inter)
            qk_w = (lax.dot_general(q_h, k_h, NT_DIMS, preferred_element_type=F32)
                    * jnp.exp(log_d - m_t))
            w_inter = jnp.exp(log_inter - m_t)
            c_ext = c_ref[h]
            numden = (jnp.dot(qk_w.astype(BF16), v_ext, preferred_element_type=F32)
                      + w_inter * jnp.dot(q_h, c_ext.astype(BF16), preferred_element_type=F32))
            num = numden[:, 0:dv]
            den = numden[:, dv:2 * dv]
            hh = num / jnp.maximum(jnp.abs(den), jnp.exp(-m_t))
            m_new = m_t[CHUNK - 1:CHUNK, :]
            b_last = bcol[CHUNK - 1:CHUNK, :]
            w_state = jnp.exp(b_last - bcol + ig_col - m_new)
            decay = jnp.exp(b_last + m_prev - m_new)
            c_ref[h] = decay * c_ext + lax.dot_general((k_f * w_state).astype(BF16), v_ext, TN_DIMS,
                                                       preferred_element_type=F32)
            m_ref[h:h + 1, :] = jnp.broadcast_to(m_new, (1, LANE))
            ms = jnp.mean(hh * hh, axis=1, keepdims=True)
            gate = _sigmoid(og_ref[rows, h * dv:(h + 1) * dv])
            o_ref[rows, h * dv:(h + 1) * dv] = (hh * lax.rsqrt(ms + EPS) * hn * gate).astype(o_ref.dtype)
        return carry

    lax.fori_loop(0, nchunk, body, 0, unroll=True)


def _mlstm(proj, tail, conv_w, gate_bias, hnorm, *, batch, seq, group, g_off):
    m = batch * seq
    nh = group // HEAD_DIM
    dqk = HEAD_DIM // 2
    dv = HEAD_DIM
    t = _pick_tile(seq, (256, 128, 64))
    nt = seq // t
    tri3, trit3 = _mlstm_constants()
    qk_blk = 4
    gates_row = (tail[:, g_off:g_off + 2 * nh].reshape(batch, seq // CHUNK, CHUNK, 2 * nh)
                 .transpose(0, 1, 3, 2))
    bias = gate_bias.astype(F32)
    bias_lane = jnp.zeros((1, LANE), F32).at[0, g_off:g_off + 2 * nh].set(bias)

    def wide(off):
        return pl.BlockSpec((t, group), lambda b, i: (b * nt + i, off))

    def const(shape):
        return pl.BlockSpec(shape, lambda b, i: tuple(0 for _ in shape))

    return pl.pallas_call(
        functools.partial(_mlstm_kernel, nchunk=t // CHUNK, nh=nh, dqk=dqk, dv=dv, t=t, g_off=g_off),
        grid=(batch, nt),
        in_specs=[wide(qk_blk), wide(qk_blk + 1), wide(qk_blk + 2),
                  pl.BlockSpec((t, LANE), lambda b, i: (b * nt + i, 0)),
                  pl.BlockSpec((None, t // CHUNK, 2 * nh, CHUNK), lambda b, i: (b, i, 0, 0)),
                  const((CONV_W, group)), const((1, LANE)), const((2 * nh, 1)), const((1, dv)),
                  const(tri3.shape), const(trit3.shape)],
        out_specs=pl.BlockSpec((t, group), lambda b, i: (b * nt + i, 0)),
        out_shape=jax.ShapeDtypeStruct((m, group), BF16),
        scratch_shapes=[pltpu.VMEM((t + _CONV_PAD, group), F32),
                        pltpu.VMEM((t, group), F32),
                        pltpu.VMEM((nh, dqk, 2 * dv), F32),
                        pltpu.VMEM((nh, LANE), F32)],
        compiler_params=_params("parallel", "arbitrary"),
        name="mlstm",
    )(proj, proj, proj, tail, gates_row, conv_w.astype(F32), bias_lane, bias.reshape(2 * nh, 1),
      hnorm.reshape(1, dv).astype(F32), tri3, trit3)


def _ki_kernel(x_ref, g_ref, o_ref):
    x = x_ref[:, 0:IDX_DIM]
    ms = jnp.mean(x * x, axis=-1, keepdims=True)
    o_ref[...] = (x * lax.rsqrt(ms + EPS) * g_ref[...]).astype(o_ref.dtype)


def _ki_norm(tail, g):
    m = tail.shape[0]
    tm = _pick_tile(m, (512, 256, 128, 64))
    return pl.pallas_call(
        _ki_kernel,
        grid=(m // tm,),
        in_specs=[pl.BlockSpec((tm, LANE), lambda i: (i, 0)), pl.BlockSpec((1, IDX_DIM), lambda i: (0, 0))],
        out_specs=pl.BlockSpec((tm, IDX_DIM), lambda i: (i, 0)),
        out_shape=jax.ShapeDtypeStruct((m, IDX_DIM), BF16),
        compiler_params=_params("parallel"),
        name="ki_norm",
    )(tail, g.reshape(1, IDX_DIM).astype(F32))


def _diff_kernel(q_ref, k_ref, vt_ref, lam_ref, sn_ref, o_ref, m_ref, l_ref, acc_ref, *, tq, tk, dqk, out_scale):
    qi = pl.program_id(2)
    last_k = ((qi + 1) * tq - 1) // tk
    m_ref[...] = jnp.full_like(m_ref, NEG)
    l_ref[...] = jnp.zeros_like(l_ref)
    acc_ref[...] = jnp.zeros_like(acc_ref)
    q = q_ref[...]

    def step(kb, masked):
        rows = pl.ds(pl.multiple_of(kb * tk, tk), tk)
        k = k_ref[rows, :]
        vt = vt_ref[kb]
        if masked:
            k_chunk = (kb * tk + lax.broadcasted_iota(jnp.int32, (tk, tq), 0)) // CHUNK
            q_chunk = (qi * tq + lax.broadcasted_iota(jnp.int32, (tk, tq), 1)) // CHUNK
            mask = k_chunk <= q_chunk
        s_pair = [lax.dot_general(k[:, half * dqk:(half + 1) * dqk], q[:, half * dqk:(half + 1) * dqk], NT_DIMS,
                                  preferred_element_type=F32) for half in range(2)]
        probs = []
        for half in range(2):
            s = s_pair[half]
            if masked:
                s = jnp.where(mask, s, NEG)
            m_prev = m_ref[half]
            m_new = jnp.maximum(m_prev, jnp.max(s, axis=0, keepdims=True))
            p = jnp.exp2(s - m_new)
            alpha = jnp.exp2(m_prev - m_new)
            l_ref[half] = alpha * l_ref[half] + jnp.sum(p, axis=0, keepdims=True)
            m_ref[half] = m_new
            probs.append((alpha, p.astype(BF16)))
        for half in range(2):
            alpha, p = probs[half]
            acc_ref[half] = alpha * acc_ref[half] + jnp.dot(vt, p, preferred_element_type=F32)

    def body(kb, carry):
        step(kb, False)
        return carry

    lax.fori_loop(0, last_k, body, 0)
    step(last_k, True)

    o = acc_ref[0] / l_ref[0] - lam_ref[...] * (acc_ref[1] / l_ref[1])
    ms = jnp.mean(o * o, axis=0, keepdims=True)
    o_ref[...] = (o * lax.rsqrt(ms + EPS) * (sn_ref[...] * out_scale)).T.astype(o_ref.dtype)


def _diff_attention(cd, lam_full, subln, lambda_init, *, batch, seq, group):
    m = batch * seq
    dqk = HEAD_DIM
    dv = 2 * HEAD_DIM
    nh = group // dv
    tq = _pick_tile(seq, (1024, 512, 256, 128))
    tk = _pick_tile(seq, (1024, 512, 256, 128))
    assert tk % tq == 0
    nq, nkb = seq // tq, seq // tk
    k_off = nh
    v_t = cd[:, 2 * group:3 * group].reshape(batch, nkb, tk, nh, dv).transpose(0, 1, 3, 4, 2)

    return pl.pallas_call(
        functools.partial(_diff_kernel, tq=tq, tk=tk, dqk=dqk, out_scale=1.0 - lambda_init),
        grid=(batch, nh, nq),
        in_specs=[pl.BlockSpec((tq, dv), lambda b, h, i: (b * nq + i, h)),
                  pl.BlockSpec((seq, dv), lambda b, h, i: (b, k_off + h)),
                  pl.BlockSpec((None, nkb, None, dv, tk), lambda b, h, i: (b, 0, h, 0, 0)),
                  pl.BlockSpec((1, 1), lambda b, h, i: (0, 0)),
                  pl.BlockSpec((dv, 1), lambda b, h, i: (0, 0))],
        out_specs=pl.BlockSpec((tq, dv), lambda b, h, i: (b * nq + i, h)),
        out_shape=jax.ShapeDtypeStruct((m, group), BF16),
        scratch_shapes=[pltpu.VMEM((2, 1, tq), F32), pltpu.VMEM((2, 1, tq), F32), pltpu.VMEM((2, dv, tq), F32)],
        compiler_params=_params("parallel", "parallel", "arbitrary"),
        name="diff_attn",
    )(cd, cd, v_t, lam_full.reshape(1, 1).astype(F32), subln.reshape(dv, 1).astype(F32))


_INT_MIN = -2 ** 31
_HI16 = -(1 << 16)


_CNT_ROWS = 64
_DSA_HEADS_PER_STEP = 2


def _dsa_kernel(q_ref, k_ref, vt_ref, qi_ref, ki_ref, wt_ref, o_ref, sc_ref, sc16_ref, *, tq, tk, topk, hpb,
                w_scale):
    qb = pl.program_id(1)
    h = pl.program_id(2)
    nkv = ((qb + 1) * tq + tk - 1) // tk

    @pl.when(h == 0)
    def _():
        w = wt_ref[...] * w_scale
        q_chunk = (qb * tq + lax.broadcasted_iota(jnp.int32, (tk, tq), 1)) // CHUNK

        def score_body(kb, carry):
            rows = pl.ds(pl.multiple_of(kb * tk, tk), tk)
            ki = ki_ref[rows, :]
            acc = jnp.zeros((tk, tq), F32)
            for ih in range(IDX_HEADS):
                lg = lax.dot_general(ki, qi_ref[:, ih * IDX_DIM:(ih + 1) * IDX_DIM], NT_DIMS,
                                     preferred_element_type=F32)
                acc = acc + w[ih:ih + 1, :] * jnp.maximum(lg, 0.0)
            k_chunk = (kb * tk + lax.broadcasted_iota(jnp.int32, (tk, tq), 0)) // CHUNK
            x = jnp.where(k_chunk <= q_chunk, acc, NEG)
            sc_ref[kb] = x
            hi = lax.bitcast_convert_type(x, jnp.int32) & _HI16
            sc16_ref[kb] = lax.bitcast_convert_type(hi, F32).astype(BF16)
            return carry

        lax.fori_loop(0, nkv, score_body, 0)

        def count_ge(cf):
            def cnt_body(kb, cnt):
                for r in range(tk // _CNT_ROWS):
                    hit = jnp.where(sc_ref[kb, r * _CNT_ROWS:(r + 1) * _CNT_ROWS, :] >= cf, 1.0, 0.0)
                    for g in range(_CNT_ROWS // 8):
                        cnt = cnt + hit[g * 8:(g + 1) * 8, :]
                return cnt

            return jnp.sum(lax.fori_loop(0, nkv, cnt_body, jnp.zeros((8, tq), F32)), axis=0, keepdims=True)

        def count_ge_hi(cfb):
            one, zero = jnp.ones((), BF16), jnp.zeros((), BF16)

            def cnt_body(kb, cnt):
                part = jnp.zeros((16, tq), BF16)
                for r in range(tk // _CNT_ROWS):
                    hit = jnp.where(sc16_ref[kb, r * _CNT_ROWS:(r + 1) * _CNT_ROWS, :] >= cfb, one, zero)
                    for g in range(_CNT_ROWS // 16):
                        part = part + hit[g * 16:(g + 1) * 16, :]
                return cnt + part.astype(F32)

            return jnp.sum(lax.fori_loop(0, nkv, cnt_body, jnp.zeros((16, tq), F32)), axis=0, keepdims=True)

        def bit_body(state, *, coarse):
            i, u, kept = state
            cand = u | jnp.left_shift(jnp.int32(1), 31 - i)
            key = cand ^ _INT_MIN
            if coarse:
                key = jnp.where(key < 0, key | 0xFFFF, key)
            cf = lax.bitcast_convert_type(key ^ (jnp.right_shift(key, 31) & 0x7FFFFFFF), F32)
            cnt = count_ge_hi(cf.astype(BF16)) if coarse else count_ge(cf)
            ok = cnt >= float(topk)
            return i + 1, jnp.where(ok, cand, u), jnp.where(ok, cnt, kept)

        def unsettled(state, *, last):
            i, _, kept = state
            return (i < last) & (jnp.max(jnp.where(kept == float(topk), 0.0, 1.0)) > 0.0)

        in_range = jnp.zeros((1, tq), F32) + (nkv * tk).astype(F32)
        state = (jnp.int32(0), jnp.zeros((1, tq), jnp.int32), in_range)
        state = lax.while_loop(functools.partial(unsettled, last=16), functools.partial(bit_body, coarse=True), state)
        state = lax.while_loop(functools.partial(unsettled, last=32), functools.partial(bit_body, coarse=False),
                               (jnp.maximum(state[0], 16),) + state[1:])
        key = state[1] ^ _INT_MIN
        thr = lax.bitcast_convert_type(key ^ (jnp.right_shift(key, 31) & 0x7FFFFFFF), F32)

        tied = (state[2] > float(topk)) & (thr > 0.5 * NEG)
        has_tie = jnp.max(jnp.where(tied, 1.0, 0.0)) > 0.0

        @pl.when(jnp.logical_not(has_tie))
        def _():
            def bias_body(kb, carry):
                s = sc_ref[kb]
                sc_ref[kb] = jnp.where((s >= thr) & (s > 0.5 * NEG), 0.0, NEG)
                return carry

            lax.fori_loop(0, nkv, bias_body, 0)

        @pl.when(has_tie)
        def _():
            def gt_body(kb, cnt):
                for r in range(tk // _CNT_ROWS):
                    hit = jnp.where(sc_ref[kb, r * _CNT_ROWS:(r + 1) * _CNT_ROWS, :] > thr, 1.0, 0.0)
                    for g in range(_CNT_ROWS // 8):
                        cnt = cnt + hit[g * 8:(g + 1) * 8, :]
                return cnt

            above = jnp.sum(lax.fori_loop(0, nkv, gt_body, jnp.zeros((8, tq), F32)), axis=0, keepdims=True)
            room = float(topk) - above
            lower = (lax.broadcasted_iota(jnp.int32, (LANE, LANE), 1)
                     <= lax.broadcasted_iota(jnp.int32, (LANE, LANE), 0)).astype(BF16)

            def tie_body(kb, seen):
                for r in range(tk // LANE):
                    strip = slice(r * LANE, (r + 1) * LANE)
                    s = sc_ref[kb, strip, :]
                    eq = s == thr
                    rank = seen + jnp.dot(lower, jnp.where(eq, 1.0, 0.0).astype(BF16), preferred_element_type=F32)
                    keep = ((s > thr) | (eq & (rank <= room))) & (s > 0.5 * NEG)
                    sc_ref[kb, strip, :] = jnp.where(keep, 0.0, NEG)
                    seen = rank[LANE - 1:LANE, :]
                return seen

            lax.fori_loop(0, nkv, tie_body, jnp.zeros((1, tq), F32))

    q = q_ref[...]

    def kv_body(kb, carry):
        rows = pl.ds(pl.multiple_of(kb * tk, tk), tk)
        k = k_ref[rows, :]
        bias = sc_ref[kb]
        qk = [lax.dot_general(k[:, a * HEAD_DIM:(a + 1) * HEAD_DIM], q[:, a * HEAD_DIM:(a + 1) * HEAD_DIM], NT_DIMS,
                              preferred_element_type=F32) for a in range(hpb)]
        out = []
        for a in range(hpb):
            m_prev, l_prev, acc = carry[a]
            s = qk[a] + bias
            m_new = jnp.maximum(m_prev, jnp.max(s, axis=0, keepdims=True))
            p = jnp.exp2(s - m_new)
            alpha = jnp.exp2(m_prev - m_new)
            l_new = alpha * l_prev + jnp.sum(p, axis=0, keepdims=True)
            acc = alpha * acc + jnp.dot(vt_ref[kb, a], p.astype(BF16), preferred_element_type=F32)
            out.append((m_new, l_new, acc))
        return tuple(out)

    init = tuple((jnp.full((1, tq), NEG, F32), jnp.zeros((1, tq), F32), jnp.zeros((HEAD_DIM, tq), F32))
                 for _ in range(hpb))
    fin = lax.fori_loop(0, nkv, kv_body, init)
    for a in range(hpb):
        _, l_fin, acc = fin[a]
        o_ref[:, a * HEAD_DIM:(a + 1) * HEAD_DIM] = (acc / l_fin).T.astype(o_ref.dtype)


def _dsa_attention(cd, ki_n, tail, *, batch, seq, group, w_off):
    m = batch * seq
    nh = group // HEAD_DIM
    tq = _pick_tile(seq, (512, 256, 128))
    tk = _pick_tile(seq, (1024, 512, 256))
    topk = min(TOPK_MAX, seq // 4)
    assert tk >= topk, "the threshold search needs at least topk keys in range"
    nq = seq // tq
    nc = group // LANE
    nkb = seq // tk
    q_off, k_off, v_off = 3 * nc, 4 * nc, 5 * nc
    qi_off = 6 * nc * LANE // (IDX_HEADS * IDX_DIM)
    assert qi_off * IDX_HEADS * IDX_DIM == 6 * nc * LANE
    v_t = (cd[:, v_off * LANE:(v_off + nc) * LANE].reshape(batch, nkb, tk, nh, HEAD_DIM)
           .transpose(0, 1, 3, 4, 2))
    w_t = tail[:, w_off:w_off + IDX_HEADS].reshape(batch, seq, IDX_HEADS).transpose(0, 2, 1)

    hpb = _DSA_HEADS_PER_STEP
    assert nh % hpb == 0
    wide = hpb * HEAD_DIM
    return pl.pallas_call(
        functools.partial(_dsa_kernel, tq=tq, tk=tk, topk=topk, hpb=hpb,
                          w_scale=IDX_HEADS ** -0.5 * IDX_DIM ** -0.5),
        grid=(batch, nq, nh // hpb),
        in_specs=[pl.BlockSpec((tq, wide), lambda b, i, h: (b * nq + i, q_off // hpb + h)),
                  pl.BlockSpec((seq, wide), lambda b, i, h: (b, k_off // hpb + h)),
                  pl.BlockSpec((None, nkb, hpb, HEAD_DIM, tk), lambda b, i, h: (b, 0, h, 0, 0)),
                  pl.BlockSpec((tq, IDX_HEADS * IDX_DIM), lambda b, i, h: (b * nq + i, qi_off)),
                  pl.BlockSpec((seq, IDX_DIM), lambda b, i, h: (b, 0)),
                  pl.BlockSpec((None, IDX_HEADS, tq), lambda b, i, h: (b, 0, i))],
        out_specs=pl.BlockSpec((tq, wide), lambda b, i, h: (b * nq + i, h)),
        out_shape=jax.ShapeDtypeStruct((m, group), BF16),
        scratch_shapes=[pltpu.VMEM((nkb, tk, tq), F32), pltpu.VMEM((nkb, tk, tq), BF16)],
        compiler_params=_params("parallel", "arbitrary", "arbitrary"),
        name="dsa",
    )(cd, cd, v_t, cd, ki_n, w_t)


def _repack_w_in(w, group, nhb):
    p0 = 7 * group
    p1 = p0 + 2 * nhb
    p2 = p1 + 6 * group + IDX_HEADS * IDX_DIM
    used = IDX_DIM + 2 * nhb + IDX_HEADS
    wt = jnp.swapaxes(w, 1, 2)
    pad = jnp.zeros((w.shape[0], LANE - used, w.shape[1]), w.dtype)
    return jnp.concatenate([wt[:, :p0], wt[:, p1:p2], wt[:, p2:p2 + IDX_DIM], wt[:, p0:p1],
                            wt[:, p2 + IDX_DIM:], pad], axis=1).astype(BF16)


def kernel(x, norm_mix, w_in, hgrn_lb_logits, hgrn_gnorm, mlstm_conv, mlstm_gate_bias, mlstm_hnorm,
           diff_qk_norm, diff_lambda, diff_subln, dsa_qk_norm, dsa_idx_knorm, w_out, norm_mlp, w1, w2):
    batch, seq, d_model = x.shape
    depth = w_in.shape[0]
    group = d_model // 4
    nhb = group // HEAD_DIM
    nc = group // LANE
    m = batch * seq
    dims = dict(batch=batch, seq=seq, group=group)

    p_lb = jax.nn.softmax(hgrn_lb_logits.astype(F32), axis=0)
    lower_bounds = jnp.cumsum(p_lb, axis=0) - p_lb[0:1]

    w_in_b = _repack_w_in(w_in, group, nhb)
    w_out_b, w1_b, w2_b = (w.astype(BF16) for w in (w_out, w1, w2))
    n_ab, n_cd = 7 * group, 6 * group + IDX_HEADS * IDX_DIM

    x2 = x.reshape(m, d_model).astype(F32)
    ones_g = jnp.ones((group,), F32)
    for l in range(depth):
        lambda_init = 0.8 - 0.6 * math.exp(-0.3 * l)
        h = _rmsnorm_rows(x2, norm_mix[l])
        proj = _matmul(h, w_in_b, layer=l, cols=(0, n_ab), b_rows=True, out_dtype=F32, name="in_proj_ab")
        tail = _matmul(h, w_in_b, layer=l, cols=(n_ab + n_cd, LANE), b_rows=True, out_dtype=F32,
                       name="in_proj_tail")

        y_a = _hgrn(proj, lower_bounds[l], hgrn_gnorm[l], **dims)

        y_b = _mlstm(proj, tail, mlstm_conv[l], mlstm_gate_bias[l], mlstm_hnorm[l], g_off=IDX_DIM, **dims)

        cqn = diff_qk_norm[l].astype(F32)
        dqn = dsa_qk_norm[l].astype(F32)
        q_scale = HEAD_DIM ** -0.5 * LOG2_E
        gains = jnp.concatenate([
            jnp.tile(cqn[0], nc) * q_scale, jnp.tile(cqn[1], nc), ones_g,
            jnp.tile(dqn[0], nc) * q_scale, jnp.tile(dqn[1], nc), ones_g,
            jnp.ones((IDX_HEADS * IDX_DIM,), F32)]).reshape(1, -1)
        use_norm = jnp.concatenate([
            jnp.ones((2 * group,), F32), jnp.zeros((group,), F32),
            jnp.ones((2 * group,), F32), jnp.zeros((group + IDX_HEADS * IDX_DIM,), F32)]).reshape(1, -1)
        cd = _matmul(h, w_in_b, layer=l, cols=(n_ab, n_cd), b_rows=True, out_dtype=BF16, act="groupnorm",
                     extra=(gains, use_norm), name="in_proj_cd")
        ki_n = _ki_norm(tail, dsa_idx_knorm[l])

        lam = diff_lambda[l].astype(F32)
        lam_full = jnp.exp(jnp.sum(lam[0] * lam[1])) - jnp.exp(jnp.sum(lam[2] * lam[3])) + lambda_init
        y_c = _diff_attention(cd, lam_full, diff_subln[l], lambda_init, **dims)
        y_d = _dsa_attention(cd, ki_n, tail, w_off=IDX_DIM + 2 * nhb, **dims)

        x2 = _matmul((y_a, y_b, y_c, y_d), w_out_b, layer=l, out_dtype=F32, act="residual", extra=(x2,),
                     tm_max=512, name="out_proj")
        h = _rmsnorm_rows(x2, norm_mlp[l])
        hidden = _matmul(h, w1_b, layer=l, out_dtype=BF16, act="relu2", name="mlp_up")
        x2 = _matmul(hidden, w2_b, layer=l, out_dtype=F32, act="residual", extra=(x2,), tk_max=2048,
                     name="mlp_down")
    return x2.reshape(batch, seq, d_model).astype(x.dtype)
```

```python
import functools
import math

import numpy as np
import jax
import jax.numpy as jnp
from jax import lax
from jax.experimental import pallas as pl
from jax.experimental.pallas import tpu as pltpu

CHUNK = 64
EPS = 1e-6
NEG = -1e30
F_FLOOR = 1e-20
HEAD_DIM = 128
CONV_W = 4
IDX_HEADS = 16
IDX_DIM = 64
TOPK_MAX = 256
LOG2_E = math.log2(math.e)
LANE = 128
VMEM_LIMIT = 56 * 1024 * 1024

F32 = jnp.float32
BF16 = jnp.bfloat16
NT_DIMS = (((1,), (1,)), ((), ()))
TN_DIMS = (((0,), (0,)), ((), ()))


def _params(*sem):
    return pltpu.CompilerParams(dimension_semantics=sem, vmem_limit_bytes=VMEM_LIMIT)


def _sigmoid(x):
    return 1.0 / (1.0 + jnp.exp(-x))


def _log_sigmoid(x):
    return jnp.minimum(x, 0.0) - jnp.log1p(jnp.exp(-jnp.abs(x)))


def _split3(x):
    hi = x.astype(BF16)
    r1 = x - hi.astype(F32)
    mid = r1.astype(BF16)
    lo = (r1 - mid.astype(F32)).astype(BF16)
    return hi, mid, lo


def _pick_tile(n, candidates):
    for c in candidates:
        if n % c == 0:
            return c
    raise ValueError(f"no tile in {candidates} divides {n}")


def _rmsnorm_kernel(x_ref, g_ref, o_ref):
    x = x_ref[...]
    ms = jnp.mean(x * x, axis=-1, keepdims=True)
    o_ref[...] = (x * lax.rsqrt(ms + EPS) * g_ref[...]).astype(o_ref.dtype)


def _rmsnorm_rows(x2, g):
    m, d = x2.shape
    tm = _pick_tile(m, (256, 128, 64, 8))
    return pl.pallas_call(
        _rmsnorm_kernel,
        grid=(m // tm,),
        in_specs=[pl.BlockSpec((tm, d), lambda i: (i, 0)), pl.BlockSpec((1, d), lambda i: (0, 0))],
        out_specs=pl.BlockSpec((tm, d), lambda i: (i, 0)),
        out_shape=jax.ShapeDtypeStruct((m, d), BF16),
        compiler_params=_params("parallel"),
        name="rmsnorm",
    )(x2, g.reshape(1, d).astype(F32))


MM_MAX_TK = 4096
MM_TN = 1024


def _mm_epilogue(r, act, extra):
    if act == "relu2":
        r = jnp.square(jnp.maximum(r, 0.0))
    elif act == "groupnorm":
        g_ref, u_ref = extra
        segs = []
        for c in range(r.shape[1] // LANE):
            seg = r[:, c * LANE:(c + 1) * LANE]
            ms = jnp.mean(seg * seg, axis=1, keepdims=True)
            inv = jnp.where(u_ref[:, c * LANE:(c + 1) * LANE] > 0.0, lax.rsqrt(ms + EPS), 1.0)
            segs.append(seg * inv * g_ref[:, c * LANE:(c + 1) * LANE])
        r = jnp.concatenate(segs, axis=1)
    elif act == "residual":
        r = r + extra[0][...]
    return r


def _mm_kernel(*refs, n_a, nk, act, n_extra, b_rows):
    a_refs = refs[:n_a]
    b_ref = refs[n_a]
    extra = refs[n_a + 1:n_a + 1 + n_extra]
    o_ref = refs[n_a + 1 + n_extra]

    def product():
        r, off = None, 0
        for a_ref in a_refs:
            kw = a_ref.shape[1]
            if b_rows:
                d = lax.dot_general(a_ref[...], b_ref[:, off:off + kw], NT_DIMS, preferred_element_type=F32)
            else:
                d = jnp.dot(a_ref[...], b_ref[off:off + kw, :], preferred_element_type=F32)
            r = d if r is None else r + d
            off += kw
        return r

    if nk == 1:
        o_ref[...] = _mm_epilogue(product(), act, extra).astype(o_ref.dtype)
        return
    acc_ref = refs[n_a + 2 + n_extra]
    k = pl.program_id(2)

    @pl.when(k == 0)
    def _():
        acc_ref[...] = jnp.zeros_like(acc_ref)

    acc_ref[...] += product()

    @pl.when(k == nk - 1)
    def _():
        o_ref[...] = _mm_epilogue(acc_ref[...], act, extra).astype(o_ref.dtype)


def _matmul(a, b, *, out_dtype, layer, cols=None, b_rows=False, act=None, extra=(), tm_max=1024, tn_max=MM_TN,
            tk_max=MM_MAX_TK, name="matmul"):
    a_parts = a if isinstance(a, tuple) else (a,)
    m = a_parts[0].shape[0]
    kdim = sum(p.shape[1] for p in a_parts)
    col0, n = (0, b.shape[1 if b_rows else 2]) if cols is None else cols
    tm = _pick_tile(m, tuple(t for t in (1024, 512, 256, 128, 8) if t <= tm_max))
    tn = _pick_tile(math.gcd(n, col0), tuple(t for t in (1024, 512, 256, 128) if t <= tn_max))
    j0 = col0 // tn
    tk = kdim if kdim <= tk_max else _pick_tile(kdim, tuple(t for t in (4096, 2048, 1024, 512, 256, 128)
                                                            if t <= tk_max))
    nk = kdim // tk
    assert nk == 1 or len(a_parts) == 1
    in_specs = [pl.BlockSpec((tm, tk if nk > 1 else p.shape[1]), lambda i, j, k: (i, k)) for p in a_parts]
    if b_rows:
        in_specs.append(pl.BlockSpec((None, tn, tk), lambda i, j, k: (layer, j0 + j, k)))
    else:
        in_specs.append(pl.BlockSpec((None, tk, tn), lambda i, j, k: (layer, k, j0 + j)))
    for e in extra:
        if e.shape[0] == 1:
            in_specs.append(pl.BlockSpec((1, tn), lambda i, j, k: (0, j)))
        else:
            in_specs.append(pl.BlockSpec((tm, tn), lambda i, j, k: (i, j)))
    return pl.pallas_call(
        functools.partial(_mm_kernel, n_a=len(a_parts), nk=nk, act=act, n_extra=len(extra), b_rows=b_rows),
        grid=(m // tm, n // tn, nk),
        in_specs=in_specs,
        out_specs=pl.BlockSpec((tm, tn), lambda i, j, k: (i, j)),
        out_shape=jax.ShapeDtypeStruct((m, n), out_dtype),
        scratch_shapes=[pltpu.VMEM((tm, tn), F32)] if nk > 1 else [],
        compiler_params=_params("parallel", "parallel", "arbitrary"),
        name=name,
    )(*a_parts, b, *extra)


_HGRN_LEVELS = 6
_HGRN_HEADS_PER_STEP = 4


def _hgrn_constants():
    idx = np.arange(CHUNK)
    j = idx[None, :]
    r = idx[:, None]
    blocks = [j <= r, j > r]
    masks = []
    for lvl in range(_HGRN_LEVELS):
        half = (CHUNK // 2) >> lvl
        parent = idx // (2 * half)
        mid = parent * 2 * half + half
        is_right = (idx % (2 * half)) >= half
        right_rng = (j >= mid[:, None]) & (j <= r)
        left_rng = (j > r) & (j < mid[:, None])
        blocks.append(np.where(is_right[:, None], right_rng, left_rng))
        masks.append((parent[:, None] == parent[None, :]) & is_right[:, None] & (~is_right)[None, :])
    cm = np.concatenate(blocks, axis=0).astype(np.float32)
    cm3 = np.concatenate([cm, cm, cm, np.zeros_like(cm)], axis=1)
    mk = np.stack(masks, axis=0).astype(np.float32)
    return jnp.asarray(cm3, dtype=BF16), jnp.asarray(mk, dtype=F32)


def _hgrn_kernel(q_ref, f_ref, i_ref, g_ref, lb_ref, gn_ref, cm_ref, mk_ref, o_ref, st_ref, *, nchunk, hb, scale):
    @pl.when(pl.program_id(2) == 0)
    def _():
        st_ref[...] = jnp.zeros_like(st_ref)

    lb = lb_ref[...]
    gn = gn_ref[...]

    def body(c, carry):
        rows = pl.ds(pl.multiple_of(c * CHUNK, CHUNK), CHUNK)
        f = lb + (1.0 - lb) * _sigmoid(f_ref[rows, :])
        logf = jnp.log(jnp.maximum(f, F_FLOOR))
        kk_w = 1.0 - f
        qr = q_ref[rows, :]
        qq_w = qr * _sigmoid(qr) * scale
        v_w = i_ref[rows, :]
        gr = g_ref[rows, :]
        gate_w = gr * _sigmoid(gr)
        l3 = jnp.concatenate(_split3(logf) + (jnp.zeros(logf.shape, BF16),), axis=0)
        sums_w = jnp.dot(cm_ref[...], l3, preferred_element_type=F32)
        new_state = []
        for a in range(hb):
            hs = slice(a * HEAD_DIM, (a + 1) * HEAD_DIM)
            sums, qq, kk, v =sums_w[:, hs], qq_w[:, hs], kk_w[:, hs], v_w[:, hs]
            vb = v.astype(BF16)
            b = sums[0:CHUNK]
            b_rest = sums[CHUNK:2 * CHUNK]
            scores = jnp.zeros((CHUNK, CHUNK), F32)
            for lvl in range(_HGRN_LEVELS):
                e = jnp.exp(sums[(2 + lvl) * CHUNK:(3 + lvl) * CHUNK])
                s = lax.dot_general((qq * e).astype(BF16), (kk * e).astype(BF16), NT_DIMS,
                                    preferred_element_type=F32)
                scores = scores + s * mk_ref[lvl]
            diag = jnp.sum(qq * kk, axis=1, keepdims=True)
            st = carry[a]
            o = (jnp.dot(scores.astype(BF16), vb, preferred_element_type=F32) + diag * v
                 + lax.dot_general((qq * jnp.exp(b)).astype(BF16), st.astype(BF16), NT_DIMS,
                                   preferred_element_type=F32))
            ke = (kk * jnp.exp(b_rest)).astype(BF16)
            new_state.append(st * jnp.exp(b[CHUNK - 1:CHUNK, :])
                             + lax.dot_general(vb, ke, TN_DIMS, preferred_element_type=F32))
            ms = jnp.mean(o * o, axis=1, keepdims=True)
            o_ref[rows, hs] = (o * lax.rsqrt(ms + EPS) * gn * gate_w[:, hs]).astype(o_ref.dtype)
        return tuple(new_state)

    state = lax.fori_loop(0, nchunk, body, tuple(st_ref[a] for a in range(hb)), unroll=True)
    for a in range(hb):
        st_ref[a] = state[a]


def _hgrn(proj, lb, gnorm, *, batch, seq, group):
    m = batch * seq
    nh = group // HEAD_DIM
    t = _pick_tile(seq, (512, 256, 128, 64))
    nt = seq // t
    cm3, mk = _hgrn_constants()

    hb = min(_HGRN_HEADS_PER_STEP, nh)
    assert nh % hb == 0
    wide = hb * HEAD_DIM
    ng = nh // hb

    def col(off):
        return pl.BlockSpec((t, wide), lambda b, h, i: (b * nt + i, off + h))

    return pl.pallas_call(
        functools.partial(_hgrn_kernel, nchunk=t // CHUNK, hb=hb, scale=HEAD_DIM ** -0.5),
        grid=(batch, ng, nt),
        in_specs=[col(0), col(ng), col(2 * ng), col(3 * ng),
                  pl.BlockSpec((1, wide), lambda b, h, i: (0, h)),
                  pl.BlockSpec((1, HEAD_DIM), lambda b, h, i: (0, 0)),
                  pl.BlockSpec(cm3.shape, lambda b, h, i: (0, 0)),
                  pl.BlockSpec(mk.shape, lambda b, h, i: (0, 0, 0))],
        out_specs=pl.BlockSpec((t, wide), lambda b, h, i: (b * nt + i, h)),
        out_shape=jax.ShapeDtypeStruct((m, group), BF16),
        scratch_shapes=[pltpu.VMEM((hb, HEAD_DIM, HEAD_DIM), F32)],
        compiler_params=_params("parallel", "parallel", "arbitrary"),
        name="hgrn2",
    )(proj, proj, proj, proj, lb.reshape(1, group).astype(F32), gnorm.reshape(1, HEAD_DIM).astype(F32), cm3, mk)


_CONV_PAD = 8


def _mlstm_constants():
    idx = np.arange(CHUNK)
    tri = (idx[None, :] <= idx[:, None]).astype(np.float32)
    zero = np.zeros_like(tri)
    tri3 = np.concatenate([tri, tri, tri, zero], axis=1)
    trit3 = np.concatenate([tri.T, tri.T, tri.T, zero], axis=0)
    trit3 = np.concatenate([trit3, np.zeros_like(trit3)], axis=1)
    return jnp.asarray(tri3, dtype=BF16), jnp.asarray(trit3, dtype=BF16)


def _mlstm_kernel(qk_ref, v_ref, og_ref, gc_ref, gr_ref, cw_ref, bc_ref, br_ref, hn_ref, tri_ref, trit_ref,
                  o_ref, xp_ref, qk_s, c_ref, m_ref, *, nchunk, nh, dqk, dv, t, g_off):
    @pl.when(pl.program_id(1) == 0)
    def _():
        xp_ref[0:_CONV_PAD, :] = jnp.zeros((_CONV_PAD, xp_ref.shape[1]), F32)
        c_ref[...] = jnp.zeros_like(c_ref)
        m_ref[...] = jnp.zeros_like(m_ref)

    xp_ref[_CONV_PAD:_CONV_PAD + t, :] = qk_ref[...]
    conv = jnp.zeros((t, 2 * nh * dqk), F32)
    for j in range(CONV_W):
        start = _CONV_PAD - (CONV_W - 1) + j
        conv = conv + cw_ref[j:j + 1, :] * xp_ref[start:start + t, :]
    qk_s[...] = conv * _sigmoid(conv)
    xp_ref[0:_CONV_PAD, :] = xp_ref[t:t + _CONV_PAD, :]

    hn = hn_ref[...]
    kscale = dqk ** -0.5
    r_i = lax.broadcasted_iota(jnp.int32, (CHUNK, CHUNK), 0)
    c_i = lax.broadcasted_iota(jnp.int32, (CHUNK, CHUNK), 1)
    causal = c_i <= r_i
    ones = jnp.ones((CHUNK, dv), BF16)
    zpad_c = jnp.zeros((CHUNK, LANE), BF16)
    zpad_r = jnp.zeros((2 * nh, CHUNK), BF16)

    def body(c, carry):
        base = pl.multiple_of(c * CHUNK, CHUNK)
        rows = pl.ds(base, CHUNK)
        qk = qk_s[rows, :]
        gcol = gc_ref[rows, :] + bc_ref[...]
        b_c = jnp.dot(tri_ref[...], jnp.concatenate(_split3(_log_sigmoid(gcol)) + (zpad_c,), axis=0),
                      preferred_element_type=F32)
        grow = gr_ref[c] + br_ref[...]
        ig_r = grow[0:nh, :]
        b_r = jnp.dot(jnp.concatenate(_split3(_log_sigmoid(grow)) + (zpad_r,), axis=1), trit_ref[...],
                      preferred_element_type=F32)[nh:2 * nh, 0:CHUNK]
        new_state = []
        for h in range(nh):
            ig_col = gcol[:, g_off + h:g_off + h + 1]
            q_h = qk[:, h * dqk:(h + 1) * dqk].astype(BF16)
            k_f = qk[:, (nh + h) * dqk:(nh + h + 1) * dqk] * kscale
            k_h = k_f.astype(BF16)
            v_ext = jnp.concatenate([v_ref[rows, h * dv:(h + 1) * dv].astype(BF16), ones], axis=1)
            bcol = b_c[:, g_off + nh + h:g_off + nh + h + 1]
            c_ext, m_prev = carry[h]
            log_d = jnp.where(causal, bcol - b_r[h:h + 1, :] + ig_r[h:h + 1, :], NEG)
            log_inter = bcol + m_prev
            m_t = jnp.maximum(jnp.max(log_d, axis=1, keepdims=True), log_inter)
            qk_w = (lax.dot_general(q_h, k_h, NT_DIMS, preferred_element_type=F32)
                    * jnp.exp(log_d - m_t))
            w_inter = jnp.exp(log_inter - m_t)
            numden = (jnp.dot(qk_w.astype(BF16), v_ext, preferred_element_type=F32)
                      + w_inter * jnp.dot(q_h, c_ext.astype(BF16), preferred_element_type=F32))
            num = numden[:, 0:dv]
            den = numden[:, dv:2 * dv]
            hh = num / jnp.maximum(jnp.abs(den), jnp.exp(-m_t))
            m_new = m_t[CHUNK - 1:CHUNK, :]
            b_last = bcol[CHUNK - 1:CHUNK, :]
            w_state = jnp.exp(b_last - bcol + ig_col - m_new)
            decay = jnp.exp(b_last + m_prev - m_new)
            new_state.append((decay * c_ext + lax.dot_general((k_f * w_state).astype(BF16), v_ext, TN_DIMS,
                                                              preferred_element_type=F32), m_new))
            ms = jnp.mean(hh * hh, axis=1, keepdims=True)
            gate = _sigmoid(og_ref[rows, h * dv:(h + 1) * dv])
            o_ref[rows, h * dv:(h + 1) * dv] = (hh * lax.rsqrt(ms + EPS) * hn * gate).astype(o_ref.dtype)
        return tuple(new_state)

    state = lax.fori_loop(0, nchunk, body, tuple((c_ref[h], m_ref[h:h + 1, 0:1]) for h in range(nh)), unroll=True)
    for h in range(nh):
        c_ref[h] = state[h][0]
        m_ref[h:h + 1, :] = jnp.broadcast_to(state[h][1], (1, LANE))


def _mlstm(proj, tail, conv_w, gate_bias, hnorm, *, batch, seq, group, g_off):
    m = batch * seq
    nh = group // HEAD_DIM
    dqk = HEAD_DIM // 2
    dv = HEAD_DIM
    t = _pick_tile(seq, (256, 128, 64))
    nt = seq // t
    tri3, trit3 = _mlstm_constants()
    qk_blk = 4
    gates_row = (tail[:, g_off:g_off + 2 * nh].reshape(batch, seq // CHUNK, CHUNK, 2 * nh)
                 .transpose(0, 1, 3, 2))
    bias = gate_bias.astype(F32)
    bias_lane = jnp.zeros((1, LANE), F32).at[0, g_off:g_off + 2 * nh].set(bias)

    def wide(off):
        return pl.BlockSpec((t, group), lambda b, i: (b * nt + i, off))

    def const(shape):
        return pl.BlockSpec(shape, lambda b, i: tuple(0 for _ in shape))

    return pl.pallas_call(
        functools.partial(_mlstm_kernel, nchunk=t // CHUNK, nh=nh, dqk=dqk, dv=dv, t=t, g_off=g_off),
        grid=(batch, nt),
        in_specs=[wide(qk_blk), wide(qk_blk + 1), wide(qk_blk + 2),
                  pl.BlockSpec((t, LANE), lambda b, i: (b * nt + i, 0)),
                  pl.BlockSpec((None, t // CHUNK, 2 * nh, CHUNK), lambda b, i: (b, i, 0, 0)),
                  const((CONV_W, group)), const((1, LANE)), const((2 * nh, 1)), const((1, dv)),
                  const(tri3.shape), const(trit3.shape)],
        out_specs=pl.BlockSpec((t, group), lambda b, i: (b * nt + i, 0)),
        out_shape=jax.ShapeDtypeStruct((m, group), BF16),
        scratch_shapes=[pltpu.VMEM((t + _CONV_PAD, group), F32),
                        pltpu.VMEM((t, group), F32),
                        pltpu.VMEM((nh, dqk, 2 * dv), F32),
                        pltpu.VMEM((nh, LANE), F32)],
        compiler_params=_params("parallel", "arbitrary"),
        name="mlstm",
    )(proj, proj, proj, tail, gates_row, conv_w.astype(F32), bias_lane, bias.reshape(2 * nh, 1),
      hnorm.reshape(1, dv).astype(F32), tri3, trit3)


def _ki_kernel(x_ref, g_ref, o_ref):
    x = x_ref[:, 0:IDX_DIM]
    ms = jnp.mean(x * x, axis=-1, keepdims=True)
    o_ref[...] = (x * lax.rsqrt(ms + EPS) * g_ref[...]).astype(o_ref.dtype)


def _ki_norm(tail, g):
    m = tail.shape[0]
    tm = _pick_tile(m, (512, 256, 128, 64))
    return pl.pallas_call(
        _ki_kernel,
        grid=(m // tm,),
        in_specs=[pl.BlockSpec((tm, LANE), lambda i: (i, 0)), pl.BlockSpec((1, IDX_DIM), lambda i: (0, 0))],
        out_specs=pl.BlockSpec((tm, IDX_DIM), lambda i: (i, 0)),
        out_shape=jax.ShapeDtypeStruct((m, IDX_DIM), BF16),
        compiler_params=_params("parallel"),
        name="ki_norm",
    )(tail, g.reshape(1, IDX_DIM).astype(F32))


def _diff_kernel(q_ref, k_ref, vt_ref, lam_ref, sn_ref, o_ref, m_ref, l_ref, acc_ref, *, tq, tk, dqk, out_scale):
    qi = pl.program_id(2)
    last_k = ((qi + 1) * tq - 1) // tk
    m_ref[...] = jnp.full_like(m_ref, NEG)
    l_ref[...] = jnp.zeros_like(l_ref)
    acc_ref[...] = jnp.zeros_like(acc_ref)
    q = q_ref[...]

    def step(kb, masked):
        rows = pl.ds(pl.multiple_of(kb * tk, tk), tk)
        k = k_ref[rows, :]
        vt = vt_ref[kb]
        if masked:
            k_chunk = (kb * tk + lax.broadcasted_iota(jnp.int32, (tk, tq), 0)) // CHUNK
            q_chunk = (qi * tq + lax.broadcasted_iota(jnp.int32, (tk, tq), 1)) // CHUNK
            mask = k_chunk <= q_chunk
        s_pair = [lax.dot_general(k[:, half * dqk:(half + 1) * dqk], q[:, half * dqk:(half + 1) * dqk], NT_DIMS,
                                  preferred_element_type=F32) for half in range(2)]
        probs = []
        for half in range(2):
            s = s_pair[half]
            if masked:
                s = jnp.where(mask, s, NEG)
            m_prev = m_ref[half]
            m_new = jnp.maximum(m_prev, jnp.max(s, axis=0, keepdims=True))
            p = jnp.exp2(s - m_new)
            alpha = jnp.exp2(m_prev - m_new)
            l_ref[half] = alpha * l_ref[half] + jnp.sum(p, axis=0, keepdims=True)
            m_ref[half] = m_new
            probs.append((alpha, p.astype(BF16)))
        for half in range(2):
            alpha, p = probs[half]
            acc_ref[half] = alpha * acc_ref[half] + jnp.dot(vt, p, preferred_element_type=F32)

    def body(kb, carry):
        step(kb, False)
        return carry

    lax.fori_loop(0, last_k, body, 0)
    step(last_k, True)

    o = acc_ref[0] / l_ref[0] - lam_ref[...] * (acc_ref[1] / l_ref[1])
    ms = jnp.mean(o * o, axis=0, keepdims=True)
    o_ref[...] = (o * lax.rsqrt(ms + EPS) * (sn_ref[...] * out_scale)).T.astype(o_ref.dtype)


def _diff_attention(cd, lam_full, subln, lambda_init, *, batch, seq, group):
    m = batch * seq
    dqk = HEAD_DIM
    dv = 2 * HEAD_DIM
    nh = group // dv
    tq = _pick_tile(seq, (1024, 512, 256, 128))
    tk = _pick_tile(seq, (1024, 512, 256, 128))
    assert tk % tq == 0
    nq, nkb = seq // tq, seq // tk
    k_off = nh
    v_t = cd[:, 2 * group:3 * group].reshape(batch, nkb, tk, nh, dv).transpose(0, 1, 3, 4, 2)

    return pl.pallas_call(
        functools.partial(_diff_kernel, tq=tq, tk=tk, dqk=dqk, out_scale=1.0 - lambda_init),
        grid=(batch, nh, nq),
        in_specs=[pl.BlockSpec((tq, dv), lambda b, h, i: (b * nq + i, h)),
                  pl.BlockSpec((seq, dv), lambda b, h, i: (b, k_off + h)),
                  pl.BlockSpec((None, nkb, None, dv, tk), lambda b, h, i: (b, 0, h, 0, 0)),
                  pl.BlockSpec((1, 1), lambda b, h, i: (0, 0)),
                  pl.BlockSpec((dv, 1), lambda b, h, i: (0, 0))],
        out_specs=pl.BlockSpec((tq, dv), lambda b, h, i: (b * nq + i, h)),
        out_shape=jax.ShapeDtypeStruct((m, group), BF16),
        scratch_shapes=[pltpu.VMEM((2, 1, tq), F32), pltpu.VMEM((2, 1, tq), F32), pltpu.VMEM((2, dv, tq), F32)],
        compiler_params=_params("parallel", "parallel", "arbitrary"),
        name="diff_attn",
    )(cd, cd, v_t, lam_full.reshape(1, 1).astype(F32), subln.reshape(dv, 1).astype(F32))


_INT_MIN = -2 ** 31
_HI16 = -(1 << 16)


_CNT_ROWS = 64
_DSA_HEADS_PER_STEP = 2


def _dsa_kernel(q_ref, k_ref, vt_ref, qi_ref, ki_ref, wt_ref, o_ref, sc_ref, sc16_ref, *, tq, tk, topk, hpb,
                w_scale):
    qb = pl.program_id(1)
    h = pl.program_id(2)
    nkv = ((qb + 1) * tq + tk - 1) // tk

    @pl.when(h == 0)
    def _():
        w = wt_ref[...] * w_scale
        q_chunk = (qb * tq + lax.broadcasted_iota(jnp.int32, (tk, tq), 1)) // CHUNK

        def score_body(kb, carry):
            rows = pl.ds(pl.multiple_of(kb * tk, tk), tk)
            ki = ki_ref[rows, :]
            acc = jnp.zeros((tk, tq), F32)
            for ih in range(IDX_HEADS):
                lg = lax.dot_general(ki, qi_ref[:, ih * IDX_DIM:(ih + 1) * IDX_DIM], NT_DIMS,
                                     preferred_element_type=F32)
                acc = acc + w[ih:ih + 1, :] * jnp.maximum(lg, 0.0)
            k_chunk = (kb * tk + lax.broadcasted_iota(jnp.int32, (tk, tq), 0)) // CHUNK
            x = jnp.where(k_chunk <= q_chunk, acc, NEG)
            sc_ref[kb] = x
            hi = lax.bitcast_convert_type(x, jnp.int32) & _HI16
            sc16_ref[kb] = lax.bitcast_convert_type(hi, F32).astype(BF16)
            return carry

        lax.fori_loop(0, nkv, score_body, 0)

        def count_ge(cf):
            def cnt_body(kb, cnt):
                for r in range(tk // _CNT_ROWS):
                    hit = jnp.where(sc_ref[kb, r * _CNT_ROWS:(r + 1) * _CNT_ROWS, :] >= cf, 1.0, 0.0)
                    for g in range(_CNT_ROWS // 8):
                        cnt = cnt + hit[g * 8:(g + 1) * 8, :]
                return cnt

            return jnp.sum(lax.fori_loop(0, nkv, cnt_body, jnp.zeros((8, tq), F32)), axis=0, keepdims=True)

        def count_ge_hi(cfb):
            one, zero = jnp.ones((), BF16), jnp.zeros((), BF16)

            def cnt_body(kb, cnt):
                part = jnp.zeros((16, tq), BF16)
                for r in range(tk // _CNT_ROWS):
                    hit = jnp.where(sc16_ref[kb, r * _CNT_ROWS:(r + 1) * _CNT_ROWS, :] >= cfb, one, zero)
                    for g in range(_CNT_ROWS // 16):
                        part = part + hit[g * 16:(g + 1) * 16, :]
                return cnt + part.astype(F32)

            return jnp.sum(lax.fori_loop(0, nkv, cnt_body, jnp.zeros((16, tq), F32)), axis=0, keepdims=True)

        def bit_body(state, *, coarse):
            i, u, kept = state
            cand = u | jnp.left_shift(jnp.int32(1), 31 - i)
            key = cand ^ _INT_MIN
            if coarse:
                key = jnp.where(key < 0, key | 0xFFFF, key)
            cf = lax.bitcast_convert_type(key ^ (jnp.right_shift(key, 31) & 0x7FFFFFFF), F32)
            cnt = count_ge_hi(cf.astype(BF16)) if coarse else count_ge(cf)
            ok = cnt >= float(topk)
            return i + 1, jnp.where(ok, cand, u), jnp.where(ok, cnt, kept)

        def unsettled(state, *, last):
            i, _, kept = state
            return (i < last) & (jnp.max(jnp.where(kept == float(topk), 0.0, 1.0)) > 0.0)

        in_range = jnp.zeros((1, tq), F32) + (nkv * tk).astype(F32)
        state = (jnp.int32(0), jnp.zeros((1, tq), jnp.int32), in_range)
        state = lax.while_loop(functools.partial(unsettled, last=16), functools.partial(bit_body, coarse=True), state)
        state = lax.while_loop(functools.partial(unsettled, last=32), functools.partial(bit_body, coarse=False),
                               (jnp.maximum(state[0], 16),) + state[1:])
        key = state[1] ^ _INT_MIN
        thr = lax.bitcast_convert_type(key ^ (jnp.right_shift(key, 31) & 0x7FFFFFFF), F32)

        tied = (state[2] > float(topk)) & (thr > 0.5 * NEG)
        has_tie = jnp.max(jnp.where(tied, 1.0, 0.0)) > 0.0

        @pl.when(jnp.logical_not(has_tie))
        def _():
            def bias_body(kb, carry):
                s = sc_ref[kb]
                sc_ref[kb] = jnp.where((s >= thr) & (s > 0.5 * NEG), 0.0, NEG)
                return carry

            lax.fori_loop(0, nkv, bias_body, 0)

        @pl.when(has_tie)
        def _():
            def gt_body(kb, cnt):
                for r in range(tk // _CNT_ROWS):
                    hit = jnp.where(sc_ref[kb, r * _CNT_ROWS:(r + 1) * _CNT_ROWS, :] > thr, 1.0, 0.0)
                    for g in range(_CNT_ROWS // 8):
                        cnt = cnt + hit[g * 8:(g + 1) * 8, :]
                return cnt

            above = jnp.sum(lax.fori_loop(0, nkv, gt_body, jnp.zeros((8, tq), F32)), axis=0, keepdims=True)
            room = float(topk) - above
            lower = (lax.broadcasted_iota(jnp.int32, (LANE, LANE), 1)
                     <= lax.broadcasted_iota(jnp.int32, (LANE, LANE), 0)).astype(BF16)

            def tie_body(kb, seen):
                for r in range(tk // LANE):
                    strip = slice(r * LANE, (r + 1) * LANE)
                    s = sc_ref[kb, strip, :]
                    eq = s == thr
                    rank = seen + jnp.dot(lower, jnp.where(eq, 1.0, 0.0).astype(BF16), preferred_element_type=F32)
                    keep = ((s > thr) | (eq & (rank <= room))) & (s > 0.5 * NEG)
                    sc_ref[kb, strip, :] = jnp.where(keep, 0.0, NEG)
                    seen = rank[LANE - 1:LANE, :]
                return seen

            lax.fori_loop(0, nkv, tie_body, jnp.zeros((1, tq), F32))

    q = q_ref[...]

    def kv_body(kb, carry):
        rows = pl.ds(pl.multiple_of(kb * tk, tk), tk)
        k = k_ref[rows, :]
        bias = sc_ref[kb]
        qk = [lax.dot_general(k[:, a * HEAD_DIM:(a + 1) * HEAD_DIM], q[:, a * HEAD_DIM:(a + 1) * HEAD_DIM], NT_DIMS,
                              preferred_element_type=F32) for a in range(hpb)]
        out = []
        for a in range(hpb):
            m_prev, l_prev, acc = carry[a]
            s = qk[a] + bias
            m_new = jnp.maximum(m_prev, jnp.max(s, axis=0, keepdims=True))
            p = jnp.exp2(s - m_new)
            alpha = jnp.exp2(m_prev - m_new)
            l_new = alpha * l_prev + jnp.sum(p, axis=0, keepdims=True)
            acc = alpha * acc + jnp.dot(vt_ref[kb, a], p.astype(BF16), preferred_element_type=F32)
            out.append((m_new, l_new, acc))
        return tuple(out)

    init = tuple((jnp.full((1, tq), NEG, F32), jnp.zeros((1, tq), F32), jnp.zeros((HEAD_DIM, tq), F32))
                 for _ in range(hpb))
    fin = lax.fori_loop(0, nkv, kv_body, init)
    for a in range(hpb):
        _, l_fin, acc = fin[a]
        o_ref[:, a * HEAD_DIM:(a + 1) * HEAD_DIM] = (acc / l_fin).T.astype(o_ref.dtype)


def _dsa_attention(cd, ki_n, tail, *, batch, seq, group, w_off):
    m = batch * seq
    nh = group // HEAD_DIM
    tq = _pick_tile(seq, (512, 256, 128))
    tk = _pick_tile(seq, (1024, 512, 256))
    topk = min(TOPK_MAX, seq // 4)
    assert tk >= topk, "the threshold search needs at least topk keys in range"
    nq = seq // tq
    nc = group // LANE
    nkb = seq // tk
    q_off, k_off, v_off = 3 * nc, 4 * nc, 5 * nc
    qi_off = 6 * nc * LANE // (IDX_HEADS * IDX_DIM)
    assert qi_off * IDX_HEADS * IDX_DIM == 6 * nc * LANE
    v_t = (cd[:, v_off * LANE:(v_off + nc) * LANE].reshape(batch, nkb, tk, nh, HEAD_DIM)
           .transpose(0, 1, 3, 4, 2))
    w_t = tail[:, w_off:w_off + IDX_HEADS].reshape(batch, seq, IDX_HEADS).transpose(0, 2, 1)

    hpb = _DSA_HEADS_PER_STEP
    assert nh % hpb == 0
    wide = hpb * HEAD_DIM
    return pl.pallas_call(
        functools.partial(_dsa_kernel, tq=tq, tk=tk, topk=topk, hpb=hpb,
                          w_scale=IDX_HEADS ** -0.5 * IDX_DIM ** -0.5),
        grid=(batch, nq, nh // hpb),
        in_specs=[pl.BlockSpec((tq, wide), lambda b, i, h: (b * nq + i, q_off // hpb + h)),
                  pl.BlockSpec((seq, wide), lambda b, i, h: (b, k_off // hpb + h)),
                  pl.BlockSpec((None, nkb, hpb, HEAD_DIM, tk), lambda b, i, h: (b, 0, h, 0, 0)),
                  pl.BlockSpec((tq, IDX_HEADS * IDX_DIM), lambda b, i, h: (b * nq + i, qi_off)),
                  pl.BlockSpec((seq, IDX_DIM), lambda b, i, h: (b, 0)),
                  pl.BlockSpec((None, IDX_HEADS, tq), lambda b, i, h: (b, 0, i))],
        out_specs=pl.BlockSpec((tq, wide), lambda b, i, h: (b * nq + i, h)),
        out_shape=jax.ShapeDtypeStruct((m, group), BF16),
        scratch_shapes=[pltpu.VMEM((nkb, tk, tq), F32), pltpu.VMEM((nkb, tk, tq), BF16)],
        compiler_params=_params("parallel", "arbitrary", "arbitrary"),
        name="dsa",
    )(cd, cd, v_t, cd, ki_n, w_t)


def _repack_w_in(w, group, nhb):
    p0 = 7 * group
    p1 = p0 + 2 * nhb
    p2 = p1 + 6 * group + IDX_HEADS * IDX_DIM
    used = IDX_DIM + 2 * nhb + IDX_HEADS
    wt = jnp.swapaxes(w, 1, 2)
    pad = jnp.zeros((w.shape[0], LANE - used, w.shape[1]), w.dtype)
    return jnp.concatenate([wt[:, :p0], wt[:, p1:p2], wt[:, p2:p2 + IDX_DIM], wt[:, p0:p1],
                            wt[:, p2 + IDX_DIM:], pad], axis=1).astype(BF16)


def kernel(x, norm_mix, w_in, hgrn_lb_logits, hgrn_gnorm, mlstm_conv, mlstm_gate_bias, mlstm_hnorm,
           diff_qk_norm, diff_lambda, diff_subln, dsa_qk_norm, dsa_idx_knorm, w_out, norm_mlp, w1, w2):
    batch, seq, d_model = x.shape
    depth = w_in.shape[0]
    group = d_model // 4
    nhb = group // HEAD_DIM
    nc = group // LANE
    m = batch * seq
    dims = dict(batch=batch, seq=seq, group=group)

    p_lb = jax.nn.softmax(hgrn_lb_logits.astype(F32), axis=0)
    lower_bounds = jnp.cumsum(p_lb, axis=0) - p_lb[0:1]

    w_in_b = _repack_w_in(w_in, group, nhb)
    w_out_b, w1_b, w2_b = (w.astype(BF16) for w in (w_out, w1, w2))
    n_ab, n_cd = 7 * group, 6 * group + IDX_HEADS * IDX_DIM

    x2 = x.reshape(m, d_model).astype(F32)
    ones_g = jnp.ones((group,), F32)
    for l in range(depth):
        lambda_init = 0.8 - 0.6 * math.exp(-0.3 * l)
        h = _rmsnorm_rows(x2, norm_mix[l])
        proj = _matmul(h, w_in_b, layer=l, cols=(0, n_ab), b_rows=True, out_dtype=F32, name="in_proj_ab")
        tail = _matmul(h, w_in_b, layer=l, cols=(n_ab + n_cd, LANE), b_rows=True, out_dtype=F32,
                       name="in_proj_tail")

        y_a = _hgrn(proj, lower_bounds[l], hgrn_gnorm[l], **dims)

        y_b = _mlstm(proj, tail, mlstm_conv[l], mlstm_gate_bias[l], mlstm_hnorm[l], g_off=IDX_DIM, **dims)

        cqn = diff_qk_norm[l].astype(F32)
        dqn = dsa_qk_norm[l].astype(F32)
        q_scale = HEAD_DIM ** -0.5 * LOG2_E
        gains = jnp.concatenate([
            jnp.tile(cqn[0], nc) * q_scale, jnp.tile(cqn[1], nc), ones_g,
            jnp.tile(dqn[0], nc) * q_scale, jnp.tile(dqn[1], nc), ones_g,
            jnp.ones((IDX_HEADS * IDX_DIM,), F32)]).reshape(1, -1)
        use_norm = jnp.concatenate([
            jnp.ones((2 * group,), F32), jnp.zeros((group,), F32),
            jnp.ones((2 * group,), F32), jnp.zeros((group + IDX_HEADS * IDX_DIM,), F32)]).reshape(1, -1)
        cd = _matmul(h, w_in_b, layer=l, cols=(n_ab, n_cd), b_rows=True, out_dtype=BF16, act="groupnorm",
                     extra=(gains, use_norm), name="in_proj_cd")
        ki_n = _ki_norm(tail, dsa_idx_knorm[l])

        lam = diff_lambda[l].astype(F32)
        lam_full = jnp.exp(jnp.sum(lam[0] * lam[1])) - jnp.exp(jnp.sum(lam[2] * lam[3])) + lambda_init
        y_c = _diff_attention(cd, lam_full, diff_subln[l], lambda_init, **dims)
        y_d = _dsa_attention(cd, ki_n, tail, w_off=IDX_DIM + 2 * nhb, **dims)

        x2 = _matmul((y_a, y_b, y_c, y_d), w_out_b, layer=l, out_dtype=F32, act="residual", extra=(x2,),
                     name="out_proj")
        h = _rmsnorm_rows(x2, norm_mlp[l])
        hidden = _matmul(h, w1_b, layer=l, out_dtype=BF16, act="relu2", name="mlp_up")
        x2 = _matmul(hidden, w2_b, layer=l, out_dtype=F32, act="residual", extra=(x2,), tk_max=2048,
                     name="mlp_down")
    return x2.reshape(batch, seq, d_model).astype(x.dtype)
```

```python
import functools
import math

import numpy as np
import jax
import jax.numpy as jnp
from jax import lax
from jax.experimental import pallas as pl
from jax.experimental.pallas import tpu as pltpu

CHUNK = 64
EPS = 1e-6
NEG = -1e30
F_FLOOR = 1e-20
HEAD_DIM = 128
CONV_W = 4
IDX_HEADS = 16
IDX_DIM = 64
TOPK_MAX = 256
LOG2_E = math.log2(math.e)
LANE = 128
VMEM_LIMIT = 56 * 1024 * 1024

F32 = jnp.float32
BF16 = jnp.bfloat16
NT_DIMS = (((1,), (1,)), ((), ()))
TN_DIMS = (((0,), (0,)), ((), ()))


def _params(*sem):
    return pltpu.CompilerParams(dimension_semantics=sem, vmem_limit_bytes=VMEM_LIMIT)


def _sigmoid(x):
    return 1.0 / (1.0 + jnp.exp(-x))


def _log_sigmoid(x):
    return jnp.minimum(x, 0.0) - jnp.log1p(jnp.exp(-jnp.abs(x)))


def _split3(x):
    hi = x.astype(BF16)
    r1 = x - hi.astype(F32)
    mid = r1.astype(BF16)
    lo = (r1 - mid.astype(F32)).astype(BF16)
    return hi, mid, lo


def _pick_tile(n, candidates):
    for c in candidates:
        if n % c == 0:
            return c
    raise ValueError(f"no tile in {candidates} divides {n}")


def _rmsnorm_kernel(x_ref, g_ref, o_ref):
    x = x_ref[...]
    ms = jnp.mean(x * x, axis=-1, keepdims=True)
    o_ref[...] = (x * lax.rsqrt(ms + EPS) * g_ref[...]).astype(o_ref.dtype)


def _rmsnorm_rows(x2, g):
    m, d = x2.shape
    tm = _pick_tile(m, (256, 128, 64, 8))
    return pl.pallas_call(
        _rmsnorm_kernel,
        grid=(m // tm,),
        in_specs=[pl.BlockSpec((tm, d), lambda i: (i, 0)), pl.BlockSpec((1, d), lambda i: (0, 0))],
        out_specs=pl.BlockSpec((tm, d), lambda i: (i, 0)),
        out_shape=jax.ShapeDtypeStruct((m, d), BF16),
        compiler_params=_params("parallel"),
        name="rmsnorm",
    )(x2, g.reshape(1, d).astype(F32))


MM_MAX_TK = 4096
MM_TN = 1024


def _mm_epilogue(r, act, extra):
    if act == "relu2":
        r = jnp.square(jnp.maximum(r, 0.0))
    elif act == "groupnorm":
        g_ref, u_ref = extra
        segs = []
        for c in range(r.shape[1] // LANE):
            seg = r[:, c * LANE:(c + 1) * LANE]
            ms = jnp.mean(seg * seg, axis=1, keepdims=True)
            inv = jnp.where(u_ref[:, c * LANE:(c + 1) * LANE] > 0.0, lax.rsqrt(ms + EPS), 1.0)
            segs.append(seg * inv * g_ref[:, c * LANE:(c + 1) * LANE])
        r = jnp.concatenate(segs, axis=1)
    elif act == "residual":
        r = r + extra[0][...]
    return r


def _mm_kernel(*refs, n_a, nk, act, n_extra, b_rows):
    a_refs = refs[:n_a]
    b_ref = refs[n_a]
    extra = refs[n_a + 1:n_a + 1 + n_extra]
    o_ref = refs[n_a + 1 + n_extra]

    def product():
        r, off = None, 0
        for a_ref in a_refs:
            kw = a_ref.shape[1]
            if b_rows:
                d = lax.dot_general(a_ref[...], b_ref[:, off:off + kw], NT_DIMS, preferred_element_type=F32)
            else:
                d = jnp.dot(a_ref[...], b_ref[off:off + kw, :], preferred_element_type=F32)
            r = d if r is None else r + d
            off += kw
        return r

    if nk == 1:
        o_ref[...] = _mm_epilogue(product(), act, extra).astype(o_ref.dtype)
        return
    acc_ref = refs[n_a + 2 + n_extra]
    k = pl.program_id(2)

    @pl.when(k == 0)
    def _():
        acc_ref[...] = jnp.zeros_like(acc_ref)

    acc_ref[...] += product()

    @pl.when(k == nk - 1)
    def _():
        o_ref[...] = _mm_epilogue(acc_ref[...], act, extra).astype(o_ref.dtype)


def _matmul(a, b, *, out_dtype, layer, cols=None, b_rows=False, act=None, extra=(), tm_max=1024, tn_max=MM_TN,
            tk_max=MM_MAX_TK, name="matmul"):
    a_parts = a if isinstance(a, tuple) else (a,)
    m = a_parts[0].shape[0]
    kdim = sum(p.shape[1] for p in a_parts)
    col0, n = (0, b.shape[1 if b_rows else 2]) if cols is None else cols
    tm = _pick_tile(m, tuple(t for t in (1024, 512, 256, 128, 8) if t <= tm_max))
    tn = _pick_tile(math.gcd(n, col0), tuple(t for t in (1024, 512, 256, 128) if t <= tn_max))
    j0 = col0 // tn
    tk = kdim if kdim <= tk_max else _pick_tile(kdim, tuple(t for t in (4096, 2048, 1024, 512, 256, 128)
                                                            if t <= tk_max))
    nk = kdim // tk
    assert nk == 1 or len(a_parts) == 1
    in_specs = [pl.BlockSpec((tm, tk if nk > 1 else p.shape[1]), lambda i, j, k: (i, k)) for p in a_parts]
    if b_rows:
        in_specs.append(pl.BlockSpec((None, tn, tk), lambda i, j, k: (layer, j0 + j, k)))
    else:
        in_specs.append(pl.BlockSpec((None, tk, tn), lambda i, j, k: (layer, k, j0 + j)))
    for e in extra:
        if e.shape[0] == 1:
            in_specs.append(pl.BlockSpec((1, tn), lambda i, j, k: (0, j)))
        else:
            in_specs.append(pl.BlockSpec((tm, tn), lambda i, j, k: (i, j)))
    return pl.pallas_call(
        functools.partial(_mm_kernel, n_a=len(a_parts), nk=nk, act=act, n_extra=len(extra), b_rows=b_rows),
        grid=(m // tm, n // tn, nk),
        in_specs=in_specs,
        out_specs=pl.BlockSpec((tm, tn), lambda i, j, k: (i, j)),
        out_shape=jax.ShapeDtypeStruct((m, n), out_dtype),
        scratch_shapes=[pltpu.VMEM((tm, tn), F32)] if nk > 1 else [],
        compiler_params=_params("parallel", "parallel", "arbitrary"),
        name=name,
    )(*a_parts, b, *extra)


def _vt_kernel(a_ref, w_ref, o_ref):
    o_ref[...] = lax.dot_general(w_ref[...], a_ref[...], NT_DIMS, preferred_element_type=F32).astype(o_ref.dtype)


def _project_transposed(a, w_rows, *, layer, rows, tk):
    m, kdim = a.shape
    r0, n = rows
    tn = _pick_tile(math.gcd(n, r0), (1024, 512, 256, 128))
    j0 = r0 // tn
    return pl.pallas_call(
        _vt_kernel,
        grid=(m // tk, n // tn),
        in_specs=[pl.BlockSpec((tk, kdim), lambda i, j: (i, 0)),
                  pl.BlockSpec((None, tn, kdim), lambda i, j: (layer, j0 + j, 0))],
        out_specs=pl.BlockSpec((None, tn, tk), lambda i, j: (i, j, 0)),
        out_shape=jax.ShapeDtypeStruct((m // tk, n, tk), BF16),
        compiler_params=_params("parallel", "parallel"),
        name="in_proj_vt",
    )(a, w_rows)


_HGRN_LEVELS = 6
_HGRN_HEADS_PER_STEP = 4


def _hgrn_constants():
    idx = np.arange(CHUNK)
    j = idx[None, :]
    r = idx[:, None]
    blocks = [j <= r, j > r]
    masks = []
    for lvl in range(_HGRN_LEVELS):
        half = (CHUNK // 2) >> lvl
        parent = idx // (2 * half)
        mid = parent * 2 * half + half
        is_right = (idx % (2 * half)) >= half
        right_rng = (j >= mid[:, None]) & (j <= r)
        left_rng = (j > r) & (j < mid[:, None])
        blocks.append(np.where(is_right[:, None], right_rng, left_rng))
        masks.append((parent[:, None] == parent[None, :]) & is_right[:, None] & (~is_right)[None, :])
    cm = np.concatenate(blocks, axis=0).astype(np.float32)
    cm3 = np.concatenate([cm, cm, cm, np.zeros_like(cm)], axis=1)
    mk = np.stack(masks, axis=0).astype(np.float32)
    return jnp.asarray(cm3, dtype=BF16), jnp.asarray(mk, dtype=F32)


def _hgrn_kernel(q_ref, f_ref, i_ref, g_ref, lb_ref, gn_ref, cm_ref, mk_ref, o_ref, st_ref, *, nchunk, hb, scale):
    @pl.when(pl.program_id(2) == 0)
    def _():
        st_ref[...] = jnp.zeros_like(st_ref)

    lb = lb_ref[...]
    gn = gn_ref[...]

    def body(c, carry):
        rows = pl.ds(pl.multiple_of(c * CHUNK, CHUNK), CHUNK)
        f = lb + (1.0 - lb) * _sigmoid(f_ref[rows, :])
        logf = jnp.log(jnp.maximum(f, F_FLOOR))
        kk_w = 1.0 - f
        qr = q_ref[rows, :]
        qq_w = qr * _sigmoid(qr) * scale
        v_w = i_ref[rows, :]
        gr = g_ref[rows, :]
        gate_w = gr * _sigmoid(gr)
        l3 = jnp.concatenate(_split3(logf) + (jnp.zeros(logf.shape, BF16),), axis=0)
        sums_w = jnp.dot(cm_ref[...], l3, preferred_element_type=F32)
        new_state = []
        for a in range(hb):
            hs = slice(a * HEAD_DIM, (a + 1) * HEAD_DIM)
            sums, qq, kk, v = sums_w[:, hs], qq_w[:, hs], kk_w[:, hs], v_w[:, hs]
            vb = v.astype(BF16)
            b = sums[0:CHUNK]
            b_rest = sums[CHUNK:2 * CHUNK]
            scores = jnp.zeros((CHUNK, CHUNK), F32)
            for lvl in range(_HGRN_LEVELS):
                e = jnp.exp(sums[(2 + lvl) * CHUNK:(3 + lvl) * CHUNK])
                s = lax.dot_general((qq * e).astype(BF16), (kk * e).astype(BF16), NT_DIMS,
                                    preferred_element_type=F32)
                scores = scores + s * mk_ref[lvl]
            diag = jnp.sum(qq * kk, axis=1, keepdims=True)
            st = carry[a]
            o = (jnp.dot(scores.astype(BF16), vb, preferred_element_type=F32) + diag * v
                 + lax.dot_general((qq * jnp.exp(b)).astype(BF16), st.astype(BF16), NT_DIMS,
                                   preferred_element_type=F32))
            ke = (kk * jnp.exp(b_rest)).astype(BF16)
            new_state.append(st * jnp.exp(b[CHUNK - 1:CHUNK, :])
                             + lax.dot_general(vb, ke, TN_DIMS, preferred_element_type=F32))
            ms = jnp.mean(o * o, axis=1, keepdims=True)
            o_ref[rows, hs] = (o * lax.rsqrt(ms + EPS) * gn * gate_w[:, hs]).astype(o_ref.dtype)
        return tuple(new_state)

    state = lax.fori_loop(0, nchunk, body, tuple(st_ref[a] for a in range(hb)), unroll=True)
    for a in range(hb):
        st_ref[a] = state[a]


def _hgrn(proj, lb, gnorm, *, batch, seq, group):
    m = batch * seq
    nh = group // HEAD_DIM
    t = _pick_tile(seq, (512, 256, 128, 64))
    nt = seq // t
    cm3, mk = _hgrn_constants()

    hb = min(_HGRN_HEADS_PER_STEP, nh)
    assert nh % hb == 0
    wide = hb * HEAD_DIM
    ng = nh // hb

    def col(off):
        return pl.BlockSpec((t, wide), lambda b, h, i: (b * nt + i, off + h))

    return pl.pallas_call(
        functools.partial(_hgrn_kernel, nchunk=t // CHUNK, hb=hb, scale=HEAD_DIM ** -0.5),
        grid=(batch, ng, nt),
        in_specs=[col(0), col(ng), col(2 * ng), col(3 * ng),
                  pl.BlockSpec((1, wide), lambda b, h, i: (0, h)),
                  pl.BlockSpec((1, HEAD_DIM), lambda b, h, i: (0, 0)),
                  pl.BlockSpec(cm3.shape, lambda b, h, i: (0, 0)),
                  pl.BlockSpec(mk.shape, lambda b, h, i: (0, 0, 0))],
        out_specs=pl.BlockSpec((t, wide), lambda b, h, i: (b * nt + i, h)),
        out_shape=jax.ShapeDtypeStruct((m, group), BF16),
        scratch_shapes=[pltpu.VMEM((hb, HEAD_DIM, HEAD_DIM), F32)],
        compiler_params=_params("parallel", "parallel", "arbitrary"),
        name="hgrn2",
    )(proj, proj, proj, proj, lb.reshape(1, group).astype(F32), gnorm.reshape(1, HEAD_DIM).astype(F32), cm3, mk)


_CONV_PAD = 8


def _mlstm_constants():
    idx = np.arange(CHUNK)
    tri = (idx[None, :] <= idx[:, None]).astype(np.float32)
    zero = np.zeros_like(tri)
    tri3 = np.concatenate([tri, tri, tri, zero], axis=1)
    trit3 = np.concatenate([tri.T, tri.T, tri.T, zero], axis=0)
    trit3 = np.concatenate([trit3, np.zeros_like(trit3)], axis=1)
    return jnp.asarray(tri3, dtype=BF16), jnp.asarray(trit3, dtype=BF16)


def _mlstm_kernel(qk_ref, v_ref, og_ref, gc_ref, gr_ref, cw_ref, bc_ref, br_ref, hn_ref, tri_ref, trit_ref,
                  o_ref, xp_ref, qk_s, c_ref, m_ref, *, nchunk, nh, dqk, dv, t, g_off):
    @pl.when(pl.program_id(1) == 0)
    def _():
        xp_ref[0:_CONV_PAD, :] = jnp.zeros((_CONV_PAD, xp_ref.shape[1]), F32)
        c_ref[...] = jnp.zeros_like(c_ref)
        m_ref[...] = jnp.zeros_like(m_ref)

    xp_ref[_CONV_PAD:_CONV_PAD + t, :] = qk_ref[...]
    conv = jnp.zeros((t, 2 * nh * dqk), F32)
    for j in range(CONV_W):
        start = _CONV_PAD - (CONV_W - 1) + j
        conv = conv + cw_ref[j:j + 1, :] * xp_ref[start:start + t, :]
    qk_s[...] = conv * _sigmoid(conv)
    xp_ref[0:_CONV_PAD, :] = xp_ref[t:t + _CONV_PAD, :]

    hn = hn_ref[...]
    kscale = dqk ** -0.5
    r_i = lax.broadcasted_iota(jnp.int32, (CHUNK, CHUNK), 0)
    c_i = lax.broadcasted_iota(jnp.int32, (CHUNK, CHUNK), 1)
    causal = c_i <= r_i
    ones = jnp.ones((CHUNK, dv), BF16)
    zpad_c = jnp.zeros((CHUNK, LANE), BF16)
    zpad_r = jnp.zeros((2 * nh, CHUNK), BF16)

    def body(c, carry):
        base = pl.multiple_of(c * CHUNK, CHUNK)
        rows = pl.ds(base, CHUNK)
        qk = qk_s[rows, :]
        gcol = gc_ref[rows, :] + bc_ref[...]
        b_c = jnp.dot(tri_ref[...], jnp.concatenate(_split3(_log_sigmoid(gcol)) + (zpad_c,), axis=0),
                      preferred_element_type=F32)
        grow = gr_ref[c] + br_ref[...]
        ig_r = grow[0:nh, :]
        b_r = jnp.dot(jnp.concatenate(_split3(_log_sigmoid(grow)) + (zpad_r,), axis=1), trit_ref[...],
                      preferred_element_type=F32)[nh:2 * nh, 0:CHUNK]
        new_state = []
        for h in range(nh):
            ig_col = gcol[:, g_off + h:g_off + h + 1]
            q_h = qk[:, h * dqk:(h + 1) * dqk].astype(BF16)
            k_f = qk[:, (nh + h) * dqk:(nh + h + 1) * dqk] * kscale
            k_h = k_f.astype(BF16)
            v_ext = jnp.concatenate([v_ref[rows, h * dv:(h + 1) * dv].astype(BF16), ones], axis=1)
            bcol = b_c[:, g_off + nh + h:g_off + nh + h + 1]
            c_ext, m_prev = carry[h]
            log_d = jnp.where(causal, bcol - b_r[h:h + 1, :] + ig_r[h:h + 1, :], NEG)
            log_inter = bcol + m_prev
            m_t = jnp.maximum(jnp.max(log_d, axis=1, keepdims=True), log_inter)
            qk_w = (lax.dot_general(q_h, k_h, NT_DIMS, preferred_element_type=F32)
                    * jnp.exp(log_d - m_t))
            w_inter = jnp.exp(log_inter - m_t)
            numden = (jnp.dot(qk_w.astype(BF16), v_ext, preferred_element_type=F32)
                      + w_inter * jnp.dot(q_h, c_ext.astype(BF16), preferred_element_type=F32))
            num = numden[:, 0:dv]
            den = numden[:, dv:2 * dv]
            hh = num / jnp.maximum(jnp.abs(den), jnp.exp(-m_t))
            m_new = m_t[CHUNK - 1:CHUNK, :]
            b_last = bcol[CHUNK - 1:CHUNK, :]
            w_state = jnp.exp(b_last - bcol + ig_col - m_new)
            decay = jnp.exp(b_last + m_prev - m_new)
            new_state.append((decay * c_ext + lax.dot_general((k_f * w_state).astype(BF16), v_ext, TN_DIMS,
                                                              preferred_element_type=F32), m_new))
            ms = jnp.mean(hh * hh, axis=1, keepdims=True)
            gate = _sigmoid(og_ref[rows, h * dv:(h + 1) * dv])
            o_ref[rows, h * dv:(h + 1) * dv] = (hh * lax.rsqrt(ms + EPS) * hn * gate).astype(o_ref.dtype)
        return tuple(new_state)

    state = lax.fori_loop(0, nchunk, body, tuple((c_ref[h], m_ref[h:h + 1, 0:1]) for h in range(nh)), unroll=True)
    for h in range(nh):
        c_ref[h] = state[h][0]
        m_ref[h:h + 1, :] = jnp.broadcast_to(state[h][1], (1, LANE))


def _mlstm(proj, tail, conv_w, gate_bias, hnorm, *, batch, seq, group, g_off):
    m = batch * seq
    nh = group // HEAD_DIM
    dqk = HEAD_DIM // 2
    dv = HEAD_DIM
    t = _pick_tile(seq, (256, 128, 64))
    nt = seq // t
    tri3, trit3 = _mlstm_constants()
    qk_blk = 4
    gates_row = (tail[:, g_off:g_off + 2 * nh].reshape(batch, seq // CHUNK, CHUNK, 2 * nh)
                 .transpose(0, 1, 3, 2))
    bias = gate_bias.astype(F32)
    bias_lane = jnp.zeros((1, LANE), F32).at[0, g_off:g_off + 2 * nh].set(bias)

    def wide(off):
        return pl.BlockSpec((t, group), lambda b, i: (b * nt + i, off))

    def const(shape):
        return pl.BlockSpec(shape, lambda b, i: tuple(0 for _ in shape))

    return pl.pallas_call(
        functools.partial(_mlstm_kernel, nchunk=t // CHUNK, nh=nh, dqk=dqk, dv=dv, t=t, g_off=g_off),
        grid=(batch, nt),
        in_specs=[wide(qk_blk), wide(qk_blk + 1), wide(qk_blk + 2),
                  pl.BlockSpec((t, LANE), lambda b, i: (b * nt + i, 0)),
                  pl.BlockSpec((None, t // CHUNK, 2 * nh, CHUNK), lambda b, i: (b, i, 0, 0)),
                  const((CONV_W, group)), const((1, LANE)), const((2 * nh, 1)), const((1, dv)),
                  const(tri3.shape), const(trit3.shape)],
        out_specs=pl.BlockSpec((t, group), lambda b, i: (b * nt + i, 0)),
        out_shape=jax.ShapeDtypeStruct((m, group), BF16),
        scratch_shapes=[pltpu.VMEM((t + _CONV_PAD, group), F32),
                        pltpu.VMEM((t, group), F32),
                        pltpu.VMEM((nh, dqk, 2 * dv), F32),
                        pltpu.VMEM((nh, LANE), F32)],
        compiler_params=_params("parallel", "arbitrary"),
        name="mlstm",
    )(proj, proj, proj, tail, gates_row, conv_w.astype(F32), bias_lane, bias.reshape(2 * nh, 1),
      hnorm.reshape(1, dv).astype(F32), tri3, trit3)


def _ki_kernel(x_ref, g_ref, o_ref):
    x = x_ref[:, 0:IDX_DIM]
    ms = jnp.mean(x * x, axis=-1, keepdims=True)
    o_ref[...] = (x * lax.rsqrt(ms + EPS) * g_ref[...]).astype(o_ref.dtype)


def _ki_norm(tail, g):
    m = tail.shape[0]
    tm = _pick_tile(m, (512, 256, 128, 64))
    return pl.pallas_call(
        _ki_kernel,
        grid=(m // tm,),
        in_specs=[pl.BlockSpec((tm, LANE), lambda i: (i, 0)), pl.BlockSpec((1, IDX_DIM), lambda i: (0, 0))],
        out_specs=pl.BlockSpec((tm, IDX_DIM), lambda i: (i, 0)),
        out_shape=jax.ShapeDtypeStruct((m, IDX_DIM), BF16),
        compiler_params=_params("parallel"),
        name="ki_norm",
    )(tail, g.reshape(1, IDX_DIM).astype(F32))


def _diff_kernel(q_ref, k_ref, vt_ref, lam_ref, sn_ref, o_ref, m_ref, l_ref, acc_ref, *, tq, tk, dqk, out_scale):
    qi = pl.program_id(2)
    last_k = ((qi + 1) * tq - 1) // tk
    m_ref[...] = jnp.full_like(m_ref, NEG)
    l_ref[...] = jnp.zeros_like(l_ref)
    acc_ref[...] = jnp.zeros_like(acc_ref)
    q = q_ref[...]

    def step(kb, masked):
        rows = pl.ds(pl.multiple_of(kb * tk, tk), tk)
        k = k_ref[rows, :]
        vt = vt_ref[kb]
        if masked:
            k_chunk = (kb * tk + lax.broadcasted_iota(jnp.int32, (tk, tq), 0)) // CHUNK
            q_chunk = (qi * tq + lax.broadcasted_iota(jnp.int32, (tk, tq), 1)) // CHUNK
            mask = k_chunk <= q_chunk
        s_pair = [lax.dot_general(k[:, half * dqk:(half + 1) * dqk], q[:, half * dqk:(half + 1) * dqk], NT_DIMS,
                                  preferred_element_type=F32) for half in range(2)]
        probs = []
        for half in range(2):
            s = s_pair[half]
            if masked:
                s = jnp.where(mask, s, NEG)
            m_prev = m_ref[half]
            m_new = jnp.maximum(m_prev, jnp.max(s, axis=0, keepdims=True))
            p = jnp.exp2(s - m_new)
            alpha = jnp.exp2(m_prev - m_new)
            l_ref[half] = alpha * l_ref[half] + jnp.sum(p, axis=0, keepdims=True)
            m_ref[half] = m_new
            probs.append((alpha, p.astype(BF16)))
        for half in range(2):
            alpha, p = probs[half]
            acc_ref[half] = alpha * acc_ref[half] + jnp.dot(vt, p, preferred_element_type=F32)

    def body(kb, carry):
        step(kb, False)
        return carry

    lax.fori_loop(0, last_k, body, 0)
    step(last_k, True)

    o = acc_ref[0] / l_ref[0] - lam_ref[...] * (acc_ref[1] / l_ref[1])
    ms = jnp.mean(o * o, axis=0, keepdims=True)
    o_ref[...] = (o * lax.rsqrt(ms + EPS) * (sn_ref[...] * out_scale)).T.astype(o_ref.dtype)


def _attention_key_tile(seq):
    return _pick_tile(seq, (1024, 512, 256))


def _diff_attention(cd, v_t, lam_full, subln, lambda_init, *, batch, seq, group):
    m = batch * seq
    dqk = HEAD_DIM
    dv = 2 * HEAD_DIM
    nh = group // dv
    tk = _attention_key_tile(seq)
    tq = tk
    nq, nkb = seq // tq, seq // tk
    k_off = nh

    return pl.pallas_call(
        functools.partial(_diff_kernel, tq=tq, tk=tk, dqk=dqk, out_scale=1.0 - lambda_init),
        grid=(batch, nh, nq),
        in_specs=[pl.BlockSpec((tq, dv), lambda b, h, i: (b * nq + i, h)),
                  pl.BlockSpec((seq, dv), lambda b, h, i: (b, k_off + h)),
                  pl.BlockSpec((nkb, dv, tk), lambda b, h, i: (b, h, 0)),
                  pl.BlockSpec((1, 1), lambda b, h, i: (0, 0)),
                  pl.BlockSpec((dv, 1), lambda b, h, i: (0, 0))],
        out_specs=pl.BlockSpec((tq, dv), lambda b, h, i: (b * nq + i, h)),
        out_shape=jax.ShapeDtypeStruct((m, group), BF16),
        scratch_shapes=[pltpu.VMEM((2, 1, tq), F32), pltpu.VMEM((2, 1, tq), F32), pltpu.VMEM((2, dv, tq), F32)],
        compiler_params=_params("parallel", "parallel", "arbitrary"),
        name="diff_attn",
    )(cd, cd, v_t, lam_full.reshape(1, 1).astype(F32), subln.reshape(dv, 1).astype(F32))


_INT_MIN = -2 ** 31
_HI16 = -(1 << 16)


_CNT_ROWS = 64
_DSA_HEADS_PER_STEP = 2


def _dsa_kernel(q_ref, k_ref, vt_ref, qi_ref, ki_ref, wt_ref, o_ref, sc_ref, sc16_ref, *, tq, tk, topk, hpb,
                w_scale):
    qb = pl.program_id(1)
    h = pl.program_id(2)
    nkv = ((qb + 1) * tq + tk - 1) // tk

    @pl.when(h == 0)
    def _():
        w = wt_ref[...] * w_scale
        q_chunk = (qb * tq + lax.broadcasted_iota(jnp.int32, (tk, tq), 1)) // CHUNK

        def score_body(kb, carry):
            rows = pl.ds(pl.multiple_of(kb * tk, tk), tk)
            ki = ki_ref[rows, :]
            acc = jnp.zeros((tk, tq), F32)
            for ih in range(IDX_HEADS):
                lg = lax.dot_general(ki, qi_ref[:, ih * IDX_DIM:(ih + 1) * IDX_DIM], NT_DIMS,
                                     preferred_element_type=F32)
                acc = acc + w[ih:ih + 1, :] * jnp.maximum(lg, 0.0)
            k_chunk = (kb * tk + lax.broadcasted_iota(jnp.int32, (tk, tq), 0)) // CHUNK
            x = jnp.where(k_chunk <= q_chunk, acc, NEG)
            sc_ref[kb] = x
            hi = lax.bitcast_convert_type(x, jnp.int32) & _HI16
            sc16_ref[kb] = lax.bitcast_convert_type(hi, F32).astype(BF16)
            return carry

        lax.fori_loop(0, nkv, score_body, 0)

        def count_ge(cf):
            def cnt_body(kb, cnt):
                for r in range(tk // _CNT_ROWS):
                    hit = jnp.where(sc_ref[kb, r * _CNT_ROWS:(r + 1) * _CNT_ROWS, :] >= cf, 1.0, 0.0)
                    for g in range(_CNT_ROWS // 8):
                        cnt = cnt + hit[g * 8:(g + 1) * 8, :]
                return cnt

            return jnp.sum(lax.fori_loop(0, nkv, cnt_body, jnp.zeros((8, tq), F32)), axis=0, keepdims=True)

        def count_ge_hi(cfb):
            one, zero = jnp.ones((), BF16), jnp.zeros((), BF16)

            def cnt_body(kb, cnt):
                part = jnp.zeros((16, tq), BF16)
                for r in range(tk // _CNT_ROWS):
                    hit = jnp.where(sc16_ref[kb, r * _CNT_ROWS:(r + 1) * _CNT_ROWS, :] >= cfb, one, zero)
                    for g in range(_CNT_ROWS // 16):
                        part = part + hit[g * 16:(g + 1) * 16, :]
                return cnt + part.astype(F32)

            return jnp.sum(lax.fori_loop(0, nkv, cnt_body, jnp.zeros((16, tq), F32)), axis=0, keepdims=True)

        def bit_body(state, *, coarse):
            i, u, kept = state
            cand = u | jnp.left_shift(jnp.int32(1), 31 - i)
            key = cand ^ _INT_MIN
            if coarse:
                key = jnp.where(key < 0, key | 0xFFFF, key)
            cf = lax.bitcast_convert_type(key ^ (jnp.right_shift(key, 31) & 0x7FFFFFFF), F32)
            cnt = count_ge_hi(cf.astype(BF16)) if coarse else count_ge(cf)
            ok = cnt >= float(topk)
            return i + 1, jnp.where(ok, cand, u), jnp.where(ok, cnt, kept)

        def unsettled(state, *, last):
            i, _, kept = state
            return (i < last) & (jnp.max(jnp.where(kept == float(topk), 0.0, 1.0)) > 0.0)

        in_range = jnp.zeros((1, tq), F32) + (nkv * tk).astype(F32)
        state = (jnp.int32(0), jnp.zeros((1, tq), jnp.int32), in_range)
        state = lax.while_loop(functools.partial(unsettled, last=16), functools.partial(bit_body, coarse=True), state)
        state = lax.while_loop(functools.partial(unsettled, last=32), functools.partial(bit_body, coarse=False),
                               (jnp.maximum(state[0], 16),) + state[1:])
        key = state[1] ^ _INT_MIN
        thr = lax.bitcast_convert_type(key ^ (jnp.right_shift(key, 31) & 0x7FFFFFFF), F32)

        tied = (state[2] > float(topk)) & (thr > 0.5 * NEG)
        has_tie = jnp.max(jnp.where(tied, 1.0, 0.0)) > 0.0

        @pl.when(jnp.logical_not(has_tie))
        def _():
            def bias_body(kb, carry):
                s = sc_ref[kb]
                sc_ref[kb] = jnp.where((s >= thr) & (s > 0.5 * NEG), 0.0, NEG)
                return carry

            lax.fori_loop(0, nkv, bias_body, 0)

        @pl.when(has_tie)
        def _():
            def gt_body(kb, cnt):
                for r in range(tk // _CNT_ROWS):
                    hit = jnp.where(sc_ref[kb, r * _CNT_ROWS:(r + 1) * _CNT_ROWS, :] > thr, 1.0, 0.0)
                    for g in range(_CNT_ROWS // 8):
                        cnt = cnt + hit[g * 8:(g + 1) * 8, :]
                return cnt

            above = jnp.sum(lax.fori_loop(0, nkv, gt_body, jnp.zeros((8, tq), F32)), axis=0, keepdims=True)
            room = float(topk) - above
            lower = (lax.broadcasted_iota(jnp.int32, (LANE, LANE), 1)
                     <= lax.broadcasted_iota(jnp.int32, (LANE, LANE), 0)).astype(BF16)

            def tie_body(kb, seen):
                for r in range(tk // LANE):
                    strip = slice(r * LANE, (r + 1) * LANE)
                    s = sc_ref[kb, strip, :]
                    eq = s == thr
                    rank = seen + jnp.dot(lower, jnp.where(eq, 1.0, 0.0).astype(BF16), preferred_element_type=F32)
                    keep = ((s > thr) | (eq & (rank <= room))) & (s > 0.5 * NEG)
                    sc_ref[kb, strip, :] = jnp.where(keep, 0.0, NEG)
                    seen = rank[LANE - 1:LANE, :]
                return seen

            lax.fori_loop(0, nkv, tie_body, jnp.zeros((1, tq), F32))

    q = q_ref[...]

    def kv_body(kb, carry):
        rows = pl.ds(pl.multiple_of(kb * tk, tk), tk)
        k = k_ref[rows, :]
        bias = sc_ref[kb]
        qk = [lax.dot_general(k[:, a * HEAD_DIM:(a + 1) * HEAD_DIM], q[:, a * HEAD_DIM:(a + 1) * HEAD_DIM], NT_DIMS,
                              preferred_element_type=F32) for a in range(hpb)]
        out = []
        for a in range(hpb):
            m_prev, l_prev, acc = carry[a]
            s = qk[a] + bias
            m_new = jnp.maximum(m_prev, jnp.max(s, axis=0, keepdims=True))
            p = jnp.exp2(s - m_new)
            alpha = jnp.exp2(m_prev - m_new)
            l_new = alpha * l_prev + jnp.sum(p, axis=0, keepdims=True)
            acc = alpha * acc + jnp.dot(vt_ref[kb, a * HEAD_DIM:(a + 1) * HEAD_DIM, :], p.astype(BF16),
                                        preferred_element_type=F32)
            out.append((m_new, l_new, acc))
        return tuple(out)

    init = tuple((jnp.full((1, tq), NEG, F32), jnp.zeros((1, tq), F32), jnp.zeros((HEAD_DIM, tq), F32))
                 for _ in range(hpb))
    fin = lax.fori_loop(0, nkv, kv_body, init)
    for a in range(hpb):
        _, l_fin, acc = fin[a]
        o_ref[:, a * HEAD_DIM:(a + 1) * HEAD_DIM] = (acc / l_fin).T.astype(o_ref.dtype)


def _dsa_attention(cd, v_t, ki_n, tail, *, batch, seq, group, w_off):
    m = batch * seq
    nh = group // HEAD_DIM
    tq = _pick_tile(seq, (512, 256, 128))
    tk = _attention_key_tile(seq)
    topk = min(TOPK_MAX, seq // 4)
    assert tk >= topk, "the threshold search needs at least topk keys in range"
    nq = seq // tq
    nc = group // LANE
    nkb = seq // tk
    q_off, k_off = 2 * nc, 3 * nc
    qi_off = 4 * nc * LANE // (IDX_HEADS * IDX_DIM)
    assert qi_off * IDX_HEADS * IDX_DIM == 4 * nc * LANE
    w_t = tail[:, w_off:w_off + IDX_HEADS].reshape(batch, seq, IDX_HEADS).transpose(0, 2, 1)

    hpb = _DSA_HEADS_PER_STEP
    assert nh % hpb == 0
    wide = hpb * HEAD_DIM
    return pl.pallas_call(
        functools.partial(_dsa_kernel, tq=tq, tk=tk, topk=topk, hpb=hpb,
                          w_scale=IDX_HEADS ** -0.5 * IDX_DIM ** -0.5),
        grid=(batch, nq, nh // hpb),
        in_specs=[pl.BlockSpec((tq, wide), lambda b, i, h: (b * nq + i, q_off // hpb + h)),
                  pl.BlockSpec((seq, wide), lambda b, i, h: (b, k_off // hpb + h)),
                  pl.BlockSpec((nkb, wide, tk), lambda b, i, h: (b, group // wide + h, 0)),
                  pl.BlockSpec((tq, IDX_HEADS * IDX_DIM), lambda b, i, h: (b * nq + i, qi_off)),
                  pl.BlockSpec((seq, IDX_DIM), lambda b, i, h: (b, 0)),
                  pl.BlockSpec((None, IDX_HEADS, tq), lambda b, i, h: (b, 0, i))],
        out_specs=pl.BlockSpec((tq, wide), lambda b, i, h: (b * nq + i, h)),
        out_shape=jax.ShapeDtypeStruct((m, group), BF16),
        scratch_shapes=[pltpu.VMEM((nkb, tk, tq), F32), pltpu.VMEM((nkb, tk, tq), BF16)],
        compiler_params=_params("parallel", "arbitrary", "arbitrary"),
        name="dsa",
    )(cd, cd, v_t, cd, ki_n, w_t)


def _repack_w_in(w, group, nhb):
    p0 = 7 * group
    p1 = p0 + 2 * nhb
    p2 = p1 + 6 * group + IDX_HEADS * IDX_DIM
    used = IDX_DIM + 2 * nhb + IDX_HEADS
    wt = jnp.swapaxes(w, 1, 2)
    pad = jnp.zeros((w.shape[0], LANE - used, w.shape[1]), w.dtype)

    def sec(first, count):
        return wt[:, p1 + first * group:p1 + (first + count) * group]

    return jnp.concatenate([wt[:, :p0], sec(0, 2), sec(3, 2), wt[:, p1 + 6 * group:p2], sec(2, 1), sec(5, 1),
                            wt[:, p2:p2 + IDX_DIM], wt[:, p0:p1], wt[:, p2 + IDX_DIM:], pad], axis=1).astype(BF16)


def kernel(x, norm_mix, w_in, hgrn_lb_logits, hgrn_gnorm, mlstm_conv, mlstm_gate_bias, mlstm_hnorm,
           diff_qk_norm, diff_lambda, diff_subln, dsa_qk_norm, dsa_idx_knorm, w_out, norm_mlp, w1, w2):
    batch, seq, d_model = x.shape
    depth = w_in.shape[0]
    group = d_model // 4
    nhb = group // HEAD_DIM
    nc = group // LANE
    m = batch * seq
    dims = dict(batch=batch, seq=seq, group=group)

    p_lb = jax.nn.softmax(hgrn_lb_logits.astype(F32), axis=0)
    lower_bounds = jnp.cumsum(p_lb, axis=0) - p_lb[0:1]

    w_in_b = _repack_w_in(w_in, group, nhb)
    w_out_b, w1_b, w2_b = (w.astype(BF16) for w in (w_out, w1, w2))
    n_ab, n_qk, n_v = 7 * group, 4 * group + IDX_HEADS * IDX_DIM, 2 * group
    tk_att = _attention_key_tile(seq)

    x2 = x.reshape(m, d_model).astype(F32)
    ones_g = jnp.ones((group,), F32)
    for l in range(depth):
        lambda_init = 0.8 - 0.6 * math.exp(-0.3 * l)
        h = _rmsnorm_rows(x2, norm_mix[l])
        proj = _matmul(h, w_in_b, layer=l, cols=(0, n_ab), b_rows=True, out_dtype=F32, name="in_proj_ab")
        tail = _matmul(h, w_in_b, layer=l, cols=(n_ab + n_qk + n_v, LANE), b_rows=True, out_dtype=F32,
                       name="in_proj_tail")
        v_t = _project_transposed(h, w_in_b, layer=l, rows=(n_ab + n_qk, n_v), tk=tk_att)

        y_a = _hgrn(proj, lower_bounds[l], hgrn_gnorm[l], **dims)

        y_b = _mlstm(proj, tail, mlstm_conv[l], mlstm_gate_bias[l], mlstm_hnorm[l], g_off=IDX_DIM, **dims)

        cqn = diff_qk_norm[l].astype(F32)
        dqn = dsa_qk_norm[l].astype(F32)
        q_scale = HEAD_DIM ** -0.5 * LOG2_E
        gains = jnp.concatenate([
            jnp.tile(cqn[0], nc) * q_scale, jnp.tile(cqn[1], nc),
            jnp.tile(dqn[0], nc) * q_scale, jnp.tile(dqn[1], nc),
            jnp.ones((IDX_HEADS * IDX_DIM,), F32)]).reshape(1, -1)
        use_norm = jnp.concatenate([
            jnp.ones((4 * group,), F32), jnp.zeros((IDX_HEADS * IDX_DIM,), F32)]).reshape(1, -1)
        cd = _matmul(h, w_in_b, layer=l, cols=(n_ab, n_qk), b_rows=True, out_dtype=BF16, act="groupnorm",
                     extra=(gains, use_norm), name="in_proj_cd")
        ki_n = _ki_norm(tail, dsa_idx_knorm[l])

        lam = diff_lambda[l].astype(F32)
        lam_full = jnp.exp(jnp.sum(lam[0] * lam[1])) - jnp.exp(jnp.sum(lam[2] * lam[3])) + lambda_init
        y_c = _diff_attention(cd, v_t, lam_full, diff_subln[l], lambda_init, **dims)
        y_d = _dsa_attention(cd, v_t, ki_n, tail, w_off=IDX_DIM + 2 * nhb, **dims)

        x2 = _matmul((y_a, y_b, y_c, y_d), w_out_b, layer=l, out_dtype=F32, act="residual", extra=(x2,),
                     name="out_proj")
        h = _rmsnorm_rows(x2, norm_mlp[l])
        hidden = _matmul(h, w1_b, layer=l, out_dtype=BF16, act="relu2", name="mlp_up")
        x2 = _matmul(hidden, w2_b, layer=l, out_dtype=F32, act="residual", extra=(x2,), tk_max=2048,
                     name="mlp_down")
    return x2.reshape(batch, seq, d_model).astype(x.dtype)
```

```python
import functools
import math

import numpy as np
import jax
import jax.numpy as jnp
from jax import lax
from jax.experimental import pallas as pl
from jax.experimental.pallas import tpu as pltpu

CHUNK = 64
EPS = 1e-6
NEG = -1e30
F_FLOOR = 1e-20
HEAD_DIM = 128
CONV_W = 4
IDX_HEADS = 16
IDX_DIM = 64
TOPK_MAX = 256
LOG2_E = math.log2(math.e)
LANE = 128
SUBLANE = 8
PACKED_SUBLANE = 16
VMEM_LIMIT = 56 * 1024 * 1024

F32 = jnp.float32
BF16 = jnp.bfloat16
NT_DIMS = (((1,), (1,)), ((), ()))
TN_DIMS = (((0,), (0,)), ((), ()))


def _params(*sem):
    return pltpu.CompilerParams(dimension_semantics=sem, vmem_limit_bytes=VMEM_LIMIT)


def _sigmoid(x):
    return 1.0 / (1.0 + jnp.exp(-x))


def _log_sigmoid(x):
    return jnp.minimum(x, 0.0) - jnp.log1p(jnp.exp(-jnp.abs(x)))


def _split3(x):
    hi = x.astype(BF16)
    r1 = x - hi.astype(F32)
    mid = r1.astype(BF16)
    lo = (r1 - mid.astype(F32)).astype(BF16)
    return hi, mid, lo


def _pick_tile(n, candidates):
    for c in candidates:
        if n % c == 0:
            return c
    raise ValueError(f"no tile in {candidates} divides {n}")


def _rmsnorm_kernel(x_ref, g_ref, o_ref):
    x = x_ref[...]
    ms = jnp.mean(x * x, axis=-1, keepdims=True)
    o_ref[...] = (x * lax.rsqrt(ms + EPS) * g_ref[...]).astype(o_ref.dtype)


def _rmsnorm_rows(x2, g):
    m, d = x2.shape
    tm = _pick_tile(m, (256, 128, 64, 8))
    return pl.pallas_call(
        _rmsnorm_kernel,
        grid=(m // tm,),
        in_specs=[pl.BlockSpec((tm, d), lambda i: (i, 0)), pl.BlockSpec((1, d), lambda i: (0, 0))],
        out_specs=pl.BlockSpec((tm, d), lambda i: (i, 0)),
        out_shape=jax.ShapeDtypeStruct((m, d), BF16),
        compiler_params=_params("parallel"),
        name="rmsnorm",
    )(x2, g.reshape(1, d).astype(F32))


MM_MAX_TK = 4096
MM_TN = 1024


def _mm_epilogue(r, act, extra):
    if act == "relu2":
        r = jnp.square(jnp.maximum(r, 0.0))
    elif act == "groupnorm":
        g_ref, u_ref = extra
        segs = []
        for c in range(r.shape[1] // LANE):
            seg = r[:, c * LANE:(c + 1) * LANE]
            ms = jnp.mean(seg * seg, axis=1, keepdims=True)
            inv = jnp.where(u_ref[:, c * LANE:(c + 1) * LANE] > 0.0, lax.rsqrt(ms + EPS), 1.0)
            segs.append(seg * inv * g_ref[:, c * LANE:(c + 1) * LANE])
        r = jnp.concatenate(segs, axis=1)
    elif act == "residual":
        r = r + extra[0][...]
    return r


def _mm_kernel(*refs, n_a, nk, act, n_extra, b_rows):
    a_refs = refs[:n_a]
    b_ref = refs[n_a]
    extra = refs[n_a + 1:n_a + 1 + n_extra]
    o_ref = refs[n_a + 1 + n_extra]

    def product():
        r, off = None, 0
        for a_ref in a_refs:
            kw = a_ref.shape[1]
            if b_rows:
                d = lax.dot_general(a_ref[...], b_ref[:, off:off + kw], NT_DIMS, preferred_element_type=F32)
            else:
                d = jnp.dot(a_ref[...], b_ref[off:off + kw, :], preferred_element_type=F32)
            r = d if r is None else r + d
            off += kw
        return r

    if nk == 1:
        o_ref[...] = _mm_epilogue(product(), act, extra).astype(o_ref.dtype)
        return
    acc_ref = refs[n_a + 2 + n_extra]
    k = pl.program_id(2)

    @pl.when(k == 0)
    def _():
        acc_ref[...] = jnp.zeros_like(acc_ref)

    acc_ref[...] += product()

    @pl.when(k == nk - 1)
    def _():
        o_ref[...] = _mm_epilogue(acc_ref[...], act, extra).astype(o_ref.dtype)


def _matmul(a, b, *, out_dtype, layer, cols=None, b_rows=False, act=None, extra=(), tm_max=1024, tn_max=MM_TN,
            tk_max=MM_MAX_TK, name="matmul"):
    a_parts = a if isinstance(a, tuple) else (a,)
    m = a_parts[0].shape[0]
    kdim = sum(p.shape[1] for p in a_parts)
    col0, n = (0, b.shape[1 if b_rows else 2]) if cols is None else cols
    tm = _pick_tile(m, tuple(t for t in (1024, 512, 256, 128, 8) if t <= tm_max))
    tn = _pick_tile(math.gcd(n, col0), tuple(t for t in (1024, 512, 256, 128) if t <= tn_max))
    j0 = col0 // tn
    tk = kdim if kdim <= tk_max else _pick_tile(kdim, tuple(t for t in (4096, 2048, 1024, 512, 256, 128)
                                                            if t <= tk_max))
    nk = kdim // tk
    assert nk == 1 or len(a_parts) == 1
    in_specs = [pl.BlockSpec((tm, tk if nk > 1 else p.shape[1]), lambda i, j, k: (i, k)) for p in a_parts]
    if b_rows:
        in_specs.append(pl.BlockSpec((None, tn, tk), lambda i, j, k: (layer, j0 + j, k)))
    else:
        in_specs.append(pl.BlockSpec((None, tk, tn), lambda i, j, k: (layer, k, j0 + j)))
    for e in extra:
        if e.shape[0] == 1:
            in_specs.append(pl.BlockSpec((1, tn), lambda i, j, k: (0, j)))
        else:
            in_specs.append(pl.BlockSpec((tm, tn), lambda i, j, k: (i, j)))
    return pl.pallas_call(
        functools.partial(_mm_kernel, n_a=len(a_parts), nk=nk, act=act, n_extra=len(extra), b_rows=b_rows),
        grid=(m // tm, n // tn, nk),
        in_specs=in_specs,
        out_specs=pl.BlockSpec((tm, tn), lambda i, j, k: (i, j)),
        out_shape=jax.ShapeDtypeStruct((m, n), out_dtype),
        scratch_shapes=[pltpu.VMEM((tm, tn), F32)] if nk > 1 else [],
        compiler_params=_params("parallel", "parallel", "arbitrary"),
        name=name,
    )(*a_parts, b, *extra)


def _vt_kernel(a_ref, w_ref, o_ref):
    o_ref[...] = lax.dot_general(w_ref[...], a_ref[...], NT_DIMS, preferred_element_type=F32).astype(o_ref.dtype)


def _project_transposed(a, w_rows, *, layer, rows, tk):
    m, kdim = a.shape
    r0, n = rows
    tn = _pick_tile(math.gcd(n, r0), (1024, 512, 256, 128))
    j0 = r0 // tn
    return pl.pallas_call(
        _vt_kernel,
        grid=(m // tk, n // tn),
        in_specs=[pl.BlockSpec((tk, kdim), lambda i, j: (i, 0)),
                  pl.BlockSpec((None, tn, kdim), lambda i, j: (layer, j0 + j, 0))],
        out_specs=pl.BlockSpec((None, tn, tk), lambda i, j: (i, j, 0)),
        out_shape=jax.ShapeDtypeStruct((m // tk, n, tk), BF16),
        compiler_params=_params("parallel", "parallel"),
        name="in_proj_vt",
    )(a, w_rows)


_HGRN_LEVELS = 6
_HGRN_HEADS_PER_STEP = 4


def _hgrn_constants():
    idx = np.arange(CHUNK)
    j = idx[None, :]
    r = idx[:, None]
    blocks = [j <= r, j > r]
    masks = []
    for lvl in range(_HGRN_LEVELS):
        half = (CHUNK // 2) >> lvl
        parent = idx // (2 * half)
        mid = parent * 2 * half + half
        is_right = (idx % (2 * half)) >= half
        right_rng = (j >= mid[:, None]) & (j <= r)
        left_rng = (j > r) & (j < mid[:, None])
        blocks.append(np.where(is_right[:, None], right_rng, left_rng))
        masks.append((parent[:, None] == parent[None, :]) & is_right[:, None] & (~is_right)[None, :])
    cm = np.concatenate(blocks, axis=0).astype(np.float32)
    cm3 = np.concatenate([cm, cm, cm, np.zeros_like(cm)], axis=1)
    mk = np.stack(masks, axis=0).astype(np.float32)
    return jnp.asarray(cm3, dtype=BF16), jnp.asarray(mk, dtype=F32)


def _hgrn_kernel(q_ref, f_ref, i_ref, g_ref, lb_ref, gn_ref, cm_ref, mk_ref, o_ref, st_ref, *, nchunk, hb, scale):
    @pl.when(pl.program_id(2) == 0)
    def _():
        st_ref[...] = jnp.zeros_like(st_ref)

    lb = lb_ref[...]
    gn = gn_ref[...]

    def body(c, carry):
        rows = pl.ds(pl.multiple_of(c * CHUNK, CHUNK), CHUNK)
        f = lb + (1.0 - lb) * _sigmoid(f_ref[rows, :])
        logf = jnp.log(jnp.maximum(f, F_FLOOR))
        kk_w = 1.0 - f
        qr = q_ref[rows, :]
        qq_w = qr * _sigmoid(qr) * scale
        v_w = i_ref[rows, :]
        gr = g_ref[rows, :]
        gate_w = gr * _sigmoid(gr)
        l3 = jnp.concatenate(_split3(logf) + (jnp.zeros(logf.shape, BF16),), axis=0)
        sums_w = jnp.dot(cm_ref[...], l3, preferred_element_type=F32)
        new_state = []
        for a in range(hb):
            hs = slice(a * HEAD_DIM, (a + 1) * HEAD_DIM)
            sums, qq, kk, v = sums_w[:, hs], qq_w[:, hs], kk_w[:, hs], v_w[:, hs]
            vb = v.astype(BF16)
            b = sums[0:CHUNK]
            b_rest = sums[CHUNK:2 * CHUNK]
            scores = jnp.zeros((CHUNK, CHUNK), F32)
            for lvl in range(_HGRN_LEVELS):
                e = jnp.exp(sums[(2 + lvl) * CHUNK:(3 + lvl) * CHUNK])
                s = lax.dot_general((qq * e).astype(BF16), (kk * e).astype(BF16), NT_DIMS,
                                    preferred_element_type=F32)
                scores = scores + s * mk_ref[lvl]
            diag = jnp.sum(qq * kk, axis=1, keepdims=True)
            st = carry[a]
            o = (jnp.dot(scores.astype(BF16), vb, preferred_element_type=F32) + diag * v
                 + lax.dot_general((qq * jnp.exp(b)).astype(BF16), st.astype(BF16), NT_DIMS,
                                   preferred_element_type=F32))
            ke = (kk * jnp.exp(b_rest)).astype(BF16)
            new_state.append(st * jnp.exp(b[CHUNK - 1:CHUNK, :])
                             + lax.dot_general(vb, ke, TN_DIMS, preferred_element_type=F32))
            ms = jnp.mean(o * o, axis=1, keepdims=True)
            o_ref[rows, hs] = (o * lax.rsqrt(ms + EPS) * gn * gate_w[:, hs]).astype(o_ref.dtype)
        return tuple(new_state)

    state = lax.fori_loop(0, nchunk, body, tuple(st_ref[a] for a in range(hb)), unroll=True)
    for a in range(hb):
        st_ref[a] = state[a]


def _hgrn(proj, lb, gnorm, *, batch, seq, group):
    m = batch * seq
    nh = group // HEAD_DIM
    t = _pick_tile(seq, (512, 256, 128, 64))
    nt = seq // t
    cm3, mk = _hgrn_constants()

    hb = min(_HGRN_HEADS_PER_STEP, nh)
    assert nh % hb == 0
    wide = hb * HEAD_DIM
    ng = nh // hb

    def col(off):
        return pl.BlockSpec((t, wide), lambda b, h, i: (b * nt + i, off + h))

    return pl.pallas_call(
        functools.partial(_hgrn_kernel, nchunk=t // CHUNK, hb=hb, scale=HEAD_DIM ** -0.5),
        grid=(batch, ng, nt),
        in_specs=[col(0), col(ng), col(2 * ng), col(3 * ng),
                  pl.BlockSpec((1, wide), lambda b, h, i: (0, h)),
                  pl.BlockSpec((1, HEAD_DIM), lambda b, h, i: (0, 0)),
                  pl.BlockSpec(cm3.shape, lambda b, h, i: (0, 0)),
                  pl.BlockSpec(mk.shape, lambda b, h, i: (0, 0, 0))],
        out_specs=pl.BlockSpec((t, wide), lambda b, h, i: (b * nt + i, h)),
        out_shape=jax.ShapeDtypeStruct((m, group), BF16),
        scratch_shapes=[pltpu.VMEM((hb, HEAD_DIM, HEAD_DIM), F32)],
        compiler_params=_params("parallel", "parallel", "arbitrary"),
        name="hgrn2",
    )(proj, proj, proj, proj, lb.reshape(1, group).astype(F32), gnorm.reshape(1, HEAD_DIM).astype(F32), cm3, mk)


_CONV_PAD = 8


def _mlstm_constants():
    idx = np.arange(CHUNK)
    tri = (idx[None, :] <= idx[:, None]).astype(np.float32)
    zero = np.zeros_like(tri)
    tri3 = np.concatenate([tri, tri, tri, zero], axis=1)
    trit3 = np.concatenate([tri.T, tri.T, tri.T, zero], axis=0)
    trit3 = np.concatenate([trit3, np.zeros_like(trit3)], axis=1)
    return jnp.asarray(tri3, dtype=BF16), jnp.asarray(trit3, dtype=BF16)


def _mlstm_kernel(qk_ref, v_ref, og_ref, gc_ref, gr_ref, cw_ref, bc_ref, br_ref, hn_ref, tri_ref, trit_ref,
                  o_ref, xp_ref, qk_s, c_ref, m_ref, *, nchunk, nh, dqk, dv, t, g_off):
    @pl.when(pl.program_id(1) == 0)
    def _():
        xp_ref[0:_CONV_PAD, :] = jnp.zeros((_CONV_PAD, xp_ref.shape[1]), F32)
        c_ref[...] = jnp.zeros_like(c_ref)
        m_ref[...] = jnp.zeros_like(m_ref)

    xp_ref[_CONV_PAD:_CONV_PAD + t, :] = qk_ref[...]
    conv = jnp.zeros((t, 2 * nh * dqk), F32)
    for j in range(CONV_W):
        start = _CONV_PAD - (CONV_W - 1) + j
        conv = conv + cw_ref[j:j + 1, :] * xp_ref[start:start + t, :]
    qk_s[...] = conv * _sigmoid(conv)
    xp_ref[0:_CONV_PAD, :] = xp_ref[t:t + _CONV_PAD, :]

    hn = hn_ref[...]
    kscale = dqk ** -0.5
    r_i = lax.broadcasted_iota(jnp.int32, (CHUNK, CHUNK), 0)
    c_i = lax.broadcasted_iota(jnp.int32, (CHUNK, CHUNK), 1)
    causal = c_i <= r_i
    ones = jnp.ones((CHUNK, dv), BF16)
    zpad_c = jnp.zeros((CHUNK, LANE), BF16)
    zpad_r = jnp.zeros((2 * nh, CHUNK), BF16)

    def body(c, carry):
        base = pl.multiple_of(c * CHUNK, CHUNK)
        rows = pl.ds(base, CHUNK)
        qk = qk_s[rows, :]
        gcol = gc_ref[rows, :] + bc_ref[...]
        b_c = jnp.dot(tri_ref[...], jnp.concatenate(_split3(_log_sigmoid(gcol)) + (zpad_c,), axis=0),
                      preferred_element_type=F32)
        grow = gr_ref[c] + br_ref[...]
        ig_r = grow[0:nh, :]
        b_r = jnp.dot(jnp.concatenate(_split3(_log_sigmoid(grow)) + (zpad_r,), axis=1), trit_ref[...],
                      preferred_element_type=F32)[nh:2 * nh, 0:CHUNK]
        new_state = []
        for h in range(nh):
            ig_col = gcol[:, g_off + h:g_off + h + 1]
            q_h = qk[:, h * dqk:(h + 1) * dqk].astype(BF16)
            k_f = qk[:, (nh + h) * dqk:(nh + h + 1) * dqk] * kscale
            k_h = k_f.astype(BF16)
            v_ext = jnp.concatenate([v_ref[rows, h * dv:(h + 1) * dv].astype(BF16), ones], axis=1)
            bcol = b_c[:, g_off + nh + h:g_off + nh + h + 1]
            c_ext, m_prev = carry[h]
            log_d = jnp.where(causal, bcol - b_r[h:h + 1, :] + ig_r[h:h + 1, :], NEG)
            log_inter = bcol + m_prev
            m_t = jnp.maximum(jnp.max(log_d, axis=1, keepdims=True), log_inter)
            qk_w = (lax.dot_general(q_h, k_h, NT_DIMS, preferred_element_type=F32)
                    * jnp.exp(log_d - m_t))
            w_inter = jnp.exp(log_inter - m_t)
            numden = (jnp.dot(qk_w.astype(BF16), v_ext, preferred_element_type=F32)
                      + w_inter * jnp.dot(q_h, c_ext.astype(BF16), preferred_element_type=F32))
            num = numden[:, 0:dv]
            den = numden[:, dv:2 * dv]
            hh = num / jnp.maximum(jnp.abs(den), jnp.exp(-m_t))
            m_new = m_t[CHUNK - 1:CHUNK, :]
            b_last = bcol[CHUNK - 1:CHUNK, :]
            w_state = jnp.exp(b_last - bcol + ig_col - m_new)
            decay = jnp.exp(b_last + m_prev - m_new)
            new_state.append((decay * c_ext + lax.dot_general((k_f * w_state).astype(BF16), v_ext, TN_DIMS,
                                                              preferred_element_type=F32), m_new))
            ms = jnp.mean(hh * hh, axis=1, keepdims=True)
            gate = _sigmoid(og_ref[rows, h * dv:(h + 1) * dv])
            o_ref[rows, h * dv:(h + 1) * dv] = (hh * lax.rsqrt(ms + EPS) * hn * gate).astype(o_ref.dtype)
        return tuple(new_state)

    state = lax.fori_loop(0, nchunk, body, tuple((c_ref[h], m_ref[h:h + 1, 0:1]) for h in range(nh)), unroll=True)
    for h in range(nh):
        c_ref[h] = state[h][0]
        m_ref[h:h + 1, :] = jnp.broadcast_to(state[h][1], (1, LANE))


def _mlstm(proj, tail, conv_w, gate_bias, hnorm, *, batch, seq, group, g_off):
    m = batch * seq
    nh = group // HEAD_DIM
    dqk = HEAD_DIM // 2
    dv = HEAD_DIM
    t = _pick_tile(seq, (256, 128, 64))
    nt = seq // t
    tri3, trit3 = _mlstm_constants()
    qk_blk = 4
    gates_row = (tail[:, g_off:g_off + 2 * nh].reshape(batch, seq // CHUNK, CHUNK, 2 * nh)
                 .transpose(0, 1, 3, 2))
    bias = gate_bias.astype(F32)
    bias_lane = jnp.zeros((1, LANE), F32).at[0, g_off:g_off + 2 * nh].set(bias)

    def wide(off):
        return pl.BlockSpec((t, group), lambda b, i: (b * nt + i, off))

    def const(shape):
        return pl.BlockSpec(shape, lambda b, i: tuple(0 for _ in shape))

    return pl.pallas_call(
        functools.partial(_mlstm_kernel, nchunk=t // CHUNK, nh=nh, dqk=dqk, dv=dv, t=t, g_off=g_off),
        grid=(batch, nt),
        in_specs=[wide(qk_blk), wide(qk_blk + 1), wide(qk_blk + 2),
                  pl.BlockSpec((t, LANE), lambda b, i: (b * nt + i, 0)),
                  pl.BlockSpec((None, t // CHUNK, 2 * nh, CHUNK), lambda b, i: (b, i, 0, 0)),
                  const((CONV_W, group)), const((1, LANE)), const((2 * nh, 1)), const((1, dv)),
                  const(tri3.shape), const(trit3.shape)],
        out_specs=pl.BlockSpec((t, group), lambda b, i: (b * nt + i, 0)),
        out_shape=jax.ShapeDtypeStruct((m, group), BF16),
        scratch_shapes=[pltpu.VMEM((t + _CONV_PAD, group), F32),
                        pltpu.VMEM((t, group), F32),
                        pltpu.VMEM((nh, dqk, 2 * dv), F32),
                        pltpu.VMEM((nh, LANE), F32)],
        compiler_params=_params("parallel", "arbitrary"),
        name="mlstm",
    )(proj, proj, proj, tail, gates_row, conv_w.astype(F32), bias_lane, bias.reshape(2 * nh, 1),
      hnorm.reshape(1, dv).astype(F32), tri3, trit3)


def _ki_kernel(x_ref, g_ref, o_ref):
    x = x_ref[:, 0:IDX_DIM]
    ms = jnp.mean(x * x, axis=-1, keepdims=True)
    o_ref[...] = (x * lax.rsqrt(ms + EPS) * g_ref[...]).astype(o_ref.dtype)


def _ki_norm(tail, g):
    m = tail.shape[0]
    tm = _pick_tile(m, (512, 256, 128, 64))
    return pl.pallas_call(
        _ki_kernel,
        grid=(m // tm,),
        in_specs=[pl.BlockSpec((tm, LANE), lambda i: (i, 0)), pl.BlockSpec((1, IDX_DIM), lambda i: (0, 0))],
        out_specs=pl.BlockSpec((tm, IDX_DIM), lambda i: (i, 0)),
        out_shape=jax.ShapeDtypeStruct((m, IDX_DIM), BF16),
        compiler_params=_params("parallel"),
        name="ki_norm",
    )(tail, g.reshape(1, IDX_DIM).astype(F32))


def _diff_kernel(q_ref, k_ref, vt_ref, lam_ref, sn_ref, o_ref, m_ref, l_ref, acc_ref, *, tq, tk, dqk, out_scale):
    qi = pl.program_id(2)
    last_k = ((qi + 1) * tq - 1) // tk
    m_ref[...] = jnp.full_like(m_ref, NEG)
    l_ref[...] = jnp.zeros_like(l_ref)
    acc_ref[...] = jnp.zeros_like(acc_ref)
    q = q_ref[...]

    def step(kb, masked):
        rows = pl.ds(pl.multiple_of(kb * tk, tk), tk)
        k = k_ref[rows, :]
        vt = vt_ref[kb]
        if masked:
            k_chunk = (kb * tk + lax.broadcasted_iota(jnp.int32, (tk, tq), 0)) // CHUNK
            q_chunk = (qi * tq + lax.broadcasted_iota(jnp.int32, (tk, tq), 1)) // CHUNK
            mask = k_chunk <= q_chunk
        s_pair = [lax.dot_general(k[:, half * dqk:(half + 1) * dqk], q[:, half * dqk:(half + 1) * dqk], NT_DIMS,
                                  preferred_element_type=F32) for half in range(2)]
        probs = []
        for half in range(2):
            s = s_pair[half]
            if masked:
                s = jnp.where(mask, s, NEG)
            m_prev = m_ref[half]
            m_new = jnp.maximum(m_prev, jnp.max(s, axis=0, keepdims=True))
            p = jnp.exp2(s - m_new)
            alpha = jnp.exp2(m_prev - m_new)
            l_ref[half] = alpha * l_ref[half] + jnp.sum(p, axis=0, keepdims=True)
            m_ref[half] = m_new
            probs.append((alpha, p.astype(BF16)))
        for half in range(2):
            alpha, p = probs[half]
            acc_ref[half] = alpha * acc_ref[half] + jnp.dot(vt, p, preferred_element_type=F32)

    def body(kb, carry):
        step(kb, False)
        return carry

    lax.fori_loop(0, last_k, body, 0)
    step(last_k, True)

    o = acc_ref[0] / l_ref[0] - lam_ref[...] * (acc_ref[1] / l_ref[1])
    ms = jnp.mean(o * o, axis=0, keepdims=True)
    o_ref[...] = (o * lax.rsqrt(ms + EPS) * (sn_ref[...] * out_scale)).T.astype(o_ref.dtype)


def _attention_key_tile(seq):
    return _pick_tile(seq, (1024, 512, 256))


def _diff_attention(cd, v_t, lam_full, subln, lambda_init, *, batch, seq, group):
    m = batch * seq
    dqk = HEAD_DIM
    dv = 2 * HEAD_DIM
    nh = group // dv
    tk = _attention_key_tile(seq)
    tq = tk
    nq, nkb = seq // tq, seq // tk
    k_off = nh

    return pl.pallas_call(
        functools.partial(_diff_kernel, tq=tq, tk=tk, dqk=dqk, out_scale=1.0 - lambda_init),
        grid=(batch, nh, nq),
        in_specs=[pl.BlockSpec((tq, dv), lambda b, h, i: (b * nq + i, h)),
                  pl.BlockSpec((seq, dv), lambda b, h, i: (b, k_off + h)),
                  pl.BlockSpec((nkb, dv, tk), lambda b, h, i: (b, h, 0)),
                  pl.BlockSpec((1, 1), lambda b, h, i: (0, 0)),
                  pl.BlockSpec((dv, 1), lambda b, h, i: (0, 0))],
        out_specs=pl.BlockSpec((tq, dv), lambda b, h, i: (b * nq + i, h)),
        out_shape=jax.ShapeDtypeStruct((m, group), BF16),
        scratch_shapes=[pltpu.VMEM((2, 1, tq), F32), pltpu.VMEM((2, 1, tq), F32), pltpu.VMEM((2, dv, tq), F32)],
        compiler_params=_params("parallel", "parallel", "arbitrary"),
        name="diff_attn",
    )(cd, cd, v_t, lam_full.reshape(1, 1).astype(F32), subln.reshape(dv, 1).astype(F32))


_INT_MIN = -2 ** 31
_HI16 = -(1 << 16)


_CNT_ROWS = 64
_DSA_HEADS_PER_STEP = 2


def _dsa_kernel(q_ref, k_ref, vt_ref, qi_ref, ki_ref, wt_ref, o_ref, sc_ref, sc16_ref, *, tq, tk, topk, hpb,
                w_scale):
    qb = pl.program_id(1)
    h = pl.program_id(2)
    nkv = ((qb + 1) * tq + tk - 1) // tk

    @pl.when(h == 0)
    def _():
        w = wt_ref[...] * w_scale
        q_chunk = (qb * tq + lax.broadcasted_iota(jnp.int32, (tk, tq), 1)) // CHUNK

        def score_body(kb, carry):
            rows = pl.ds(pl.multiple_of(kb * tk, tk), tk)
            ki = ki_ref[rows, :]
            acc = jnp.zeros((tk, tq), F32)
            for ih in range(IDX_HEADS):
                lg = lax.dot_general(ki, qi_ref[:, ih * IDX_DIM:(ih + 1) * IDX_DIM], NT_DIMS,
                                     preferred_element_type=F32)
                acc = acc + w[ih:ih + 1, :] * jnp.maximum(lg, 0.0)
            k_chunk = (kb * tk + lax.broadcasted_iota(jnp.int32, (tk, tq), 0)) // CHUNK
            x = jnp.where(k_chunk <= q_chunk, acc, NEG)
            sc_ref[kb] = x
            hi = lax.bitcast_convert_type(x, jnp.int32) & _HI16
            sc16_ref[kb] = lax.bitcast_convert_type(hi, F32).astype(BF16)
            return carry

        lax.fori_loop(0, nkv, score_body, 0)

        def count_ge(cf):
            def cnt_body(kb, cnt):
                for r in range(tk // _CNT_ROWS):
                    hit = jnp.where(sc_ref[kb, r * _CNT_ROWS:(r + 1) * _CNT_ROWS, :] >= cf, 1.0, 0.0)
                    for g in range(_CNT_ROWS // SUBLANE):
                        cnt = cnt + hit[g * SUBLANE:(g + 1) * SUBLANE, :]
                return cnt

            return jnp.sum(lax.fori_loop(0, nkv, cnt_body, jnp.zeros((SUBLANE, tq), F32)), axis=0, keepdims=True)

        def count_ge_hi(cfb):
            one, zero = jnp.ones((), BF16), jnp.zeros((), BF16)

            def cnt_body(kb, cnt):
                part = jnp.zeros((PACKED_SUBLANE, tq),BF16)
                for r in range(tk // _CNT_ROWS):
                    hit = jnp.where(sc16_ref[kb, r * _CNT_ROWS:(r + 1) * _CNT_ROWS, :] >= cfb, one, zero)
                    for g in range(_CNT_ROWS // PACKED_SUBLANE):
                        part = part + hit[g * PACKED_SUBLANE:(g + 1) * PACKED_SUBLANE, :]
                return cnt + part.astype(F32)

            return jnp.sum(lax.fori_loop(0, nkv, cnt_body, jnp.zeros((PACKED_SUBLANE, tq),F32)), axis=0, keepdims=True)

        def bit_body(state, *, coarse):
            i, u, kept = state
            cand = u | jnp.left_shift(jnp.int32(1), 31 - i)
            key = cand ^ _INT_MIN
            if coarse:
                key = jnp.where(key < 0, key | 0xFFFF, key)
            cf = lax.bitcast_convert_type(key ^ (jnp.right_shift(key, 31) & 0x7FFFFFFF), F32)
            cnt = count_ge_hi(cf.astype(BF16)) if coarse else count_ge(cf)
            ok = cnt >= float(topk)
            return i + 1, jnp.where(ok, cand, u), jnp.where(ok, cnt, kept)

        def unsettled(state, *, last):
            i, _, kept = state
            return (i < last) & (jnp.max(jnp.where(kept == float(topk), 0.0, 1.0)) > 0.0)

        in_range = jnp.zeros((1, tq), F32) + (nkv * tk).astype(F32)
        state = (jnp.int32(0), jnp.zeros((1, tq), jnp.int32), in_range)
        state = lax.while_loop(functools.partial(unsettled, last=16), functools.partial(bit_body, coarse=True), state)
        state = lax.while_loop(functools.partial(unsettled, last=32), functools.partial(bit_body, coarse=False),
                               (jnp.maximum(state[0], 16),) + state[1:])
        key = state[1] ^ _INT_MIN
        thr = lax.bitcast_convert_type(key ^ (jnp.right_shift(key, 31) & 0x7FFFFFFF), F32)

        tied = (state[2] > float(topk)) & (thr > 0.5 * NEG)
        has_tie = jnp.max(jnp.where(tied, 1.0, 0.0)) > 0.0

        @pl.when(jnp.logical_not(has_tie))
        def _():
            def bias_body(kb, carry):
                s = sc_ref[kb]
                sc_ref[kb] = jnp.where((s >= thr) & (s > 0.5 * NEG), 0.0, NEG)
                return carry

            lax.fori_loop(0, nkv, bias_body, 0)

        @pl.when(has_tie)
        def _():
            def gt_body(kb, cnt):
                for r in range(tk // _CNT_ROWS):
                    hit = jnp.where(sc_ref[kb, r * _CNT_ROWS:(r + 1) * _CNT_ROWS, :] > thr, 1.0, 0.0)
                    for g in range(_CNT_ROWS // SUBLANE):
                        cnt = cnt + hit[g * SUBLANE:(g + 1) * SUBLANE, :]
                return cnt

            above = jnp.sum(lax.fori_loop(0, nkv, gt_body, jnp.zeros((SUBLANE, tq), F32)), axis=0, keepdims=True)
            room = float(topk) - above
            lower = (lax.broadcasted_iota(jnp.int32, (LANE, LANE), 1)
                     <= lax.broadcasted_iota(jnp.int32, (LANE, LANE), 0)).astype(BF16)

            def tie_body(kb, seen):
                for r in range(tk // LANE):
                    strip = slice(r * LANE, (r + 1) * LANE)
                    s = sc_ref[kb, strip, :]
                    eq = s == thr
                    rank = seen + jnp.dot(lower, jnp.where(eq, 1.0, 0.0).astype(BF16), preferred_element_type=F32)
                    keep = ((s > thr) | (eq & (rank <= room))) & (s > 0.5 * NEG)
                    sc_ref[kb, strip, :] = jnp.where(keep, 0.0, NEG)
                    seen = rank[LANE - 1:LANE, :]
                return seen

            lax.fori_loop(0, nkv, tie_body, jnp.zeros((1, tq), F32))

    q = q_ref[...]

    def kv_body(kb, carry):
        rows = pl.ds(pl.multiple_of(kb * tk, tk), tk)
        k = k_ref[rows, :]
        bias = sc_ref[kb]
        qk = [lax.dot_general(k[:, a * HEAD_DIM:(a + 1) * HEAD_DIM], q[:, a * HEAD_DIM:(a + 1) * HEAD_DIM], NT_DIMS,
                              preferred_element_type=F32) for a in range(hpb)]
        out = []
        for a in range(hpb):
            m_prev, l_prev, acc = carry[a]
            s = qk[a] + bias
            m_new = jnp.maximum(m_prev, jnp.max(s, axis=0, keepdims=True))
            p = jnp.exp2(s - m_new)
            alpha = jnp.exp2(m_prev - m_new)
            l_new = alpha * l_prev + jnp.sum(p, axis=0, keepdims=True)
            acc = alpha * acc + jnp.dot(vt_ref[kb, a * HEAD_DIM:(a + 1) * HEAD_DIM, :], p.astype(BF16),
                                        preferred_element_type=F32)
            out.append((m_new, l_new, acc))
        return tuple(out)

    init = tuple((jnp.full((1, tq), NEG, F32), jnp.zeros((1, tq), F32), jnp.zeros((HEAD_DIM, tq), F32))
                 for _ in range(hpb))
    fin = lax.fori_loop(0, nkv, kv_body, init)
    for a in range(hpb):
        _, l_fin, acc = fin[a]
        o_ref[:, a * HEAD_DIM:(a + 1) * HEAD_DIM] = (acc / l_fin).T.astype(o_ref.dtype)


def _dsa_attention(cd, v_t, ki_n, tail, *, batch, seq, group, w_off):
    m = batch * seq
    nh = group // HEAD_DIM
    tq = _pick_tile(seq, (512, 256, 128))
    tk = _attention_key_tile(seq)
    topk = min(TOPK_MAX, seq // 4)
    assert tk >= topk, "the threshold search needs at least topk keys in range"
    nq = seq // tq
    nc = group // LANE
    nkb = seq // tk
    q_off, k_off = 2 * nc, 3 * nc
    qi_off = 4 * nc * LANE // (IDX_HEADS * IDX_DIM)
    assert qi_off * IDX_HEADS * IDX_DIM == 4 * nc * LANE
    w_t = tail[:, w_off:w_off + IDX_HEADS].reshape(batch, seq, IDX_HEADS).transpose(0, 2, 1)

    hpb = _DSA_HEADS_PER_STEP
    assert nh % hpb == 0
    wide = hpb * HEAD_DIM
    return pl.pallas_call(
        functools.partial(_dsa_kernel, tq=tq, tk=tk, topk=topk, hpb=hpb,
                          w_scale=IDX_HEADS ** -0.5 * IDX_DIM ** -0.5),
        grid=(batch, nq, nh // hpb),
        in_specs=[pl.BlockSpec((tq, wide), lambda b, i, h: (b * nq + i, q_off // hpb + h)),
                  pl.BlockSpec((seq, wide), lambda b, i, h: (b, k_off // hpb + h)),
                  pl.BlockSpec((nkb, wide, tk), lambda b, i, h: (b, group // wide + h, 0)),
                  pl.BlockSpec((tq, IDX_HEADS * IDX_DIM), lambda b, i, h: (b * nq + i, qi_off)),
                  pl.BlockSpec((seq, IDX_DIM), lambda b, i, h: (b, 0)),
                  pl.BlockSpec((None, IDX_HEADS, tq), lambda b, i, h: (b, 0, i))],
        out_specs=pl.BlockSpec((tq, wide), lambda b, i, h: (b * nq + i, h)),
        out_shape=jax.ShapeDtypeStruct((m, group), BF16),
        scratch_shapes=[pltpu.VMEM((nkb, tk, tq), F32), pltpu.VMEM((nkb, tk, tq), BF16)],
        compiler_params=_params("parallel", "arbitrary", "arbitrary"),
        name="dsa",
    )(cd, cd, v_t, cd, ki_n, w_t)


def _repack_w_in(w, group, nhb):
    p0 = 7 * group
    p1 = p0 + 2 * nhb
    p2 = p1 + 6 * group + IDX_HEADS * IDX_DIM
    used = IDX_DIM + 2 * nhb + IDX_HEADS
    wt = jnp.swapaxes(w, 1, 2)
    pad = jnp.zeros((w.shape[0], LANE - used, w.shape[1]), w.dtype)

    def sec(first, count):
        return wt[:, p1 + first * group:p1 + (first + count) * group]

    return jnp.concatenate([wt[:, :p0], sec(0, 2), sec(3, 2), wt[:, p1 + 6 * group:p2], sec(2, 1), sec(5, 1),
                            wt[:, p2:p2 + IDX_DIM], wt[:, p0:p1], wt[:, p2 + IDX_DIM:], pad], axis=1).astype(BF16)


def kernel(x, norm_mix, w_in, hgrn_lb_logits, hgrn_gnorm, mlstm_conv, mlstm_gate_bias, mlstm_hnorm,
           diff_qk_norm, diff_lambda, diff_subln, dsa_qk_norm, dsa_idx_knorm, w_out, norm_mlp, w1, w2):
    batch, seq, d_model = x.shape
    depth = w_in.shape[0]
    group = d_model // 4
    nhb = group // HEAD_DIM
    nc = group // LANE
    m = batch * seq
    dims = dict(batch=batch, seq=seq, group=group)

    p_lb = jax.nn.softmax(hgrn_lb_logits.astype(F32), axis=0)
    lower_bounds = jnp.cumsum(p_lb, axis=0) - p_lb[0:1]

    w_in_b = _repack_w_in(w_in, group, nhb)
    w_out_b, w1_b, w2_b = (w.astype(BF16) for w in (w_out, w1, w2))
    n_ab, n_qk, n_v = 7 * group, 4 * group + IDX_HEADS * IDX_DIM, 2 * group
    tk_att = _attention_key_tile(seq)

    x2 = x.reshape(m, d_model).astype(F32)
    ones_g = jnp.ones((group,), F32)
    for l in range(depth):
        lambda_init = 0.8 - 0.6 * math.exp(-0.3 * l)
        h = _rmsnorm_rows(x2, norm_mix[l])
        proj = _matmul(h, w_in_b, layer=l, cols=(0, n_ab), b_rows=True, out_dtype=F32, name="in_proj_ab")
        tail = _matmul(h, w_in_b, layer=l, cols=(n_ab + n_qk + n_v, LANE), b_rows=True, out_dtype=F32,
                       name="in_proj_tail")
        v_t = _project_transposed(h, w_in_b, layer=l, rows=(n_ab + n_qk, n_v), tk=tk_att)

        y_a = _hgrn(proj, lower_bounds[l], hgrn_gnorm[l], **dims)

        y_b = _mlstm(proj, tail, mlstm_conv[l], mlstm_gate_bias[l], mlstm_hnorm[l], g_off=IDX_DIM, **dims)

        cqn = diff_qk_norm[l].astype(F32)
        dqn = dsa_qk_norm[l].astype(F32)
        q_scale = HEAD_DIM ** -0.5 * LOG2_E
        gains = jnp.concatenate([
            jnp.tile(cqn[0], nc) * q_scale, jnp.tile(cqn[1], nc),
            jnp.tile(dqn[0], nc) * q_scale, jnp.tile(dqn[1], nc),
            jnp.ones((IDX_HEADS * IDX_DIM,), F32)]).reshape(1, -1)
        use_norm = jnp.concatenate([
            jnp.ones((4 * group,), F32), jnp.zeros((IDX_HEADS * IDX_DIM,), F32)]).reshape(1, -1)
        cd = _matmul(h, w_in_b, layer=l, cols=(n_ab, n_qk), b_rows=True, out_dtype=BF16, act="groupnorm",
                     extra=(gains, use_norm), name="in_proj_cd")
        ki_n = _ki_norm(tail, dsa_idx_knorm[l])

        lam = diff_lambda[l].astype(F32)
        lam_full = jnp.exp(jnp.sum(lam[0] * lam[1])) - jnp.exp(jnp.sum(lam[2] * lam[3])) + lambda_init
        y_c = _diff_attention(cd, v_t, lam_full, diff_subln[l], lambda_init, **dims)
        y_d = _dsa_attention(cd, v_t, ki_n, tail, w_off=IDX_DIM + 2 * nhb, **dims)

        x2 = _matmul((y_a, y_b, y_c, y_d), w_out_b, layer=l, out_dtype=F32, act="residual", extra=(x2,),
                     name="out_proj")
        h = _rmsnorm_rows(x2, norm_mlp[l])
        hidden = _matmul(h, w1_b, layer=l, out_dtype=BF16, act="relu2", name="mlp_up")
        x2 = _matmul(hidden, w2_b, layer=l, out_dtype=F32, act="residual", extra=(x2,), tk_max=2048,
                     name="mlp_down")
    return x2.reshape(batch, seq, d_model).astype(x.dtype)
```

```python
import functools
import math

import numpy as np
import jax
import jax.numpy as jnp
from jax import lax
from jax.experimental import pallas as pl
from jax.experimental.pallas import tpu as pltpu

CHUNK = 64
EPS = 1e-6
NEG = -1e30
F_FLOOR = 1e-20
HEAD_DIM = 128
CONV_W = 4
IDX_HEADS = 16
IDX_DIM = 64
TOPK_MAX = 256
LOG2_E = math.log2(math.e)
LANE = 128
SUBLANE = 8
PACKED_SUBLANE = 16
VMEM_LIMIT = 56 * 1024 * 1024

F32 = jnp.float32
BF16 = jnp.bfloat16
NT_DIMS = (((1,), (1,)), ((), ()))
TN_DIMS = (((0,), (0,)), ((), ()))


def _params(*sem):
    return pltpu.CompilerParams(dimension_semantics=sem, vmem_limit_bytes=VMEM_LIMIT)


def _sigmoid(x):
    return 1.0 / (1.0 + jnp.exp(-x))


def _log_sigmoid(x):
    return jnp.minimum(x, 0.0) - jnp.log1p(jnp.exp(-jnp.abs(x)))


def _split3(x):
    hi = x.astype(BF16)
    r1 = x - hi.astype(F32)
    mid = r1.astype(BF16)
    lo = (r1 - mid.astype(F32)).astype(BF16)
    return hi, mid, lo


def _pick_tile(n, candidates):
    for c in candidates:
        if n % c == 0:
            return c
    raise ValueError(f"no tile in {candidates} divides {n}")


def _rmsnorm_kernel(x_ref, g_ref, o_ref):
    x = x_ref[...]
    ms = jnp.mean(x * x, axis=-1, keepdims=True)
    o_ref[...] = (x * lax.rsqrt(ms + EPS) * g_ref[...]).astype(o_ref.dtype)


def _rmsnorm_rows(x2, g):
    m, d = x2.shape
    tm = _pick_tile(m, (256, 128, 64, 8))
    return pl.pallas_call(
        _rmsnorm_kernel,
        grid=(m // tm,),
        in_specs=[pl.BlockSpec((tm, d), lambda i: (i, 0)), pl.BlockSpec((1, d), lambda i: (0, 0))],
        out_specs=pl.BlockSpec((tm, d), lambda i: (i, 0)),
        out_shape=jax.ShapeDtypeStruct((m, d), BF16),
        compiler_params=_params("parallel"),
        name="rmsnorm",
    )(x2, g.reshape(1, d).astype(F32))


MM_MAX_TK = 4096
MM_TN = 1024


def _mm_epilogue(r, act, extra):
    if act == "relu2":
        r = jnp.square(jnp.maximum(r, 0.0))
    elif act == "groupnorm":
        g_ref, u_ref = extra
        segs = []
        for c in range(r.shape[1] // LANE):
            seg = r[:, c * LANE:(c + 1) * LANE]
            ms = jnp.mean(seg * seg, axis=1, keepdims=True)
            inv = jnp.where(u_ref[:, c * LANE:(c + 1) * LANE] > 0.0, lax.rsqrt(ms + EPS), 1.0)
            segs.append(seg * inv * g_ref[:, c * LANE:(c + 1) * LANE])
        r = jnp.concatenate(segs, axis=1)
    elif act == "residual":
        r = r + extra[0][...]
    return r


def _mm_kernel(*refs, n_a, nk, act, n_extra, b_rows):
    a_refs = refs[:n_a]
    b_ref = refs[n_a]
    extra = refs[n_a + 1:n_a + 1 + n_extra]
    o_ref = refs[n_a + 1 + n_extra]

    def product():
        r, off = None, 0
        for a_ref in a_refs:
            kw = a_ref.shape[1]
            if b_rows:
                d = lax.dot_general(a_ref[...], b_ref[:, off:off + kw], NT_DIMS, preferred_element_type=F32)
            else:
                d = jnp.dot(a_ref[...], b_ref[off:off + kw, :], preferred_element_type=F32)
            r = d if r is None else r + d
            off += kw
        return r

    if nk == 1:
        o_ref[...] = _mm_epilogue(product(), act, extra).astype(o_ref.dtype)
        return
    k = pl.program_id(2)

    @pl.when(k == 0)
    def _():
        o_ref[...] = extra[0][...] + product()

    @pl.when(k > 0)
    def _():
        o_ref[...] += product()


def _matmul(a, b, *, out_dtype, layer, cols=None, b_rows=False, act=None, extra=(), tm_max=1024, tn_max=MM_TN,
            tk_max=MM_MAX_TK, name="matmul"):
    a_parts = a if isinstance(a, tuple) else (a,)
    m = a_parts[0].shape[0]
    kdim = sum(p.shape[1] for p in a_parts)
    col0, n = (0, b.shape[1 if b_rows else 2]) if cols is None else cols
    tm = _pick_tile(m, tuple(t for t in (1024, 512, 256, 128, 8) if t <= tm_max))
    tn = _pick_tile(math.gcd(n, col0), tuple(t for t in (1024, 512, 256, 128) if t <= tn_max))
    j0 = col0 // tn
    tk = kdim if kdim <= tk_max else _pick_tile(kdim, tuple(t for t in (4096, 2048, 1024, 512, 256, 128)
                                                            if t <= tk_max))
    nk = kdim // tk
    assert nk == 1 or (len(a_parts) == 1 and act == "residual" and out_dtype == F32)
    in_specs = [pl.BlockSpec((tm, tk if nk > 1 else p.shape[1]), lambda i, j, k: (i, k)) for p in a_parts]
    if b_rows:
        in_specs.append(pl.BlockSpec((None, tn, tk), lambda i, j, k: (layer, j0 + j, k)))
    else:
        in_specs.append(pl.BlockSpec((None, tk, tn), lambda i, j, k: (layer, k, j0 + j)))
    for e in extra:
        if e.shape[0] == 1:
            in_specs.append(pl.BlockSpec((1, tn), lambda i, j, k: (0, j)))
        else:
            in_specs.append(pl.BlockSpec((tm, tn), lambda i, j, k: (i, j)))
    return pl.pallas_call(
        functools.partial(_mm_kernel, n_a=len(a_parts), nk=nk, act=act, n_extra=len(extra), b_rows=b_rows),
        grid=(m // tm, n // tn, nk),
        in_specs=in_specs,
        out_specs=pl.BlockSpec((tm, tn), lambda i, j, k: (i, j)),
        out_shape=jax.ShapeDtypeStruct((m, n), out_dtype),
        compiler_params=_params("parallel", "parallel", "arbitrary"),
        name=name,
    )(*a_parts, b, *extra)


def _vt_kernel(a_ref, w_ref, o_ref):
    o_ref[...] = lax.dot_general(w_ref[...], a_ref[...], NT_DIMS, preferred_element_type=F32).astype(o_ref.dtype)


def _project_transposed(a, w_rows, *, layer, rows, tk):
    m, kdim = a.shape
    r0, n = rows
    tn = _pick_tile(math.gcd(n, r0), (1024, 512, 256, 128))
    j0 = r0 // tn
    return pl.pallas_call(
        _vt_kernel,
        grid=(m // tk, n // tn),
        in_specs=[pl.BlockSpec((tk, kdim), lambda i, j: (i, 0)),
                  pl.BlockSpec((None, tn, kdim), lambda i, j: (layer, j0 + j, 0))],
        out_specs=pl.BlockSpec((None, tn, tk), lambda i, j: (i, j, 0)),
        out_shape=jax.ShapeDtypeStruct((m // tk, n, tk), BF16),
        compiler_params=_params("parallel", "parallel"),
        name="in_proj_vt",
    )(a, w_rows)


_HGRN_LEVELS = 6
_HGRN_HEADS_PER_STEP = 4


def _hgrn_constants():
    idx = np.arange(CHUNK)
    j = idx[None, :]
    r = idx[:, None]
    blocks = [j <= r, j > r]
    masks = []
    for lvl in range(_HGRN_LEVELS):
        half = (CHUNK // 2) >> lvl
        parent = idx // (2 * half)
        mid = parent * 2 * half + half
        is_right = (idx % (2 * half)) >= half
        right_rng = (j >= mid[:, None]) & (j <= r)
        left_rng = (j > r) & (j < mid[:, None])
        blocks.append(np.where(is_right[:, None], right_rng, left_rng))
        masks.append((parent[:, None] == parent[None, :]) & is_right[:, None] & (~is_right)[None, :])
    cm = np.concatenate(blocks, axis=0).astype(np.float32)
    cm3 = np.concatenate([cm, cm, cm, np.zeros_like(cm)], axis=1)
    mk = np.stack(masks, axis=0).astype(np.float32)
    return jnp.asarray(cm3, dtype=BF16), jnp.asarray(mk, dtype=F32)


def _hgrn_kernel(q_ref, f_ref, i_ref, g_ref, lb_ref, gn_ref, cm_ref, mk_ref, o_ref, st_ref, *, nchunk, hb, scale):
    @pl.when(pl.program_id(2) == 0)
    def _():
        st_ref[...] = jnp.zeros_like(st_ref)

    lb = lb_ref[...]
    gn = gn_ref[...]

    def body(c, carry):
        rows = pl.ds(pl.multiple_of(c * CHUNK, CHUNK), CHUNK)
        f = lb + (1.0 - lb) * _sigmoid(f_ref[rows, :])
        logf = jnp.log(jnp.maximum(f, F_FLOOR))
        kk_w = 1.0 - f
        qr = q_ref[rows, :]
        qq_w = qr * _sigmoid(qr) * scale
        v_w = i_ref[rows, :]
        gr = g_ref[rows, :]
        gate_w = gr * _sigmoid(gr)
        l3 = jnp.concatenate(_split3(logf) + (jnp.zeros(logf.shape, BF16),), axis=0)
        sums_w = jnp.dot(cm_ref[...], l3, preferred_element_type=F32)
        new_state = []
        for a in range(hb):
            hs = slice(a * HEAD_DIM, (a + 1) * HEAD_DIM)
            sums, qq, kk, v = sums_w[:, hs], qq_w[:, hs], kk_w[:, hs], v_w[:, hs]
            vb = v.astype(BF16)
            b = sums[0:CHUNK]
            b_rest = sums[CHUNK:2 * CHUNK]
            scores = jnp.zeros((CHUNK, CHUNK), F32)
            for lvl in range(_HGRN_LEVELS):
                e = jnp.exp(sums[(2 + lvl) * CHUNK:(3 + lvl) * CHUNK])
                s = lax.dot_general((qq * e).astype(BF16), (kk * e).astype(BF16), NT_DIMS,
                                    preferred_element_type=F32)
                scores = scores + s * mk_ref[lvl]
            diag = jnp.sum(qq * kk, axis=1, keepdims=True)
            st = carry[a]
            o = (jnp.dot(scores.astype(BF16), vb, preferred_element_type=F32) + diag * v
                 + lax.dot_general((qq * jnp.exp(b)).astype(BF16), st.astype(BF16), NT_DIMS,
                                   preferred_element_type=F32))
            ke = (kk * jnp.exp(b_rest)).astype(BF16)
            new_state.append(st * jnp.exp(b[CHUNK - 1:CHUNK, :])
                             + lax.dot_general(vb, ke, TN_DIMS, preferred_element_type=F32))
            ms = jnp.mean(o * o, axis=1, keepdims=True)
            o_ref[rows, hs] = (o * lax.rsqrt(ms + EPS) * gn * gate_w[:, hs]).astype(o_ref.dtype)
        return tuple(new_state)

    state = lax.fori_loop(0, nchunk, body, tuple(st_ref[a] for a in range(hb)), unroll=True)
    for a in range(hb):
        st_ref[a] = state[a]


def _hgrn(proj, lb, gnorm, *, batch, seq, group):
    m = batch * seq
    nh = group // HEAD_DIM
    t = _pick_tile(seq, (512, 256, 128, 64))
    nt = seq // t
    cm3, mk = _hgrn_constants()

    hb = min(_HGRN_HEADS_PER_STEP, nh)
    assert nh % hb == 0
    wide = hb * HEAD_DIM
    ng = nh // hb

    def col(off):
        return pl.BlockSpec((t, wide), lambda b, h, i: (b * nt + i, off + h))

    return pl.pallas_call(
        functools.partial(_hgrn_kernel, nchunk=t // CHUNK, hb=hb, scale=HEAD_DIM ** -0.5),
        grid=(batch, ng, nt),
        in_specs=[col(0), col(ng), col(2 * ng), col(3 * ng),
                  pl.BlockSpec((1, wide), lambda b, h, i: (0, h)),
                  pl.BlockSpec((1, HEAD_DIM), lambda b, h, i: (0, 0)),
                  pl.BlockSpec(cm3.shape, lambda b, h, i: (0, 0)),
                  pl.BlockSpec(mk.shape, lambda b, h, i: (0, 0, 0))],
        out_specs=pl.BlockSpec((t, wide), lambda b, h, i: (b * nt + i, h)),
        out_shape=jax.ShapeDtypeStruct((m, group), BF16),
        scratch_shapes=[pltpu.VMEM((hb, HEAD_DIM, HEAD_DIM), F32)],
        compiler_params=_params("parallel", "parallel", "arbitrary"),
        name="hgrn2",
    )(proj, proj, proj, proj, lb.reshape(1, group).astype(F32), gnorm.reshape(1, HEAD_DIM).astype(F32), cm3, mk)


_CONV_PAD = 8


def _mlstm_constants():
    idx = np.arange(CHUNK)
    tri = (idx[None, :] <= idx[:, None]).astype(np.float32)
    zero = np.zeros_like(tri)
    tri3 = np.concatenate([tri, tri, tri, zero], axis=1)
    trit3 = np.concatenate([tri.T, tri.T, tri.T, zero], axis=0)
    trit3 = np.concatenate([trit3, np.zeros_like(trit3)], axis=1)
    return jnp.asarray(tri3, dtype=BF16), jnp.asarray(trit3, dtype=BF16)


def _mlstm_kernel(qk_ref, v_ref, og_ref, gc_ref, gr_ref, cw_ref, bc_ref, br_ref, hn_ref, tri_ref, trit_ref,
                  o_ref, xp_ref, qk_s, c_ref, m_ref, *, nchunk, nh, dqk, dv, t, g_off):
    @pl.when(pl.program_id(1) == 0)
    def _():
        xp_ref[0:_CONV_PAD, :] = jnp.zeros((_CONV_PAD, xp_ref.shape[1]), F32)
        c_ref[...] = jnp.zeros_like(c_ref)
        m_ref[...] = jnp.zeros_like(m_ref)

    xp_ref[_CONV_PAD:_CONV_PAD + t, :] = qk_ref[...]
    conv = jnp.zeros((t, 2 * nh * dqk), F32)
    for j in range(CONV_W):
        start = _CONV_PAD - (CONV_W - 1) + j
        conv = conv + cw_ref[j:j + 1, :] * xp_ref[start:start + t, :]
    qk_s[...] = conv * _sigmoid(conv)
    xp_ref[0:_CONV_PAD, :] = xp_ref[t:t + _CONV_PAD, :]

    hn = hn_ref[...]
    kscale = dqk ** -0.5
    r_i = lax.broadcasted_iota(jnp.int32, (CHUNK, CHUNK), 0)
    c_i = lax.broadcasted_iota(jnp.int32, (CHUNK, CHUNK), 1)
    causal = c_i <= r_i
    ones = jnp.ones((CHUNK, dv), BF16)
    zpad_c = jnp.zeros((CHUNK, LANE), BF16)
    zpad_r = jnp.zeros((2 * nh, CHUNK), BF16)

    def body(c, carry):
        base = pl.multiple_of(c * CHUNK, CHUNK)
        rows = pl.ds(base, CHUNK)
        qk = qk_s[rows, :]
        gcol = gc_ref[rows, :] + bc_ref[...]
        b_c = jnp.dot(tri_ref[...], jnp.concatenate(_split3(_log_sigmoid(gcol)) + (zpad_c,), axis=0),
                      preferred_element_type=F32)
        grow = gr_ref[c] + br_ref[...]
        ig_r = grow[0:nh, :]
        b_r = jnp.dot(jnp.concatenate(_split3(_log_sigmoid(grow)) + (zpad_r,), axis=1), trit_ref[...],
                      preferred_element_type=F32)[nh:2 * nh, 0:CHUNK]
        new_state = []
        for h in range(nh):
            ig_col = gcol[:, g_off + h:g_off + h + 1]
            q_h = qk[:, h * dqk:(h + 1) * dqk].astype(BF16)
            k_f = qk[:, (nh + h) * dqk:(nh + h + 1) * dqk] * kscale
            k_h = k_f.astype(BF16)
            v_ext = jnp.concatenate([v_ref[rows, h * dv:(h + 1) * dv].astype(BF16), ones], axis=1)
            bcol = b_c[:, g_off + nh + h:g_off + nh + h + 1]
            c_ext, m_prev = carry[h]
            log_d = jnp.where(causal, bcol - b_r[h:h + 1, :] + ig_r[h:h + 1, :], NEG)
            log_inter = bcol + m_prev
            m_t = jnp.maximum(jnp.max(log_d, axis=1, keepdims=True), log_inter)
            qk_w = (lax.dot_general(q_h, k_h, NT_DIMS, preferred_element_type=F32)
                    * jnp.exp(log_d - m_t))
            w_inter = jnp.exp(log_inter - m_t)
            numden = (jnp.dot(qk_w.astype(BF16), v_ext, preferred_element_type=F32)
                      + w_inter * jnp.dot(q_h, c_ext.astype(BF16), preferred_element_type=F32))
            num = numden[:, 0:dv]
            den = numden[:, dv:2 * dv]
            hh = num / jnp.maximum(jnp.abs(den), jnp.exp(-m_t))
            m_new = m_t[CHUNK - 1:CHUNK, :]
            b_last = bcol[CHUNK - 1:CHUNK, :]
            w_state = jnp.exp(b_last - bcol + ig_col - m_new)
            decay = jnp.exp(b_last + m_prev - m_new)
            new_state.append((decay * c_ext + lax.dot_general((k_f * w_state).astype(BF16), v_ext, TN_DIMS,
                                                              preferred_element_type=F32), m_new))
            ms = jnp.mean(hh * hh, axis=1, keepdims=True)
            gate = _sigmoid(og_ref[rows, h * dv:(h + 1) * dv])
            o_ref[rows, h * dv:(h + 1) * dv] = (hh * lax.rsqrt(ms + EPS) * hn * gate).astype(o_ref.dtype)
        return tuple(new_state)

    state = lax.fori_loop(0, nchunk, body, tuple((c_ref[h], m_ref[h:h + 1, 0:1]) for h in range(nh)), unroll=True)
    for h in range(nh):
        c_ref[h] = state[h][0]
        m_ref[h:h + 1, :] = jnp.broadcast_to(state[h][1], (1, LANE))


def _mlstm(proj, tail, conv_w, gate_bias, hnorm, *, batch, seq, group, g_off):
    m = batch * seq
    nh = group // HEAD_DIM
    dqk = HEAD_DIM // 2
    dv = HEAD_DIM
    t = _pick_tile(seq, (256, 128, 64))
    nt = seq // t
    tri3, trit3 = _mlstm_constants()
    qk_blk = 4
    gates_row = (tail[:, g_off:g_off + 2 * nh].reshape(batch, seq // CHUNK, CHUNK, 2 * nh)
                 .transpose(0, 1, 3, 2))
    bias = gate_bias.astype(F32)
    bias_lane = jnp.zeros((1, LANE), F32).at[0, g_off:g_off + 2 * nh].set(bias)

    def wide(off):
        return pl.BlockSpec((t, group), lambda b, i: (b * nt + i, off))

    def const(shape):
        return pl.BlockSpec(shape, lambda b, i: tuple(0 for _ in shape))

    return pl.pallas_call(
        functools.partial(_mlstm_kernel, nchunk=t // CHUNK, nh=nh, dqk=dqk, dv=dv, t=t, g_off=g_off),
        grid=(batch, nt),
        in_specs=[wide(qk_blk), wide(qk_blk + 1), wide(qk_blk + 2),
                  pl.BlockSpec((t, LANE), lambda b, i: (b * nt + i, 0)),
                  pl.BlockSpec((None, t // CHUNK, 2 * nh, CHUNK), lambda b, i: (b, i, 0, 0)),
                  const((CONV_W, group)), const((1, LANE)), const((2 * nh, 1)), const((1, dv)),
                  const(tri3.shape), const(trit3.shape)],
        out_specs=pl.BlockSpec((t, group), lambda b, i: (b * nt + i, 0)),
        out_shape=jax.ShapeDtypeStruct((m, group), BF16),
        scratch_shapes=[pltpu.VMEM((t + _CONV_PAD, group), F32),
                        pltpu.VMEM((t, group), F32),
                        pltpu.VMEM((nh, dqk, 2 * dv), F32),
                        pltpu.VMEM((nh, LANE), F32)],
        compiler_params=_params("parallel", "arbitrary"),
        name="mlstm",
    )(proj, proj, proj, tail, gates_row, conv_w.astype(F32), bias_lane, bias.reshape(2 * nh, 1),
      hnorm.reshape(1, dv).astype(F32), tri3, trit3)


def _ki_kernel(x_ref, g_ref, o_ref):
    x = x_ref[:, 0:IDX_DIM]
    ms = jnp.mean(x * x, axis=-1, keepdims=True)
    o_ref[...] = (x * lax.rsqrt(ms + EPS) * g_ref[...]).astype(o_ref.dtype)


def _ki_norm(tail, g):
    m = tail.shape[0]
    tm = _pick_tile(m, (512, 256, 128, 64))
    return pl.pallas_call(
        _ki_kernel,
        grid=(m // tm,),
        in_specs=[pl.BlockSpec((tm, LANE), lambda i: (i, 0)), pl.BlockSpec((1, IDX_DIM), lambda i: (0, 0))],
        out_specs=pl.BlockSpec((tm, IDX_DIM), lambda i: (i, 0)),
        out_shape=jax.ShapeDtypeStruct((m, IDX_DIM), BF16),
        compiler_params=_params("parallel"),
        name="ki_norm",
    )(tail, g.reshape(1, IDX_DIM).astype(F32))


def _diff_kernel(q_ref, k_ref, vt_ref, lam_ref, sn_ref, o_ref, m_ref, l_ref, acc_ref, *, tq, tk, dqk, out_scale):
    qi = pl.program_id(2)
    last_k = ((qi + 1) * tq - 1) // tk
    m_ref[...] = jnp.full_like(m_ref, NEG)
    l_ref[...] = jnp.zeros_like(l_ref)
    acc_ref[...] = jnp.zeros_like(acc_ref)
    q = q_ref[...]

    def step(kb, masked):
        rows = pl.ds(pl.multiple_of(kb * tk, tk), tk)
        k = k_ref[rows, :]
        vt = vt_ref[kb]
        if masked:
            k_chunk = (kb * tk + lax.broadcasted_iota(jnp.int32, (tk, tq), 0)) // CHUNK
            q_chunk = (qi * tq + lax.broadcasted_iota(jnp.int32, (tk, tq), 1)) // CHUNK
            mask = k_chunk <= q_chunk
        s_pair = [lax.dot_general(k[:, half * dqk:(half + 1) * dqk], q[:, half * dqk:(half + 1) * dqk], NT_DIMS,
                                  preferred_element_type=F32) for half in range(2)]
        probs = []
        for half in range(2):
            s = s_pair[half]
            if masked:
                s = jnp.where(mask, s, NEG)
            m_prev = m_ref[half]
            m_new = jnp.maximum(m_prev, jnp.max(s, axis=0, keepdims=True))
            p = jnp.exp2(s - m_new)
            alpha = jnp.exp2(m_prev - m_new)
            l_ref[half] = alpha * l_ref[half] + jnp.sum(p, axis=0, keepdims=True)
            m_ref[half] = m_new
            probs.append((alpha, p.astype(BF16)))
        for half in range(2):
            alpha, p = probs[half]
            acc_ref[half] = alpha * acc_ref[half] + jnp.dot(vt, p, preferred_element_type=F32)

    def body(kb, carry):
        step(kb, False)
        return carry

    lax.fori_loop(0, last_k, body, 0)
    step(last_k, True)

    o = acc_ref[0] / l_ref[0] - lam_ref[...] * (acc_ref[1] / l_ref[1])
    ms = jnp.mean(o * o, axis=0, keepdims=True)
    o_ref[...] = (o * lax.rsqrt(ms + EPS) * (sn_ref[...] * out_scale)).T.astype(o_ref.dtype)


def _attention_key_tile(seq):
    return _pick_tile(seq, (1024, 512, 256))


def _diff_attention(cd, v_t, lam_full, subln, lambda_init, *, batch, seq, group):
    m = batch * seq
    dqk = HEAD_DIM
    dv = 2 * HEAD_DIM
    nh = group // dv
    tk = _attention_key_tile(seq)
    tq = tk
    nq, nkb = seq // tq, seq // tk
    k_off = nh

    return pl.pallas_call(
        functools.partial(_diff_kernel, tq=tq, tk=tk, dqk=dqk, out_scale=1.0 - lambda_init),
        grid=(batch, nh, nq),
        in_specs=[pl.BlockSpec((tq, dv), lambda b, h, i: (b * nq + i, h)),
                  pl.BlockSpec((seq, dv), lambda b, h, i: (b, k_off + h)),
                  pl.BlockSpec((nkb, dv, tk), lambda b, h, i: (b, h, 0)),
                  pl.BlockSpec((1, 1), lambda b, h, i: (0, 0)),
                  pl.BlockSpec((dv, 1), lambda b, h, i: (0, 0))],
        out_specs=pl.BlockSpec((tq, dv), lambda b, h, i: (b * nq + i, h)),
        out_shape=jax.ShapeDtypeStruct((m, group), BF16),
        scratch_shapes=[pltpu.VMEM((2, 1, tq), F32), pltpu.VMEM((2, 1, tq), F32), pltpu.VMEM((2, dv, tq), F32)],
        compiler_params=_params("parallel", "parallel", "arbitrary"),
        name="diff_attn",
    )(cd, cd, v_t, lam_full.reshape(1, 1).astype(F32), subln.reshape(dv, 1).astype(F32))


_INT_MIN = -2 ** 31
_HI16 = -(1 << 16)


_CNT_ROWS = 64
_DSA_HEADS_PER_STEP = 2


def _dsa_kernel(q_ref, k_ref, vt_ref, qi_ref, ki_ref, wt_ref, o_ref, sc_ref, sc16_ref, *, tq, tk, topk, hpb,
                w_scale):
    qb = pl.program_id(1)
    h = pl.program_id(2)
    nkv = ((qb + 1) * tq + tk - 1) // tk

    @pl.when(h == 0)
    def _():
        w = wt_ref[...] * w_scale
        q_chunk = (qb * tq + lax.broadcasted_iota(jnp.int32, (tk, tq), 1)) // CHUNK

        def score_body(kb, carry):
            rows = pl.ds(pl.multiple_of(kb * tk, tk), tk)
            ki = ki_ref[rows, :]
            acc = jnp.zeros((tk, tq), F32)
            for ih in range(IDX_HEADS):
                lg = lax.dot_general(ki, qi_ref[:, ih * IDX_DIM:(ih + 1) * IDX_DIM], NT_DIMS,
                                     preferred_element_type=F32)
                acc = acc + w[ih:ih + 1, :] * jnp.maximum(lg, 0.0)
            k_chunk = (kb * tk + lax.broadcasted_iota(jnp.int32, (tk, tq), 0)) // CHUNK
            x = jnp.where(k_chunk <= q_chunk, acc, NEG)
            sc_ref[kb] = x
            hi = lax.bitcast_convert_type(x, jnp.int32) & _HI16
            sc16_ref[kb] = lax.bitcast_convert_type(hi, F32).astype(BF16)
            return carry

        lax.fori_loop(0, nkv, score_body, 0)

        def count_ge(cf):
            def cnt_body(kb, cnt):
                for r in range(tk // _CNT_ROWS):
                    hit = jnp.where(sc_ref[kb, r * _CNT_ROWS:(r + 1) * _CNT_ROWS, :] >= cf, 1.0, 0.0)
                    for g in range(_CNT_ROWS // SUBLANE):
                        cnt = cnt + hit[g * SUBLANE:(g + 1) * SUBLANE, :]
                return cnt

            return jnp.sum(lax.fori_loop(0, nkv, cnt_body, jnp.zeros((SUBLANE, tq), F32)), axis=0, keepdims=True)

        def count_ge_hi(cfb):
            one, zero = jnp.ones((), BF16), jnp.zeros((), BF16)

            def cnt_body(kb, cnt):
                part = jnp.zeros((PACKED_SUBLANE, tq),BF16)
                for r in range(tk // _CNT_ROWS):
                    hit = jnp.where(sc16_ref[kb, r * _CNT_ROWS:(r + 1) * _CNT_ROWS, :] >= cfb, one, zero)
                    for g in range(_CNT_ROWS // PACKED_SUBLANE):
                        part = part + hit[g * PACKED_SUBLANE:(g + 1) * PACKED_SUBLANE, :]
                return cnt + part.astype(F32)

            return jnp.sum(lax.fori_loop(0, nkv, cnt_body, jnp.zeros((PACKED_SUBLANE, tq),F32)), axis=0, keepdims=True)

        def bit_body(state, *, coarse):
            i, u, kept = state
            cand = u | jnp.left_shift(jnp.int32(1), 31 - i)
            key = cand ^ _INT_MIN
            if coarse:
                key = jnp.where(key < 0, key | 0xFFFF, key)
            cf = lax.bitcast_convert_type(key ^ (jnp.right_shift(key, 31) & 0x7FFFFFFF), F32)
            cnt = count_ge_hi(cf.astype(BF16)) if coarse else count_ge(cf)
            ok = cnt >= float(topk)
            return i + 1, jnp.where(ok, cand, u), jnp.where(ok, cnt, kept)

        def unsettled(state, *, last):
            i, _, kept = state
            return (i < last) & (jnp.max(jnp.where(kept == float(topk), 0.0, 1.0)) > 0.0)

        in_range = jnp.zeros((1, tq), F32) + (nkv * tk).astype(F32)
        state = (jnp.int32(0), jnp.zeros((1, tq), jnp.int32), in_range)
        state = lax.while_loop(functools.partial(unsettled, last=16), functools.partial(bit_body, coarse=True), state)
        state = lax.while_loop(functools.partial(unsettled, last=32), functools.partial(bit_body, coarse=False),
                               (jnp.maximum(state[0], 16),) + state[1:])
        key = state[1] ^ _INT_MIN
        thr = lax.bitcast_convert_type(key ^ (jnp.right_shift(key, 31) & 0x7FFFFFFF), F32)

        tied = (state[2] > float(topk)) & (thr > 0.5 * NEG)
        has_tie = jnp.max(jnp.where(tied, 1.0, 0.0)) > 0.0

        @pl.when(jnp.logical_not(has_tie))
        def _():
            def bias_body(kb, carry):
                s = sc_ref[kb]
                sc_ref[kb] = jnp.where((s >= thr) & (s > 0.5 * NEG), 0.0, NEG)
                return carry

            lax.fori_loop(0, nkv, bias_body, 0)

        @pl.when(has_tie)
        def _():
            def gt_body(kb, cnt):
                for r in range(tk // _CNT_ROWS):
                    hit = jnp.where(sc_ref[kb, r * _CNT_ROWS:(r + 1) * _CNT_ROWS, :] > thr, 1.0, 0.0)
                    for g in range(_CNT_ROWS // SUBLANE):
                        cnt = cnt + hit[g * SUBLANE:(g + 1) * SUBLANE, :]
                return cnt

            above = jnp.sum(lax.fori_loop(0, nkv, gt_body, jnp.zeros((SUBLANE, tq), F32)), axis=0, keepdims=True)
            room = float(topk) - above
            lower = (lax.broadcasted_iota(jnp.int32, (LANE, LANE), 1)
                     <= lax.broadcasted_iota(jnp.int32, (LANE, LANE), 0)).astype(BF16)

            def tie_body(kb, seen):
                for r in range(tk // LANE):
                    strip = slice(r * LANE, (r + 1) * LANE)
                    s = sc_ref[kb, strip, :]
                    eq = s == thr
                    rank = seen + jnp.dot(lower, jnp.where(eq, 1.0, 0.0).astype(BF16), preferred_element_type=F32)
                    keep = ((s > thr) | (eq & (rank <= room))) & (s > 0.5 * NEG)
                    sc_ref[kb, strip, :] = jnp.where(keep, 0.0, NEG)
                    seen = rank[LANE - 1:LANE, :]
                return seen

            lax.fori_loop(0, nkv, tie_body, jnp.zeros((1, tq), F32))

    q = q_ref[...]

    def kv_body(kb, carry):
        rows = pl.ds(pl.multiple_of(kb * tk, tk), tk)
        k = k_ref[rows, :]
        bias = sc_ref[kb]
        qk = [lax.dot_general(k[:, a * HEAD_DIM:(a + 1) * HEAD_DIM], q[:, a * HEAD_DIM:(a + 1) * HEAD_DIM], NT_DIMS,
                              preferred_element_type=F32) for a in range(hpb)]
        out = []
        for a in range(hpb):
            m_prev, l_prev, acc = carry[a]
            s = qk[a] + bias
            m_new = jnp.maximum(m_prev, jnp.max(s, axis=0, keepdims=True))
            p = jnp.exp2(s - m_new)
            alpha = jnp.exp2(m_prev - m_new)
            l_new = alpha * l_prev + jnp.sum(p, axis=0, keepdims=True)
            acc = alpha * acc + jnp.dot(vt_ref[kb, a * HEAD_DIM:(a + 1) * HEAD_DIM, :], p.astype(BF16),
                                        preferred_element_type=F32)
            out.append((m_new, l_new, acc))
        return tuple(out)

    init = tuple((jnp.full((1, tq), NEG, F32), jnp.zeros((1, tq), F32), jnp.zeros((HEAD_DIM, tq), F32))
                 for _ in range(hpb))
    fin = lax.fori_loop(0, nkv, kv_body, init)
    for a in range(hpb):
        _, l_fin, acc = fin[a]
        o_ref[:, a * HEAD_DIM:(a + 1) * HEAD_DIM] = (acc / l_fin).T.astype(o_ref.dtype)


def _dsa_attention(cd, v_t, ki_n, tail, *, batch, seq, group, w_off):
    m = batch * seq
    nh = group // HEAD_DIM
    tq = _pick_tile(seq, (512, 256, 128))
    tk = _attention_key_tile(seq)
    topk = min(TOPK_MAX, seq // 4)
    assert tk >= topk, "the threshold search needs at least topk keys in range"
    nq = seq // tq
    nc = group // LANE
    nkb = seq // tk
    q_off, k_off = 2 * nc, 3 * nc
    qi_off = 4 * nc * LANE // (IDX_HEADS * IDX_DIM)
    assert qi_off * IDX_HEADS * IDX_DIM == 4 * nc * LANE
    w_t = tail[:, w_off:w_off + IDX_HEADS].reshape(batch, seq, IDX_HEADS).transpose(0, 2, 1)

    hpb = _DSA_HEADS_PER_STEP
    assert nh % hpb == 0
    wide = hpb * HEAD_DIM
    return pl.pallas_call(
        functools.partial(_dsa_kernel, tq=tq, tk=tk, topk=topk, hpb=hpb,
                          w_scale=IDX_HEADS ** -0.5 * IDX_DIM ** -0.5),
        grid=(batch, nq, nh // hpb),
        in_specs=[pl.BlockSpec((tq, wide), lambda b, i, h: (b * nq + i, q_off // hpb + h)),
                  pl.BlockSpec((seq, wide), lambda b, i, h: (b, k_off // hpb + h)),
                  pl.BlockSpec((nkb, wide, tk), lambda b, i, h: (b, group // wide + h, 0)),
                  pl.BlockSpec((tq, IDX_HEADS * IDX_DIM), lambda b, i, h: (b * nq + i, qi_off)),
                  pl.BlockSpec((seq, IDX_DIM), lambda b, i, h: (b, 0)),
                  pl.BlockSpec((None, IDX_HEADS, tq), lambda b, i, h: (b, 0, i))],
        out_specs=pl.BlockSpec((tq, wide), lambda b, i, h: (b * nq + i, h)),
        out_shape=jax.ShapeDtypeStruct((m, group), BF16),
        scratch_shapes=[pltpu.VMEM((nkb, tk, tq), F32), pltpu.VMEM((nkb, tk, tq), BF16)],
        compiler_params=_params("parallel", "arbitrary", "arbitrary"),
        name="dsa",
    )(cd, cd, v_t, cd, ki_n, w_t)


def _repack_w_in(w, group, nhb):
    p0 = 7 * group
    p1 = p0 + 2 * nhb
    p2 = p1 + 6 * group + IDX_HEADS * IDX_DIM
    used = IDX_DIM + 2 * nhb + IDX_HEADS
    wt = jnp.swapaxes(w, 1, 2)
    pad = jnp.zeros((w.shape[0], LANE - used, w.shape[1]), w.dtype)

    def sec(first, count):
        return wt[:, p1 + first * group:p1 + (first + count) * group]

    return jnp.concatenate([wt[:, :p0], sec(0, 2), sec(3, 2), wt[:, p1 + 6 * group:p2], sec(2, 1), sec(5, 1),
                            wt[:, p2:p2 + IDX_DIM], wt[:, p0:p1], wt[:, p2 + IDX_DIM:], pad], axis=1).astype(BF16)


def kernel(x, norm_mix, w_in, hgrn_lb_logits, hgrn_gnorm, mlstm_conv, mlstm_gate_bias, mlstm_hnorm,
           diff_qk_norm, diff_lambda, diff_subln, dsa_qk_norm, dsa_idx_knorm, w_out, norm_mlp, w1, w2):
    batch, seq, d_model = x.shape
    depth = w_in.shape[0]
    group = d_model // 4
    nhb = group // HEAD_DIM
    nc = group // LANE
    m = batch * seq
    dims = dict(batch=batch, seq=seq, group=group)

    p_lb = jax.nn.softmax(hgrn_lb_logits.astype(F32), axis=0)
    lower_bounds = jnp.cumsum(p_lb, axis=0) - p_lb[0:1]

    w_in_b = _repack_w_in(w_in, group, nhb)
    w_out_b, w1_b, w2_b = (w.astype(BF16) for w in (w_out, w1, w2))
    n_ab, n_qk, n_v = 7 * group, 4 * group + IDX_HEADS * IDX_DIM, 2 * group
    tk_att = _attention_key_tile(seq)

    x2 = x.reshape(m, d_model).astype(F32)
    ones_g = jnp.ones((group,), F32)
    for l in range(depth):
        lambda_init = 0.8 - 0.6 * math.exp(-0.3 * l)
        h = _rmsnorm_rows(x2, norm_mix[l])
        proj = _matmul(h, w_in_b, layer=l, cols=(0, n_ab), b_rows=True, out_dtype=F32, name="in_proj_ab")
        tail = _matmul(h, w_in_b, layer=l, cols=(n_ab + n_qk + n_v, LANE), b_rows=True, out_dtype=F32,
                       name="in_proj_tail")
        v_t = _project_transposed(h, w_in_b, layer=l, rows=(n_ab + n_qk, n_v), tk=tk_att)

        y_a = _hgrn(proj, lower_bounds[l], hgrn_gnorm[l], **dims)

        y_b = _mlstm(proj, tail, mlstm_conv[l], mlstm_gate_bias[l], mlstm_hnorm[l], g_off=IDX_DIM, **dims)

        cqn = diff_qk_norm[l].astype(F32)
        dqn = dsa_qk_norm[l].astype(F32)
        q_scale = HEAD_DIM ** -0.5 * LOG2_E
        gains = jnp.concatenate([
            jnp.tile(cqn[0], nc) * q_scale, jnp.tile(cqn[1], nc),
            jnp.tile(dqn[0], nc) * q_scale, jnp.tile(dqn[1], nc),
            jnp.ones((IDX_HEADS * IDX_DIM,), F32)]).reshape(1, -1)
        use_norm = jnp.concatenate([
            jnp.ones((4 * group,), F32), jnp.zeros((IDX_HEADS * IDX_DIM,), F32)]).reshape(1, -1)
        cd = _matmul(h, w_in_b, layer=l, cols=(n_ab, n_qk), b_rows=True, out_dtype=BF16, act="groupnorm",
                     extra=(gains, use_norm), name="in_proj_cd")
        ki_n = _ki_norm(tail, dsa_idx_knorm[l])

        lam = diff_lambda[l].astype(F32)
        lam_full = jnp.exp(jnp.sum(lam[0] * lam[1])) - jnp.exp(jnp.sum(lam[2] * lam[3])) + lambda_init
        y_c = _diff_attention(cd, v_t, lam_full, diff_subln[l], lambda_init, **dims)
        y_d = _dsa_attention(cd, v_t, ki_n, tail, w_off=IDX_DIM + 2 * nhb, **dims)

        x2 = _matmul((y_a, y_b, y_c, y_d), w_out_b, layer=l, out_dtype=F32, act="residual", extra=(x2,),
                     name="out_proj")
        h = _rmsnorm_rows(x2, norm_mlp[l])
        hidden = _matmul(h, w1_b, layer=l, out_dtype=BF16, act="relu2", name="mlp_up")
        x2 = _matmul(hidden, w2_b, layer=l, out_dtype=F32, act="residual", extra=(x2,), name="mlp_down")
    return x2.reshape(batch, seq, d_model).astype(x.dtype)
```

```python
import functools
import math

import numpy as np
import jax
import jax.numpy as jnp
from jax import lax
from jax.experimental import pallas as pl
from jax.experimental.pallas import tpu as pltpu

CHUNK = 64
EPS = 1e-6
NEG = -1e30
F_FLOOR = 1e-20
HEAD_DIM = 128
CONV_W = 4
IDX_HEADS = 16
IDX_DIM = 64
TOPK_MAX = 256
LOG2_E = math.log2(math.e)
LANE = 128
SUBLANE = 8
PACKED_SUBLANE = 16
VMEM_LIMIT = 56 * 1024 * 1024

F32 = jnp.float32
BF16 = jnp.bfloat16
NT_DIMS = (((1,), (1,)), ((), ()))
TN_DIMS = (((0,), (0,)), ((), ()))


def _params(*sem):
    return pltpu.CompilerParams(dimension_semantics=sem, vmem_limit_bytes=VMEM_LIMIT)


def _sigmoid(x):
    return 1.0 / (1.0 + jnp.exp(-x))


def _log_sigmoid(x):
    return jnp.minimum(x, 0.0) - jnp.log1p(jnp.exp(-jnp.abs(x)))


def _split3(x):
    hi = x.astype(BF16)
    r1 = x - hi.astype(F32)
    mid = r1.astype(BF16)
    lo = (r1 - mid.astype(F32)).astype(BF16)
    return hi, mid, lo


def _pick_tile(n, candidates):
    for c in candidates:
        if n % c == 0:
            return c
    raise ValueError(f"no tile in {candidates} divides {n}")


def _rmsnorm_kernel(x_ref, g_ref, o_ref):
    x = x_ref[...]
    ms = jnp.mean(x * x, axis=-1, keepdims=True)
    o_ref[...] = (x * lax.rsqrt(ms + EPS) * g_ref[...]).astype(o_ref.dtype)


def _rmsnorm_rows(x2, g):
    m, d = x2.shape
    tm = _pick_tile(m, (512, 256, 128, 64, 8))
    return pl.pallas_call(
        _rmsnorm_kernel,
        grid=(m // tm,),
        in_specs=[pl.BlockSpec((tm, d), lambda i: (i, 0)), pl.BlockSpec((1, d), lambda i: (0, 0))],
        out_specs=pl.BlockSpec((tm, d), lambda i: (i, 0)),
        out_shape=jax.ShapeDtypeStruct((m, d), BF16),
        compiler_params=_params("parallel"),
        name="rmsnorm",
    )(x2, g.reshape(1, d).astype(F32))


MM_MAX_TK = 4096
MM_TN = 1024


def _mm_epilogue(r, act, extra):
    if act == "relu2":
        r = jnp.square(jnp.maximum(r, 0.0))
    elif act == "groupnorm":
        g_ref, u_ref = extra
        segs = []
        for c in range(r.shape[1] // LANE):
            seg = r[:, c * LANE:(c + 1) * LANE]
            ms = jnp.mean(seg * seg, axis=1, keepdims=True)
            inv = jnp.where(u_ref[:, c * LANE:(c + 1) * LANE] > 0.0, lax.rsqrt(ms + EPS), 1.0)
            segs.append(seg * inv * g_ref[:, c * LANE:(c + 1) * LANE])
        r = jnp.concatenate(segs, axis=1)
    elif act == "residual":
        r = r + extra[0][...]
    return r


def _mm_kernel(*refs, n_a, nk, act, n_extra, b_rows):
    a_refs = refs[:n_a]
    b_ref = refs[n_a]
    extra = refs[n_a + 1:n_a + 1 + n_extra]
    o_ref = refs[n_a + 1 + n_extra]

    def product():
        r, off = None, 0
        for a_ref in a_refs:
            kw = a_ref.shape[1]
            if b_rows:
                d = lax.dot_general(a_ref[...], b_ref[:, off:off + kw], NT_DIMS, preferred_element_type=F32)
            else:
                d = jnp.dot(a_ref[...], b_ref[off:off + kw, :], preferred_element_type=F32)
            r = d if r is None else r + d
            off += kw
        return r

    if nk == 1:
        o_ref[...] = _mm_epilogue(product(), act, extra).astype(o_ref.dtype)
        return
    k = pl.program_id(2)

    @pl.when(k == 0)
    def _():
        o_ref[...] = extra[0][...] + product()

    @pl.when(k > 0)
    def _():
        o_ref[...] += product()


def _matmul(a, b, *, out_dtype, layer, cols=None, b_rows=False, act=None, extra=(), tm_max=1024, tn_max=MM_TN,
            tk_max=MM_MAX_TK, name="matmul"):
    a_parts = a if isinstance(a, tuple) else (a,)
    m = a_parts[0].shape[0]
    kdim = sum(p.shape[1] for p in a_parts)
    col0, n = (0, b.shape[1 if b_rows else 2]) if cols is None else cols
    tm = _pick_tile(m, tuple(t for t in (1024, 512, 256, 128, 8) if t <= tm_max))
    tn = _pick_tile(math.gcd(n, col0), tuple(t for t in (1024, 512, 256, 128) if t <= tn_max))
    j0 = col0 // tn
    tk = kdim if kdim <= tk_max else _pick_tile(kdim, tuple(t for t in (4096, 2048, 1024, 512, 256, 128)
                                                            if t <= tk_max))
    nk = kdim // tk
    assert nk == 1 or (len(a_parts) == 1 and act == "residual" and out_dtype == F32)
    in_specs = [pl.BlockSpec((tm, tk if nk > 1 else p.shape[1]), lambda i, j, k: (i, k)) for p in a_parts]
    if b_rows:
        in_specs.append(pl.BlockSpec((None, tn, tk), lambda i, j, k: (layer, j0 + j, k)))
    else:
        in_specs.append(pl.BlockSpec((None, tk, tn), lambda i, j, k: (layer, k, j0 + j)))
    for e in extra:
        if e.shape[0] == 1:
            in_specs.append(pl.BlockSpec((1, tn), lambda i, j, k: (0, j)))
        else:
            in_specs.append(pl.BlockSpec((tm, tn), lambda i, j, k: (i, j)))
    return pl.pallas_call(
        functools.partial(_mm_kernel, n_a=len(a_parts), nk=nk, act=act, n_extra=len(extra), b_rows=b_rows),
        grid=(m // tm, n // tn, nk),
        in_specs=in_specs,
        out_specs=pl.BlockSpec((tm, tn), lambda i, j, k: (i, j)),
        out_shape=jax.ShapeDtypeStruct((m, n), out_dtype),
        compiler_params=_params("parallel", "parallel", "arbitrary"),
        name=name,
    )(*a_parts, b, *extra)


def _vt_kernel(a_ref, w_ref, o_ref):
    o_ref[...] = lax.dot_general(w_ref[...], a_ref[...], NT_DIMS, preferred_element_type=F32).astype(o_ref.dtype)


def _project_transposed(a, w_rows, *, layer, rows, tk):
    m, kdim = a.shape
    r0, n = rows
    tn = _pick_tile(math.gcd(n, r0), (1024, 512, 256, 128))
    j0 = r0 // tn
    return pl.pallas_call(
        _vt_kernel,
        grid=(m // tk, n // tn),
        in_specs=[pl.BlockSpec((tk, kdim), lambda i, j: (i, 0)),
                  pl.BlockSpec((None, tn, kdim), lambda i, j: (layer, j0 + j, 0))],
        out_specs=pl.BlockSpec((None, tn, tk), lambda i, j: (i, j, 0)),
        out_shape=jax.ShapeDtypeStruct((m // tk, n, tk), BF16),
        compiler_params=_params("parallel", "parallel"),
        name="in_proj_vt",
    )(a, w_rows)


_HGRN_LEVELS = 6
_HGRN_HEADS_PER_STEP = 4


def _hgrn_constants():
    idx = np.arange(CHUNK)
    j = idx[None, :]
    r = idx[:, None]
    blocks = [j <= r, j > r]
    masks = []
    for lvl in range(_HGRN_LEVELS):
        half = (CHUNK // 2) >> lvl
        parent = idx // (2 * half)
        mid = parent * 2 * half + half
        is_right = (idx % (2 * half)) >= half
        right_rng = (j >= mid[:, None]) & (j <= r)
        left_rng = (j > r) & (j < mid[:, None])
        blocks.append(np.where(is_right[:, None], right_rng, left_rng))
        masks.append((parent[:, None] == parent[None, :]) & is_right[:, None] & (~is_right)[None, :])
    cm = np.concatenate(blocks, axis=0).astype(np.float32)
    cm3 = np.concatenate([cm, cm, cm, np.zeros_like(cm)], axis=1)
    mk = np.stack(masks, axis=0).astype(np.float32)
    return jnp.asarray(cm3, dtype=BF16), jnp.asarray(mk, dtype=F32)


def _hgrn_kernel(q_ref, f_ref, i_ref, g_ref, lb_ref, gn_ref, cm_ref, mk_ref, o_ref, st_ref, *, nchunk, hb, scale):
    @pl.when(pl.program_id(2) == 0)
    def _():
        st_ref[...] = jnp.zeros_like(st_ref)

    lb = lb_ref[...]
    gn = gn_ref[...]

    def body(c, carry):
        rows = pl.ds(pl.multiple_of(c * CHUNK, CHUNK), CHUNK)
        f = lb + (1.0 - lb) * _sigmoid(f_ref[rows, :])
        logf = jnp.log(jnp.maximum(f, F_FLOOR))
        kk_w = 1.0 - f
        qr = q_ref[rows, :]
        qq_w = qr * _sigmoid(qr) * scale
        v_w = i_ref[rows, :]
        gr = g_ref[rows, :]
        gate_w = gr * _sigmoid(gr)
        l3 = jnp.concatenate(_split3(logf) + (jnp.zeros(logf.shape, BF16),), axis=0)
        sums_w = jnp.dot(cm_ref[...], l3, preferred_element_type=F32)
        new_state = []
        for a in range(hb):
            hs = slice(a * HEAD_DIM, (a + 1) * HEAD_DIM)
            sums, qq, kk, v = sums_w[:, hs], qq_w[:, hs], kk_w[:, hs], v_w[:, hs]
            vb = v.astype(BF16)
            b = sums[0:CHUNK]
            b_rest = sums[CHUNK:2 * CHUNK]
            scores = jnp.zeros((CHUNK, CHUNK), F32)
            for lvl in range(_HGRN_LEVELS):
                e = jnp.exp(sums[(2 + lvl) * CHUNK:(3 + lvl) * CHUNK])
                s = lax.dot_general((qq * e).astype(BF16), (kk * e).astype(BF16), NT_DIMS,
                                    preferred_element_type=F32)
                scores = scores + s * mk_ref[lvl]
            diag = jnp.sum(qq * kk, axis=1, keepdims=True)
            st = carry[a]
            o = (jnp.dot(scores.astype(BF16), vb, preferred_element_type=F32) + diag * v
                 + lax.dot_general((qq * jnp.exp(b)).astype(BF16), st.astype(BF16), NT_DIMS,
                                   preferred_element_type=F32))
            ke = (kk * jnp.exp(b_rest)).astype(BF16)
            new_state.append(st * jnp.exp(b[CHUNK - 1:CHUNK, :])
                             + lax.dot_general(vb, ke, TN_DIMS, preferred_element_type=F32))
            ms = jnp.mean(o * o, axis=1, keepdims=True)
            o_ref[rows, hs] = (o * lax.rsqrt(ms + EPS) * gn * gate_w[:, hs]).astype(o_ref.dtype)
        return tuple(new_state)

    state = lax.fori_loop(0, nchunk, body, tuple(st_ref[a] for a in range(hb)), unroll=True)
    for a in range(hb):
        st_ref[a] = state[a]


def _hgrn(proj, lb, gnorm, *, batch, seq, group):
    m = batch * seq
    nh = group // HEAD_DIM
    t = _pick_tile(seq, (512, 256, 128, 64))
    nt = seq // t
    cm3, mk = _hgrn_constants()

    hb = min(_HGRN_HEADS_PER_STEP, nh)
    assert nh % hb == 0
    wide = hb * HEAD_DIM
    ng = nh // hb

    def col(off):
        return pl.BlockSpec((t, wide), lambda b, h, i: (b * nt + i, off + h))

    return pl.pallas_call(
        functools.partial(_hgrn_kernel, nchunk=t // CHUNK, hb=hb, scale=HEAD_DIM ** -0.5),
        grid=(batch, ng, nt),
        in_specs=[col(0), col(ng), col(2 * ng), col(3 * ng),
                  pl.BlockSpec((1, wide), lambda b, h, i: (0, h)),
                  pl.BlockSpec((1, HEAD_DIM), lambda b, h, i: (0, 0)),
                  pl.BlockSpec(cm3.shape, lambda b, h, i: (0, 0)),
                  pl.BlockSpec(mk.shape, lambda b, h, i: (0, 0, 0))],
        out_specs=pl.BlockSpec((t, wide), lambda b, h, i: (b * nt + i, h)),
        out_shape=jax.ShapeDtypeStruct((m, group), BF16),
        scratch_shapes=[pltpu.VMEM((hb, HEAD_DIM, HEAD_DIM), F32)],
        compiler_params=_params("parallel", "parallel", "arbitrary"),
        name="hgrn2",
    )(proj, proj, proj, proj, lb.reshape(1, group).astype(F32), gnorm.reshape(1, HEAD_DIM).astype(F32), cm3, mk)


_CONV_PAD = 8


def _mlstm_constants():
    idx = np.arange(CHUNK)
    tri = (idx[None, :] <= idx[:, None]).astype(np.float32)
    zero = np.zeros_like(tri)
    tri3 = np.concatenate([tri, tri, tri, zero], axis=1)
    trit3 = np.concatenate([tri.T, tri.T, tri.T, zero], axis=0)
    trit3 = np.concatenate([trit3, np.zeros_like(trit3)], axis=1)
    return jnp.asarray(tri3, dtype=BF16), jnp.asarray(trit3, dtype=BF16)


def _mlstm_kernel(qk_ref, v_ref, og_ref, gc_ref, gr_ref, cw_ref, bc_ref, br_ref, hn_ref, tri_ref, trit_ref,
                  o_ref, xp_ref, qk_s, c_ref, m_ref, *, nchunk, nh, dqk, dv, t, g_off):
    @pl.when(pl.program_id(1) == 0)
    def _():
        xp_ref[0:_CONV_PAD, :] = jnp.zeros((_CONV_PAD, xp_ref.shape[1]), F32)
        c_ref[...] = jnp.zeros_like(c_ref)
        m_ref[...] = jnp.zeros_like(m_ref)

    xp_ref[_CONV_PAD:_CONV_PAD + t, :] = qk_ref[...]
    conv = jnp.zeros((t, 2 * nh * dqk), F32)
    for j in range(CONV_W):
        start = _CONV_PAD - (CONV_W - 1) + j
        conv = conv + cw_ref[j:j + 1, :] * xp_ref[start:start + t, :]
    qk_s[...] = conv * _sigmoid(conv)
    xp_ref[0:_CONV_PAD, :] = xp_ref[t:t + _CONV_PAD, :]

    hn = hn_ref[...]
    kscale = dqk ** -0.5
    r_i = lax.broadcasted_iota(jnp.int32, (CHUNK, CHUNK), 0)
    c_i = lax.broadcasted_iota(jnp.int32, (CHUNK, CHUNK), 1)
    causal = c_i <= r_i
    ones = jnp.ones((CHUNK, dv), BF16)
    zpad_c = jnp.zeros((CHUNK, LANE), BF16)
    zpad_r = jnp.zeros((2 * nh, CHUNK), BF16)

    def body(c, carry):
        base = pl.multiple_of(c * CHUNK, CHUNK)
        rows = pl.ds(base, CHUNK)
        qk = qk_s[rows, :]
        gcol = gc_ref[rows, :] + bc_ref[...]
        b_c = jnp.dot(tri_ref[...], jnp.concatenate(_split3(_log_sigmoid(gcol)) + (zpad_c,), axis=0),
                      preferred_element_type=F32)
        grow = gr_ref[c] + br_ref[...]
        ig_r = grow[0:nh, :]
        b_r = jnp.dot(jnp.concatenate(_split3(_log_sigmoid(grow)) + (zpad_r,), axis=1), trit_ref[...],
                      preferred_element_type=F32)[nh:2 * nh, 0:CHUNK]
        new_state = []
        for h in range(nh):
            ig_col = gcol[:, g_off + h:g_off + h + 1]
            q_h = qk[:, h * dqk:(h + 1) * dqk].astype(BF16)
            k_f = qk[:, (nh + h) * dqk:(nh + h + 1) * dqk] * kscale
            k_h = k_f.astype(BF16)
            v_ext = jnp.concatenate([v_ref[rows, h * dv:(h + 1) * dv].astype(BF16), ones], axis=1)
            bcol = b_c[:, g_off + nh + h:g_off + nh + h + 1]
            c_ext, m_prev = carry[h]
            log_d = jnp.where(causal, bcol - b_r[h:h + 1, :] + ig_r[h:h + 1, :], NEG)
            log_inter = bcol + m_prev
            m_t = jnp.maximum(jnp.max(log_d, axis=1, keepdims=True), log_inter)
            qk_w = (lax.dot_general(q_h, k_h, NT_DIMS, preferred_element_type=F32)
                    * jnp.exp(log_d - m_t))
            w_inter = jnp.exp(log_inter - m_t)
            numden = (jnp.dot(qk_w.astype(BF16), v_ext, preferred_element_type=F32)
                      + w_inter * jnp.dot(q_h, c_ext.astype(BF16), preferred_element_type=F32))
            num = numden[:, 0:dv]
            den = numden[:, dv:2 * dv]
            hh = num / jnp.maximum(jnp.abs(den), jnp.exp(-m_t))
            m_new = m_t[CHUNK - 1:CHUNK, :]
            b_last = bcol[CHUNK - 1:CHUNK, :]
            w_state = jnp.exp(b_last - bcol + ig_col - m_new)
            decay = jnp.exp(b_last + m_prev - m_new)
            new_state.append((decay * c_ext + lax.dot_general((k_f * w_state).astype(BF16), v_ext, TN_DIMS,
                                                              preferred_element_type=F32), m_new))
            ms = jnp.mean(hh * hh, axis=1, keepdims=True)
            gate = _sigmoid(og_ref[rows, h * dv:(h + 1) * dv])
            o_ref[rows, h * dv:(h + 1) * dv] = (hh * lax.rsqrt(ms + EPS) * hn * gate).astype(o_ref.dtype)
        return tuple(new_state)

    state = lax.fori_loop(0, nchunk, body, tuple((c_ref[h], m_ref[h:h + 1, 0:1]) for h in range(nh)), unroll=True)
    for h in range(nh):
        c_ref[h] = state[h][0]
        m_ref[h:h + 1, :] = jnp.broadcast_to(state[h][1], (1, LANE))


def _mlstm(proj, tail, conv_w, gate_bias, hnorm, *, batch, seq, group, g_off):
    m = batch * seq
    nh = group // HEAD_DIM
    dqk = HEAD_DIM // 2
    dv = HEAD_DIM
    t = _pick_tile(seq, (256, 128, 64))
    nt = seq // t
    tri3, trit3 = _mlstm_constants()
    qk_blk = 4
    gates_row = (tail[:, g_off:g_off + 2 * nh].reshape(batch, seq // CHUNK, CHUNK, 2 * nh)
                 .transpose(0, 1, 3, 2))
    bias = gate_bias.astype(F32)
    bias_lane = jnp.zeros((1, LANE), F32).at[0, g_off:g_off + 2 * nh].set(bias)

    def wide(off):
        return pl.BlockSpec((t, group), lambda b, i: (b * nt + i, off))

    def const(shape):
        return pl.BlockSpec(shape, lambda b, i: tuple(0 for _ in shape))

    return pl.pallas_call(
        functools.partial(_mlstm_kernel, nchunk=t // CHUNK, nh=nh, dqk=dqk, dv=dv, t=t, g_off=g_off),
        grid=(batch, nt),
        in_specs=[wide(qk_blk), wide(qk_blk + 1), wide(qk_blk + 2),
                  pl.BlockSpec((t, LANE), lambda b, i: (b * nt + i, 0)),
                  pl.BlockSpec((None, t // CHUNK, 2 * nh, CHUNK), lambda b, i: (b, i, 0, 0)),
                  const((CONV_W, group)), const((1, LANE)), const((2 * nh, 1)), const((1, dv)),
                  const(tri3.shape), const(trit3.shape)],
        out_specs=pl.BlockSpec((t, group), lambda b, i: (b * nt + i, 0)),
        out_shape=jax.ShapeDtypeStruct((m, group), BF16),
        scratch_shapes=[pltpu.VMEM((t + _CONV_PAD, group), F32),
                        pltpu.VMEM((t, group), F32),
                        pltpu.VMEM((nh, dqk, 2 * dv), F32),
                        pltpu.VMEM((nh, LANE), F32)],
        compiler_params=_params("parallel", "arbitrary"),
        name="mlstm",
    )(proj, proj, proj, tail, gates_row, conv_w.astype(F32), bias_lane, bias.reshape(2 * nh, 1),
      hnorm.reshape(1, dv).astype(F32), tri3, trit3)


def _ki_kernel(x_ref, g_ref, o_ref):
    x = x_ref[:, 0:IDX_DIM]
    ms = jnp.mean(x * x, axis=-1, keepdims=True)
    o_ref[...] = (x * lax.rsqrt(ms + EPS) * g_ref[...]).astype(o_ref.dtype)


def _ki_norm(tail, g):
    m = tail.shape[0]
    tm = _pick_tile(m, (512, 256, 128, 64))
    return pl.pallas_call(
        _ki_kernel,
        grid=(m // tm,),
        in_specs=[pl.BlockSpec((tm, LANE), lambda i: (i, 0)), pl.BlockSpec((1, IDX_DIM), lambda i: (0, 0))],
        out_specs=pl.BlockSpec((tm, IDX_DIM), lambda i: (i, 0)),
        out_shape=jax.ShapeDtypeStruct((m, IDX_DIM), BF16),
        compiler_params=_params("parallel"),
        name="ki_norm",
    )(tail, g.reshape(1, IDX_DIM).astype(F32))


def _diff_kernel(q_ref, k_ref, vt_ref, lam_ref, sn_ref, o_ref, m_ref, l_ref, acc_ref, *, tq, tk, dqk, out_scale):
    qi = pl.program_id(2)
    last_k = ((qi + 1) * tq - 1) // tk
    m_ref[...] = jnp.full_like(m_ref, NEG)
    l_ref[...] = jnp.zeros_like(l_ref)
    acc_ref[...] = jnp.zeros_like(acc_ref)
    q = q_ref[...]

    def step(kb, masked):
        rows = pl.ds(pl.multiple_of(kb * tk, tk), tk)
        k = k_ref[rows, :]
        vt = vt_ref[kb]
        if masked:
            k_chunk = (kb * tk + lax.broadcasted_iota(jnp.int32, (tk, tq), 0)) // CHUNK
            q_chunk = (qi * tq + lax.broadcasted_iota(jnp.int32, (tk, tq), 1)) // CHUNK
            mask = k_chunk <= q_chunk
        s_pair = [lax.dot_general(k[:, half * dqk:(half + 1) * dqk], q[:, half * dqk:(half + 1) * dqk], NT_DIMS,
                                  preferred_element_type=F32) for half in range(2)]
        probs = []
        for half in range(2):
            s = s_pair[half]
            if masked:
                s = jnp.where(mask, s, NEG)
            m_prev = m_ref[half]
            m_new = jnp.maximum(m_prev, jnp.max(s, axis=0, keepdims=True))
            p = jnp.exp2(s - m_new)
            alpha = jnp.exp2(m_prev - m_new)
            l_ref[half] = alpha * l_ref[half] + jnp.sum(p, axis=0, keepdims=True)
            m_ref[half] = m_new
            probs.append((alpha, p.astype(BF16)))
        for half in range(2):
            alpha, p = probs[half]
            acc_ref[half] = alpha * acc_ref[half] + jnp.dot(vt, p, preferred_element_type=F32)

    def body(kb, carry):
        step(kb, False)
        return carry

    lax.fori_loop(0, last_k, body, 0)
    step(last_k, True)

    o = acc_ref[0] / l_ref[0] - lam_ref[...] * (acc_ref[1] / l_ref[1])
    ms = jnp.mean(o * o, axis=0, keepdims=True)
    o_ref[...] = (o * lax.rsqrt(ms + EPS) * (sn_ref[...] * out_scale)).T.astype(o_ref.dtype)


def _attention_key_tile(seq):
    return _pick_tile(seq, (1024, 512, 256))


def _diff_attention(cd, v_t, lam_full, subln, lambda_init, *, batch, seq, group):
    m = batch * seq
    dqk = HEAD_DIM
    dv = 2 * HEAD_DIM
    nh = group // dv
    tk = _attention_key_tile(seq)
    tq = tk
    nq, nkb = seq // tq, seq // tk
    k_off = nh

    return pl.pallas_call(
        functools.partial(_diff_kernel, tq=tq, tk=tk, dqk=dqk, out_scale=1.0 - lambda_init),
        grid=(batch, nh, nq),
        in_specs=[pl.BlockSpec((tq, dv), lambda b, h, i: (b * nq + i, h)),
                  pl.BlockSpec((seq, dv), lambda b, h, i: (b, k_off + h)),
                  pl.BlockSpec((nkb, dv, tk), lambda b, h, i: (b, h, 0)),
                  pl.BlockSpec((1, 1), lambda b, h, i: (0, 0)),
                  pl.BlockSpec((dv, 1), lambda b, h, i: (0, 0))],
        out_specs=pl.BlockSpec((tq, dv), lambda b, h, i: (b * nq + i, h)),
        out_shape=jax.ShapeDtypeStruct((m, group), BF16),
        scratch_shapes=[pltpu.VMEM((2, 1, tq), F32), pltpu.VMEM((2, 1, tq), F32), pltpu.VMEM((2, dv, tq), F32)],
        compiler_params=_params("parallel", "parallel", "arbitrary"),
        name="diff_attn",
    )(cd, cd, v_t, lam_full.reshape(1, 1).astype(F32), subln.reshape(dv, 1).astype(F32))


_INT_MIN = -2 ** 31
_HI16 = -(1 << 16)


_CNT_ROWS = 64
_DSA_HEADS_PER_STEP = 2


def _dsa_kernel(q_ref, k_ref, vt_ref, qi_ref, ki_ref, wt_ref, o_ref, sc_ref, sc16_ref, *, tq, tk, topk, hpb,
                w_scale):
    qb = pl.program_id(1)
    h = pl.program_id(2)
    nkv = ((qb + 1) * tq + tk - 1) // tk

    @pl.when(h == 0)
    def _():
        w = wt_ref[...] * w_scale
        q_chunk = (qb * tq + lax.broadcasted_iota(jnp.int32, (tk, tq), 1)) // CHUNK

        def score_body(kb, carry):
            rows = pl.ds(pl.multiple_of(kb * tk, tk), tk)
            ki = ki_ref[rows, :]
            acc = jnp.zeros((tk, tq), F32)
            for ih in range(IDX_HEADS):
                lg = lax.dot_general(ki, qi_ref[:, ih * IDX_DIM:(ih + 1) * IDX_DIM], NT_DIMS,
                                     preferred_element_type=F32)
                acc = acc + w[ih:ih + 1, :] * jnp.maximum(lg, 0.0)
            k_chunk = (kb * tk + lax.broadcasted_iota(jnp.int32, (tk, tq), 0)) // CHUNK
            x = jnp.where(k_chunk <= q_chunk, acc, NEG)
            sc_ref[kb] = x
            hi = lax.bitcast_convert_type(x, jnp.int32) & _HI16
            sc16_ref[kb] = lax.bitcast_convert_type(hi, F32).astype(BF16)
            return carry

        lax.fori_loop(0, nkv, score_body, 0)

        def count_ge(cf):
            def cnt_body(kb, cnt):
                for r in range(tk // _CNT_ROWS):
                    hit = jnp.where(sc_ref[kb, r * _CNT_ROWS:(r + 1) * _CNT_ROWS, :] >= cf, 1.0, 0.0)
                    for g in range(_CNT_ROWS // SUBLANE):
                        cnt = cnt + hit[g * SUBLANE:(g + 1) * SUBLANE, :]
                return cnt

            return jnp.sum(lax.fori_loop(0, nkv, cnt_body, jnp.zeros((SUBLANE, tq), F32)), axis=0, keepdims=True)

        def count_ge_hi(cfb):
            one, zero = jnp.ones((), BF16), jnp.zeros((), BF16)

            def cnt_body(kb, cnt):
                part = jnp.zeros((PACKED_SUBLANE, tq),BF16)
                for r in range(tk // _CNT_ROWS):
                    hit = jnp.where(sc16_ref[kb, r * _CNT_ROWS:(r + 1) * _CNT_ROWS, :] >= cfb, one, zero)
                    for g in range(_CNT_ROWS // PACKED_SUBLANE):
                        part = part + hit[g * PACKED_SUBLANE:(g + 1) * PACKED_SUBLANE, :]
                return cnt + part.astype(F32)

            return jnp.sum(lax.fori_loop(0, nkv, cnt_body, jnp.zeros((PACKED_SUBLANE, tq),F32)), axis=0, keepdims=True)

        def bit_body(state, *, coarse):
            i, u, kept = state
            cand = u | jnp.left_shift(jnp.int32(1), 31 - i)
            key = cand ^ _INT_MIN
            if coarse:
                key = jnp.where(key < 0, key | 0xFFFF, key)
            cf = lax.bitcast_convert_type(key ^ (jnp.right_shift(key, 31) & 0x7FFFFFFF), F32)
            cnt = count_ge_hi(cf.astype(BF16)) if coarse else count_ge(cf)
            ok = cnt >= float(topk)
            return i + 1, jnp.where(ok, cand, u), jnp.where(ok, cnt, kept)

        def unsettled(state, *, last):
            i, _, kept = state
            return (i < last) & (jnp.max(jnp.where(kept == float(topk), 0.0, 1.0)) > 0.0)

        in_range = jnp.zeros((1, tq), F32) + (nkv * tk).astype(F32)
        state = (jnp.int32(0), jnp.zeros((1, tq), jnp.int32), in_range)
        state = lax.while_loop(functools.partial(unsettled, last=16), functools.partial(bit_body, coarse=True), state)
        state = lax.while_loop(functools.partial(unsettled, last=32), functools.partial(bit_body, coarse=False),
                               (jnp.maximum(state[0], 16),) + state[1:])
        key = state[1] ^ _INT_MIN
        thr = lax.bitcast_convert_type(key ^ (jnp.right_shift(key, 31) & 0x7FFFFFFF), F32)

        tied = (state[2] > float(topk)) & (thr > 0.5 * NEG)
        has_tie = jnp.max(jnp.where(tied, 1.0, 0.0)) > 0.0

        @pl.when(jnp.logical_not(has_tie))
        def _():
            def bias_body(kb, carry):
                s = sc_ref[kb]
                sc_ref[kb] = jnp.where((s >= thr) & (s > 0.5 * NEG), 0.0, NEG)
                return carry

            lax.fori_loop(0, nkv, bias_body, 0)

        @pl.when(has_tie)
        def _():
            def gt_body(kb, cnt):
                for r in range(tk // _CNT_ROWS):
                    hit = jnp.where(sc_ref[kb, r * _CNT_ROWS:(r + 1) * _CNT_ROWS, :] > thr, 1.0, 0.0)
                    for g in range(_CNT_ROWS // SUBLANE):
                        cnt = cnt + hit[g * SUBLANE:(g + 1) * SUBLANE, :]
                return cnt

            above = jnp.sum(lax.fori_loop(0, nkv, gt_body, jnp.zeros((SUBLANE, tq), F32)), axis=0, keepdims=True)
            room = float(topk) - above
            lower = (lax.broadcasted_iota(jnp.int32, (LANE, LANE), 1)
                     <= lax.broadcasted_iota(jnp.int32, (LANE, LANE), 0)).astype(BF16)

            def tie_body(kb, seen):
                for r in range(tk // LANE):
                    strip = slice(r * LANE, (r + 1) * LANE)
                    s = sc_ref[kb, strip, :]
                    eq = s == thr
                    rank = seen + jnp.dot(lower, jnp.where(eq, 1.0, 0.0).astype(BF16), preferred_element_type=F32)
                    keep = ((s > thr) | (eq & (rank <= room))) & (s > 0.5 * NEG)
                    sc_ref[kb, strip, :] = jnp.where(keep, 0.0, NEG)
                    seen = rank[LANE - 1:LANE, :]
                return seen

            lax.fori_loop(0, nkv, tie_body, jnp.zeros((1, tq), F32))

    q = q_ref[...]

    def kv_body(kb, carry):
        rows = pl.ds(pl.multiple_of(kb * tk, tk), tk)
        k = k_ref[rows, :]
        bias = sc_ref[kb]
        qk = [lax.dot_general(k[:, a * HEAD_DIM:(a + 1) * HEAD_DIM], q[:, a * HEAD_DIM:(a + 1) * HEAD_DIM], NT_DIMS,
                              preferred_element_type=F32) for a in range(hpb)]
        out = []
        for a in range(hpb):
            m_prev, l_prev, acc = carry[a]
            s = qk[a] + bias
            m_new = jnp.maximum(m_prev, jnp.max(s, axis=0, keepdims=True))
            p = jnp.exp2(s - m_new)
            alpha = jnp.exp2(m_prev - m_new)
            l_new = alpha * l_prev + jnp.sum(p, axis=0, keepdims=True)
            acc = alpha * acc + jnp.dot(vt_ref[kb, a * HEAD_DIM:(a + 1) * HEAD_DIM, :], p.astype(BF16),
                                        preferred_element_type=F32)
            out.append((m_new, l_new, acc))
        return tuple(out)

    init = tuple((jnp.full((1, tq), NEG, F32), jnp.zeros((1, tq), F32), jnp.zeros((HEAD_DIM, tq), F32))
                 for _ in range(hpb))
    fin = lax.fori_loop(0, nkv, kv_body, init)
    for a in range(hpb):
        _, l_fin, acc = fin[a]
        o_ref[:, a * HEAD_DIM:(a + 1) * HEAD_DIM] = (acc / l_fin).T.astype(o_ref.dtype)


def _dsa_attention(cd, v_t, ki_n, tail, *, batch, seq, group, w_off):
    m = batch * seq
    nh = group // HEAD_DIM
    tq = _pick_tile(seq, (512, 256, 128))
    tk = _attention_key_tile(seq)
    topk = min(TOPK_MAX, seq // 4)
    assert tk >= topk, "the threshold search needs at least topk keys in range"
    nq = seq // tq
    nc = group // LANE
    nkb = seq // tk
    q_off, k_off = 2 * nc, 3 * nc
    qi_off = 4 * nc * LANE // (IDX_HEADS * IDX_DIM)
    assert qi_off * IDX_HEADS * IDX_DIM == 4 * nc * LANE
    w_t = tail[:, w_off:w_off + IDX_HEADS].reshape(batch, seq, IDX_HEADS).transpose(0, 2, 1)

    hpb = _DSA_HEADS_PER_STEP
    assert nh % hpb == 0
    wide = hpb * HEAD_DIM
    return pl.pallas_call(
        functools.partial(_dsa_kernel, tq=tq, tk=tk, topk=topk, hpb=hpb,
                          w_scale=IDX_HEADS ** -0.5 * IDX_DIM ** -0.5),
        grid=(batch, nq, nh // hpb),
        in_specs=[pl.BlockSpec((tq, wide), lambda b, i, h: (b * nq + i, q_off // hpb + h)),
                  pl.BlockSpec((seq, wide), lambda b, i, h: (b, k_off // hpb + h)),
                  pl.BlockSpec((nkb, wide, tk), lambda b, i, h: (b, group // wide + h, 0)),
                  pl.BlockSpec((tq, IDX_HEADS * IDX_DIM), lambda b, i, h: (b * nq + i, qi_off)),
                  pl.BlockSpec((seq, IDX_DIM), lambda b, i, h: (b, 0)),
                  pl.BlockSpec((None, IDX_HEADS, tq), lambda b, i, h: (b, 0, i))],
        out_specs=pl.BlockSpec((tq, wide), lambda b, i, h: (b * nq + i, h)),
        out_shape=jax.ShapeDtypeStruct((m, group), BF16),
        scratch_shapes=[pltpu.VMEM((nkb, tk, tq), F32), pltpu.VMEM((nkb, tk, tq), BF16)],
        compiler_params=_params("parallel", "arbitrary", "arbitrary"),
        name="dsa",
    )(cd, cd, v_t, cd, ki_n, w_t)


def _repack_w_in(w, group, nhb):
    p0 = 7 * group
    p1 = p0 + 2 * nhb
    p2 = p1 + 6 * group + IDX_HEADS * IDX_DIM
    used = IDX_DIM + 2 * nhb + IDX_HEADS
    wt = jnp.swapaxes(w, 1, 2)
    pad = jnp.zeros((w.shape[0], LANE - used, w.shape[1]), w.dtype)

    def sec(first, count):
        return wt[:, p1 + first * group:p1 + (first + count) * group]

    return jnp.concatenate([wt[:, :p0], sec(0, 2), sec(3, 2), wt[:, p1 + 6 * group:p2], sec(2, 1), sec(5, 1),
                            wt[:, p2:p2 + IDX_DIM], wt[:, p0:p1], wt[:, p2 + IDX_DIM:], pad], axis=1).astype(BF16)


def kernel(x, norm_mix, w_in, hgrn_lb_logits, hgrn_gnorm, mlstm_conv, mlstm_gate_bias, mlstm_hnorm,
           diff_qk_norm, diff_lambda, diff_subln, dsa_qk_norm, dsa_idx_knorm, w_out, norm_mlp, w1, w2):
    batch, seq, d_model = x.shape
    depth = w_in.shape[0]
    group = d_model // 4
    nhb = group // HEAD_DIM
    nc = group // LANE
    m = batch * seq
    dims = dict(batch=batch, seq=seq, group=group)

    p_lb = jax.nn.softmax(hgrn_lb_logits.astype(F32), axis=0)
    lower_bounds = jnp.cumsum(p_lb, axis=0) - p_lb[0:1]

    w_in_b = _repack_w_in(w_in, group, nhb)
    w_out_b, w1_b, w2_b = (w.astype(BF16) for w in (w_out, w1, w2))
    n_ab, n_qk, n_v = 7 * group, 4 * group + IDX_HEADS * IDX_DIM, 2 * group
    tk_att = _attention_key_tile(seq)

    x2 = x.reshape(m, d_model).astype(F32)
    ones_g = jnp.ones((group,), F32)
    for l in range(depth):
        lambda_init = 0.8 - 0.6 * math.exp(-0.3 * l)
        h = _rmsnorm_rows(x2, norm_mix[l])
        proj = _matmul(h, w_in_b, layer=l, cols=(0, n_ab), b_rows=True, out_dtype=F32, name="in_proj_ab")
        tail = _matmul(h, w_in_b, layer=l, cols=(n_ab + n_qk + n_v, LANE), b_rows=True, out_dtype=F32,
                       name="in_proj_tail")
        v_t = _project_transposed(h, w_in_b, layer=l, rows=(n_ab + n_qk, n_v), tk=tk_att)

        y_a = _hgrn(proj, lower_bounds[l], hgrn_gnorm[l], **dims)

        y_b = _mlstm(proj, tail, mlstm_conv[l], mlstm_gate_bias[l], mlstm_hnorm[l], g_off=IDX_DIM, **dims)

        cqn = diff_qk_norm[l].astype(F32)
        dqn = dsa_qk_norm[l].astype(F32)
        q_scale = HEAD_DIM ** -0.5 * LOG2_E
        gains = jnp.concatenate([
            jnp.tile(cqn[0], nc) * q_scale, jnp.tile(cqn[1], nc),
            jnp.tile(dqn[0], nc) * q_scale, jnp.tile(dqn[1], nc),
            jnp.ones((IDX_HEADS * IDX_DIM,), F32)]).reshape(1, -1)
        use_norm = jnp.concatenate([
            jnp.ones((4 * group,), F32), jnp.zeros((IDX_HEADS * IDX_DIM,), F32)]).reshape(1, -1)
        cd = _matmul(h, w_in_b, layer=l, cols=(n_ab, n_qk), b_rows=True, out_dtype=BF16, act="groupnorm",
                     extra=(gains, use_norm), name="in_proj_cd")
        ki_n = _ki_norm(tail, dsa_idx_knorm[l])

        lam = diff_lambda[l].astype(F32)
        lam_full = jnp.exp(jnp.sum(lam[0] * lam[1])) - jnp.exp(jnp.sum(lam[2] * lam[3])) + lambda_init
        y_c = _diff_attention(cd, v_t, lam_full, diff_subln[l], lambda_init, **dims)
        y_d = _dsa_attention(cd, v_t, ki_n, tail, w_off=IDX_DIM + 2 * nhb, **dims)

        x2 = _matmul((y_a, y_b, y_c, y_d), w_out_b, layer=l, out_dtype=F32, act="residual", extra=(x2,),
                     name="out_proj")
        h = _rmsnorm_rows(x2, norm_mlp[l])
        hidden = _matmul(h, w1_b, layer=l, out_dtype=BF16, act="relu2", name="mlp_up")
        x2 = _matmul(hidden, w2_b, layer=l, out_dtype=F32, act="residual", extra=(x2,), name="mlp_down")
    return x2.reshape(batch, seq, d_model).astype(x.dtype)
```
